```python
import jax
import jax.numpy as jnp
from jax import lax
import numpy as np

D_MODEL = 1024
BATCH = 16
SEQ = 2048
DEPTH = 2

D_GROUP = D_MODEL // 4
HEAD_DIM = 64
N_GROUP_HEADS = D_GROUP // HEAD_DIM

RWKV_LORA_DECAY = 64
RWKV_LORA_AAA = 64
RWKV_LORA_GATE = 128
RWKV_LN_EPS = 64e-5

DILATED_PATTERNS = ((128, 1), (512, 4), (2048, 16))

MLA_Q_RANK = D_MODEL // 4
MLA_KV_RANK = D_MODEL // 8
MLA_NOPE_DIM = 64
MLA_ROPE_DIM = 32
MLA_V_DIM = D_GROUP // N_GROUP_HEADS

HGRN_CHUNK = 16

D_FF = 2816
N_EXPERTS = 8
TOP_K = 2
D_FF_EXPERT = 3584

ROPE_THETA = 10000.0
RMS_EPS = 1e-6
ATTN_BLOCK = 128
NEG_INF = -1e30

RWKV_COLS = (D_GROUP, D_GROUP, D_GROUP, RWKV_LORA_DECAY, RWKV_LORA_AAA, RWKV_LORA_GATE)
N_RWKV_COLS = sum(RWKV_COLS)
N_DIL_COLS = 3 * D_GROUP
MLA_COLS = (MLA_Q_RANK, MLA_KV_RANK, MLA_ROPE_DIM)
N_MLA_COLS = sum(MLA_COLS)
N_HGRN_COLS = 4 * D_GROUP
MIXER_COLS = (N_RWKV_COLS, N_DIL_COLS, N_MLA_COLS, N_HGRN_COLS)
N_IN_COLS = sum(MIXER_COLS)
N_DENSE_LAYERS = (DEPTH + 1) // 2
N_MOE_LAYERS = DEPTH // 2

kernel_name = 'hybrid_rwkv7_dilated_mla_hgrn2_moe'


def split_cols(a, sizes):
    out, start = [], 0
    for s in sizes:
        out.append(a[..., start:start + s])
        start += s
    return out


def rms_norm(x, w):
    xf = x.astype(jnp.float32)
    y = xf * lax.rsqrt(jnp.mean(xf * xf, axis=-1, keepdims=True) + RMS_EPS)
    return (y * w).astype(x.dtype)


def rope_tables(positions, dim):
    inv_freq = 1.0 / (ROPE_THETA ** (jnp.arange(0, dim, 2, dtype=jnp.float32) / dim))
    ang = positions.astype(jnp.float32)[..., None] * inv_freq
    return jnp.cos(ang), jnp.sin(ang)


def apply_rope(x, cos, sin):
    half = x.shape[-1] // 2
    x1 = x[..., :half].astype(jnp.float32)
    x2 = x[..., half:].astype(jnp.float32)
    c, s = cos[:, None], sin[:, None]
    return jnp.concatenate([x1 * c - x2 * s, x2 * c + x1 * s], axis=-1).astype(x.dtype)


def to_heads(t, n_heads):
    b, s, _ = t.shape
    return t.reshape(b, s, n_heads, -1).transpose(0, 2, 1, 3)


def token_shift(a, mu):
    prev = jnp.pad(a, ((0, 0), (1, 0), (0, 0)))[:, :-1]
    return a + (prev - a) * mu


def rwkv7_time_mix(cols, mu, w0, w2, a0, a2, g2, k_k, k_a, r_k, ln_w, ln_b):
    b_, t_, _ = cols.shape
    h_, n_ = N_GROUP_HEADS, HEAD_DIM
    r, k, v, xw, xa, xg = split_cols(token_shift(cols, mu), RWKV_COLS)
    w = -jax.nn.softplus(-(w0 + jnp.tanh(xw) @ w2).astype(jnp.float32)) - 0.5
    decay = jnp.exp(-jnp.exp(w))
    a = jax.nn.sigmoid((a0 + xa @ a2).astype(jnp.float32))
    g = (jax.nn.sigmoid(xg) @ g2).astype(jnp.float32)
    heads = lambda t: t.reshape(b_, t_, h_, n_)
    kk = heads((k * k_k).astype(jnp.float32))
    kk = kk / jnp.maximum(jnp.sqrt(jnp.sum(kk * kk, axis=-1, keepdims=True)), 1e-12)
    k = k.astype(jnp.float32) * (1.0 + (a - 1.0) * k_a)
    r_h, k_h, v_h = heads(r.astype(jnp.float32)), heads(k), heads(v.astype(jnp.float32))
    w_h, a_h = heads(decay), heads(a)

    def step(state, inp):
        r_t, w_t, k_t, v_t, kk_t, a_t = inp
        s_kk = jnp.einsum('bhvk,bhk->bhv', state, kk_t)
        state = (state * w_t[:, :, None, :]
                 - s_kk[..., None] * (kk_t * a_t)[:, :, None, :]
                 + v_t[..., None] * k_t[:, :, None, :])
        return state, jnp.einsum('bhvk,bhk->bhv', state, r_t)

    xs = tuple(jnp.moveaxis(t, 1, 0) for t in (r_h, w_h, k_h, v_h, kk, a_h))
    s0 = jnp.zeros((b_, h_, n_, n_), jnp.float32)
    _, y = lax.scan(step, s0, xs)
    y = jnp.moveaxis(y, 0, 1)
    mean = jnp.mean(y, axis=-1, keepdims=True)
    var = jnp.mean((y - mean) ** 2, axis=-1, keepdims=True)
    y = ((y - mean) * lax.rsqrt(var + RWKV_LN_EPS)).reshape(b_, t_, D_GROUP) * ln_w + ln_b
    bonus = jnp.sum(r_h * k_h * r_k.reshape(h_, n_), axis=-1, keepdims=True) * v_h
    y = y + bonus.reshape(b_, t_, D_GROUP)
    return (y * g).astype(cols.dtype)


def banded_causal_attention(q, k, v, span):
    *lead, l_, hd = q.shape
    blk = span
    nb = -(-l_ // blk)
    pad = nb * blk - l_
    padw = [(0, 0)] * len(lead) + [(0, pad), (0, 0)]
    q, k, v = (jnp.pad(t, padw).reshape(*lead, nb, blk, hd) for t in (q, k, v))

    def with_prev(t):
        prev = jnp.concatenate([jnp.zeros_like(t[..., :1, :, :]), t[..., :-1, :, :]], axis=-3)
        return jnp.concatenate([prev, t], axis=-2)

    kb, vb = with_prev(k), with_prev(v).astype(jnp.float32)
    s = jnp.einsum('...qd,...kd->...qk', q, kb).astype(jnp.float32) * (hd ** -0.5)
    dist = (jnp.arange(blk)[:, None] + blk) - jnp.arange(2 * blk)[None, :]
    in_band = (dist >= 0) & (dist <= span)
    after_start = (jnp.arange(nb)[:, None, None] > 0) | (jnp.arange(2 * blk)[None, None, :] >= blk)
    s = jnp.where(in_band & after_start, s, NEG_INF)
    m = jnp.max(s, axis=-1, keepdims=True)
    p = jnp.exp(s - m)
    den = jnp.sum(p, axis=-1, keepdims=True)
    o = jnp.einsum('...qk,...kd->...qd', p, vb) / den
    lse = (m + jnp.log(den))[..., 0]
    o = o.reshape(*lead, nb * blk, hd)[..., :l_, :]
    lse = lse.reshape(*lead, nb * blk)[..., :l_]
    return o, lse


def dilated_attention(q, k, v):
    b_, h_, t_, hd = q.shape
    outs, lses = [], []
    for window, dil in DILATED_PATTERNS:
        l_ = t_ // dil

        def fold(a):
            return a.reshape(b_, h_, l_, dil, hd).swapaxes(2, 3)

        o, lse = banded_causal_attention(fold(q), fold(k), fold(v), window // dil)
        outs.append(o.swapaxes(2, 3).reshape(b_, h_, t_, hd))
        lses.append(lse.swapaxes(2, 3).reshape(b_, h_, t_))
    weights = jax.nn.softmax(jnp.stack(lses), axis=0)
    return jnp.sum(weights[..., None] * jnp.stack(outs), axis=0)


def blocked_causal_attention(q, k, v):
    b_, h_, t_, dq = q.shape
    nb = t_ // ATTN_BLOCK
    qb = jnp.moveaxis(q.reshape(b_, h_, nb, ATTN_BLOCK, dq), 2, 0)
    vf = v.astype(jnp.float32)
    kpos = jnp.arange(t_)
    scale = dq ** -0.5

    def attend(args):
        q_blk, blk_idx = args
        s = jnp.einsum('bhqd,bhkd->bhqk', q_blk, k).astype(jnp.float32) * scale
        qpos = blk_idx * ATTN_BLOCK + jnp.arange(ATTN_BLOCK)
        s = jnp.where(kpos[None, :] <= qpos[:, None], s, NEG_INF)
        return jnp.einsum('bhqk,bhkd->bhqd', jax.nn.softmax(s, axis=-1), vf)

    o = lax.map(attend, (qb, jnp.arange(nb)))
    return jnp.moveaxis(o, 0, 2).reshape(b_, h_, t_, v.shape[-1])


def mla_attention(cols, q_norm, w_uq, kv_norm, w_ukv, cos_r, sin_r):
    b_, t_, _ = cols.shape
    h_ = N_GROUP_HEADS
    c_q, c_kv, k_rope = split_cols(cols, MLA_COLS)
    q = to_heads(rms_norm(c_q, q_norm) @ w_uq, h_)
    kv = to_heads(rms_norm(c_kv, kv_norm) @ w_ukv, h_)
    q_nope, q_rope = q[..., :MLA_NOPE_DIM], q[..., MLA_NOPE_DIM:]
    k_nope, v = kv[..., :MLA_NOPE_DIM], kv[..., MLA_NOPE_DIM:]
    q_rope = apply_rope(q_rope, cos_r, sin_r)
    k_rope = apply_rope(k_rope[:, None], cos_r, sin_r)
    qf = jnp.concatenate([q_nope, q_rope], axis=-1)
    kf = jnp.concatenate([k_nope, jnp.broadcast_to(k_rope, (b_, h_, t_, MLA_ROPE_DIM))], axis=-1)
    o = blocked_causal_attention(qf, kf, v)
    return o.transpose(0, 2, 1, 3).reshape(b_, t_, h_ * MLA_V_DIM).astype(cols.dtype)


def gated_linear_recurrence_chunked(q, k, v, log_f):
    b_, h_, t_, dk = q.shape
    dv = v.shape[-1]
    c_ = HGRN_CHUNK
    n_ = t_ // c_
    q, k, v, log_f = (t.reshape(b_, h_, n_, c_, t.shape[-1]) for t in (q, k, v, log_f))
    b = jnp.cumsum(log_f, axis=3)
    b_end = b[:, :, :, c_ - 1:]
    causal = jnp.tril(jnp.ones((c_, c_), dtype=bool))[:, :, None]
    diff = b[:, :, :, :, None, :] - b[:, :, :, None, :, :]
    intra_decay = jnp.exp(jnp.where(causal, diff, NEG_INF))
    scores = jnp.einsum('bhnck,bhnsk,bhncsk->bhncs', q, k, intra_decay)
    o_intra = jnp.einsum('bhncs,bhnsv->bhncv', scores, v)
    chunk_update = jnp.einsum('bhnsk,bhnsv->bhnkv', k * jnp.exp(b_end - b), v)
    chunk_decay = jnp.exp(b_end[:, :, :, 0])

    def step(state, inp):
        upd, dec = inp
        return state * dec[..., None] + upd, state

    s0 = jnp.zeros((b_, h_, dk, dv), q.dtype)
    _, s_start = lax.scan(step, s0, (jnp.moveaxis(chunk_update, 2, 0), jnp.moveaxis(chunk_decay, 2, 0)))
    o_inter = jnp.einsum('bhnck,nbhkv->bhncv', q * jnp.exp(b), s_start)
    return (o_intra + o_inter).reshape(b_, h_, t_, dv)


def hgrn2_mixer(cols, lower_bound, g_norm):
    b_, t_, _ = cols.shape
    h_, n_ = N_GROUP_HEADS, HEAD_DIM
    q, f, i, g = split_cols(cols, (D_GROUP,) * 4)
    heads = lambda t: to_heads(t, h_).astype(jnp.float32)
    z = heads(f)
    lb = lower_bound.astype(jnp.float32).reshape(h_, n_)[:, None, :]
    forget = lb + (1.0 - lb) * jax.nn.sigmoid(z)
    log_f = jnp.log(forget)
    k = 1.0 - forget
    o = gated_linear_recurrence_chunked(heads(jax.nn.silu(q)), k, heads(i), log_f)
    o = o * lax.rsqrt(jnp.mean(o * o, axis=-1, keepdims=True) + RMS_EPS)
    o = o.transpose(0, 2, 1, 3).reshape(b_, t_, D_GROUP) * g_norm
    return (o * jax.nn.silu(g)).astype(cols.dtype)


def swiglu(h, w1, w3, w2):
    return (jax.nn.silu(h @ w1) * (h @ w3)) @ w2


def moe_swiglu(h, w_router, w1, w3, w2):
    logits = (h @ w_router).astype(jnp.float32)
    top_v, top_i = lax.top_k(logits, TOP_K)
    gates = jax.nn.softmax(top_v, axis=-1)
    combine = jnp.sum(jax.nn.one_hot(top_i, N_EXPERTS, dtype=jnp.float32) * gates[..., None], axis=-2)
    out = jnp.zeros_like(h)
    for e in range(N_EXPERTS):
        out = out + combine[..., e:e + 1].astype(h.dtype) * swiglu(h, w1[e], w3[e], w2[e])
    return out


def setup_inputs(seed: int = 0) -> dict:
    key = jax.random.key(seed)
    keys = jax.random.split(key, 40)
    counter = [0]

    def nxt():
        kk = keys[counter[0]]
        counter[0] += 1
        return kk

    def nrm(shape, scale):
        return scale * jax.random.normal(nxt(), shape, jnp.float32)

    def gain(shape):
        return 1.0 + 0.02 * jax.random.normal(nxt(), shape, jnp.float32)

    x = nrm((BATCH, SEQ, D_MODEL), 1.0)
    positions = (jax.random.randint(nxt(), (BATCH, 1), 0, SEQ, dtype=jnp.int32)
                 + jnp.arange(SEQ, dtype=jnp.int32)[None, :])
    return {
        'x': x,
        'positions': positions,
        'mix_norm': gain((DEPTH, D_MODEL)),
        'w_in': nrm((DEPTH, D_MODEL, N_IN_COLS), D_MODEL ** -0.5),
        'w_out': nrm((DEPTH, D_MODEL, D_MODEL), D_MODEL ** -0.5),
        'rwkv_mu': jax.random.uniform(nxt(), (DEPTH, N_RWKV_COLS), jnp.float32),
        'rwkv_w0': nrm((DEPTH, D_GROUP), 0.5) - 1.0,
        'rwkv_w2': nrm((DEPTH, RWKV_LORA_DECAY, D_GROUP), 0.5 * RWKV_LORA_DECAY ** -0.5),
        'rwkv_a0': nrm((DEPTH, D_GROUP), 0.1),
        'rwkv_a2': nrm((DEPTH, RWKV_LORA_AAA, D_GROUP), RWKV_LORA_AAA ** -0.5),
        'rwkv_g2': nrm((DEPTH, RWKV_LORA_GATE, D_GROUP), RWKV_LORA_GATE ** -0.5),
        'rwkv_k_k': nrm((DEPTH, D_GROUP), 0.05) + 0.85,
        'rwkv_k_a': gain((DEPTH, D_GROUP)),
        'rwkv_r_k': nrm((DEPTH, D_GROUP), 0.1),
        'rwkv_ln_w': gain((DEPTH, D_GROUP)),
        'rwkv_ln_b': nrm((DEPTH, D_GROUP), 0.02),
        'mla_q_norm': gain((DEPTH, MLA_Q_RANK)),
        'mla_w_uq': nrm((DEPTH, MLA_Q_RANK, N_GROUP_HEADS * (MLA_NOPE_DIM + MLA_ROPE_DIM)), MLA_Q_RANK ** -0.5),
        'mla_kv_norm': gain((DEPTH, MLA_KV_RANK)),
        'mla_w_ukv': nrm((DEPTH, MLA_KV_RANK, N_GROUP_HEADS * (MLA_NOPE_DIM + MLA_V_DIM)), MLA_KV_RANK ** -0.5),
        'hgrn_lb_logits': nrm((DEPTH, D_GROUP), 1.0),
        'hgrn_g_norm': gain((DEPTH, D_GROUP)),
        'ffn_norm': gain((DEPTH, D_MODEL)),
        'ffn_w1': nrm((N_DENSE_LAYERS, D_MODEL, D_FF), D_MODEL ** -0.5),
        'ffn_w3': nrm((N_DENSE_LAYERS, D_MODEL, D_FF), D_MODEL ** -0.5),
        'ffn_w2': nrm((N_DENSE_LAYERS, D_FF, D_MODEL), D_FF ** -0.5),
        'moe_router': nrm((N_MOE_LAYERS, D_MODEL, N_EXPERTS), D_MODEL ** -0.5),
        'moe_w1': nrm((N_MOE_LAYERS, N_EXPERTS, D_MODEL, D_FF_EXPERT), D_MODEL ** -0.5),
        'moe_w3': nrm((N_MOE_LAYERS, N_EXPERTS, D_MODEL, D_FF_EXPERT), D_MODEL ** -0.5),
        'moe_w2': nrm((N_MOE_LAYERS, N_EXPERTS, D_FF_EXPERT, D_MODEL), D_FF_EXPERT ** -0.5),
        'final_norm': gain((D_MODEL,)),
    }


def reference(x, positions, mix_norm, w_in, w_out, rwkv_mu, rwkv_w0, rwkv_w2, rwkv_a0, rwkv_a2,
              rwkv_g2, rwkv_k_k, rwkv_k_a, rwkv_r_k, rwkv_ln_w, rwkv_ln_b, mla_q_norm, mla_w_uq,
              mla_kv_norm, mla_w_ukv, hgrn_lb_logits, hgrn_g_norm, ffn_norm, ffn_w1, ffn_w3, ffn_w2,
              moe_router, moe_w1, moe_w3, moe_w2, final_norm):
    b_, t_, _ = x.shape
    cos_d, sin_d = rope_tables(positions, HEAD_DIM)
    cos_m, sin_m = rope_tables(positions, MLA_ROPE_DIM)
    lb_sm = jax.nn.softmax(hgrn_lb_logits.astype(jnp.float32), axis=0)
    lower_bounds = jnp.cumsum(lb_sm, axis=0) - lb_sm[0]
    for layer in range(DEPTH):
        h = rms_norm(x, mix_norm[layer])
        c_rwkv, c_dil, c_mla, c_hgrn = split_cols(h @ w_in[layer], MIXER_COLS)
        y_a = rwkv7_time_mix(c_rwkv, rwkv_mu[layer], rwkv_w0[layer], rwkv_w2[layer], rwkv_a0[layer],
                             rwkv_a2[layer], rwkv_g2[layer], rwkv_k_k[layer], rwkv_k_a[layer],
                             rwkv_r_k[layer], rwkv_ln_w[layer], rwkv_ln_b[layer])
        q_d, k_d, v_d = (to_heads(t, N_GROUP_HEADS) for t in split_cols(c_dil, (D_GROUP,) * 3))
        y_b = dilated_attention(apply_rope(q_d, cos_d, sin_d), apply_rope(k_d, cos_d, sin_d), v_d)
        y_b = y_b.transpose(0, 2, 1, 3).reshape(b_, t_, D_GROUP).astype(x.dtype)
        y_c = mla_attention(c_mla, mla_q_norm[layer], mla_w_uq[layer], mla_kv_norm[layer],
                            mla_w_ukv[layer], cos_m, sin_m)
        y_d = hgrn2_mixer(c_hgrn, lower_bounds[layer], hgrn_g_norm[layer])
        x = x + jnp.concatenate([y_a, y_b, y_c, y_d], axis=-1) @ w_out[layer]
        h = rms_norm(x, ffn_norm[layer])
        if layer % 2 == 0:
            j = layer // 2
            x = x + swiglu(h, ffn_w1[j], ffn_w3[j], ffn_w2[j])
        else:
            j = layer // 2
            x = x + moe_swiglu(h, moe_router[j], moe_w1[j], moe_w3[j], moe_w2[j])
    return rms_norm(x, final_norm)
```

```python
import functools

import jax
import jax.numpy as jnp
from jax import lax
from jax.experimental import pallas as pl
from jax.experimental.pallas import tpu as pltpu

F32 = jnp.float32
BF16 = jnp.bfloat16

D_GROUP = 256
HEAD_DIM = 64
N_HEADS = 4
RMS_EPS = 1e-6
RWKV_LN_EPS = 64e-5
ROPE_THETA = 10000.0
NEG_INF = -1e30
N_EXPERTS = 8
MLA_ROPE = 32
MLA_NOPE = 64

VMEM_LIMIT = 56 * 1024 * 1024

C_RWKV = 0
C_DIL = 1024
C_HGRN = 2304
C_MLA = 3328
N_PACK = 3968


def _cparams(sem, vmem=VMEM_LIMIT):
    return pltpu.CompilerParams(dimension_semantics=sem, vmem_limit_bytes=vmem)


def _rms(x, w):
    return x * lax.rsqrt(jnp.mean(x * x, axis=-1, keepdims=True) + RMS_EPS) * w


def _dot(a, b):
    return jnp.dot(a, b, preferred_element_type=F32)


def _dot_nt(a, b):
    return lax.dot_general(a, b, (((1,), (1,)), ((), ())), preferred_element_type=F32)


def _split_dot(a_f32, b_bf16):
    hi = a_f32.astype(BF16)
    lo = (a_f32 - hi.astype(F32)).astype(BF16)
    return _dot(hi, b_bf16) + _dot(lo, b_bf16)


def _split_dot_lhs(a_bf16, b_f32):
    hi = b_f32.astype(BF16)
    lo = (b_f32 - hi.astype(F32)).astype(BF16)
    return _dot(a_bf16, hi) + _dot(a_bf16, lo)


def _rope_kernel(pos_ref, cd_ref, sd_ref, cm_ref, sm_ref):
    pos = pos_ref[...].astype(F32)
    tm = pos.shape[0]
    lane = lax.broadcasted_iota(jnp.int32, (tm, 256), 1)
    j = (lane & 31).astype(F32)
    inv = jnp.exp(j * (-2.0 / HEAD_DIM * jnp.log(ROPE_THETA)))
    ang = pos * inv
    cd_ref[...] = jnp.cos(ang)
    sd_ref[...] = jnp.sin(ang)
    lane = lax.broadcasted_iota(jnp.int32, (tm, 128), 1)
    jm = (lane & 15).astype(F32)
    invm = jnp.exp(jm * (-2.0 / MLA_ROPE * jnp.log(ROPE_THETA)))
    angm = pos * invm
    rope_lane = (lane >= MLA_NOPE) & (lane < MLA_NOPE + MLA_ROPE)
    cm_ref[...] = jnp.where(rope_lane, jnp.cos(angm), jnp.where(lane < MLA_NOPE, 1.0, 0.0))
    sm_ref[...] = jnp.where(rope_lane, jnp.sin(angm), 0.0)


def rope_tables(pos_col, tm=512):
    n = pos_col.shape[0]
    return pl.pallas_call(
        _rope_kernel,
        grid=(n // tm,),
        in_specs=[pl.BlockSpec((tm, 1), lambda i: (i, 0))],
        out_specs=[pl.BlockSpec((tm, 256), lambda i: (i, 0)), pl.BlockSpec((tm, 256), lambda i: (i, 0)),
                   pl.BlockSpec((tm, 128), lambda i: (i, 0)), pl.BlockSpec((tm, 128), lambda i: (i, 0))],
        out_shape=[jax.ShapeDtypeStruct((n, 256), F32), jax.ShapeDtypeStruct((n, 256), F32),
                   jax.ShapeDtypeStruct((n, 128), F32), jax.ShapeDtypeStruct((n, 128), F32)],
        compiler_params=_cparams(("parallel",)),
        name="rope_tables",
    )(pos_col)


def _in_proj_kernel(x_ref, nw_ref, w_ref, o_ref):
    h = _rms(x_ref[...], nw_ref[...]).astype(BF16)
    o_ref[...] = _dot(h, w_ref[...])


def in_proj(x2, norm_w, w_packed, tm=512):
    n, d = x2.shape
    nc = w_packed.shape[1]
    return pl.pallas_call(
        _in_proj_kernel,
        grid=(n // tm,),
        in_specs=[pl.BlockSpec((tm, d), lambda i: (i, 0)),
                  pl.BlockSpec((1, d), lambda i: (0, 0)),
                  pl.BlockSpec((d, nc), lambda i: (0, 0))],
        out_specs=pl.BlockSpec((tm, nc), lambda i: (i, 0)),
        out_shape=jax.ShapeDtypeStruct((n, nc), F32),
        compiler_params=_cparams(("parallel",)),
        name="in_proj",
    )(x2, norm_w.reshape(1, d), w_packed)


def _out_proj_kernel(x_ref, ya_ref, yb_ref, yc_ref, yd_ref, w_ref, nw_ref, xo_ref, h_ref):
    acc = x_ref[...]
    for g, y_ref in enumerate((ya_ref, yb_ref, yc_ref, yd_ref)):
        acc = acc + _dot(y_ref[...].astype(BF16), w_ref[g * D_GROUP:(g + 1) * D_GROUP, :])
    xo_ref[...] = acc
    h_ref[...] = _rms(acc, nw_ref[...]).astype(BF16)


def out_proj(x2, ys, w_out_bf16, norm_w, tm=512):
    n, d = x2.shape
    yspec = pl.BlockSpec((tm, D_GROUP), lambda i: (i, 0))
    return pl.pallas_call(
        _out_proj_kernel,
        grid=(n // tm,),
        in_specs=[pl.BlockSpec((tm, d), lambda i: (i, 0)), yspec, yspec, yspec, yspec,
                  pl.BlockSpec((d, d), lambda i: (0, 0)),
                  pl.BlockSpec((1, d), lambda i: (0, 0))],
        out_specs=[pl.BlockSpec((tm, d), lambda i: (i, 0)), pl.BlockSpec((tm, d), lambda i: (i, 0))],
        out_shape=[jax.ShapeDtypeStruct((n, d), F32), jax.ShapeDtypeStruct((n, d), BF16)],
        compiler_params=_cparams(("parallel",)),
        name="out_proj",
    )(x2, *ys, w_out_bf16, norm_w.reshape(1, d))


def _ffn_kernel(h_ref, x_ref, w1_ref, w3_ref, w2_ref, o_ref, acc_ref):
    j = pl.program_id(1)

    @pl.when(j == 0)
    def _():
        acc_ref[...] = x_ref[...]

    h = h_ref[...]
    a = _dot(h, w1_ref[...])
    b = _dot(h, w3_ref[...])
    act = (a * jax.nn.sigmoid(a) * b).astype(BF16)
    acc_ref[...] += _dot(act, w2_ref[...])

    @pl.when(j == pl.num_programs(1) - 1)
    def _():
        o_ref[...] = acc_ref[...]


def ffn_dense(h_bf16, x2, w1, w3, w2, tm=512, tf=1408):
    n, d = x2.shape
    f = w1.shape[1]
    return pl.pallas_call(
        _ffn_kernel,
        grid=(n // tm, f // tf),
        in_specs=[pl.BlockSpec((tm, d), lambda i, j: (i, 0)),
                  pl.BlockSpec((tm, d), lambda i, j: (i, 0)),
                  pl.BlockSpec((d, tf), lambda i, j: (0, j)),
                  pl.BlockSpec((d, tf), lambda i, j: (0, j)),
                  pl.BlockSpec((tf, d), lambda i, j: (j, 0))],
        out_specs=pl.BlockSpec((tm, d), lambda i, j: (i, 0)),
        out_shape=jax.ShapeDtypeStruct((n, d), F32),
        scratch_shapes=[pltpu.VMEM((tm, d), F32)],
        compiler_params=_cparams(("parallel", "arbitrary")),
        name="ffn_dense",
    )(h_bf16, x2, w1, w3, w2)


def _router_kernel(x_ref, nw_ref, wr_ref, comb_ref):
    h = _rms(x_ref[...], nw_ref[...])
    logits = jnp.dot(h, wr_ref[...], preferred_element_type=F32, precision=lax.Precision.HIGHEST)
    lane = lax.broadcasted_iota(jnp.int32, logits.shape, 1)
    logits = jnp.where(lane < N_EXPERTS, logits, -jnp.inf)
    m1 = jnp.max(logits, axis=-1, keepdims=True)
    i1 = jnp.min(jnp.where(logits == m1, lane, 128), axis=-1, keepdims=True)
    rest = jnp.where(lane == i1, -jnp.inf, logits)
    m2 = jnp.max(rest, axis=-1, keepdims=True)
    i2 = jnp.min(jnp.where(rest == m2, lane, 128), axis=-1, keepdims=True)
    e2 = jnp.exp(m2 - m1)
    g1 = 1.0 / (1.0 + e2)
    g2 = e2 * g1
    comb_ref[...] = jnp.where(lane == i1, g1, 0.0) + jnp.where(lane == i2, g2, 0.0)


def moe_route(x2, norm_w, w_router_pad, tm=512):
    n, d = x2.shape
    return pl.pallas_call(
        _router_kernel,
        grid=(n // tm,),
        in_specs=[pl.BlockSpec((tm, d), lambda i: (i, 0)),
                  pl.BlockSpec((1, d), lambda i: (0, 0)),
                  pl.BlockSpec((d, 128), lambda i: (0, 0))],
        out_specs=pl.BlockSpec((tm, 128), lambda i: (i, 0)),
        out_shape=jax.ShapeDtypeStruct((n, 128), F32),
        compiler_params=_cparams(("parallel",)),
        name="moe_router",
    )(x2, norm_w.reshape(1, d), w_router_pad)


def _moe_dense_kernel(h_ref, x_ref, comb_ref, w1_ref, w3_ref, w2_ref, o_ref, acc_ref):
    e = pl.program_id(1)
    j = pl.program_id(2)

    @pl.when((e == 0) & (j == 0))
    def _():
        acc_ref[...] = x_ref[...]

    h = h_ref[...]
    a = _dot(h, w1_ref[...])
    b = _dot(h, w3_ref[...])
    lane = lax.broadcasted_iota(jnp.int32, comb_ref.shape, 1)
    c = jnp.sum(jnp.where(lane == e, comb_ref[...], 0.0), axis=-1, keepdims=True)
    act = (a * jax.nn.sigmoid(a) * b * c).astype(BF16)
    acc_ref[...] += _dot(act, w2_ref[...])

    @pl.when((e == pl.num_programs(1) - 1) & (j == pl.num_programs(2) - 1))
    def _():
        o_ref[...] = acc_ref[...]


def moe_dense(h_bf16, x2, comb, w1, w3, w2, tm=512, tf=1792):
    n, d = x2.shape
    ne, _, f = w1.shape
    return pl.pallas_call(
        _moe_dense_kernel,
        grid=(n // tm, ne, f // tf),
        in_specs=[pl.BlockSpec((tm, d), lambda i, e, j: (i, 0)),
                  pl.BlockSpec((tm, d), lambda i, e, j: (i, 0)),
                  pl.BlockSpec((tm, 128), lambda i, e, j: (i, 0)),
                  pl.BlockSpec((None, d, tf), lambda i, e, j: (e, 0, j)),
                  pl.BlockSpec((None, d, tf), lambda i, e, j: (e, 0, j)),
                  pl.BlockSpec((None, tf, d), lambda i, e, j: (e, j, 0))],
        out_specs=pl.BlockSpec((tm, d), lambda i, e, j: (i, 0)),
        out_shape=jax.ShapeDtypeStruct((n, d), F32),
        scratch_shapes=[pltpu.VMEM((tm, d), F32)],
        compiler_params=_cparams(("parallel", "arbitrary", "arbitrary")),
        name="moe_dense",
    )(h_bf16, x2, comb, w1, w3, w2)


def _final_norm_kernel(x_ref, nw_ref, o_ref):
    o_ref[...] = _rms(x_ref[...], nw_ref[...])


def final_rms(x2, norm_w, tm=1024):
    n, d = x2.shape
    return pl.pallas_call(
        _final_norm_kernel,
        grid=(n // tm,),
        in_specs=[pl.BlockSpec((tm, d), lambda i: (i, 0)), pl.BlockSpec((1, d), lambda i: (0, 0))],
        out_specs=pl.BlockSpec((tm, d), lambda i: (i, 0)),
        out_shape=jax.ShapeDtypeStruct((n, d), F32),
        compiler_params=_cparams(("parallel",)),
        name="final_norm",
    )(x2, norm_w.reshape(1, d))


def _mla_prep_kernel(cq_ref, ckv_ref, kra_ref, krb_ref, cm_ref, sm_ref, qn_ref, kvn_ref,
                     wq1_ref, wq2_ref, wk_ref, wv_ref, q_ref, k_ref, v_ref):
    qn = _rms(cq_ref[...], qn_ref[...]).astype(BF16)
    kvn = _rms(ckv_ref[...], kvn_ref[...]).astype(BF16)
    cm = cm_ref[...]
    sm = sm_ref[...]
    cm4 = jnp.concatenate([cm] * N_HEADS, axis=1)
    sm4 = jnp.concatenate([sm] * N_HEADS, axis=1)
    scale = (MLA_NOPE + MLA_ROPE) ** -0.5
    q = (_dot(qn, wq1_ref[...]) * cm4 + _dot(qn, wq2_ref[...]) * sm4) * scale
    q_ref[...] = q.astype(BF16)
    kr = kra_ref[...] * cm + krb_ref[...] * sm
    k = _dot(kvn, wk_ref[...]) + jnp.concatenate([kr] * N_HEADS, axis=1)
    k_ref[...] = k.astype(BF16)
    v_ref[...] = _dot(kvn, wv_ref[...]).astype(BF16)


def mla_prep(cols, cm, sm, q_norm, kv_norm, wq1, wq2, wk, wv, tm=512):
    n = cols.shape[0]
    b256 = C_MLA // 256
    b128 = C_MLA // 128
    full = lambda shape: pl.BlockSpec(shape, lambda i: (0, 0))
    return pl.pallas_call(
        _mla_prep_kernel,
        grid=(n // tm,),
        in_specs=[pl.BlockSpec((tm, 256), lambda i: (i, b256)),
                  pl.BlockSpec((tm, 128), lambda i: (i, b128 + 2)),
                  pl.BlockSpec((tm, 128), lambda i: (i, b128 + 3)),
                  pl.BlockSpec((tm, 128), lambda i: (i, b128 + 4)),
                  pl.BlockSpec((tm, 128), lambda i: (i, 0)),
                  pl.BlockSpec((tm, 128), lambda i: (i, 0)),
                  full((1, 256)), full((1, 128)),
                  full((256, 512)), full((256, 512)), full((128, 512)), full((128, 256))],
        out_specs=[pl.BlockSpec((tm, 512), lambda i: (i, 0)), pl.BlockSpec((tm, 512), lambda i: (i, 0)),
                   pl.BlockSpec((tm, 256), lambda i: (i, 0))],
        out_shape=[jax.ShapeDtypeStruct((n, 512), BF16), jax.ShapeDtypeStruct((n, 512), BF16),
                   jax.ShapeDtypeStruct((n, 256), BF16)],
        compiler_params=_cparams(("parallel",)),
        name="mla_prep",
    )(cols, cols, cols, cols, cm, sm, q_norm.reshape(1, 256), kv_norm.reshape(1, 128), wq1, wq2, wk, wv)


def _mla_flash_kernel(q_ref, k_ref, v_ref, o_ref, *, tq, tk):
    i = pl.program_id(2)
    lane = lax.broadcasted_iota(jnp.int32, (tq, 128), 1)
    qpos = i * tq + lax.broadcasted_iota(jnp.int32, (tq, tk), 0)
    kloc = lax.broadcasted_iota(jnp.int32, (tq, tk), 1)
    nkv = (i * tq + tq + tk - 1) // tk
    outs = []
    for h in range(2):
        q = q_ref[:, h * 128:(h + 1) * 128]

        def body(j, carry, q=q, h=h):
            m, l, acc = carry
            ks = pl.multiple_of(j * tk, tk)
            k = k_ref[pl.ds(ks, tk), h * 128:(h + 1) * 128]
            s = _dot_nt(q, k)
            s = jnp.where(ks + kloc <= qpos, s, NEG_INF)
            mn = jnp.maximum(m, jnp.max(s, axis=-1, keepdims=True))
            a = jnp.exp(m - mn)
            p = jnp.exp(s - mn)
            l = l * a + jnp.sum(p, axis=-1, keepdims=True)
            acc = acc * a + _dot(p.astype(BF16), v_ref[pl.ds(ks, tk), :])
            return mn, l, acc

        m0 = jnp.full((tq, 1), NEG_INF, F32)
        l0 = jnp.zeros((tq, 1), F32)
        a0 = jnp.zeros((tq, 128), F32)
        m, l, acc = lax.fori_loop(0, nkv, body, (m0, l0, a0))
        outs.append(acc / l)
    o_ref[...] = jnp.where(lane < HEAD_DIM, outs[0], outs[1])


def mla_flash(q3, k3, v3, tq=128, tk=256):
    b, t, _ = q3.shape
    return pl.pallas_call(
        functools.partial(_mla_flash_kernel, tq=tq, tk=tk),
        grid=(b, 2, t // tq),
        in_specs=[pl.BlockSpec((None, tq, 256), lambda b_, p, i: (b_, i, p)),
                  pl.BlockSpec((None, t, 256), lambda b_, p, i: (b_, 0, p)),
                  pl.BlockSpec((None, t, 128), lambda b_, p, i: (b_, 0, p))],
        out_specs=pl.BlockSpec((None, tq, 128), lambda b_, p, i: (b_, i, p)),
        out_shape=jax.ShapeDtypeStruct((b, t, 256), F32),
        compiler_params=_cparams(("parallel", "parallel", "arbitrary")),
        name="mla_flash",
    )(q3, k3, v3)


DILATED_PATTERNS = ((128, 1), (512, 4), (2048, 16))
DIL_BLOCK = 128


def _dil_prep_kernel(q_ref, k_ref, qr_ref, kr_ref, cd_ref, sd_ref, qo_ref, ko_ref):
    cd = cd_ref[...]
    sd = sd_ref[...]
    qo_ref[...] = (q_ref[...] * cd + qr_ref[...] * sd) * (HEAD_DIM ** -0.5)
    ko_ref[...] = k_ref[...] * cd + kr_ref[...] * sd


def dil_prep(cols, cd, sd, tm=1024):
    n = cols.shape[0]
    b0 = C_DIL // 256
    spec = lambda j: pl.BlockSpec((tm, 256), lambda i: (i, j))
    return pl.pallas_call(
        _dil_prep_kernel,
        grid=(n // tm,),
        in_specs=[spec(b0), spec(b0 + 1), spec(b0 + 3), spec(b0 + 4), spec(0), spec(0)],
        out_specs=[spec(0), spec(0)],
        out_shape=[jax.ShapeDtypeStruct((n, 256), F32), jax.ShapeDtypeStruct((n, 256), F32)],
        compiler_params=_cparams(("parallel",)),
        name="dil_prep",
    )(cols, cols, cols, cols, cd, sd)


def _dil_block(qb, kb, vb, mask):
    nq = qb.shape[0]
    lane = lax.broadcasted_iota(jnp.int32, (nq, 128), 1)
    kbb = kb.astype(BF16)
    vbb = vb.astype(BF16)
    o = jnp.zeros((nq, 128), F32)
    lse = jnp.zeros((nq, 128), F32)
    for h in range(2):
        hm = (lane >> 6) == h
        s = _dot_nt(jnp.where(hm, qb, 0.0).astype(BF16), kbb)
        s = jnp.where(mask, s, NEG_INF)
        m = jnp.max(s, axis=-1, keepdims=True)
        p = jnp.exp(s - m)
        l = jnp.sum(p, axis=-1, keepdims=True)
        oh = _dot(p.astype(BF16), vbb) / l
        o = jnp.where(hm, oh, o)
        lse = jnp.where(hm, m + jnp.log(l), lse)
    return o, lse


def _dil_kernel(q_ref, k_ref, v_ref, o_ref, m_ref, z_ref, acc_ref, *, t):
    blk = DIL_BLOCK
    for pi, (window, dil) in enumerate(DILATED_PATTERNS):
        assert window // dil == blk
        length = t // dil
        nb = length // blk
        nk = 2 * blk if nb > 1 else blk
        uq = lax.broadcasted_iota(jnp.int32, (blk, nk), 0)
        uk = lax.broadcasted_iota(jnp.int32, (blk, nk), 1)

        def body(it, carry, dil=dil, nb=nb, nk=nk, uq=uq, uk=uk, first=(pi == 0)):
            r = it // nb
            jb = it % nb
            kb0 = jnp.maximum(jb - 1, 0) * blk
            qs = r + dil * blk * jb
            ks = r + dil * kb0
            if dil == 1:
                qidx = pl.ds(pl.multiple_of(qs, blk), blk)
                kidx = pl.ds(pl.multiple_of(ks, blk), nk)
            else:
                qidx = pl.ds(qs, blk, stride=dil)
                kidx = pl.ds(ks, nk, stride=dil)
            dist = (jb * blk + uq) - (kb0 + uk)
            mask = (dist >= 0) & (dist <= blk)
            o, lse = _dil_block(q_ref[qidx, :], k_ref[kidx, :], v_ref[kidx, :], mask)
            if first:
                m_ref[qidx, :] = lse
                z_ref[qidx, :] = jnp.ones_like(lse)
                acc_ref[qidx, :] = o
            else:
                m_old = m_ref[qidx, :]
                m_new = jnp.maximum(m_old, lse)
                a = jnp.exp(m_old - m_new)
                b = jnp.exp(lse - m_new)
                m_ref[qidx, :] = m_new
                z_ref[qidx, :] = z_ref[qidx, :] * a + b
                acc_ref[qidx, :] = acc_ref[qidx, :] * a + o * b
            return carry

        lax.fori_loop(0, dil * nb, body, 0)
    o_ref[...] = acc_ref[...] / z_ref[...]


def dilated_attention(q3, k3, cols3):
    b, t, _ = q3.shape
    assert t % (DIL_BLOCK * DILATED_PATTERNS[-1][1]) == 0
    vblk = C_DIL // 128 + 4
    return pl.pallas_call(
        functools.partial(_dil_kernel, t=t),
        grid=(b, 2),
        in_specs=[pl.BlockSpec((None, t, 128), lambda i, p: (i, 0, p)),
                  pl.BlockSpec((None, t, 128), lambda i, p: (i, 0, p)),
                  pl.BlockSpec((None, t, 128), lambda i, p: (i, 0, vblk + p))],
        out_specs=pl.BlockSpec((None, t, 128), lambda i, p: (i, 0, p)),
        out_shape=jax.ShapeDtypeStruct((b, t, 256), F32),
        scratch_shapes=[pltpu.VMEM((t, 128), F32)] * 3,
        compiler_params=_cparams(("parallel", "parallel")),
        name="dilated_attention",
    )(q3, k3, cols3)


CHUNK = 64
STACK = N_HEADS * CHUNK


def _stack_heads(x):
    lane = lax.broadcasted_iota(jnp.int32, x.shape, 1)
    return jnp.concatenate([jnp.where((lane >> 6) == h, x, 0.0) for h in range(N_HEADS)], axis=0)


def _unstack_heads(y):
    out = y[0:CHUNK]
    for h in range(1, N_HEADS):
        out = out + y[h * CHUNK:(h + 1) * CHUNK]
    return out


def _head_mean_matrix():
    r = lax.broadcasted_iota(jnp.int32, (256, 256), 0)
    c = lax.broadcasted_iota(jnp.int32, (256, 256), 1)
    return jnp.where((r >> 6) == (c >> 6), 1.0 / HEAD_DIM, 0.0).astype(BF16)


HGRN_HALVES = (32, 16, 8, 4, 2, 1)


def _hgrn_sum_matrix():
    t = lax.broadcasted_iota(jnp.int32, (CHUNK, CHUNK), 0)
    j = lax.broadcasted_iota(jnp.int32, (CHUNK, CHUNK), 1)
    tri = jnp.where(j <= t, 1.0, 0.0)
    blocks = [tri]
    for half in HGRN_HALVES:
        ref = (t // (2 * half)) * (2 * half) + half - 1
        blocks.append(tri - jnp.where(j <= ref, 1.0, 0.0))
    return jnp.concatenate(blocks, axis=0).astype(BF16)


def _hgrn_kernel(q_ref, f_ref, i_ref, g_ref, lbl_ref, gn_ref, o_ref, st_ref, *, layer, nchunk):
    @pl.when(pl.program_id(1) == 0)
    def _():
        st_ref[...] = jnp.zeros_like(st_ref)

    logits = lbl_ref[...]
    e = jnp.exp(logits - jnp.max(logits, axis=0, keepdims=True))
    sm = e / jnp.sum(e, axis=0, keepdims=True)
    lb = jnp.zeros((1, 256), F32)
    for l in range(1, layer + 1):
        lb = lb + sm[l:l + 1, :]
    gn = gn_ref[...]
    summat = _hgrn_sum_matrix()
    hmean = _head_mean_matrix()
    r = lax.broadcasted_iota(jnp.int32, (STACK, STACK), 0)
    c = lax.broadcasted_iota(jnp.int32, (STACK, STACK), 1)
    same_head = (r >> 6) == (c >> 6)
    rc = r & (CHUNK - 1)
    cc = c & (CHUNK - 1)

    def chunk(ci, carry):
        rows = pl.ds(pl.multiple_of(ci * CHUNK, CHUNK), CHUNK)
        qv = q_ref[rows, :]
        qq = qv * jax.nn.sigmoid(qv)
        forget = lb + (1.0 - lb) * jax.nn.sigmoid(f_ref[rows, :])
        logf = jnp.log(forget)
        kk = 1.0 - forget
        v = i_ref[rows, :]
        sums = _split_dot_lhs(summat, logf)
        b = sums[0:CHUNK]
        a = jnp.where(rc == cc, _dot_nt(_stack_heads(qq).astype(BF16), _stack_heads(kk).astype(BF16)), 0.0)
        for li, half in enumerate(HGRN_HALVES):
            d = sums[(li + 1) * CHUNK:(li + 2) * CHUNK]
            ql = qq * jnp.exp(jnp.minimum(d, 0.0))
            kl = kk * jnp.exp(jnp.minimum(-d, 0.0))
            m = ((rc // (2 * half)) == (cc // (2 * half))) & ((rc % (2 * half)) >= half) & ((cc % (2 * half)) < half)
            a = a + jnp.where(m, _dot_nt(_stack_heads(ql).astype(BF16), _stack_heads(kl).astype(BF16)), 0.0)
        o = _unstack_heads(_dot(a.astype(BF16), _stack_heads(v).astype(BF16)))
        st = st_ref[...]
        o = o + _dot_nt((qq * jnp.exp(b)).astype(BF16), st.astype(BF16))
        b_end = b[CHUNK - 1:CHUNK, :]
        kbar = kk * jnp.exp(b_end - b)
        upd = lax.dot_general(v.astype(BF16), kbar.astype(BF16), (((0,), (0,)), ((), ())), preferred_element_type=F32)
        st_ref[...] = st * jnp.exp(b_end) + jnp.where(same_head, upd, 0.0)
        ms = _split_dot(o * o, hmean)
        gv = g_ref[rows, :]
        o_ref[rows, :] = o * lax.rsqrt(ms + RMS_EPS) * gn * (gv * jax.nn.sigmoid(gv))
        return carry

    lax.fori_loop(0, nchunk, chunk, 0)


def hgrn(cols3, lb_logits, g_norm, layer, tb=512):
    b, t, _ = cols3.shape
    c0 = C_HGRN // 256
    spec = lambda j: pl.BlockSpec((None, tb, 256), lambda b_, i: (b_, i, j))
    depth = lb_logits.shape[0]
    return pl.pallas_call(
        functools.partial(_hgrn_kernel, layer=layer, nchunk=tb // CHUNK),
        grid=(b, t // tb),
        in_specs=[spec(c0), spec(c0 + 1), spec(c0 + 2), spec(c0 + 3),
                  pl.BlockSpec((depth, 256), lambda b_, i: (0, 0)),
                  pl.BlockSpec((1, 256), lambda b_, i: (0, 0))],
        out_specs=spec(0),
        out_shape=jax.ShapeDtypeStruct((b, t, 256), F32),
        scratch_shapes=[pltpu.VMEM((STACK, STACK), F32)],
        compiler_params=_cparams(("parallel", "arbitrary")),
        name="hgrn",
    )(cols3, cols3, cols3, cols3, lb_logits, g_norm.reshape(1, 256))


def _head_sum_matrix():
    r = lax.broadcasted_iota(jnp.int32, (256, 256), 0)
    c = lax.broadcasted_iota(jnp.int32, (256, 256), 1)
    return jnp.where((r >> 6) == (c >> 6), 1.0, 0.0).astype(BF16)


def _rwkv_prep_kernel(c_ref, p_ref, mu_ref, w0_ref, w2_ref, a0_ref, a2_ref, g2_ref, kk_ref, ka_ref, rk_ref,
                      r_o, lw_o, k_o, v_o, kap_o, at_o, bonus_o, g_o):
    i = pl.program_id(1)
    a = c_ref[...]
    tm = a.shape[0]
    last_prev = jnp.where(i > 0, p_ref[7:8, :], 0.0)
    row = lax.broadcasted_iota(jnp.int32, a.shape, 0)
    prev = jnp.where(row == 0, last_prev, pltpu.roll(a, 1, 0))
    xs = a + (prev - a) * mu_ref[...]
    r = xs[:, 0:256]
    k = xs[:, 256:512]
    v = xs[:, 512:768]
    xwa = xs[:, 768:896]
    xg = xs[:, 896:1024]
    w_in = w0_ref[...] + _dot(jnp.tanh(xwa).astype(BF16), w2_ref[...])
    z = -w_in
    softplus = jnp.maximum(z, 0.0) + jnp.log(1.0 + jnp.exp(-jnp.abs(z)))
    lw_o[...] = -jnp.exp(-softplus - 0.5)
    lr = jax.nn.sigmoid(a0_ref[...] + _dot(xwa.astype(BF16), a2_ref[...]))
    g_o[...] = _dot(jax.nn.sigmoid(xg).astype(BF16), g2_ref[...])
    hsum = _head_sum_matrix()
    kk = k * kk_ref[...]
    norm = jnp.sqrt(_split_dot(kk * kk, hsum))
    kk = kk / jnp.maximum(norm, 1e-12)
    k2 = k * (1.0 + (lr - 1.0) * ka_ref[...])
    r_o[...] = r
    k_o[...] = k2
    v_o[...] = v
    kap_o[...] = kk
    at_o[...] = kk * lr
    bonus_o[...] = _split_dot(r * k2 * rk_ref[...], hsum) * v


def rwkv_prep(cols3, mu, w0, w2p, a0, a2p, g2, k_k, k_a, r_k, tm=512):
    b, t, _ = cols3.shape
    full = lambda shape: pl.BlockSpec(shape, lambda b_, i: (0, 0))
    ospec = pl.BlockSpec((None, tm, 256), lambda b_, i: (b_, i, 0))
    row = lambda x: x.reshape(1, -1)
    return pl.pallas_call(
        _rwkv_prep_kernel,
        grid=(b, t // tm),
        in_specs=[pl.BlockSpec((None, tm, 1024), lambda b_, i: (b_, i, 0)),
                  pl.BlockSpec((None, 8, 1024), lambda b_, i: (b_, jnp.maximum(i * (tm // 8) - 1, 0), 0)),
                  full((1, 1024)), full((1, 256)), full((128, 256)), full((1, 256)), full((128, 256)),
                  full((128, 256)), full((1, 256)), full((1, 256)), full((1, 256))],
        out_specs=[ospec] * 8,
        out_shape=[jax.ShapeDtypeStruct((b, t, 256), F32)] * 8,
        compiler_params=_cparams(("parallel", "parallel")),
        name="rwkv_prep",
    )(cols3, cols3, row(mu), row(w0), w2p, row(a0), a2p, g2, row(k_k), row(k_a), row(r_k))


RWKV_MERGE_HALVES = (2, 4, 8, 16, 32)


def _rwkv_kernel(r_ref, lw_ref, k_ref, v_ref, kap_ref, at_ref, bonus_ref, g_ref, lnw_ref, lnb_ref,
                 o_ref, ht_ref, *, nchunk):
    @pl.when(pl.program_id(1) == 0)
    def _():
        ht_ref[...] = jnp.zeros_like(ht_ref)

    t = lax.broadcasted_iota(jnp.int32, (CHUNK, CHUNK), 0)
    j = lax.broadcasted_iota(jnp.int32, (CHUNK, CHUNK), 1)
    tri = jnp.where(j <= t, 1.0, 0.0).astype(BF16)
    hmean = _head_mean_matrix()
    rr = lax.broadcasted_iota(jnp.int32, (STACK, STACK), 0)
    cc = lax.broadcasted_iota(jnp.int32, (STACK, STACK), 1)
    same_head = (rr >> 6) == (cc >> 6)
    rc = rr & (CHUNK - 1)
    sc = cc & (CHUNK - 1)
    eye = jnp.where(rr == cc, 1.0, 0.0)
    lnw = lnw_ref[...]
    lnb = lnb_ref[...]

    def chunk(ci, carry):
        rows = pl.ds(pl.multiple_of(ci * CHUNK, CHUNK), CHUNK)
        lw = lw_ref[rows, :]
        lc = _split_dot_lhs(tri, lw)
        l_end = lc[CHUNK - 1:CHUNK, :]
        kap = kap_ref[rows, :]
        at = at_ref[rows, :]
        k2 = k_ref[rows, :]
        v = v_ref[rows, :]
        inv = jnp.exp(-lc)
        fwd = jnp.exp(l_end - lc)
        s_kap = _stack_heads(kap * jnp.exp(lc - lw)).astype(BF16)
        s_r32 = _stack_heads(r_ref[rows, :] * jnp.exp(lc))
        s_r = s_r32.astype(BF16)
        s_a = _stack_heads(at * inv).astype(BF16)
        s_k = _stack_heads(k2 * inv).astype(BF16)
        s_v = _stack_heads(v).astype(BF16)
        strict = rc > sc
        incl = rc >= sc
        a_ka = jnp.where(strict, _dot_nt(s_kap, s_a), 0.0)
        a_kk = jnp.where(strict, _dot_nt(s_kap, s_k), 0.0).astype(BF16)
        a_ra = jnp.where(incl, _dot_nt(s_r, s_a), 0.0).astype(BF16)
        a_rk = jnp.where(incl, _dot_nt(s_r, s_k), 0.0).astype(BF16)
        m1 = ((rc >> 1) == (sc >> 1)) & strict
        tinv = eye - jnp.where(m1, a_ka, 0.0)
        for half in RWKV_MERGE_HALVES:
            m = ((rc // (2 * half)) == (sc // (2 * half))) & ((rc % (2 * half)) >= half) & ((sc % (2 * half)) < half)
            off = jnp.where(m, a_ka, 0.0).astype(BF16)
            tb = tinv.astype(BF16)
            tinv = tinv - _dot(tb, _dot(off, tb).astype(BF16))
        tb = tinv.astype(BF16)
        kt_st = _dot(tb, s_kap)
        vt_st = _dot(tb, _dot(a_kk, s_v).astype(BF16))
        rt = _unstack_heads(s_r32 - _dot(a_ra, kt_st.astype(BF16)))
        y0 = _unstack_heads(_dot(a_rk, s_v) - _dot(a_ra, vt_st.astype(BF16)))
        kt = _unstack_heads(kt_st)
        vt = _unstack_heads(vt_st)
        ht = ht_ref[...]
        htb = ht.astype(BF16)
        y = y0 + _dot_nt(rt.astype(BF16), htb)
        u = vt + _dot_nt(kt.astype(BF16), htb)
        lhs = jnp.concatenate([v, u], axis=0).astype(BF16)
        rhs = jnp.concatenate([k2 * fwd, -(at * fwd)], axis=0).astype(BF16)
        upd = lax.dot_general(lhs, rhs, (((0,), (0,)), ((), ())), preferred_element_type=F32)
        ht_ref[...] = ht * jnp.exp(l_end) + jnp.where(same_head, upd, 0.0)
        mean = _split_dot(y, hmean)
        yc = y - mean
        var = _split_dot(yc * yc, hmean)
        yn = yc * lax.rsqrt(var + RWKV_LN_EPS) * lnw + lnb
        o_ref[rows, :] = (yn + bonus_ref[rows, :]) * g_ref[rows, :]
        return carry

    lax.fori_loop(0, nchunk, chunk, 0)


def rwkv_mix(prep, ln_w, ln_b, tb=512):
    b, t, _ = prep[0].shape
    spec = pl.BlockSpec((None, tb, 256), lambda b_, i: (b_, i, 0))
    full = pl.BlockSpec((1, 256), lambda b_, i: (0, 0))
    return pl.pallas_call(
        functools.partial(_rwkv_kernel, nchunk=tb // CHUNK),
        grid=(b, t // tb),
        in_specs=[spec] * 8 + [full, full],
        out_specs=spec,
        out_shape=jax.ShapeDtypeStruct((b, t, 256), F32),
        scratch_shapes=[pltpu.VMEM((STACK, STACK), F32)],
        compiler_params=_cparams(("parallel", "arbitrary")),
        name="rwkv_mix",
    )(*prep, ln_w.reshape(1, 256), ln_b.reshape(1, 256))


def _rot_cols(w, half):
    g = w.reshape(w.shape[0], -1, 2, half)
    return jnp.concatenate([-g[:, :, 1:2, :], g[:, :, 0:1, :]], axis=2).reshape(w.shape)


def pack_w_in(w):
    d = w.shape[0]
    rw = w[:, 0:1024]
    dq, dk, dv = w[:, 1024:1280], w[:, 1280:1536], w[:, 1536:1792]
    cq, ckv, kr = w[:, 1792:2048], w[:, 2048:2176], w[:, 2176:2208]
    hg = w[:, 2208:3232]
    z = lambda n: jnp.zeros((d, n), w.dtype)
    kra = jnp.concatenate([z(MLA_NOPE), kr, z(128 - MLA_NOPE - MLA_ROPE)], axis=1)
    krb = jnp.concatenate([z(MLA_NOPE), _rot_cols(kr, MLA_ROPE // 2), z(128 - MLA_NOPE - MLA_ROPE)], axis=1)
    packed = jnp.concatenate([rw, dq, dk, dv, _rot_cols(dq, HEAD_DIM // 2), _rot_cols(dk, HEAD_DIM // 2),
                              hg, cq, ckv, kra, krb], axis=1)
    return packed.astype(BF16)


def pack_mla(w_uq, w_ukv):
    rq = w_uq.shape[0]
    q = w_uq.reshape(rq, N_HEADS, MLA_NOPE + MLA_ROPE)
    nope, rope = q[..., :MLA_NOPE], q[..., MLA_NOPE:]
    pad = jnp.zeros((rq, N_HEADS, 128 - MLA_NOPE - MLA_ROPE), w_uq.dtype)
    wq1 = jnp.concatenate([nope, rope, pad], axis=-1).reshape(rq, N_HEADS * 128)
    rrot = _rot_cols(rope.reshape(rq, N_HEADS * MLA_ROPE), MLA_ROPE // 2).reshape(rq, N_HEADS, MLA_ROPE)
    wq2 = jnp.concatenate([jnp.zeros_like(nope), rrot, pad], axis=-1).reshape(rq, N_HEADS * 128)
    rk = w_ukv.shape[0]
    kv = w_ukv.reshape(rk, N_HEADS, MLA_NOPE + HEAD_DIM)
    wk = jnp.concatenate([kv[..., :MLA_NOPE], jnp.zeros((rk, N_HEADS, 128 - MLA_NOPE), w_ukv.dtype)],
                         axis=-1).reshape(rk, N_HEADS * 128)
    wv = kv[..., MLA_NOPE:].reshape(rk, N_HEADS * HEAD_DIM)
    return wq1.astype(BF16), wq2.astype(BF16), wk.astype(BF16), wv.astype(BF16)


def kernel(x, positions, mix_norm, w_in, w_out, rwkv_mu, rwkv_w0, rwkv_w2, rwkv_a0, rwkv_a2, rwkv_g2, rwkv_k_k,
           rwkv_k_a, rwkv_r_k, rwkv_ln_w, rwkv_ln_b, mla_q_norm, mla_w_uq, mla_kv_norm, mla_w_ukv, hgrn_lb_logits,
           hgrn_g_norm, ffn_norm, ffn_w1, ffn_w3, ffn_w2, moe_router, moe_w1, moe_w3, moe_w2, final_norm):
    b, t, d = x.shape
    n = b * t
    depth = w_in.shape[0]
    x2 = x.reshape(n, d)
    cd, sd, cm, sm = rope_tables(positions.reshape(n, 1))
    lora_pad = jnp.zeros((rwkv_w2.shape[1], D_GROUP), F32)
    for layer in range(depth):
        cols = in_proj(x2, mix_norm[layer], pack_w_in(w_in[layer]))
        cols3 = cols.reshape(b, t, N_PACK)
        prep = rwkv_prep(cols3, rwkv_mu[layer], rwkv_w0[layer],
                         jnp.concatenate([rwkv_w2[layer], lora_pad], axis=0).astype(BF16), rwkv_a0[layer],
                         jnp.concatenate([lora_pad, rwkv_a2[layer]], axis=0).astype(BF16),
                         rwkv_g2[layer].astype(BF16), rwkv_k_k[layer], rwkv_k_a[layer], rwkv_r_k[layer])
        y_a = rwkv_mix(prep, rwkv_ln_w[layer], rwkv_ln_b[layer])
        q_d, k_d = dil_prep(cols, cd, sd)
        y_b = dilated_attention(q_d.reshape(b, t, D_GROUP), k_d.reshape(b, t, D_GROUP), cols3)
        q_m, k_m, v_m = mla_prep(cols, cm, sm, mla_q_norm[layer], mla_kv_norm[layer],
                                 *pack_mla(mla_w_uq[layer], mla_w_ukv[layer]))
        y_c = mla_flash(q_m.reshape(b, t, -1), k_m.reshape(b, t, -1), v_m.reshape(b, t, -1))
        y_d = hgrn(cols3, hgrn_lb_logits, hgrn_g_norm[layer], layer)
        ys = [y.reshape(n, D_GROUP) for y in (y_a, y_b, y_c, y_d)]
        x2, h = out_proj(x2, ys, w_out[layer].astype(BF16), ffn_norm[layer])
        j = layer // 2
        if layer % 2 == 0:
            x2 = ffn_dense(h, x2, ffn_w1[j].astype(BF16), ffn_w3[j].astype(BF16), ffn_w2[j].astype(BF16))
        else:
            comb = moe_route(x2, ffn_norm[layer], jnp.pad(moe_router[j], ((0, 0), (0, 128 - N_EXPERTS))))
            x2 = moe_dense(h, x2, comb, moe_w1[j].astype(BF16), moe_w3[j].astype(BF16), moe_w2[j].astype(BF16))
    return final_rms(x2, final_norm).reshape(b, t, d)
```

```python
import functools

import jax
import jax.numpy as jnp
from jax import lax
from jax.experimental import pallas as pl
from jax.experimental.pallas import tpu as pltpu

F32 = jnp.float32
BF16 = jnp.bfloat16

D_GROUP = 256
HEAD_DIM = 64
N_HEADS = 4
RMS_EPS = 1e-6
RWKV_LN_EPS = 64e-5
ROPE_THETA = 10000.0
NEG_INF = -1e30
N_EXPERTS = 8
MLA_ROPE = 32
MLA_NOPE = 64

VMEM_LIMIT = 56 * 1024 * 1024

C_RWKV = 0
C_DIL = 1024
C_HGRN = 2304
C_MLA = 3328
N_PACK = 3968


def _cparams(sem, vmem=VMEM_LIMIT):
    return pltpu.CompilerParams(dimension_semantics=sem, vmem_limit_bytes=vmem)


def _rms(x, w):
    return x * lax.rsqrt(jnp.mean(x * x, axis=-1, keepdims=True) + RMS_EPS) * w


def _dot(a, b):
    return jnp.dot(a, b, preferred_element_type=F32)


def _dot_nt(a, b):
    return lax.dot_general(a, b, (((1,), (1,)), ((), ())), preferred_element_type=F32)


def _split_dot(a_f32, b_bf16):
    hi = a_f32.astype(BF16)
    lo = (a_f32 - hi.astype(F32)).astype(BF16)
    return _dot(hi, b_bf16) + _dot(lo, b_bf16)


def _split_dot_lhs(a_bf16, b_f32):
    hi = b_f32.astype(BF16)
    lo = (b_f32 - hi.astype(F32)).astype(BF16)
    return _dot(a_bf16, hi) + _dot(a_bf16, lo)


def _rope_kernel(pos_ref, cd_ref, sd_ref, cm_ref, sm_ref):
    pos = pos_ref[...].astype(F32)
    tm = pos.shape[0]
    lane = lax.broadcasted_iota(jnp.int32, (tm, 256), 1)
    j = (lane & 31).astype(F32)
    inv = jnp.exp(j * (-2.0 / HEAD_DIM * jnp.log(ROPE_THETA)))
    ang = pos * inv
    cd_ref[...] = jnp.cos(ang)
    sd_ref[...] = jnp.sin(ang)
    lane = lax.broadcasted_iota(jnp.int32, (tm, 128), 1)
    jm = (lane & 15).astype(F32)
    invm = jnp.exp(jm * (-2.0 / MLA_ROPE * jnp.log(ROPE_THETA)))
    angm = pos * invm
    rope_lane = (lane >= MLA_NOPE) & (lane < MLA_NOPE + MLA_ROPE)
    cm_ref[...] = jnp.where(rope_lane, jnp.cos(angm), jnp.where(lane < MLA_NOPE, 1.0, 0.0))
    sm_ref[...] = jnp.where(rope_lane, jnp.sin(angm), 0.0)


def rope_tables(pos_col, tm=512):
    n = pos_col.shape[0]
    return pl.pallas_call(
        _rope_kernel,
        grid=(n // tm,),
        in_specs=[pl.BlockSpec((tm, 1), lambda i: (i, 0))],
        out_specs=[pl.BlockSpec((tm, 256), lambda i: (i, 0)), pl.BlockSpec((tm, 256), lambda i: (i, 0)),
                   pl.BlockSpec((tm, 128), lambda i: (i, 0)), pl.BlockSpec((tm, 128), lambda i: (i, 0))],
        out_shape=[jax.ShapeDtypeStruct((n, 256), F32), jax.ShapeDtypeStruct((n, 256), F32),
                   jax.ShapeDtypeStruct((n, 128), F32), jax.ShapeDtypeStruct((n, 128), F32)],
        compiler_params=_cparams(("parallel",)),
        name="rope_tables",
    )(pos_col)


def _in_proj_kernel(x_ref, nw_ref, w_ref, o_ref):
    h = _rms(x_ref[...], nw_ref[...]).astype(BF16)
    o_ref[...] = _dot(h, w_ref[...])


def in_proj(x2, norm_w, w_packed, tm=512):
    n, d = x2.shape
    nc = w_packed.shape[1]
    return pl.pallas_call(
        _in_proj_kernel,
        grid=(n // tm,),
        in_specs=[pl.BlockSpec((tm, d), lambda i: (i, 0)),
                  pl.BlockSpec((1, d), lambda i: (0, 0)),
                  pl.BlockSpec((d, nc), lambda i: (0, 0))],
        out_specs=pl.BlockSpec((tm, nc), lambda i: (i, 0)),
        out_shape=jax.ShapeDtypeStruct((n, nc), F32),
        compiler_params=_cparams(("parallel",)),
        name="in_proj",
    )(x2, norm_w.reshape(1, d), w_packed)


def _out_proj_kernel(x_ref, ya_ref, yb_ref, yc_ref, yd_ref, w_ref, nw_ref, xo_ref, h_ref):
    acc = x_ref[...]
    for g, y_ref in enumerate((ya_ref, yb_ref, yc_ref, yd_ref)):
        acc = acc + _dot(y_ref[...].astype(BF16), w_ref[g * D_GROUP:(g + 1) * D_GROUP, :])
    xo_ref[...] = acc
    h_ref[...] = _rms(acc, nw_ref[...]).astype(BF16)


def out_proj(x2, ys, w_out_bf16, norm_w, tm=512):
    n, d = x2.shape
    yspec = pl.BlockSpec((tm, D_GROUP), lambda i: (i, 0))
    return pl.pallas_call(
        _out_proj_kernel,
        grid=(n // tm,),
        in_specs=[pl.BlockSpec((tm, d), lambda i: (i, 0)), yspec, yspec, yspec, yspec,
                  pl.BlockSpec((d, d), lambda i: (0, 0)),
                  pl.BlockSpec((1, d), lambda i: (0, 0))],
        out_specs=[pl.BlockSpec((tm, d), lambda i: (i, 0)), pl.BlockSpec((tm, d), lambda i: (i, 0))],
        out_shape=[jax.ShapeDtypeStruct((n, d), F32), jax.ShapeDtypeStruct((n, d), BF16)],
        compiler_params=_cparams(("parallel",)),
        name="out_proj",
    )(x2, *ys, w_out_bf16, norm_w.reshape(1, d))


def _ffn_kernel(h_ref, x_ref, w1_ref, w3_ref, w2_ref, o_ref, acc_ref):
    j = pl.program_id(1)

    @pl.when(j == 0)
    def _():
        acc_ref[...] = x_ref[...]

    h = h_ref[...]
    a = _dot(h, w1_ref[...])
    b = _dot(h, w3_ref[...])
    act = (a * jax.nn.sigmoid(a) * b).astype(BF16)
    acc_ref[...] += _dot(act, w2_ref[...])

    @pl.when(j == pl.num_programs(1) - 1)
    def _():
        o_ref[...] = acc_ref[...]


def ffn_dense(h_bf16, x2, w1, w3, w2, tm=512, tf=1408):
    n, d = x2.shape
    f = w1.shape[1]
    return pl.pallas_call(
        _ffn_kernel,
        grid=(n // tm, f // tf),
        in_specs=[pl.BlockSpec((tm, d), lambda i, j: (i, 0)),
                  pl.BlockSpec((tm, d), lambda i, j: (i, 0)),
                  pl.BlockSpec((d, tf), lambda i, j: (0, j)),
                  pl.BlockSpec((d, tf), lambda i, j: (0, j)),
                  pl.BlockSpec((tf, d), lambda i, j: (j, 0))],
        out_specs=pl.BlockSpec((tm, d), lambda i, j: (i, 0)),
        out_shape=jax.ShapeDtypeStruct((n, d), F32),
        scratch_shapes=[pltpu.VMEM((tm, d), F32)],
        compiler_params=_cparams(("parallel", "arbitrary")),
        name="ffn_dense",
    )(h_bf16, x2, w1, w3, w2)


MOE_TILE = 512
TOP_K = 2


def _router_kernel(x_ref, nw_ref, wr_ref, route_ref, hp_ref):
    h = _rms(x_ref[...], nw_ref[...])
    logits = jnp.dot(h, wr_ref[...], preferred_element_type=F32, precision=lax.Precision.HIGHEST)
    lane = lax.broadcasted_iota(jnp.int32, logits.shape, 1)
    logits = jnp.where(lane < N_EXPERTS, logits, -jnp.inf)
    m1 = jnp.max(logits, axis=-1, keepdims=True)
    i1 = jnp.min(jnp.where(logits == m1, lane, 128), axis=-1, keepdims=True)
    rest = jnp.where(lane == i1, -jnp.inf, logits)
    m2 = jnp.max(rest, axis=-1, keepdims=True)
    i2 = jnp.min(jnp.where(rest == m2, lane, 128), axis=-1, keepdims=True)
    e2 = jnp.exp(m2 - m1)
    g1 = 1.0 / (1.0 + e2)
    g2 = e2 * g1
    route_ref[...] = jnp.where(lane == 0, i1.astype(F32), jnp.where(lane == 1, i2.astype(F32),
                               jnp.where(lane == 2, g1, jnp.where(lane == 3, g2, 0.0))))
    bits = pltpu.bitcast(h.astype(BF16).astype(F32), jnp.uint32)
    half = bits.shape[1] // 2
    hp_ref[...] = (bits[:, :half] >> 16) | (bits[:, half:] & jnp.uint32(0xFFFF0000))


def moe_route(x2, norm_w, w_router_pad, tm=512):
    n, d = x2.shape
    return pl.pallas_call(
        _router_kernel,
        grid=(n // tm,),
        in_specs=[pl.BlockSpec((tm, d), lambda i: (i, 0)),
                  pl.BlockSpec((1, d), lambda i: (0, 0)),
                  pl.BlockSpec((d, 128), lambda i: (0, 0))],
        out_specs=[pl.BlockSpec((tm, 128), lambda i: (i, 0)), pl.BlockSpec((tm, d // 2), lambda i: (i, 0))],
        out_shape=[jax.ShapeDtypeStruct((n, 128), F32), jax.ShapeDtypeStruct((n, d // 2), jnp.uint32)],
        compiler_params=_cparams(("parallel",)),
        name="moe_router",
    )(x2, norm_w.reshape(1, d), w_router_pad)


def moe_plan(route, n_rows_pad):
    e = route[:, :TOP_K].astype(jnp.int32)
    onehot = (e.reshape(-1, 1) == jnp.arange(N_EXPERTS, dtype=jnp.int32)[None, :]).astype(jnp.int32)
    csum = jnp.cumsum(onehot, axis=0)
    counts = csum[-1]
    rank = jnp.sum((csum - onehot) * onehot, axis=1)
    gsz = ((counts + MOE_TILE - 1) // MOE_TILE) * MOE_TILE
    gend = jnp.cumsum(gsz)
    goff = gend - gsz
    pos = (goff[e.reshape(-1)] + rank).astype(jnp.int32)
    tile_start = jnp.arange(n_rows_pad // MOE_TILE, dtype=jnp.int32) * MOE_TILE
    tile_expert = jnp.minimum(jnp.sum((tile_start[:, None] >= gend[None, :]).astype(jnp.int32), axis=1),
                              N_EXPERTS - 1).astype(jnp.int32)
    tile_valid = (tile_start < gend[-1]).astype(jnp.int32)
    return pos, tile_expert, tile_valid


def _moe_scatter_kernel(pos_ref, hp_ref, xs_in_ref, xs_ref, sem):
    del xs_in_ref
    i = pl.program_id(0)
    ts = hp_ref.shape[0]

    def copy(r, slot):
        dst = pos_ref[(i * ts + r) * TOP_K + slot]
        return pltpu.make_async_copy(hp_ref.at[pl.ds(r, 1), :], xs_ref.at[pl.ds(dst, 1), :], sem)

    def issue(r, carry):
        for slot in range(TOP_K):
            copy(r, slot).start()
        return carry

    def wait(r, carry):
        for slot in range(TOP_K):
            copy(r, slot).wait()
        return carry

    lax.fori_loop(0, ts, issue, 0, unroll=8)
    lax.fori_loop(0, ts, wait, 0, unroll=8)


def moe_scatter(pos, hp, n_rows_pad, ts=256):
    n, dh = hp.shape
    xs0 = jnp.zeros((n_rows_pad, dh), jnp.uint32)
    return pl.pallas_call(
        _moe_scatter_kernel,
        grid_spec=pltpu.PrefetchScalarGridSpec(
            num_scalar_prefetch=1,
            grid=(n // ts,),
            in_specs=[pl.BlockSpec((ts, dh), lambda i, pos_: (i, 0)),
                      pl.BlockSpec(memory_space=pl.ANY)],
            out_specs=pl.BlockSpec(memory_space=pl.ANY),
            scratch_shapes=[pltpu.SemaphoreType.DMA],
        ),
        out_shape=jax.ShapeDtypeStruct((n_rows_pad, dh), jnp.uint32),
        input_output_aliases={2: 0},
        compiler_params=_cparams(("arbitrary",)),
        name="moe_scatter",
    )(pos, hp, xs0)


def _moe_expert_kernel(te_ref, tv_ref, xs_ref, w1_ref, w3_ref, w2_ref, ys_ref, h_ref, acc_ref):
    i = pl.program_id(0)
    j = pl.program_id(1)

    @pl.when(tv_ref[i] > 0)
    def _():
        @pl.when(j == 0)
        def _():
            p = xs_ref[...]
            lo = pltpu.bitcast(p << 16, F32)
            hi = pltpu.bitcast(p & jnp.uint32(0xFFFF0000), F32)
            h_ref[...] = jnp.concatenate([lo, hi], axis=1).astype(BF16)
            acc_ref[...] = jnp.zeros_like(acc_ref)

        h = h_ref[...]
        a = _dot(h, w1_ref[...])
        b = _dot(h, w3_ref[...])
        act = (a * jax.nn.sigmoid(a) * b).astype(BF16)
        acc_ref[...] += _dot(act, w2_ref[...])

        @pl.when(j == pl.num_programs(1) - 1)
        def _():
            ys_ref[...] = acc_ref[...]

    @pl.when((tv_ref[i] == 0) & (j == pl.num_programs(1) - 1))
    def _():
        ys_ref[...] = jnp.zeros_like(ys_ref)


def moe_experts(tile_expert, tile_valid, xs, w1, w3, w2, tf=1792):
    rows, dh = xs.shape
    d = 2 * dh
    f = w1.shape[2]
    tm = MOE_TILE
    return pl.pallas_call(
        _moe_expert_kernel,
        grid_spec=pltpu.PrefetchScalarGridSpec(
            num_scalar_prefetch=2,
            grid=(rows // tm, f // tf),
            in_specs=[pl.BlockSpec((tm, dh), lambda i, j, te, tv: (i, 0)),
                      pl.BlockSpec((None, d, tf), lambda i, j, te, tv: (te[i], 0, j * tv[i])),
                      pl.BlockSpec((None, d, tf), lambda i, j, te, tv: (te[i], 0, j * tv[i])),
                      pl.BlockSpec((None, tf, d), lambda i, j, te, tv: (te[i], j * tv[i], 0))],
            out_specs=pl.BlockSpec((tm, d), lambda i, j, te, tv: (i, 0)),
            scratch_shapes=[pltpu.VMEM((tm, d), BF16), pltpu.VMEM((tm, d), F32)],
        ),
        out_shape=jax.ShapeDtypeStruct((rows, d), F32),
        compiler_params=_cparams(("arbitrary", "arbitrary")),
        name="moe_experts",
    )(tile_expert, tile_valid, xs, w1, w3, w2)


def _moe_combine_kernel(pos_ref, x_ref, route_ref, nw_ref, ys_ref, o_ref, buf_ref, sem, *, final):
    i = pl.program_id(0)
    tc = x_ref.shape[0]

    def copy(r, slot):
        src = pos_ref[(i * tc + r) * TOP_K + slot]
        return pltpu.make_async_copy(ys_ref.at[pl.ds(src, 1), :], buf_ref.at[slot, pl.ds(r, 1), :], sem)

    def issue(r, carry):
        for slot in range(TOP_K):
            copy(r, slot).start()
        return carry

    def wait(r, carry):
        for slot in range(TOP_K):
            copy(r, slot).wait()
        return carry

    lax.fori_loop(0, tc, issue, 0, unroll=8)
    lax.fori_loop(0, tc, wait, 0, unroll=8)
    route = route_ref[...]
    g1 = route[:, 2:3]
    g2 = route[:, 3:4]
    out = x_ref[...] + g1 * buf_ref[0] + g2 * buf_ref[1]
    if final:
        out = _rms(out, nw_ref[...])
    o_ref[...] = out


def moe_combine(pos, x2, route, ys, norm_w, final, tc=256):
    n, d = x2.shape
    return pl.pallas_call(
        functools.partial(_moe_combine_kernel, final=final),
        grid_spec=pltpu.PrefetchScalarGridSpec(
            num_scalar_prefetch=1,
            grid=(n // tc,),
            in_specs=[pl.BlockSpec((tc, d), lambda i, pos_: (i, 0)),
                      pl.BlockSpec((tc, 128), lambda i, pos_: (i, 0)),
                      pl.BlockSpec((1, d), lambda i, pos_: (0, 0)),
                      pl.BlockSpec(memory_space=pl.ANY)],
            out_specs=pl.BlockSpec((tc, d), lambda i, pos_: (i, 0)),
            scratch_shapes=[pltpu.VMEM((TOP_K, tc, d), F32), pltpu.SemaphoreType.DMA],
        ),
        out_shape=jax.ShapeDtypeStruct((n, d), F32),
        compiler_params=_cparams(("arbitrary",)),
        name="moe_combine",
    )(pos, x2, route, norm_w.reshape(1, d), ys)


def _final_norm_kernel(x_ref, nw_ref, o_ref):
    o_ref[...] = _rms(x_ref[...], nw_ref[...])


def final_rms(x2, norm_w, tm=1024):
    n, d = x2.shape
    return pl.pallas_call(
        _final_norm_kernel,
        grid=(n // tm,),
        in_specs=[pl.BlockSpec((tm, d), lambda i: (i, 0)), pl.BlockSpec((1, d), lambda i: (0, 0))],
        out_specs=pl.BlockSpec((tm, d), lambda i: (i, 0)),
        out_shape=jax.ShapeDtypeStruct((n, d), F32),
        compiler_params=_cparams(("parallel",)),
        name="final_norm",
    )(x2, norm_w.reshape(1, d))


def _mla_prep_kernel(cq_ref, ckv_ref, kra_ref, krb_ref, cm_ref, sm_ref, qn_ref, kvn_ref,
                     wq1_ref, wq2_ref, wk_ref, wv_ref, q_ref, k_ref, v_ref):
    qn = _rms(cq_ref[...], qn_ref[...]).astype(BF16)
    kvn = _rms(ckv_ref[...], kvn_ref[...]).astype(BF16)
    cm = cm_ref[...]
    sm = sm_ref[...]
    cm4 = jnp.concatenate([cm] * N_HEADS, axis=1)
    sm4 = jnp.concatenate([sm] * N_HEADS, axis=1)
    scale = (MLA_NOPE + MLA_ROPE) ** -0.5
    q = (_dot(qn, wq1_ref[...]) * cm4 + _dot(qn, wq2_ref[...]) * sm4) * scale
    q_ref[...] = q.astype(BF16)
    kr = kra_ref[...] * cm + krb_ref[...] * sm
    k = _dot(kvn, wk_ref[...]) + jnp.concatenate([kr] * N_HEADS, axis=1)
    k_ref[...] = k.astype(BF16)
    v_ref[...] = _dot(kvn, wv_ref[...]).astype(BF16)


def mla_prep(cols, cm, sm, q_norm, kv_norm, wq1, wq2, wk, wv, tm=512):
    n = cols.shape[0]
    b256 = C_MLA // 256
    b128 = C_MLA // 128
    full = lambda shape: pl.BlockSpec(shape, lambda i: (0, 0))
    return pl.pallas_call(
        _mla_prep_kernel,
        grid=(n // tm,),
        in_specs=[pl.BlockSpec((tm, 256), lambda i: (i, b256)),
                  pl.BlockSpec((tm, 128), lambda i: (i, b128 + 2)),
                  pl.BlockSpec((tm, 128), lambda i: (i, b128 + 3)),
                  pl.BlockSpec((tm, 128), lambda i: (i, b128 + 4)),
                  pl.BlockSpec((tm, 128), lambda i: (i, 0)),
                  pl.BlockSpec((tm, 128), lambda i: (i, 0)),
                  full((1, 256)), full((1, 128)),
                  full((256, 512)), full((256, 512)), full((128, 512)), full((128, 256))],
        out_specs=[pl.BlockSpec((tm, 512), lambda i: (i, 0)), pl.BlockSpec((tm, 512), lambda i: (i, 0)),
                   pl.BlockSpec((tm, 256), lambda i: (i, 0))],
        out_shape=[jax.ShapeDtypeStruct((n, 512), BF16), jax.ShapeDtypeStruct((n, 512), BF16),
                   jax.ShapeDtypeStruct((n, 256), BF16)],
        compiler_params=_cparams(("parallel",)),
        name="mla_prep",
    )(cols, cols, cols, cols, cm, sm, q_norm.reshape(1, 256), kv_norm.reshape(1, 128), wq1, wq2, wk, wv)


def _mla_flash_kernel(q_ref, k_ref, v_ref, o_ref, *, tq, tk):
    i = pl.program_id(2)
    lane = lax.broadcasted_iota(jnp.int32, (tq, 128), 1)
    qpos = i * tq + lax.broadcasted_iota(jnp.int32, (tq, tk), 0)
    kloc = lax.broadcasted_iota(jnp.int32, (tq, tk), 1)
    nkv = (i * tq + tq + tk - 1) // tk
    outs = []
    for h in range(2):
        q = q_ref[:, h * 128:(h + 1) * 128]

        def body(j, carry, q=q, h=h):
            m, l, acc = carry
            ks = pl.multiple_of(j * tk, tk)
            k = k_ref[pl.ds(ks, tk), h * 128:(h + 1) * 128]
            s = _dot_nt(q, k)
            s = jnp.where(ks + kloc <= qpos, s, NEG_INF)
            mn = jnp.maximum(m, jnp.max(s, axis=-1, keepdims=True))
            a = jnp.exp(m - mn)
            p = jnp.exp(s - mn)
            l = l * a + jnp.sum(p, axis=-1, keepdims=True)
            acc = acc * a + _dot(p.astype(BF16), v_ref[pl.ds(ks, tk), :])
            return mn, l, acc

        m0 = jnp.full((tq, 1), NEG_INF, F32)
        l0 = jnp.zeros((tq, 1), F32)
        a0 = jnp.zeros((tq, 128), F32)
        m, l, acc = lax.fori_loop(0, nkv, body, (m0, l0, a0))
        outs.append(acc / l)
    o_ref[...] = jnp.where(lane < HEAD_DIM, outs[0], outs[1])


def mla_flash(q3, k3, v3, tq=128, tk=256):
    b, t, _ = q3.shape
    return pl.pallas_call(
        functools.partial(_mla_flash_kernel, tq=tq, tk=tk),
        grid=(b, 2, t // tq),
        in_specs=[pl.BlockSpec((None, tq, 256), lambda b_, p, i: (b_, i, p)),
                  pl.BlockSpec((None, t, 256), lambda b_, p, i: (b_, 0, p)),
                  pl.BlockSpec((None, t, 128), lambda b_, p, i: (b_, 0, p))],
        out_specs=pl.BlockSpec((None, tq, 128), lambda b_, p, i: (b_, i, p)),
        out_shape=jax.ShapeDtypeStruct((b, t, 256), F32),
        compiler_params=_cparams(("parallel", "parallel", "arbitrary")),
        name="mla_flash",
    )(q3, k3, v3)


DILATED_PATTERNS = ((128, 1), (512, 4), (2048, 16))
DIL_BLOCK = 128


def _dil_prep_kernel(q_ref, k_ref, qr_ref, kr_ref, cd_ref, sd_ref, qo_ref, ko_ref):
    cd = cd_ref[...]
    sd = sd_ref[...]
    qo_ref[...] = (q_ref[...] * cd + qr_ref[...] * sd) * (HEAD_DIM ** -0.5)
    ko_ref[...] = k_ref[...] * cd + kr_ref[...] * sd


def dil_prep(cols, cd, sd, tm=1024):
    n = cols.shape[0]
    b0 = C_DIL // 256
    spec = lambda j: pl.BlockSpec((tm, 256), lambda i: (i, j))
    return pl.pallas_call(
        _dil_prep_kernel,
        grid=(n // tm,),
        in_specs=[spec(b0), spec(b0 + 1), spec(b0 + 3), spec(b0 + 4), spec(0), spec(0)],
        out_specs=[spec(0), spec(0)],
        out_shape=[jax.ShapeDtypeStruct((n, 256), F32), jax.ShapeDtypeStruct((n, 256), F32)],
        compiler_params=_cparams(("parallel",)),
        name="dil_prep",
    )(cols, cols, cols, cols, cd, sd)


def _dil_block(qb, kb, vb, mask):
    nq = qb.shape[0]
    lane = lax.broadcasted_iota(jnp.int32, (nq, 128), 1)
    kbb = kb.astype(BF16)
    vbb = vb.astype(BF16)
    o = jnp.zeros((nq, 128), F32)
    lse = jnp.zeros((nq, 128), F32)
    for h in range(2):
        hm = (lane >> 6) == h
        s = _dot_nt(jnp.where(hm, qb, 0.0).astype(BF16), kbb)
        s = jnp.where(mask, s, NEG_INF)
        m = jnp.max(s, axis=-1, keepdims=True)
        p = jnp.exp(s - m)
        l = jnp.sum(p, axis=-1, keepdims=True)
        oh = _dot(p.astype(BF16), vbb) / l
        o = jnp.where(hm, oh, o)
        lse = jnp.where(hm, m + jnp.log(l), lse)
    return o, lse


def _dil_kernel(q_ref, k_ref, v_ref, o_ref, m_ref, z_ref, acc_ref, *, t):
    blk = DIL_BLOCK
    for pi, (window, dil) in enumerate(DILATED_PATTERNS):
        assert window // dil == blk
        length = t // dil
        nb = length // blk
        nk = 2 * blk if nb > 1 else blk
        uq = lax.broadcasted_iota(jnp.int32, (blk, nk), 0)
        uk = lax.broadcasted_iota(jnp.int32, (blk, nk), 1)

        def body(it, carry, dil=dil, nb=nb, nk=nk, uq=uq, uk=uk, first=(pi == 0)):
            r = it // nb
            jb = it % nb
            kb0 = jnp.maximum(jb - 1, 0) * blk
            qs = r + dil * blk * jb
            ks = r + dil * kb0
            if dil == 1:
                qidx = pl.ds(pl.multiple_of(qs, blk), blk)
                kidx = pl.ds(pl.multiple_of(ks, blk), nk)
            else:
                qidx = pl.ds(qs, blk, stride=dil)
                kidx = pl.ds(ks, nk, stride=dil)
            dist = (jb * blk + uq) - (kb0 + uk)
            mask = (dist >= 0) & (dist <= blk)
            o, lse = _dil_block(q_ref[qidx, :], k_ref[kidx, :], v_ref[kidx, :], mask)
            if first:
                m_ref[qidx, :] = lse
                z_ref[qidx, :] = jnp.ones_like(lse)
                acc_ref[qidx, :] = o
            else:
                m_old = m_ref[qidx, :]
                m_new = jnp.maximum(m_old, lse)
                a = jnp.exp(m_old - m_new)
                b = jnp.exp(lse - m_new)
                m_ref[qidx, :] = m_new
                z_ref[qidx, :] = z_ref[qidx, :] * a + b
                acc_ref[qidx, :] = acc_ref[qidx, :] * a + o * b
            return carry

        lax.fori_loop(0, dil * nb, body, 0)
    o_ref[...] = acc_ref[...] / z_ref[...]


def dilated_attention(q3, k3, cols3):
    b, t, _ = q3.shape
    assert t % (DIL_BLOCK * DILATED_PATTERNS[-1][1]) == 0
    vblk = C_DIL // 128 + 4
    return pl.pallas_call(
        functools.partial(_dil_kernel, t=t),
        grid=(b, 2),
        in_specs=[pl.BlockSpec((None, t, 128), lambda i, p: (i, 0, p)),
                  pl.BlockSpec((None, t, 128), lambda i, p: (i, 0, p)),
                  pl.BlockSpec((None, t, 128), lambda i, p: (i, 0, vblk + p))],
        out_specs=pl.BlockSpec((None, t, 128), lambda i, p: (i, 0, p)),
        out_shape=jax.ShapeDtypeStruct((b, t, 256), F32),
        scratch_shapes=[pltpu.VMEM((t, 128), F32)] * 3,
        compiler_params=_cparams(("parallel", "parallel")),
        name="dilated_attention",
    )(q3, k3, cols3)


CHUNK = 64
STACK = N_HEADS * CHUNK


def _stack_heads(x):
    lane = lax.broadcasted_iota(jnp.int32, x.shape, 1)
    return jnp.concatenate([jnp.where((lane >> 6) == h, x, 0.0) for h in range(N_HEADS)], axis=0)


def _unstack_heads(y):
    out = y[0:CHUNK]
    for h in range(1, N_HEADS):
        out = out + y[h * CHUNK:(h + 1) * CHUNK]
    return out


def _head_mean_matrix():
    r = lax.broadcasted_iota(jnp.int32, (256, 256), 0)
    c = lax.broadcasted_iota(jnp.int32, (256, 256), 1)
    return jnp.where((r >> 6) == (c >> 6), 1.0 / HEAD_DIM, 0.0).astype(BF16)


HGRN_HALVES = (32, 16, 8, 4, 2, 1)


def _hgrn_sum_matrix():
    t = lax.broadcasted_iota(jnp.int32, (CHUNK, CHUNK), 0)
    j = lax.broadcasted_iota(jnp.int32, (CHUNK, CHUNK), 1)
    tri = jnp.where(j <= t, 1.0, 0.0)
    blocks = [tri]
    for half in HGRN_HALVES:
        ref = (t // (2 * half)) * (2 * half) + half - 1
        blocks.append(tri - jnp.where(j <= ref, 1.0, 0.0))
    return jnp.concatenate(blocks, axis=0).astype(BF16)


def _hgrn_kernel(q_ref, f_ref, i_ref, g_ref, lbl_ref, gn_ref, o_ref, st_ref, *, layer, nchunk):
    @pl.when(pl.program_id(1) == 0)
    def _():
        st_ref[...] = jnp.zeros_like(st_ref)

    logits = lbl_ref[...]
    e = jnp.exp(logits - jnp.max(logits, axis=0, keepdims=True))
    sm = e / jnp.sum(e, axis=0, keepdims=True)
    lb = jnp.zeros((1, 256), F32)
    for l in range(1, layer + 1):
        lb = lb + sm[l:l + 1, :]
    gn = gn_ref[...]
    summat = _hgrn_sum_matrix()
    hmean = _head_mean_matrix()
    r = lax.broadcasted_iota(jnp.int32, (STACK, STACK), 0)
    c = lax.broadcasted_iota(jnp.int32, (STACK, STACK), 1)
    same_head = (r >> 6) == (c >> 6)
    rc = r & (CHUNK - 1)
    cc = c & (CHUNK - 1)

    def chunk(ci, carry):
        rows = pl.ds(pl.multiple_of(ci * CHUNK, CHUNK), CHUNK)
        qv = q_ref[rows, :]
        qq = qv * jax.nn.sigmoid(qv)
        forget = lb + (1.0 - lb) * jax.nn.sigmoid(f_ref[rows, :])
        logf = jnp.log(forget)
        kk = 1.0 - forget
        v = i_ref[rows, :]
        sums = _split_dot_lhs(summat, logf)
        b = sums[0:CHUNK]
        a = jnp.where(rc == cc, _dot_nt(_stack_heads(qq).astype(BF16), _stack_heads(kk).astype(BF16)), 0.0)
        for li, half in enumerate(HGRN_HALVES):
            d = sums[(li + 1) * CHUNK:(li + 2) * CHUNK]
            ql = qq * jnp.exp(jnp.minimum(d, 0.0))
            kl = kk * jnp.exp(jnp.minimum(-d, 0.0))
            m = ((rc // (2 * half)) == (cc // (2 * half))) & ((rc % (2 * half)) >= half) & ((cc % (2 * half)) < half)
            a = a + jnp.where(m, _dot_nt(_stack_heads(ql).astype(BF16), _stack_heads(kl).astype(BF16)), 0.0)
        o = _unstack_heads(_dot(a.astype(BF16), _stack_heads(v).astype(BF16)))
        st = st_ref[...]
        o = o + _dot_nt((qq * jnp.exp(b)).astype(BF16), st.astype(BF16))
        b_end = b[CHUNK - 1:CHUNK, :]
        kbar = kk * jnp.exp(b_end - b)
        upd = lax.dot_general(v.astype(BF16), kbar.astype(BF16), (((0,), (0,)), ((), ())), preferred_element_type=F32)
        st_ref[...] = st * jnp.exp(b_end) + jnp.where(same_head, upd, 0.0)
        ms = _split_dot(o * o, hmean)
        gv = g_ref[rows, :]
        o_ref[rows, :] = o * lax.rsqrt(ms + RMS_EPS) * gn * (gv * jax.nn.sigmoid(gv))
        return carry

    lax.fori_loop(0, nchunk, chunk, 0)


def hgrn(cols3, lb_logits, g_norm, layer, tb=512):
    b, t, _ = cols3.shape
    c0 = C_HGRN // 256
    spec = lambda j: pl.BlockSpec((None, tb, 256), lambda b_, i: (b_, i, j))
    depth = lb_logits.shape[0]
    return pl.pallas_call(
        functools.partial(_hgrn_kernel, layer=layer, nchunk=tb // CHUNK),
        grid=(b, t // tb),
        in_specs=[spec(c0), spec(c0 + 1), spec(c0 + 2), spec(c0 + 3),
                  pl.BlockSpec((depth, 256), lambda b_, i: (0, 0)),
                  pl.BlockSpec((1, 256), lambda b_, i: (0, 0))],
        out_specs=spec(0),
        out_shape=jax.ShapeDtypeStruct((b, t, 256), F32),
        scratch_shapes=[pltpu.VMEM((STACK, STACK), F32)],
        compiler_params=_cparams(("parallel", "arbitrary")),
        name="hgrn",
    )(cols3, cols3, cols3, cols3, lb_logits, g_norm.reshape(1, 256))


def _head_sum_matrix():
    r = lax.broadcasted_iota(jnp.int32, (256, 256), 0)
    c = lax.broadcasted_iota(jnp.int32, (256, 256), 1)
    return jnp.where((r >> 6) == (c >> 6), 1.0, 0.0).astype(BF16)


def _rwkv_prep_kernel(c_ref, p_ref, mu_ref, w0_ref, w2_ref, a0_ref, a2_ref, g2_ref, kk_ref, ka_ref, rk_ref,
                      r_o, lw_o, k_o, v_o, kap_o, at_o, bonus_o, g_o):
    i = pl.program_id(1)
    a = c_ref[...]
    tm = a.shape[0]
    last_prev = jnp.where(i > 0, p_ref[7:8, :], 0.0)
    row = lax.broadcasted_iota(jnp.int32, a.shape, 0)
    prev = jnp.where(row == 0, last_prev, pltpu.roll(a, 1, 0))
    xs = a + (prev - a) * mu_ref[...]
    r = xs[:, 0:256]
    k = xs[:, 256:512]
    v = xs[:, 512:768]
    xwa = xs[:, 768:896]
    xg = xs[:, 896:1024]
    w_in = w0_ref[...] + _dot(jnp.tanh(xwa).astype(BF16), w2_ref[...])
    z = -w_in
    softplus = jnp.maximum(z, 0.0) + jnp.log(1.0 + jnp.exp(-jnp.abs(z)))
    lw_o[...] = -jnp.exp(-softplus - 0.5)
    lr = jax.nn.sigmoid(a0_ref[...] + _dot(xwa.astype(BF16), a2_ref[...]))
    g_o[...] = _dot(jax.nn.sigmoid(xg).astype(BF16), g2_ref[...])
    hsum = _head_sum_matrix()
    kk = k * kk_ref[...]
    norm = jnp.sqrt(_split_dot(kk * kk, hsum))
    kk = kk / jnp.maximum(norm, 1e-12)
    k2 = k * (1.0 + (lr - 1.0) * ka_ref[...])
    r_o[...] = r
    k_o[...] = k2
    v_o[...] = v
    kap_o[...] = kk
    at_o[...] = kk * lr
    bonus_o[...] = _split_dot(r * k2 * rk_ref[...], hsum) * v


def rwkv_prep(cols3, mu, w0, w2p, a0, a2p, g2, k_k, k_a, r_k, tm=512):
    b, t, _ = cols3.shape
    full = lambda shape: pl.BlockSpec(shape, lambda b_, i: (0, 0))
    ospec = pl.BlockSpec((None, tm, 256), lambda b_, i: (b_, i, 0))
    row = lambda x: x.reshape(1, -1)
    return pl.pallas_call(
        _rwkv_prep_kernel,
        grid=(b, t // tm),
        in_specs=[pl.BlockSpec((None, tm, 1024), lambda b_, i: (b_, i, 0)),
                  pl.BlockSpec((None, 8, 1024), lambda b_, i: (b_, jnp.maximum(i * (tm // 8) - 1, 0), 0)),
                  full((1, 1024)), full((1, 256)), full((128, 256)), full((1, 256)), full((128, 256)),
                  full((128, 256)), full((1, 256)), full((1, 256)), full((1, 256))],
        out_specs=[ospec] * 8,
        out_shape=[jax.ShapeDtypeStruct((b, t, 256), F32)] * 8,
        compiler_params=_cparams(("parallel", "parallel")),
        name="rwkv_prep",
    )(cols3, cols3, row(mu), row(w0), w2p, row(a0), a2p, g2, row(k_k), row(k_a), row(r_k))


RWKV_MERGE_HALVES = (2, 4, 8, 16, 32)


def _rwkv_kernel(r_ref, lw_ref, k_ref, v_ref, kap_ref, at_ref, bonus_ref, g_ref, lnw_ref, lnb_ref,
                 o_ref, ht_ref, *, nchunk):
    @pl.when(pl.program_id(1) == 0)
    def _():
        ht_ref[...] = jnp.zeros_like(ht_ref)

    t = lax.broadcasted_iota(jnp.int32, (CHUNK, CHUNK), 0)
    j = lax.broadcasted_iota(jnp.int32, (CHUNK, CHUNK), 1)
    tri = jnp.where(j <= t, 1.0, 0.0).astype(BF16)
    hmean = _head_mean_matrix()
    rr = lax.broadcasted_iota(jnp.int32, (STACK, STACK), 0)
    cc = lax.broadcasted_iota(jnp.int32, (STACK, STACK), 1)
    same_head = (rr >> 6) == (cc >> 6)
    rc = rr & (CHUNK - 1)
    sc = cc & (CHUNK - 1)
    eye = jnp.where(rr == cc, 1.0, 0.0)
    lnw = lnw_ref[...]
    lnb = lnb_ref[...]

    def chunk(ci, carry):
        rows = pl.ds(pl.multiple_of(ci * CHUNK, CHUNK), CHUNK)
        lw = lw_ref[rows, :]
        lc = _split_dot_lhs(tri, lw)
        l_end = lc[CHUNK - 1:CHUNK, :]
        kap = kap_ref[rows, :]
        at = at_ref[rows, :]
        k2 = k_ref[rows, :]
        v = v_ref[rows, :]
        inv = jnp.exp(-lc)
        fwd = jnp.exp(l_end - lc)
        s_kap = _stack_heads(kap * jnp.exp(lc - lw)).astype(BF16)
        s_r32 = _stack_heads(r_ref[rows, :] * jnp.exp(lc))
        s_r = s_r32.astype(BF16)
        s_a = _stack_heads(at * inv).astype(BF16)
        s_k = _stack_heads(k2 * inv).astype(BF16)
        s_v = _stack_heads(v).astype(BF16)
        strict = rc > sc
        incl = rc >= sc
        a_ka = jnp.where(strict, _dot_nt(s_kap, s_a), 0.0)
        a_kk = jnp.where(strict, _dot_nt(s_kap, s_k), 0.0).astype(BF16)
        a_ra = jnp.where(incl, _dot_nt(s_r, s_a), 0.0).astype(BF16)
        a_rk = jnp.where(incl, _dot_nt(s_r, s_k), 0.0).astype(BF16)
        m1 = ((rc >> 1) == (sc >> 1)) & strict
        tinv = eye - jnp.where(m1, a_ka, 0.0)
        for half in RWKV_MERGE_HALVES:
            m = ((rc // (2 * half)) == (sc // (2 * half))) & ((rc % (2 * half)) >= half) & ((sc % (2 * half)) < half)
            off = jnp.where(m, a_ka, 0.0).astype(BF16)
            tb = tinv.astype(BF16)
            tinv = tinv - _dot(tb, _dot(off, tb).astype(BF16))
        tb = tinv.astype(BF16)
        kt_st = _dot(tb, s_kap)
        vt_st = _dot(tb, _dot(a_kk, s_v).astype(BF16))
        rt = _unstack_heads(s_r32 - _dot(a_ra, kt_st.astype(BF16)))
        y0 = _unstack_heads(_dot(a_rk, s_v) - _dot(a_ra, vt_st.astype(BF16)))
        kt = _unstack_heads(kt_st)
        vt = _unstack_heads(vt_st)
        ht = ht_ref[...]
        htb = ht.astype(BF16)
        y = y0 + _dot_nt(rt.astype(BF16), htb)
        u = vt + _dot_nt(kt.astype(BF16), htb)
        lhs = jnp.concatenate([v, u], axis=0).astype(BF16)
        rhs = jnp.concatenate([k2 * fwd, -(at * fwd)], axis=0).astype(BF16)
        upd = lax.dot_general(lhs, rhs, (((0,), (0,)), ((), ())), preferred_element_type=F32)
        ht_ref[...] = ht * jnp.exp(l_end) + jnp.where(same_head, upd, 0.0)
        mean = _split_dot(y, hmean)
        yc = y - mean
        var = _split_dot(yc * yc, hmean)
        yn = yc * lax.rsqrt(var + RWKV_LN_EPS) * lnw + lnb
        o_ref[rows, :] = (yn + bonus_ref[rows, :]) * g_ref[rows, :]
        return carry

    lax.fori_loop(0, nchunk, chunk, 0)


def rwkv_mix(prep, ln_w, ln_b, tb=512):
    b, t, _ = prep[0].shape
    spec = pl.BlockSpec((None, tb, 256), lambda b_, i: (b_, i, 0))
    full = pl.BlockSpec((1, 256), lambda b_, i: (0, 0))
    return pl.pallas_call(
        functools.partial(_rwkv_kernel, nchunk=tb // CHUNK),
        grid=(b, t // tb),
        in_specs=[spec] * 8 + [full, full],
        out_specs=spec,
        out_shape=jax.ShapeDtypeStruct((b, t, 256), F32),
        scratch_shapes=[pltpu.VMEM((STACK, STACK), F32)],
        compiler_params=_cparams(("parallel", "arbitrary")),
        name="rwkv_mix",
    )(*prep, ln_w.reshape(1, 256), ln_b.reshape(1, 256))


def _rot_cols(w, half):
    g = w.reshape(w.shape[0], -1, 2, half)
    return jnp.concatenate([-g[:, :, 1:2, :], g[:, :, 0:1, :]], axis=2).reshape(w.shape)


def pack_w_in(w):
    d = w.shape[0]
    rw = w[:, 0:1024]
    dq, dk, dv = w[:, 1024:1280], w[:, 1280:1536], w[:, 1536:1792]
    cq, ckv, kr = w[:, 1792:2048], w[:, 2048:2176], w[:, 2176:2208]
    hg = w[:, 2208:3232]
    z = lambda n: jnp.zeros((d, n), w.dtype)
    kra = jnp.concatenate([z(MLA_NOPE), kr, z(128 - MLA_NOPE - MLA_ROPE)], axis=1)
    krb = jnp.concatenate([z(MLA_NOPE), _rot_cols(kr, MLA_ROPE // 2), z(128 - MLA_NOPE - MLA_ROPE)], axis=1)
    packed = jnp.concatenate([rw, dq, dk, dv, _rot_cols(dq, HEAD_DIM // 2), _rot_cols(dk, HEAD_DIM // 2),
                              hg, cq, ckv, kra, krb], axis=1)
    return packed.astype(BF16)


def pack_mla(w_uq, w_ukv):
    rq = w_uq.shape[0]
    q = w_uq.reshape(rq, N_HEADS, MLA_NOPE + MLA_ROPE)
    nope, rope = q[..., :MLA_NOPE], q[..., MLA_NOPE:]
    pad = jnp.zeros((rq, N_HEADS, 128 - MLA_NOPE - MLA_ROPE), w_uq.dtype)
    wq1 = jnp.concatenate([nope, rope, pad], axis=-1).reshape(rq, N_HEADS * 128)
    rrot = _rot_cols(rope.reshape(rq, N_HEADS * MLA_ROPE), MLA_ROPE // 2).reshape(rq, N_HEADS, MLA_ROPE)
    wq2 = jnp.concatenate([jnp.zeros_like(nope), rrot, pad], axis=-1).reshape(rq, N_HEADS * 128)
    rk = w_ukv.shape[0]
    kv = w_ukv.reshape(rk, N_HEADS, MLA_NOPE + HEAD_DIM)
    wk = jnp.concatenate([kv[..., :MLA_NOPE], jnp.zeros((rk, N_HEADS, 128 - MLA_NOPE), w_ukv.dtype)],
                         axis=-1).reshape(rk, N_HEADS * 128)
    wv = kv[..., MLA_NOPE:].reshape(rk, N_HEADS * HEAD_DIM)
    return wq1.astype(BF16), wq2.astype(BF16), wk.astype(BF16), wv.astype(BF16)


def kernel(x, positions, mix_norm, w_in, w_out, rwkv_mu, rwkv_w0, rwkv_w2, rwkv_a0, rwkv_a2, rwkv_g2, rwkv_k_k,
           rwkv_k_a, rwkv_r_k, rwkv_ln_w, rwkv_ln_b, mla_q_norm, mla_w_uq, mla_kv_norm, mla_w_ukv, hgrn_lb_logits,
           hgrn_g_norm, ffn_norm, ffn_w1, ffn_w3, ffn_w2, moe_router, moe_w1, moe_w3, moe_w2, final_norm):
    b, t, d = x.shape
    n = b * t
    depth = w_in.shape[0]
    x2 = x.reshape(n, d)
    cd, sd, cm, sm = rope_tables(positions.reshape(n, 1))
    lora_pad = jnp.zeros((rwkv_w2.shape[1], D_GROUP), F32)
    fused_final = False
    for layer in range(depth):
        cols = in_proj(x2, mix_norm[layer], pack_w_in(w_in[layer]))
        cols3 = cols.reshape(b, t, N_PACK)
        prep = rwkv_prep(cols3, rwkv_mu[layer], rwkv_w0[layer],
                         jnp.concatenate([rwkv_w2[layer], lora_pad], axis=0).astype(BF16), rwkv_a0[layer],
                         jnp.concatenate([lora_pad, rwkv_a2[layer]], axis=0).astype(BF16),
                         rwkv_g2[layer].astype(BF16), rwkv_k_k[layer], rwkv_k_a[layer], rwkv_r_k[layer])
        y_a = rwkv_mix(prep, rwkv_ln_w[layer], rwkv_ln_b[layer])
        q_d, k_d = dil_prep(cols, cd, sd)
        y_b = dilated_attention(q_d.reshape(b, t, D_GROUP), k_d.reshape(b, t, D_GROUP), cols3)
        q_m, k_m, v_m = mla_prep(cols, cm, sm, mla_q_norm[layer], mla_kv_norm[layer],
                                 *pack_mla(mla_w_uq[layer], mla_w_ukv[layer]))
        y_c = mla_flash(q_m.reshape(b, t, -1), k_m.reshape(b, t, -1), v_m.reshape(b, t, -1))
        y_d = hgrn(cols3, hgrn_lb_logits, hgrn_g_norm[layer], layer)
        ys = [y.reshape(n, D_GROUP) for y in (y_a, y_b, y_c, y_d)]
        x2, h = out_proj(x2, ys, w_out[layer].astype(BF16), ffn_norm[layer])
        j = layer // 2
        if layer % 2 == 0:
            x2 = ffn_dense(h, x2, ffn_w1[j].astype(BF16), ffn_w3[j].astype(BF16), ffn_w2[j].astype(BF16))
        else:
            route, hp = moe_route(x2, ffn_norm[layer], jnp.pad(moe_router[j], ((0, 0), (0, 128 - N_EXPERTS))))
            rows_pad = TOP_K * n + N_EXPERTS * MOE_TILE
            pos, tile_expert, tile_valid = moe_plan(route, rows_pad)
            xs = moe_scatter(pos, hp, rows_pad)
            ys = moe_experts(tile_expert, tile_valid, xs, moe_w1[j].astype(BF16), moe_w3[j].astype(BF16),
                             moe_w2[j].astype(BF16))
            fused_final = layer == depth - 1
            x2 = moe_combine(pos, x2, route, ys, final_norm, fused_final)
    if not fused_final:
        x2 = final_rms(x2, final_norm)
    return x2.reshape(b, t, d)
```

```python
import functools

import jax
import jax.numpy as jnp
from jax import lax
from jax.experimental import pallas as pl
from jax.experimental.pallas import tpu as pltpu

F32 = jnp.float32
BF16 = jnp.bfloat16

D_GROUP = 256
HEAD_DIM = 64
N_HEADS = 4
RMS_EPS = 1e-6
RWKV_LN_EPS = 64e-5
ROPE_THETA = 10000.0
NEG_INF = -1e30
N_EXPERTS = 8
MLA_ROPE = 32
MLA_NOPE = 64

VMEM_LIMIT = 56 * 1024 * 1024

C_RWKV = 0
C_DIL = 1024
C_HGRN = 2304
C_MLA = 3328
N_PACK = 3968


def _cparams(sem, vmem=VMEM_LIMIT):
    return pltpu.CompilerParams(dimension_semantics=sem, vmem_limit_bytes=vmem)


def _rms(x, w):
    return x * lax.rsqrt(jnp.mean(x * x, axis=-1, keepdims=True) + RMS_EPS) * w


def _dot(a, b):
    return jnp.dot(a, b, preferred_element_type=F32)


def _dot_nt(a, b):
    return lax.dot_general(a, b, (((1,), (1,)), ((), ())), preferred_element_type=F32)


def _split_dot(a_f32, b_bf16):
    hi = a_f32.astype(BF16)
    lo = (a_f32 - hi.astype(F32)).astype(BF16)
    return _dot(hi, b_bf16) + _dot(lo, b_bf16)


def _split_dot_lhs(a_bf16, b_f32):
    hi = b_f32.astype(BF16)
    lo = (b_f32 - hi.astype(F32)).astype(BF16)
    return _dot(a_bf16, hi) + _dot(a_bf16, lo)


def _rope_kernel(pos_ref, cd_ref, sd_ref, cm_ref, sm_ref):
    pos = pos_ref[...].astype(F32)
    tm = pos.shape[0]
    lane = lax.broadcasted_iota(jnp.int32, (tm, 256), 1)
    j = (lane & 31).astype(F32)
    inv = jnp.exp(j * (-2.0 / HEAD_DIM * jnp.log(ROPE_THETA)))
    ang = pos * inv
    cd_ref[...] = jnp.cos(ang)
    sd_ref[...] = jnp.sin(ang)
    lane = lax.broadcasted_iota(jnp.int32, (tm, 128), 1)
    jm = (lane & 15).astype(F32)
    invm = jnp.exp(jm * (-2.0 / MLA_ROPE * jnp.log(ROPE_THETA)))
    angm = pos * invm
    rope_lane = (lane >= MLA_NOPE) & (lane < MLA_NOPE + MLA_ROPE)
    cm_ref[...] = jnp.where(rope_lane, jnp.cos(angm), jnp.where(lane < MLA_NOPE, 1.0, 0.0))
    sm_ref[...] = jnp.where(rope_lane, jnp.sin(angm), 0.0)


def rope_tables(pos_col, tm=512):
    n = pos_col.shape[0]
    return pl.pallas_call(
        _rope_kernel,
        grid=(n // tm,),
        in_specs=[pl.BlockSpec((tm, 1), lambda i: (i, 0))],
        out_specs=[pl.BlockSpec((tm, 256), lambda i: (i, 0)), pl.BlockSpec((tm, 256), lambda i: (i, 0)),
                   pl.BlockSpec((tm, 128), lambda i: (i, 0)), pl.BlockSpec((tm, 128), lambda i: (i, 0))],
        out_shape=[jax.ShapeDtypeStruct((n, 256), F32), jax.ShapeDtypeStruct((n, 256), F32),
                   jax.ShapeDtypeStruct((n, 128), F32), jax.ShapeDtypeStruct((n, 128), F32)],
        compiler_params=_cparams(("parallel",)),
        name="rope_tables",
    )(pos_col)


def _in_proj_kernel(x_ref, nw_ref, w_ref, o_ref):
    h = _rms(x_ref[...], nw_ref[...]).astype(BF16)
    o_ref[...] = _dot(h, w_ref[...])


def in_proj(x2, norm_w, w_packed, tm=512):
    n, d = x2.shape
    nc = w_packed.shape[1]
    return pl.pallas_call(
        _in_proj_kernel,
        grid=(n // tm,),
        in_specs=[pl.BlockSpec((tm, d), lambda i: (i, 0)),
                  pl.BlockSpec((1, d), lambda i: (0, 0)),
                  pl.BlockSpec((d, nc), lambda i: (0, 0))],
        out_specs=pl.BlockSpec((tm, nc), lambda i: (i, 0)),
        out_shape=jax.ShapeDtypeStruct((n, nc), F32),
        compiler_params=_cparams(("parallel",)),
        name="in_proj",
    )(x2, norm_w.reshape(1, d), w_packed)


def _out_proj_kernel(x_ref, ya_ref, yb_ref, yc_ref, yd_ref, w_ref, nw_ref, xo_ref, h_ref):
    acc = x_ref[...]
    for g, y_ref in enumerate((ya_ref, yb_ref, yc_ref, yd_ref)):
        acc = acc + _dot(y_ref[...].astype(BF16), w_ref[g * D_GROUP:(g + 1) * D_GROUP, :])
    xo_ref[...] = acc
    h_ref[...] = _rms(acc, nw_ref[...]).astype(BF16)


def out_proj(x2, ys, w_out_bf16, norm_w, tm=512):
    n, d = x2.shape
    yspec = pl.BlockSpec((tm, D_GROUP), lambda i: (i, 0))
    return pl.pallas_call(
        _out_proj_kernel,
        grid=(n // tm,),
        in_specs=[pl.BlockSpec((tm, d), lambda i: (i, 0)), yspec, yspec, yspec, yspec,
                  pl.BlockSpec((d, d), lambda i: (0, 0)),
                  pl.BlockSpec((1, d), lambda i: (0, 0))],
        out_specs=[pl.BlockSpec((tm, d), lambda i: (i, 0)), pl.BlockSpec((tm, d), lambda i: (i, 0))],
        out_shape=[jax.ShapeDtypeStruct((n, d), F32), jax.ShapeDtypeStruct((n, d), BF16)],
        compiler_params=_cparams(("parallel",)),
        name="out_proj",
    )(x2, *ys, w_out_bf16, norm_w.reshape(1, d))


def _ffn_kernel(h_ref, x_ref, w1_ref, w3_ref, w2_ref, o_ref, acc_ref):
    j = pl.program_id(1)

    @pl.when(j == 0)
    def _():
        acc_ref[...] = x_ref[...]

    h = h_ref[...]
    a = _dot(h, w1_ref[...])
    b = _dot(h, w3_ref[...])
    act = (a * jax.nn.sigmoid(a) * b).astype(BF16)
    acc_ref[...] += _dot(act, w2_ref[...])

    @pl.when(j == pl.num_programs(1) - 1)
    def _():
        o_ref[...] = acc_ref[...]


def ffn_dense(h_bf16, x2, w1, w3, w2, tm=512, tf=1408):
    n, d = x2.shape
    f = w1.shape[1]
    return pl.pallas_call(
        _ffn_kernel,
        grid=(n // tm, f // tf),
        in_specs=[pl.BlockSpec((tm, d), lambda i, j: (i, 0)),
                  pl.BlockSpec((tm, d), lambda i, j: (i, 0)),
                  pl.BlockSpec((d, tf), lambda i, j: (0, j)),
                  pl.BlockSpec((d, tf), lambda i, j: (0, j)),
                  pl.BlockSpec((tf, d), lambda i, j: (j, 0))],
        out_specs=pl.BlockSpec((tm, d), lambda i, j: (i, 0)),
        out_shape=jax.ShapeDtypeStruct((n, d), F32),
        scratch_shapes=[pltpu.VMEM((tm, d), F32)],
        compiler_params=_cparams(("parallel", "arbitrary")),
        name="ffn_dense",
    )(h_bf16, x2, w1, w3, w2)


MOE_TILE = 512
TOP_K = 2


def _router_kernel(x_ref, nw_ref, wr_ref, route_ref, hp_ref):
    h = _rms(x_ref[...], nw_ref[...])
    logits = jnp.dot(h, wr_ref[...], preferred_element_type=F32, precision=lax.Precision.HIGHEST)
    lane = lax.broadcasted_iota(jnp.int32, logits.shape, 1)
    logits = jnp.where(lane < N_EXPERTS, logits, -jnp.inf)
    m1 = jnp.max(logits, axis=-1, keepdims=True)
    i1 = jnp.min(jnp.where(logits == m1, lane, 128), axis=-1, keepdims=True)
    rest = jnp.where(lane == i1, -jnp.inf, logits)
    m2 = jnp.max(rest, axis=-1, keepdims=True)
    i2 = jnp.min(jnp.where(rest == m2, lane, 128), axis=-1, keepdims=True)
    e2 = jnp.exp(m2 - m1)
    g1 = 1.0 / (1.0 + e2)
    g2 = e2 * g1
    route_ref[...] = jnp.where(lane == 0, i1.astype(F32), jnp.where(lane == 1, i2.astype(F32),
                               jnp.where(lane == 2, g1, jnp.where(lane == 3, g2, 0.0))))
    bits = pltpu.bitcast(h.astype(BF16).astype(F32), jnp.uint32)
    half = bits.shape[1] // 2
    hp_ref[...] = (bits[:, :half] >> 16) | (bits[:, half:] & jnp.uint32(0xFFFF0000))


def moe_route(x2, norm_w, w_router_pad, tm=512):
    n, d = x2.shape
    return pl.pallas_call(
        _router_kernel,
        grid=(n // tm,),
        in_specs=[pl.BlockSpec((tm, d), lambda i: (i, 0)),
                  pl.BlockSpec((1, d), lambda i: (0, 0)),
                  pl.BlockSpec((d, 128), lambda i: (0, 0))],
        out_specs=[pl.BlockSpec((tm, 128), lambda i: (i, 0)), pl.BlockSpec((tm, d // 2), lambda i: (i, 0))],
        out_shape=[jax.ShapeDtypeStruct((n, 128), F32), jax.ShapeDtypeStruct((n, d // 2), jnp.uint32)],
        compiler_params=_cparams(("parallel",)),
        name="moe_router",
    )(x2, norm_w.reshape(1, d), w_router_pad)


def moe_plan(route, n_rows_pad):
    e = route[:, :TOP_K].astype(jnp.int32)
    onehot = (e.reshape(-1, 1) == jnp.arange(N_EXPERTS, dtype=jnp.int32)[None, :]).astype(jnp.int32)
    csum = jnp.cumsum(onehot, axis=0)
    counts = csum[-1]
    rank = jnp.sum((csum - onehot) * onehot, axis=1)
    gsz = ((counts + MOE_TILE - 1) // MOE_TILE) * MOE_TILE
    gend = jnp.cumsum(gsz)
    goff = gend - gsz
    pos = (goff[e.reshape(-1)] + rank).astype(jnp.int32)
    tile_start = jnp.arange(n_rows_pad // MOE_TILE, dtype=jnp.int32) * MOE_TILE
    tile_expert = jnp.minimum(jnp.sum((tile_start[:, None] >= gend[None, :]).astype(jnp.int32), axis=1),
                              N_EXPERTS - 1).astype(jnp.int32)
    tile_valid = (tile_start < gend[-1]).astype(jnp.int32)
    return pos, tile_expert, tile_valid


def _moe_scatter_kernel(pos_ref, hp_ref, xs_in_ref, xs_ref, sem):
    del xs_in_ref
    i = pl.program_id(0)
    ts = hp_ref.shape[0]

    def copy(r, slot):
        dst = pos_ref[(i * ts + r) * TOP_K + slot]
        return pltpu.make_async_copy(hp_ref.at[pl.ds(r, 1), :], xs_ref.at[pl.ds(dst, 1), :], sem)

    def issue(r, carry):
        for slot in range(TOP_K):
            copy(r, slot).start(priority=slot)
        return carry

    def wait(r, carry):
        for slot in range(TOP_K):
            copy(r, slot).wait()
        return carry

    lax.fori_loop(0, ts, issue, 0, unroll=8)
    lax.fori_loop(0, ts, wait, 0, unroll=8)


def moe_scatter(pos, hp, n_rows_pad, ts=256):
    n, dh = hp.shape
    xs0 = jnp.zeros((n_rows_pad, dh), jnp.uint32)
    return pl.pallas_call(
        _moe_scatter_kernel,
        grid_spec=pltpu.PrefetchScalarGridSpec(
            num_scalar_prefetch=1,
            grid=(n // ts,),
            in_specs=[pl.BlockSpec((ts, dh), lambda i, pos_: (i, 0)),
                      pl.BlockSpec(memory_space=pl.ANY)],
            out_specs=pl.BlockSpec(memory_space=pl.ANY),
            scratch_shapes=[pltpu.SemaphoreType.DMA],
        ),
        out_shape=jax.ShapeDtypeStruct((n_rows_pad, dh), jnp.uint32),
        input_output_aliases={2: 0},
        compiler_params=_cparams(("arbitrary",)),
        name="moe_scatter",
    )(pos, hp, xs0)


def _moe_expert_kernel(te_ref, tv_ref, xs_ref, w1_ref, w3_ref, w2_ref, ys_ref, h_ref, acc_ref):
    i = pl.program_id(0)
    j = pl.program_id(1)

    @pl.when(tv_ref[i] > 0)
    def _():
        @pl.when(j == 0)
        def _():
            p = xs_ref[...]
            lo = pltpu.bitcast(p << 16, F32)
            hi = pltpu.bitcast(p & jnp.uint32(0xFFFF0000), F32)
            h_ref[...] = jnp.concatenate([lo, hi], axis=1).astype(BF16)
            acc_ref[...] = jnp.zeros_like(acc_ref)

        h = h_ref[...]
        a = _dot(h, w1_ref[...])
        b = _dot(h, w3_ref[...])
        act = (a * jax.nn.sigmoid(a) * b).astype(BF16)
        acc_ref[...] += _dot(act, w2_ref[...])

        @pl.when(j == pl.num_programs(1) - 1)
        def _():
            ys_ref[...] = acc_ref[...]

    @pl.when((tv_ref[i] == 0) & (j == pl.num_programs(1) - 1))
    def _():
        ys_ref[...] = jnp.zeros_like(ys_ref)


def moe_experts(tile_expert, tile_valid, xs, w1, w3, w2, tf=1792):
    rows, dh = xs.shape
    d = 2 * dh
    f = w1.shape[2]
    tm = MOE_TILE
    return pl.pallas_call(
        _moe_expert_kernel,
        grid_spec=pltpu.PrefetchScalarGridSpec(
            num_scalar_prefetch=2,
            grid=(rows // tm, f // tf),
            in_specs=[pl.BlockSpec((tm, dh), lambda i, j, te, tv: (i, 0)),
                      pl.BlockSpec((None, d, tf), lambda i, j, te, tv: (te[i], 0, j * tv[i])),
                      pl.BlockSpec((None, d, tf), lambda i, j, te, tv: (te[i], 0, j * tv[i])),
                      pl.BlockSpec((None, tf, d), lambda i, j, te, tv: (te[i], j * tv[i], 0))],
            out_specs=pl.BlockSpec((tm, d), lambda i, j, te, tv: (i, 0)),
            scratch_shapes=[pltpu.VMEM((tm, d), BF16), pltpu.VMEM((tm, d), F32)],
        ),
        out_shape=jax.ShapeDtypeStruct((rows, d), F32),
        compiler_params=_cparams(("arbitrary", "arbitrary")),
        name="moe_experts",
    )(tile_expert, tile_valid, xs, w1, w3, w2)


def _moe_combine_kernel(pos_ref, x_ref, route_ref, nw_ref, ys_ref, o_ref, buf_ref, sem, *, final):
    i = pl.program_id(0)
    tc = x_ref.shape[0]

    def copy(r, slot):
        src = pos_ref[(i * tc + r) * TOP_K + slot]
        return pltpu.make_async_copy(ys_ref.at[pl.ds(src, 1), :], buf_ref.at[slot, pl.ds(r, 1), :], sem)

    def issue(r, carry):
        for slot in range(TOP_K):
            copy(r, slot).start(priority=slot)
        return carry

    def wait(r, carry):
        for slot in range(TOP_K):
            copy(r, slot).wait()
        return carry

    lax.fori_loop(0, tc, issue, 0, unroll=8)
    lax.fori_loop(0, tc, wait, 0, unroll=8)
    route = route_ref[...]
    g1 = route[:, 2:3]
    g2 = route[:, 3:4]
    out = x_ref[...] + g1 * buf_ref[0] + g2 * buf_ref[1]
    if final:
        out = _rms(out, nw_ref[...])
    o_ref[...] = out


def moe_combine(pos, x2, route, ys, norm_w, final, tc=256):
    n, d = x2.shape
    return pl.pallas_call(
        functools.partial(_moe_combine_kernel, final=final),
        grid_spec=pltpu.PrefetchScalarGridSpec(
            num_scalar_prefetch=1,
            grid=(n // tc,),
            in_specs=[pl.BlockSpec((tc, d), lambda i, pos_: (i, 0)),
                      pl.BlockSpec((tc, 128), lambda i, pos_: (i, 0)),
                      pl.BlockSpec((1, d), lambda i, pos_: (0, 0)),
                      pl.BlockSpec(memory_space=pl.ANY)],
            out_specs=pl.BlockSpec((tc, d), lambda i, pos_: (i, 0)),
            scratch_shapes=[pltpu.VMEM((TOP_K, tc, d), F32), pltpu.SemaphoreType.DMA],
        ),
        out_shape=jax.ShapeDtypeStruct((n, d), F32),
        compiler_params=_cparams(("arbitrary",)),
        name="moe_combine",
    )(pos, x2, route, norm_w.reshape(1, d), ys)


def _final_norm_kernel(x_ref, nw_ref, o_ref):
    o_ref[...] = _rms(x_ref[...], nw_ref[...])


def final_rms(x2, norm_w, tm=1024):
    n, d = x2.shape
    return pl.pallas_call(
        _final_norm_kernel,
        grid=(n // tm,),
        in_specs=[pl.BlockSpec((tm, d), lambda i: (i, 0)), pl.BlockSpec((1, d), lambda i: (0, 0))],
        out_specs=pl.BlockSpec((tm, d), lambda i: (i, 0)),
        out_shape=jax.ShapeDtypeStruct((n, d), F32),
        compiler_params=_cparams(("parallel",)),
        name="final_norm",
    )(x2, norm_w.reshape(1, d))


def _mla_prep_kernel(cq_ref, ckv_ref, kra_ref, krb_ref, cm_ref, sm_ref, qn_ref, kvn_ref,
                     wq1_ref, wq2_ref, wk_ref, wv_ref, q_ref, k_ref, v_ref):
    qn = _rms(cq_ref[...], qn_ref[...]).astype(BF16)
    kvn = _rms(ckv_ref[...], kvn_ref[...]).astype(BF16)
    cm = cm_ref[...]
    sm = sm_ref[...]
    cm4 = jnp.concatenate([cm] * N_HEADS, axis=1)
    sm4 = jnp.concatenate([sm] * N_HEADS, axis=1)
    scale = (MLA_NOPE + MLA_ROPE) ** -0.5
    q = (_dot(qn, wq1_ref[...]) * cm4 + _dot(qn, wq2_ref[...]) * sm4) * scale
    q_ref[...] = q.astype(BF16)
    kr = kra_ref[...] * cm + krb_ref[...] * sm
    k = _dot(kvn, wk_ref[...]) + jnp.concatenate([kr] * N_HEADS, axis=1)
    k_ref[...] = k.astype(BF16)
    v_ref[...] = _dot(kvn, wv_ref[...]).astype(BF16)


def mla_prep(cols, cm, sm, q_norm, kv_norm, wq1, wq2, wk, wv, tm=512):
    n = cols.shape[0]
    b256 = C_MLA // 256
    b128 = C_MLA // 128
    full = lambda shape: pl.BlockSpec(shape, lambda i: (0, 0))
    return pl.pallas_call(
        _mla_prep_kernel,
        grid=(n // tm,),
        in_specs=[pl.BlockSpec((tm, 256), lambda i: (i, b256)),
                  pl.BlockSpec((tm, 128), lambda i: (i, b128 + 2)),
                  pl.BlockSpec((tm, 128), lambda i: (i, b128 + 3)),
                  pl.BlockSpec((tm, 128), lambda i: (i, b128 + 4)),
                  pl.BlockSpec((tm, 128), lambda i: (i, 0)),
                  pl.BlockSpec((tm, 128), lambda i: (i, 0)),
                  full((1, 256)), full((1, 128)),
                  full((256, 512)), full((256, 512)), full((128, 512)), full((128, 256))],
        out_specs=[pl.BlockSpec((tm, 512), lambda i: (i, 0)), pl.BlockSpec((tm, 512), lambda i: (i, 0)),
                   pl.BlockSpec((tm, 256), lambda i: (i, 0))],
        out_shape=[jax.ShapeDtypeStruct((n, 512), BF16), jax.ShapeDtypeStruct((n, 512), BF16),
                   jax.ShapeDtypeStruct((n, 256), BF16)],
        compiler_params=_cparams(("parallel",)),
        name="mla_prep",
    )(cols, cols, cols, cols, cm, sm, q_norm.reshape(1, 256), kv_norm.reshape(1, 128), wq1, wq2, wk, wv)


def _mla_flash_kernel(q_ref, k_ref, v_ref, o_ref, *, tq, tk):
    i = pl.program_id(2)
    lane = lax.broadcasted_iota(jnp.int32, (tq, 128), 1)
    qpos = i * tq + lax.broadcasted_iota(jnp.int32, (tq, tk), 0)
    kloc = lax.broadcasted_iota(jnp.int32, (tq, tk), 1)
    nkv = (i * tq + tq + tk - 1) // tk
    qs = [q_ref[:, h * 128:(h + 1) * 128] for h in range(2)]

    def body(j, carry):
        ks = pl.multiple_of(j * tk, tk)
        v = v_ref[pl.ds(ks, tk), :]
        mask = ks + kloc <= qpos
        ss = [jnp.where(mask, _dot_nt(qs[h], k_ref[pl.ds(ks, tk), h * 128:(h + 1) * 128]), NEG_INF) for h in range(2)]
        mns = [jnp.maximum(carry[h][0], jnp.max(ss[h], axis=-1, keepdims=True)) for h in range(2)]
        ps = [jnp.exp(ss[h] - mns[h]) for h in range(2)]
        als = [jnp.exp(carry[h][0] - mns[h]) for h in range(2)]
        pvs = [_dot(ps[h].astype(BF16), v) for h in range(2)]
        ls = [carry[h][1] * als[h] + jnp.sum(ps[h], axis=-1, keepdims=True) for h in range(2)]
        return tuple((mns[h], ls[h], carry[h][2] * als[h] + pvs[h]) for h in range(2))

    init = (jnp.full((tq, 1), NEG_INF, F32), jnp.zeros((tq, 1), F32), jnp.zeros((tq, 128), F32))
    res = lax.fori_loop(0, nkv, body, (init, init))
    outs = [acc / l for (_, l, acc) in res]
    o_ref[...] = jnp.where(lane < HEAD_DIM, outs[0], outs[1])


def mla_flash(q3, k3, v3, tq=256, tk=512):
    b, t, _ = q3.shape
    return pl.pallas_call(
        functools.partial(_mla_flash_kernel, tq=tq, tk=tk),
        grid=(b, 2, t // tq),
        in_specs=[pl.BlockSpec((None, tq, 256), lambda b_, p, i: (b_, i, p)),
                  pl.BlockSpec((None, t, 256), lambda b_, p, i: (b_, 0, p)),
                  pl.BlockSpec((None, t, 128), lambda b_, p, i: (b_, 0, p))],
        out_specs=pl.BlockSpec((None, tq, 128), lambda b_, p, i: (b_, i, p)),
        out_shape=jax.ShapeDtypeStruct((b, t, 256), F32),
        compiler_params=_cparams(("parallel", "parallel", "arbitrary")),
        name="mla_flash",
    )(q3, k3, v3)


DILATED_PATTERNS = ((128, 1), (512, 4), (2048, 16))
DIL_BLOCK = 128
DIL_UNROLL = 4


def _dil_prep_kernel(q_ref, k_ref, qr_ref, kr_ref, cd_ref, sd_ref, qo_ref, ko_ref):
    cd = cd_ref[...]
    sd = sd_ref[...]
    qo_ref[...] = (q_ref[...] * cd + qr_ref[...] * sd) * (HEAD_DIM ** -0.5)
    ko_ref[...] = k_ref[...] * cd + kr_ref[...] * sd


def dil_prep(cols, cd, sd, tm=1024):
    n = cols.shape[0]
    b0 = C_DIL // 256
    spec = lambda j: pl.BlockSpec((tm, 256), lambda i: (i, j))
    return pl.pallas_call(
        _dil_prep_kernel,
        grid=(n // tm,),
        in_specs=[spec(b0), spec(b0 + 1), spec(b0 + 3), spec(b0 + 4), spec(0), spec(0)],
        out_specs=[spec(0), spec(0)],
        out_shape=[jax.ShapeDtypeStruct((n, 256), F32), jax.ShapeDtypeStruct((n, 256), F32)],
        compiler_params=_cparams(("parallel",)),
        name="dil_prep",
    )(cols, cols, cols, cols, cd, sd)


def _dil_block(qb, kb, vb, mask, out):
    nq = qb.shape[0]
    lane = lax.broadcasted_iota(jnp.int32, (nq, 128), 1)
    kbb = kb.astype(BF16)
    vbb = vb.astype(BF16)
    hms = [(lane >> 6) == h for h in range(2)]
    ss = [jnp.where(mask, _dot_nt(jnp.where(hm, qb, 0.0).astype(BF16), kbb), NEG_INF) for hm in hms]
    yield
    ms = [jnp.max(s, axis=-1, keepdims=True) for s in ss]
    ps = [jnp.exp(s - m) for s, m in zip(ss, ms)]
    yield
    ls = [jnp.sum(p, axis=-1, keepdims=True) for p in ps]
    ohs = [_dot(p.astype(BF16), vbb) for p in ps]
    yield
    out[0] = jnp.where(hms[0], ohs[0] / ls[0], ohs[1] / ls[1])
    out[1] = jnp.where(hms[0], ms[0] + jnp.log(ls[0]), ms[1] + jnp.log(ls[1]))


def _dil_kernel(q_ref, k_ref, v_ref, o_ref, m_ref, z_ref, acc_ref, *, t):
    blk = DIL_BLOCK
    for pi, (window, dil) in enumerate(DILATED_PATTERNS):
        assert window // dil == blk
        length = t // dil
        nb = length // blk
        nk = 2 * blk if nb > 1 else blk
        uq = lax.broadcasted_iota(jnp.int32, (blk, nk), 0)
        uk = lax.broadcasted_iota(jnp.int32, (blk, nk), 1)

        def one(it, dil=dil, nb=nb, nk=nk, uq=uq, uk=uk, first=(pi == 0)):
            r = it // nb
            jb = it % nb
            kb0 = jnp.maximum(jb - 1, 0) * blk
            qs = r + dil * blk * jb
            ks = r + dil * kb0
            if dil == 1:
                qidx = pl.ds(pl.multiple_of(qs, blk), blk)
                kidx = pl.ds(pl.multiple_of(ks, blk), nk)
            else:
                qidx = pl.ds(qs, blk, stride=dil)
                kidx = pl.ds(ks, nk, stride=dil)
            dist = (jb * blk + uq) - (kb0 + uk)
            mask = (dist >= 0) & (dist <= blk)
            out = [None, None]
            yield from _dil_block(q_ref[qidx, :], k_ref[kidx, :], v_ref[kidx, :], mask, out)
            o, lse = out
            if first:
                m_ref[qidx, :] = lse
                z_ref[qidx, :] = jnp.ones_like(lse)
                acc_ref[qidx, :] = o
            else:
                m_old = m_ref[qidx, :]
                m_new = jnp.maximum(m_old, lse)
                a = jnp.exp(m_old - m_new)
                b = jnp.exp(lse - m_new)
                m_ref[qidx, :] = m_new
                z_ref[qidx, :] = z_ref[qidx, :] * a + b
                acc_ref[qidx, :] = acc_ref[qidx, :] * a + o * b

        def body(it, carry, one=one):
            _round_robin([one(it * DIL_UNROLL + u) for u in range(DIL_UNROLL)])
            return carry

        assert (dil * nb) % DIL_UNROLL == 0
        lax.fori_loop(0, dil * nb // DIL_UNROLL, body, 0)
    o_ref[...] = acc_ref[...] / z_ref[...]


def dilated_attention(q3, k3, cols3):
    b, t, _ = q3.shape
    assert t % (DIL_BLOCK * DILATED_PATTERNS[-1][1]) == 0
    vblk = C_DIL // 128 + 4
    return pl.pallas_call(
        functools.partial(_dil_kernel, t=t),
        grid=(b, 2),
        in_specs=[pl.BlockSpec((None, t, 128), lambda i, p: (i, 0, p)),
                  pl.BlockSpec((None, t, 128), lambda i, p: (i, 0, p)),
                  pl.BlockSpec((None, t, 128), lambda i, p: (i, 0, vblk + p))],
        out_specs=pl.BlockSpec((None, t, 128), lambda i, p: (i, 0, p)),
        out_shape=jax.ShapeDtypeStruct((b, t, 256), F32),
        scratch_shapes=[pltpu.VMEM((t, 128), F32)] * 3,
        compiler_params=_cparams(("parallel", "parallel")),
        name="dilated_attention",
    )(q3, k3, cols3)


CHUNK = 64
STACK = N_HEADS * CHUNK


def _stack_heads(x):
    lane = lax.broadcasted_iota(jnp.int32, x.shape, 1)
    return jnp.concatenate([jnp.where((lane >> 6) == h, x, 0.0) for h in range(N_HEADS)], axis=0)


def _unstack_heads(y):
    out = y[0:CHUNK]
    for h in range(1, N_HEADS):
        out = out + y[h * CHUNK:(h + 1) * CHUNK]
    return out


def _round_robin(stages):
    live = list(stages)
    while live:
        nxt = []
        for g in live:
            try:
                next(g)
                nxt.append(g)
            except StopIteration:
                pass
        live = nxt


def _head_mean_matrix():
    r = lax.broadcasted_iota(jnp.int32, (256, 256), 0)
    c = lax.broadcasted_iota(jnp.int32, (256, 256), 1)
    return jnp.where((r >> 6) == (c >> 6), 1.0 / HEAD_DIM, 0.0).astype(BF16)


HGRN_HALVES = (32, 16, 8, 4, 2, 1)


def _hgrn_sum_matrix():
    t = lax.broadcasted_iota(jnp.int32, (CHUNK, CHUNK), 0)
    j = lax.broadcasted_iota(jnp.int32, (CHUNK, CHUNK), 1)
    tri = jnp.where(j <= t, 1.0, 0.0)
    blocks = [tri]
    for half in HGRN_HALVES:
        ref = (t // (2 * half)) * (2 * half) + half - 1
        blocks.append(tri - jnp.where(j <= ref, 1.0, 0.0))
    return jnp.concatenate(blocks, axis=0).astype(BF16)


def _hgrn_kernel(q_ref, f_ref, i_ref, g_ref, lbl_ref, gn_ref, o_ref, st_ref, *, layer, nchunk, nb):
    @pl.when(pl.program_id(1) == 0)
    def _():
        st_ref[...] = jnp.zeros_like(st_ref)

    logits = lbl_ref[...]
    e = jnp.exp(logits - jnp.max(logits, axis=0, keepdims=True))
    sm = e / jnp.sum(e, axis=0, keepdims=True)
    lb = jnp.zeros((1, 256), F32)
    for l in range(1, layer + 1):
        lb = lb + sm[l:l + 1, :]
    gn = gn_ref[...]
    summat = _hgrn_sum_matrix()
    hmean = _head_mean_matrix()
    r = lax.broadcasted_iota(jnp.int32, (STACK, STACK), 0)
    c = lax.broadcasted_iota(jnp.int32, (STACK, STACK), 1)
    same_head = (r >> 6) == (c >> 6)
    rc = r & (CHUNK - 1)
    cc = c & (CHUNK - 1)

    def one(bi, rows):
        qv = q_ref[bi, rows, :]
        qq = qv * jax.nn.sigmoid(qv)
        forget = lb + (1.0 - lb) * jax.nn.sigmoid(f_ref[bi, rows, :])
        logf = jnp.log(forget)
        kk = 1.0 - forget
        v = i_ref[bi, rows, :]
        yield
        sums = _split_dot_lhs(summat, logf)
        b = sums[0:CHUNK]
        a = jnp.where(rc == cc, _dot_nt(_stack_heads(qq).astype(BF16), _stack_heads(kk).astype(BF16)), 0.0)
        yield
        for li, half in enumerate(HGRN_HALVES):
            d = sums[(li + 1) * CHUNK:(li + 2) * CHUNK]
            ql = qq * jnp.exp(jnp.minimum(d, 0.0))
            kl = kk * jnp.exp(jnp.minimum(-d, 0.0))
            m = ((rc // (2 * half)) == (cc // (2 * half))) & ((rc % (2 * half)) >= half) & ((cc % (2 * half)) < half)
            a = a + jnp.where(m, _dot_nt(_stack_heads(ql).astype(BF16), _stack_heads(kl).astype(BF16)), 0.0)
            yield
        o = _unstack_heads(_dot(a.astype(BF16), _stack_heads(v).astype(BF16)))
        st = st_ref[bi]
        o = o + _dot_nt((qq * jnp.exp(b)).astype(BF16), st.astype(BF16))
        yield
        b_end = b[CHUNK - 1:CHUNK, :]
        kbar = kk * jnp.exp(b_end - b)
        upd = lax.dot_general(v.astype(BF16), kbar.astype(BF16), (((0,), (0,)), ((), ())), preferred_element_type=F32)
        st_ref[bi] = st * jnp.exp(b_end) + jnp.where(same_head, upd, 0.0)
        ms = _split_dot(o * o, hmean)
        yield
        gv = g_ref[bi, rows, :]
        o_ref[bi, rows, :] = o * lax.rsqrt(ms + RMS_EPS) * gn * (gv * jax.nn.sigmoid(gv))

    def chunk(ci, carry):
        rows = pl.ds(pl.multiple_of(ci * CHUNK, CHUNK), CHUNK)
        _round_robin([one(bi, rows) for bi in range(nb)])
        return carry

    lax.fori_loop(0, nchunk, chunk, 0)


SEQ_PER_STEP = 4


def hgrn(cols3, lb_logits, g_norm, layer, tb=512):
    b, t, _ = cols3.shape
    nb = SEQ_PER_STEP if b % SEQ_PER_STEP == 0 else 1
    c0 = C_HGRN // 256
    spec = lambda j: pl.BlockSpec((nb, tb, 256), lambda b_, i: (b_, i, j))
    depth = lb_logits.shape[0]
    return pl.pallas_call(
        functools.partial(_hgrn_kernel, layer=layer, nchunk=tb // CHUNK, nb=nb),
        grid=(b // nb, t // tb),
        in_specs=[spec(c0), spec(c0 + 1), spec(c0 + 2), spec(c0 + 3),
                  pl.BlockSpec((depth, 256), lambda b_, i: (0, 0)),
                  pl.BlockSpec((1, 256), lambda b_, i: (0, 0))],
        out_specs=spec(0),
        out_shape=jax.ShapeDtypeStruct((b, t, 256), F32),
        scratch_shapes=[pltpu.VMEM((nb, STACK, STACK), F32)],
        compiler_params=_cparams(("parallel", "arbitrary")),
        name="hgrn",
    )(cols3, cols3, cols3, cols3, lb_logits, g_norm.reshape(1, 256))


def _head_sum_matrix():
    r = lax.broadcasted_iota(jnp.int32, (256, 256), 0)
    c = lax.broadcasted_iota(jnp.int32, (256, 256), 1)
    return jnp.where((r >> 6) == (c >> 6), 1.0, 0.0).astype(BF16)


def _rwkv_prep_kernel(c_ref, p_ref, mu_ref, w0_ref, w2_ref, a0_ref, a2_ref, g2_ref, kk_ref, ka_ref, rk_ref,
                      r_o, lw_o, k_o, v_o, kap_o, at_o, bonus_o, g_o):
    i = pl.program_id(1)
    a = c_ref[...]
    tm = a.shape[0]
    last_prev = jnp.where(i > 0, p_ref[7:8, :], 0.0)
    row = lax.broadcasted_iota(jnp.int32, a.shape, 0)
    prev = jnp.where(row == 0, last_prev, pltpu.roll(a, 1, 0))
    xs = a + (prev - a) * mu_ref[...]
    r = xs[:, 0:256]
    k = xs[:, 256:512]
    v = xs[:, 512:768]
    xwa = xs[:, 768:896]
    xg = xs[:, 896:1024]
    w_in = w0_ref[...] + _dot(jnp.tanh(xwa).astype(BF16), w2_ref[...])
    z = -w_in
    softplus = jnp.maximum(z, 0.0) + jnp.log(1.0 + jnp.exp(-jnp.abs(z)))
    lw_o[...] = -jnp.exp(-softplus - 0.5)
    lr = jax.nn.sigmoid(a0_ref[...] + _dot(xwa.astype(BF16), a2_ref[...]))
    g_o[...] = _dot(jax.nn.sigmoid(xg).astype(BF16), g2_ref[...])
    hsum = _head_sum_matrix()
    kk = k * kk_ref[...]
    norm = jnp.sqrt(_split_dot(kk * kk, hsum))
    kk = kk / jnp.maximum(norm, 1e-12)
    k2 = k * (1.0 + (lr - 1.0) * ka_ref[...])
    r_o[...] = r
    k_o[...] = k2
    v_o[...] = v
    kap_o[...] = kk
    at_o[...] = kk * lr
    bonus_o[...] = _split_dot(r * k2 * rk_ref[...], hsum) * v


def rwkv_prep(cols3, mu, w0, w2p, a0, a2p, g2, k_k, k_a, r_k, tm=512):
    b, t, _ = cols3.shape
    full = lambda shape: pl.BlockSpec(shape, lambda b_, i: (0, 0))
    ospec = pl.BlockSpec((None, tm, 256), lambda b_, i: (b_, i, 0))
    row = lambda x: x.reshape(1, -1)
    return pl.pallas_call(
        _rwkv_prep_kernel,
        grid=(b, t // tm),
        in_specs=[pl.BlockSpec((None, tm, 1024), lambda b_, i: (b_, i, 0)),
                  pl.BlockSpec((None, 8, 1024), lambda b_, i: (b_, jnp.maximum(i * (tm // 8) - 1, 0), 0)),
                  full((1, 1024)), full((1, 256)), full((128, 256)), full((1, 256)), full((128, 256)),
                  full((128, 256)), full((1, 256)), full((1, 256)), full((1, 256))],
        out_specs=[ospec] * 8,
        out_shape=[jax.ShapeDtypeStruct((b, t, 256), F32)] * 8,
        compiler_params=_cparams(("parallel", "parallel")),
        name="rwkv_prep",
    )(cols3, cols3, row(mu), row(w0), w2p, row(a0), a2p, g2, row(k_k), row(k_a), row(r_k))


RWKV_MERGE_HALVES = (2, 4, 8, 16, 32)


def _rwkv_kernel(r_ref, lw_ref, k_ref, v_ref, kap_ref, at_ref, bonus_ref, g_ref, lnw_ref, lnb_ref,
                 o_ref, ht_ref, *, nchunk, nb):
    @pl.when(pl.program_id(1) == 0)
    def _():
        ht_ref[...] = jnp.zeros_like(ht_ref)

    t = lax.broadcasted_iota(jnp.int32, (CHUNK, CHUNK), 0)
    j = lax.broadcasted_iota(jnp.int32, (CHUNK, CHUNK), 1)
    tri = jnp.where(j <= t, 1.0, 0.0).astype(BF16)
    hmean = _head_mean_matrix()
    rr = lax.broadcasted_iota(jnp.int32, (STACK, STACK), 0)
    cc = lax.broadcasted_iota(jnp.int32, (STACK, STACK), 1)
    same_head = (rr >> 6) == (cc >> 6)
    rc = rr & (CHUNK - 1)
    sc = cc & (CHUNK - 1)
    eye = jnp.where(rr == cc, 1.0, 0.0)
    lnw = lnw_ref[...]
    lnb = lnb_ref[...]

    def one(bi, rows):
        lw = lw_ref[bi, rows, :]
        lc = _split_dot_lhs(tri, lw)
        l_end = lc[CHUNK - 1:CHUNK, :]
        kap = kap_ref[bi, rows, :]
        at = at_ref[bi, rows, :]
        k2 = k_ref[bi, rows, :]
        v = v_ref[bi, rows, :]
        inv = jnp.exp(-lc)
        fwd = jnp.exp(l_end - lc)
        s_kap = _stack_heads(kap * jnp.exp(lc - lw)).astype(BF16)
        s_r32 = _stack_heads(r_ref[bi, rows, :] * jnp.exp(lc))
        s_r = s_r32.astype(BF16)
        s_a = _stack_heads(at * inv).astype(BF16)
        s_k = _stack_heads(k2 * inv).astype(BF16)
        s_v = _stack_heads(v).astype(BF16)
        strict = rc > sc
        incl = rc >= sc
        yield
        a_ka = jnp.where(strict, _dot_nt(s_kap, s_a), 0.0)
        a_kk = jnp.where(strict, _dot_nt(s_kap, s_k), 0.0).astype(BF16)
        m1 = ((rc >> 1) == (sc >> 1)) & strict
        tinv = eye - jnp.where(m1, a_ka, 0.0)
        yield
        a_ra = jnp.where(incl, _dot_nt(s_r, s_a), 0.0).astype(BF16)
        a_rk = jnp.where(incl, _dot_nt(s_r, s_k), 0.0).astype(BF16)
        for half in RWKV_MERGE_HALVES:
            m = ((rc // (2 * half)) == (sc // (2 * half))) & ((rc % (2 * half)) >= half) & ((sc % (2 * half)) < half)
            off = jnp.where(m, a_ka, 0.0).astype(BF16)
            tb = tinv.astype(BF16)
            x = _dot(off, tb).astype(BF16)
            yield
            tinv = tinv - _dot(tb, x)
            yield
        tb = tinv.astype(BF16)
        kt_st = _dot(tb, s_kap)
        w_st = _dot(a_kk, s_v).astype(BF16)
        yield
        vt_st = _dot(tb, w_st)
        rt = _unstack_heads(s_r32 - _dot(a_ra, kt_st.astype(BF16)))
        yield
        y0 = _unstack_heads(_dot(a_rk, s_v) - _dot(a_ra, vt_st.astype(BF16)))
        kt = _unstack_heads(kt_st)
        vt = _unstack_heads(vt_st)
        ht = ht_ref[bi]
        htb = ht.astype(BF16)
        yield
        y = y0 + _dot_nt(rt.astype(BF16), htb)
        u = vt + _dot_nt(kt.astype(BF16), htb)
        yield
        lhs = jnp.concatenate([v, u], axis=0).astype(BF16)
        rhs = jnp.concatenate([k2 * fwd, -(at * fwd)], axis=0).astype(BF16)
        upd = lax.dot_general(lhs, rhs, (((0,), (0,)), ((), ())), preferred_element_type=F32)
        ht_ref[bi] = ht * jnp.exp(l_end) + jnp.where(same_head, upd, 0.0)
        mean = _split_dot(y, hmean)
        yield
        yc = y - mean
        var = _split_dot(yc * yc, hmean)
        yield
        yn = yc * lax.rsqrt(var + RWKV_LN_EPS) * lnw + lnb
        o_ref[bi, rows, :] = (yn + bonus_ref[bi, rows, :]) * g_ref[bi, rows, :]

    def chunk(ci, carry):
        rows = pl.ds(pl.multiple_of(ci * CHUNK, CHUNK), CHUNK)
        _round_robin([one(bi, rows) for bi in range(nb)])
        return carry

    lax.fori_loop(0, nchunk, chunk, 0)


def rwkv_mix(prep, ln_w, ln_b, tb=512):
    b, t, _ = prep[0].shape
    nb = SEQ_PER_STEP if b % SEQ_PER_STEP == 0 else 1
    spec = pl.BlockSpec((nb, tb, 256), lambda b_, i: (b_, i, 0))
    full = pl.BlockSpec((1, 256), lambda b_, i: (0, 0))
    return pl.pallas_call(
        functools.partial(_rwkv_kernel, nchunk=tb // CHUNK, nb=nb),
        grid=(b // nb, t // tb),
        in_specs=[spec] * 8 + [full, full],
        out_specs=spec,
        out_shape=jax.ShapeDtypeStruct((b, t, 256), F32),
        scratch_shapes=[pltpu.VMEM((nb, STACK, STACK), F32)],
        compiler_params=_cparams(("parallel", "arbitrary")),
        name="rwkv_mix",
    )(*prep, ln_w.reshape(1, 256), ln_b.reshape(1, 256))


def _rot_cols(w, half):
    g = w.reshape(w.shape[0], -1, 2, half)
    return jnp.concatenate([-g[:, :, 1:2, :], g[:, :, 0:1, :]], axis=2).reshape(w.shape)


def pack_w_in(w):
    d = w.shape[0]
    rw = w[:, 0:1024]
    dq, dk, dv = w[:, 1024:1280], w[:, 1280:1536], w[:, 1536:1792]
    cq, ckv, kr = w[:, 1792:2048], w[:, 2048:2176], w[:, 2176:2208]
    hg = w[:, 2208:3232]
    z = lambda n: jnp.zeros((d, n), w.dtype)
    kra = jnp.concatenate([z(MLA_NOPE), kr, z(128 - MLA_NOPE - MLA_ROPE)], axis=1)
    krb = jnp.concatenate([z(MLA_NOPE), _rot_cols(kr, MLA_ROPE // 2), z(128 - MLA_NOPE - MLA_ROPE)], axis=1)
    packed = jnp.concatenate([rw, dq, dk, dv, _rot_cols(dq, HEAD_DIM // 2), _rot_cols(dk, HEAD_DIM // 2),
                              hg, cq, ckv, kra, krb], axis=1)
    return packed.astype(BF16)


def pack_mla(w_uq, w_ukv):
    rq = w_uq.shape[0]
    q = w_uq.reshape(rq, N_HEADS, MLA_NOPE + MLA_ROPE)
    nope, rope = q[..., :MLA_NOPE], q[..., MLA_NOPE:]
    pad = jnp.zeros((rq, N_HEADS, 128 - MLA_NOPE - MLA_ROPE), w_uq.dtype)
    wq1 = jnp.concatenate([nope, rope, pad], axis=-1).reshape(rq, N_HEADS * 128)
    rrot = _rot_cols(rope.reshape(rq, N_HEADS * MLA_ROPE), MLA_ROPE // 2).reshape(rq, N_HEADS, MLA_ROPE)
    wq2 = jnp.concatenate([jnp.zeros_like(nope), rrot, pad], axis=-1).reshape(rq, N_HEADS * 128)
    rk = w_ukv.shape[0]
    kv = w_ukv.reshape(rk, N_HEADS, MLA_NOPE + HEAD_DIM)
    wk = jnp.concatenate([kv[..., :MLA_NOPE], jnp.zeros((rk, N_HEADS, 128 - MLA_NOPE), w_ukv.dtype)],
                         axis=-1).reshape(rk, N_HEADS * 128)
    wv = kv[..., MLA_NOPE:].reshape(rk, N_HEADS * HEAD_DIM)
    return wq1.astype(BF16), wq2.astype(BF16), wk.astype(BF16), wv.astype(BF16)


def kernel(x, positions, mix_norm, w_in, w_out, rwkv_mu, rwkv_w0, rwkv_w2, rwkv_a0, rwkv_a2, rwkv_g2, rwkv_k_k,
           rwkv_k_a, rwkv_r_k, rwkv_ln_w, rwkv_ln_b, mla_q_norm, mla_w_uq, mla_kv_norm, mla_w_ukv, hgrn_lb_logits,
           hgrn_g_norm, ffn_norm, ffn_w1, ffn_w3, ffn_w2, moe_router, moe_w1, moe_w3, moe_w2, final_norm):
    b, t, d = x.shape
    n = b * t
    depth = w_in.shape[0]
    x2 = x.reshape(n, d)
    cd, sd, cm, sm = rope_tables(positions.reshape(n, 1))
    lora_pad = jnp.zeros((rwkv_w2.shape[1], D_GROUP), F32)
    fused_final = False
    for layer in range(depth):
        cols = in_proj(x2, mix_norm[layer], pack_w_in(w_in[layer]))
        cols3 = cols.reshape(b, t, N_PACK)
        prep = rwkv_prep(cols3, rwkv_mu[layer], rwkv_w0[layer],
                         jnp.concatenate([rwkv_w2[layer], lora_pad], axis=0).astype(BF16), rwkv_a0[layer],
                         jnp.concatenate([lora_pad, rwkv_a2[layer]], axis=0).astype(BF16),
                         rwkv_g2[layer].astype(BF16), rwkv_k_k[layer], rwkv_k_a[layer], rwkv_r_k[layer])
        y_a = rwkv_mix(prep, rwkv_ln_w[layer], rwkv_ln_b[layer])
        q_d, k_d = dil_prep(cols, cd, sd)
        y_b = dilated_attention(q_d.reshape(b, t, D_GROUP), k_d.reshape(b, t, D_GROUP), cols3)
        q_m, k_m, v_m = mla_prep(cols, cm, sm, mla_q_norm[layer], mla_kv_norm[layer],
                                 *pack_mla(mla_w_uq[layer], mla_w_ukv[layer]))
        y_c = mla_flash(q_m.reshape(b, t, -1), k_m.reshape(b, t, -1), v_m.reshape(b, t, -1))
        y_d = hgrn(cols3, hgrn_lb_logits, hgrn_g_norm[layer], layer)
        ys = [y.reshape(n, D_GROUP) for y in (y_a, y_b, y_c, y_d)]
        x2, h = out_proj(x2, ys, w_out[layer].astype(BF16), ffn_norm[layer])
        j = layer // 2
        if layer % 2 == 0:
            x2 = ffn_dense(h, x2, ffn_w1[j].astype(BF16), ffn_w3[j].astype(BF16), ffn_w2[j].astype(BF16))
        else:
            route, hp = moe_route(x2, ffn_norm[layer], jnp.pad(moe_router[j], ((0, 0), (0, 128 - N_EXPERTS))))
            rows_pad = TOP_K * n + N_EXPERTS * MOE_TILE
            pos, tile_expert, tile_valid = moe_plan(route, rows_pad)
            xs = moe_scatter(pos, hp, rows_pad)
            ys = moe_experts(tile_expert, tile_valid, xs, moe_w1[j].astype(BF16), moe_w3[j].astype(BF16),
                             moe_w2[j].astype(BF16))
            fused_final = layer == depth - 1
            x2 = moe_combine(pos, x2, route, ys, final_norm, fused_final)
    if not fused_final:
        x2 = final_rms(x2, final_norm)
    return x2.reshape(b, t, d)
```

```python
import functools

import jax
import jax.numpy as jnp
from jax import lax
from jax.experimental import pallas as pl
from jax.experimental.pallas import tpu as pltpu

F32 = jnp.float32
BF16 = jnp.bfloat16

D_GROUP = 256
HEAD_DIM = 64
N_HEADS = 4
RMS_EPS = 1e-6
RWKV_LN_EPS = 64e-5
ROPE_THETA = 10000.0
NEG_INF = -1e30
N_EXPERTS = 8
MLA_ROPE = 32
MLA_NOPE = 64

VMEM_LIMIT = 56 * 1024 * 1024

C_RWKV = 0
C_DIL = 1024
C_HGRN = 1792
C_MLA = 2816
N_PACK = 3456


def _cparams(sem, vmem=VMEM_LIMIT):
    return pltpu.CompilerParams(dimension_semantics=sem, vmem_limit_bytes=vmem)


def _rms(x, w):
    return x * lax.rsqrt(jnp.mean(x * x, axis=-1, keepdims=True) + RMS_EPS) * w


def _dot(a, b):
    return jnp.dot(a, b, preferred_element_type=F32)


def _dot_nt(a, b):
    return lax.dot_general(a, b, (((1,), (1,)), ((), ())), preferred_element_type=F32)


def _split_dot(a_f32, b_bf16):
    hi = a_f32.astype(BF16)
    lo = (a_f32 - hi.astype(F32)).astype(BF16)
    return _dot(hi, b_bf16) + _dot(lo, b_bf16)


def _split_dot_lhs(a_bf16, b_f32):
    hi = b_f32.astype(BF16)
    lo = (b_f32 - hi.astype(F32)).astype(BF16)
    return _dot(a_bf16, hi) + _dot(a_bf16, lo)


def _rope_kernel(pos_ref, cd_ref, sd_ref, cm_ref, sm_ref):
    pos = pos_ref[...].astype(F32)
    tm = pos.shape[0]
    lane = lax.broadcasted_iota(jnp.int32, (tm, 128), 1)
    j = (lane & 31).astype(F32)
    inv = jnp.exp(j * (-2.0 / HEAD_DIM * jnp.log(ROPE_THETA)))
    ang = pos * inv
    cos_d = jnp.cos(ang)
    sin_d = jnp.sin(ang)
    cd_ref[...] = jnp.concatenate([cos_d, cos_d], axis=1)
    sd_ref[...] = jnp.concatenate([sin_d, sin_d], axis=1)
    jm = (lane & 15).astype(F32)
    invm = jnp.exp(jm * (-2.0 / MLA_ROPE * jnp.log(ROPE_THETA)))
    angm = pos * invm
    rope_lane = (lane >= MLA_NOPE) & (lane < MLA_NOPE + MLA_ROPE)
    cm_ref[...] = jnp.where(rope_lane, jnp.cos(angm), jnp.where(lane < MLA_NOPE, 1.0, 0.0))
    sm_ref[...] = jnp.where(rope_lane, jnp.sin(angm), 0.0)


def rope_tables(pos_col, tm=512):
    n = pos_col.shape[0]
    return pl.pallas_call(
        _rope_kernel,
        grid=(n // tm,),
        in_specs=[pl.BlockSpec((tm, 1), lambda i: (i, 0))],
        out_specs=[pl.BlockSpec((tm, 256), lambda i: (i, 0)), pl.BlockSpec((tm, 256), lambda i: (i, 0)),
                   pl.BlockSpec((tm, 128), lambda i: (i, 0)), pl.BlockSpec((tm, 128), lambda i: (i, 0))],
        out_shape=[jax.ShapeDtypeStruct((n, 256), F32), jax.ShapeDtypeStruct((n, 256), F32),
                   jax.ShapeDtypeStruct((n, 128), F32), jax.ShapeDtypeStruct((n, 128), F32)],
        compiler_params=_cparams(("parallel",)),
        name="rope_tables",
    )(pos_col)


def _in_proj_kernel(x_ref, nw_ref, w_ref, o_ref):
    h = _rms(x_ref[...], nw_ref[...]).astype(BF16)
    o_ref[...] = _dot(h, w_ref[...])


def in_proj(x2, norm_w, w_packed, tm=512):
    n, d = x2.shape
    nc = w_packed.shape[1]
    return pl.pallas_call(
        _in_proj_kernel,
        grid=(n // tm,),
        in_specs=[pl.BlockSpec((tm, d), lambda i: (i, 0)),
                  pl.BlockSpec((1, d), lambda i: (0, 0)),
                  pl.BlockSpec((d, nc), lambda i: (0, 0))],
        out_specs=pl.BlockSpec((tm, nc), lambda i: (i, 0)),
        out_shape=jax.ShapeDtypeStruct((n, nc), F32),
        compiler_params=_cparams(("parallel",)),
        name="in_proj",
    )(x2, norm_w.reshape(1, d), w_packed)


def _out_proj_kernel(x_ref, ya_ref, yb_ref, yc_ref, yd_ref, w_ref, nw_ref, xo_ref, h_ref):
    acc = x_ref[...]
    for g, y_ref in enumerate((ya_ref, yb_ref, yc_ref, yd_ref)):
        acc = acc + _dot(y_ref[...].astype(BF16), w_ref[g * D_GROUP:(g + 1) * D_GROUP, :])
    xo_ref[...] = acc
    h_ref[...] = _rms(acc, nw_ref[...]).astype(BF16)


def out_proj(x2, ys, w_out_bf16, norm_w, tm=512):
    n, d = x2.shape
    yspec = pl.BlockSpec((tm, D_GROUP), lambda i: (i, 0))
    return pl.pallas_call(
        _out_proj_kernel,
        grid=(n // tm,),
        in_specs=[pl.BlockSpec((tm, d), lambda i: (i, 0)), yspec, yspec, yspec, yspec,
                  pl.BlockSpec((d, d), lambda i: (0, 0)),
                  pl.BlockSpec((1, d), lambda i: (0, 0))],
        out_specs=[pl.BlockSpec((tm, d), lambda i: (i, 0)), pl.BlockSpec((tm, d), lambda i: (i, 0))],
        out_shape=[jax.ShapeDtypeStruct((n, d), F32), jax.ShapeDtypeStruct((n, d), BF16)],
        compiler_params=_cparams(("parallel",)),
        name="out_proj",
    )(x2, *ys, w_out_bf16, norm_w.reshape(1, d))


def _ffn_kernel(h_ref, x_ref, w1_ref, w3_ref, w2_ref, o_ref, acc_ref):
    j = pl.program_id(1)

    @pl.when(j == 0)
    def _():
        acc_ref[...] = x_ref[...]

    h = h_ref[...]
    a = _dot(h, w1_ref[...])
    b = _dot(h, w3_ref[...])
    act = (a * jax.nn.sigmoid(a) * b).astype(BF16)
    acc_ref[...] += _dot(act, w2_ref[...])

    @pl.when(j == pl.num_programs(1) - 1)
    def _():
        o_ref[...] = acc_ref[...]


def ffn_dense(h_bf16, x2, w1, w3, w2, tm=512, tf=1408):
    n, d = x2.shape
    f = w1.shape[1]
    return pl.pallas_call(
        _ffn_kernel,
        grid=(n // tm, f // tf),
        in_specs=[pl.BlockSpec((tm, d), lambda i, j: (i, 0)),
                  pl.BlockSpec((tm, d), lambda i, j: (i, 0)),
                  pl.BlockSpec((d, tf), lambda i, j: (0, j)),
                  pl.BlockSpec((d, tf), lambda i, j: (0, j)),
                  pl.BlockSpec((tf, d), lambda i, j: (j, 0))],
        out_specs=pl.BlockSpec((tm, d), lambda i, j: (i, 0)),
        out_shape=jax.ShapeDtypeStruct((n, d), F32),
        scratch_shapes=[pltpu.VMEM((tm, d), F32)],
        compiler_params=_cparams(("parallel", "arbitrary")),
        name="ffn_dense",
    )(h_bf16, x2, w1, w3, w2)


MOE_TILE = 512
TOP_K = 2


def _router_kernel(x_ref, nw_ref, wr_ref, route_ref, hp_ref):
    h = _rms(x_ref[...], nw_ref[...])
    logits = jnp.dot(h, wr_ref[...], preferred_element_type=F32, precision=lax.Precision.HIGHEST)
    lane = lax.broadcasted_iota(jnp.int32, logits.shape, 1)
    logits = jnp.where(lane < N_EXPERTS, logits, -jnp.inf)
    m1 = jnp.max(logits, axis=-1, keepdims=True)
    i1 = jnp.min(jnp.where(logits == m1, lane, 128), axis=-1, keepdims=True)
    rest = jnp.where(lane == i1, -jnp.inf, logits)
    m2 = jnp.max(rest, axis=-1, keepdims=True)
    i2 = jnp.min(jnp.where(rest == m2, lane, 128), axis=-1, keepdims=True)
    e2 = jnp.exp(m2 - m1)
    g1 = 1.0 / (1.0 + e2)
    g2 = e2 * g1
    route_ref[...] = jnp.where(lane == 0, i1.astype(F32), jnp.where(lane == 1, i2.astype(F32),
                               jnp.where(lane == 2, g1, jnp.where(lane == 3, g2, 0.0))))
    bits = pltpu.bitcast(h.astype(BF16).astype(F32), jnp.uint32)
    half = bits.shape[1] // 2
    hp_ref[...] = (bits[:, :half] >> 16) | (bits[:, half:] & jnp.uint32(0xFFFF0000))


def moe_route(x2, norm_w, w_router_pad, tm=512):
    n, d = x2.shape
    return pl.pallas_call(
        _router_kernel,
        grid=(n // tm,),
        in_specs=[pl.BlockSpec((tm, d), lambda i: (i, 0)),
                  pl.BlockSpec((1, d), lambda i: (0, 0)),
                  pl.BlockSpec((d, 128), lambda i: (0, 0))],
        out_specs=[pl.BlockSpec((tm, 128), lambda i: (i, 0)), pl.BlockSpec((tm, d // 2), lambda i: (i, 0))],
        out_shape=[jax.ShapeDtypeStruct((n, 128), F32), jax.ShapeDtypeStruct((n, d // 2), jnp.uint32)],
        compiler_params=_cparams(("parallel",)),
        name="moe_router",
    )(x2, norm_w.reshape(1, d), w_router_pad)


def moe_plan(route, n_rows_pad):
    e = route[:, :TOP_K].astype(jnp.int32)
    onehot = (e.reshape(-1, 1) == jnp.arange(N_EXPERTS, dtype=jnp.int32)[None, :]).astype(jnp.int32)
    csum = jnp.cumsum(onehot, axis=0)
    counts = csum[-1]
    rank = jnp.sum((csum - onehot) * onehot, axis=1)
    gsz = ((counts + MOE_TILE - 1) // MOE_TILE) * MOE_TILE
    gend = jnp.cumsum(gsz)
    goff = gend - gsz
    pos = (goff[e.reshape(-1)] + rank).astype(jnp.int32)
    tile_start = jnp.arange(n_rows_pad // MOE_TILE, dtype=jnp.int32) * MOE_TILE
    tile_expert = jnp.minimum(jnp.sum((tile_start[:, None] >= gend[None, :]).astype(jnp.int32), axis=1),
                              N_EXPERTS - 1).astype(jnp.int32)
    tile_valid = (tile_start < gend[-1]).astype(jnp.int32)
    return pos, tile_expert, tile_valid


def _moe_scatter_kernel(pos_ref, hp_ref, xs_in_ref, xs_ref, sem):
    del xs_in_ref
    i = pl.program_id(0)
    ts = hp_ref.shape[0]

    def copy(r, slot):
        dst = pos_ref[(i * ts + r) * TOP_K + slot]
        return pltpu.make_async_copy(hp_ref.at[pl.ds(r, 1), :], xs_ref.at[pl.ds(dst, 1), :], sem)

    def issue(r, carry):
        for slot in range(TOP_K):
            copy(r, slot).start(priority=slot)
        return carry

    lax.fori_loop(0, ts, issue, 0, unroll=8)
    for slot in range(TOP_K):
        pltpu.make_async_copy(hp_ref, xs_ref.at[pl.ds(0, ts), :], sem).wait()


def moe_scatter(pos, hp, n_rows_pad, ts=256):
    n, dh = hp.shape
    xs0 = jnp.zeros((n_rows_pad, dh), jnp.uint32)
    return pl.pallas_call(
        _moe_scatter_kernel,
        grid_spec=pltpu.PrefetchScalarGridSpec(
            num_scalar_prefetch=1,
            grid=(n // ts,),
            in_specs=[pl.BlockSpec((ts, dh), lambda i, pos_: (i, 0)),
                      pl.BlockSpec(memory_space=pl.ANY)],
            out_specs=pl.BlockSpec(memory_space=pl.ANY),
            scratch_shapes=[pltpu.SemaphoreType.DMA],
        ),
        out_shape=jax.ShapeDtypeStruct((n_rows_pad, dh), jnp.uint32),
        input_output_aliases={2: 0},
        compiler_params=_cparams(("arbitrary",)),
        name="moe_scatter",
    )(pos, hp, xs0)


def _moe_expert_kernel(te_ref, tv_ref, xs_ref, w1_ref, w3_ref, w2_ref, ys_ref, h_ref, acc_ref):
    i = pl.program_id(0)
    j = pl.program_id(1)

    @pl.when(tv_ref[i] > 0)
    def _():
        @pl.when(j == 0)
        def _():
            p = xs_ref[...]
            lo = pltpu.bitcast(p << 16, F32)
            hi = pltpu.bitcast(p & jnp.uint32(0xFFFF0000), F32)
            h_ref[...] = jnp.concatenate([lo, hi], axis=1).astype(BF16)
            acc_ref[...] = jnp.zeros_like(acc_ref)

        h = h_ref[...]
        a = _dot(h, w1_ref[...])
        b = _dot(h, w3_ref[...])
        act = (a * jax.nn.sigmoid(a) * b).astype(BF16)
        acc_ref[...] += _dot(act, w2_ref[...])

        @pl.when(j == pl.num_programs(1) - 1)
        def _():
            ys_ref[...] = acc_ref[...]

    @pl.when((tv_ref[i] == 0) & (j == pl.num_programs(1) - 1))
    def _():
        ys_ref[...] = jnp.zeros_like(ys_ref)


def moe_experts(tile_expert, tile_valid, xs, w1, w3, w2, tf=1792):
    rows, dh = xs.shape
    d = 2 * dh
    f = w1.shape[2]
    tm = MOE_TILE
    return pl.pallas_call(
        _moe_expert_kernel,
        grid_spec=pltpu.PrefetchScalarGridSpec(
            num_scalar_prefetch=2,
            grid=(rows // tm, f // tf),
            in_specs=[pl.BlockSpec((tm, dh), lambda i, j, te, tv: (i, 0)),
                      pl.BlockSpec((None, d, tf), lambda i, j, te, tv: (te[i], 0, j * tv[i])),
                      pl.BlockSpec((None, d, tf), lambda i, j, te, tv: (te[i], 0, j * tv[i])),
                      pl.BlockSpec((None, tf, d), lambda i, j, te, tv: (te[i], j * tv[i], 0))],
            out_specs=pl.BlockSpec((tm, d), lambda i, j, te, tv: (i, 0)),
            scratch_shapes=[pltpu.VMEM((tm, d), BF16), pltpu.VMEM((tm, d), F32)],
        ),
        out_shape=jax.ShapeDtypeStruct((rows, d), F32),
        compiler_params=_cparams(("arbitrary", "arbitrary")),
        name="moe_experts",
    )(tile_expert, tile_valid, xs, w1, w3, w2)


def _moe_combine_kernel(pos_ref, x_ref, route_ref, nw_ref, ys_ref, o_ref, buf_ref, sem, *, final):
    i = pl.program_id(0)
    tc = x_ref.shape[0]

    def copy(r, slot):
        src = pos_ref[(i * tc + r) * TOP_K + slot]
        return pltpu.make_async_copy(ys_ref.at[pl.ds(src, 1), :], buf_ref.at[slot, pl.ds(r, 1), :], sem)

    def issue(r, carry):
        for slot in range(TOP_K):
            copy(r, slot).start(priority=slot)
        return carry

    lax.fori_loop(0, tc, issue, 0, unroll=8)
    for slot in range(TOP_K):
        pltpu.make_async_copy(ys_ref.at[pl.ds(0, tc), :], buf_ref.at[slot], sem).wait()
    route = route_ref[...]
    g1 = route[:, 2:3]
    g2 = route[:, 3:4]
    out = x_ref[...] + g1 * buf_ref[0] + g2 * buf_ref[1]
    if final:
        out = _rms(out, nw_ref[...])
    o_ref[...] = out


def moe_combine(pos, x2, route, ys, norm_w, final, tc=256):
    n, d = x2.shape
    return pl.pallas_call(
        functools.partial(_moe_combine_kernel, final=final),
        grid_spec=pltpu.PrefetchScalarGridSpec(
            num_scalar_prefetch=1,
            grid=(n // tc,),
            in_specs=[pl.BlockSpec((tc, d), lambda i, pos_: (i, 0)),
                      pl.BlockSpec((tc, 128), lambda i, pos_: (i, 0)),
                      pl.BlockSpec((1, d), lambda i, pos_: (0, 0)),
                      pl.BlockSpec(memory_space=pl.ANY)],
            out_specs=pl.BlockSpec((tc, d), lambda i, pos_: (i, 0)),
            scratch_shapes=[pltpu.VMEM((TOP_K, tc, d), F32), pltpu.SemaphoreType.DMA],
        ),
        out_shape=jax.ShapeDtypeStruct((n, d), F32),
        compiler_params=_cparams(("arbitrary",)),
        name="moe_combine",
    )(pos, x2, route, norm_w.reshape(1, d), ys)


def _final_norm_kernel(x_ref, nw_ref, o_ref):
    o_ref[...] = _rms(x_ref[...], nw_ref[...])


def final_rms(x2, norm_w, tm=1024):
    n, d = x2.shape
    return pl.pallas_call(
        _final_norm_kernel,
        grid=(n // tm,),
        in_specs=[pl.BlockSpec((tm, d), lambda i: (i, 0)), pl.BlockSpec((1, d), lambda i: (0, 0))],
        out_specs=pl.BlockSpec((tm, d), lambda i: (i, 0)),
        out_shape=jax.ShapeDtypeStruct((n, d), F32),
        compiler_params=_cparams(("parallel",)),
        name="final_norm",
    )(x2, norm_w.reshape(1, d))


def _mla_prep_kernel(cq_ref, ckv_ref, kra_ref, krb_ref, cm_ref, sm_ref, qn_ref, kvn_ref,
                     wq1_ref, wq2_ref, wk_ref, wv_ref, q_ref, k_ref, v_ref):
    qn = _rms(cq_ref[...], qn_ref[...]).astype(BF16)
    kvn = _rms(ckv_ref[...], kvn_ref[...]).astype(BF16)
    cm = cm_ref[...]
    sm = sm_ref[...]
    cm4 = jnp.concatenate([cm] * N_HEADS, axis=1)
    sm4 = jnp.concatenate([sm] * N_HEADS, axis=1)
    scale = (MLA_NOPE + MLA_ROPE) ** -0.5
    q = (_dot(qn, wq1_ref[...]) * cm4 + _dot(qn, wq2_ref[...]) * sm4) * scale
    q_ref[...] = q.astype(BF16)
    kr = kra_ref[...] * cm + krb_ref[...] * sm
    k = _dot(kvn, wk_ref[...]) + jnp.concatenate([kr] * N_HEADS, axis=1)
    k_ref[...] = k.astype(BF16)
    lane = lax.broadcasted_iota(jnp.int32, (cm.shape[0], N_HEADS * 128), 1)
    ones = jnp.where(((lane >> 6) & 1) == (((lane >> 7) & 1) ^ 1), 1.0, 0.0)
    v_ref[...] = (_dot(kvn, wv_ref[...]) + ones).astype(BF16)


def mla_prep(cols, cm, sm, q_norm, kv_norm, wq1, wq2, wk, wv, tm=512):
    n = cols.shape[0]
    b256 = C_MLA // 256
    b128 = C_MLA // 128
    full = lambda shape: pl.BlockSpec(shape, lambda i: (0, 0))
    return pl.pallas_call(
        _mla_prep_kernel,
        grid=(n // tm,),
        in_specs=[pl.BlockSpec((tm, 256), lambda i: (i, b256)),
                  pl.BlockSpec((tm, 128), lambda i: (i, b128 + 2)),
                  pl.BlockSpec((tm, 128), lambda i: (i, b128 + 3)),
                  pl.BlockSpec((tm, 128), lambda i: (i, b128 + 4)),
                  pl.BlockSpec((tm, 128), lambda i: (i, 0)),
                  pl.BlockSpec((tm, 128), lambda i: (i, 0)),
                  full((1, 256)), full((1, 128)),
                  full((256, 512)), full((256, 512)), full((128, 512)), full((128, 512))],
        out_specs=[pl.BlockSpec((tm, 512), lambda i: (i, 0))] * 3,
        out_shape=[jax.ShapeDtypeStruct((n, 512), BF16)] * 3,
        compiler_params=_cparams(("parallel",)),
        name="mla_prep",
    )(cols, cols, cols, cols, cm, sm, q_norm.reshape(1, 256), kv_norm.reshape(1, 128), wq1, wq2, wk, wv)


def _mla_flash_kernel(q_ref, k_ref, v_ref, o_ref, *, tq, tk):
    i = pl.program_id(2)
    lane = lax.broadcasted_iota(jnp.int32, (tq, 128), 1)
    qpos = i * tq + lax.broadcasted_iota(jnp.int32, (tq, tk), 0)
    kloc = lax.broadcasted_iota(jnp.int32, (tq, tk), 1)
    nfull = (i * tq) // tk
    qs = [q_ref[:, h * 128:(h + 1) * 128] for h in range(2)]

    def step(j, carry, masked):
        ks = pl.multiple_of(j * tk, tk)
        ss = [_dot_nt(qs[h], k_ref[pl.ds(ks, tk), h * 128:(h + 1) * 128]) for h in range(2)]
        if masked:
            mask = ks + kloc <= qpos
            ss = [jnp.where(mask, s, NEG_INF) for s in ss]
        mns = [jnp.maximum(carry[h][0], jnp.max(ss[h], axis=-1, keepdims=True)) for h in range(2)]
        ps = [jnp.exp((ss[h] - mns[h]).astype(BF16)) for h in range(2)]
        als = [jnp.exp(carry[h][0] - mns[h]) for h in range(2)]
        pvs = [_dot(ps[h], v_ref[pl.ds(ks, tk), h * 128:(h + 1) * 128]) for h in range(2)]
        return tuple((mns[h], carry[h][1] * als[h] + pvs[h]) for h in range(2))

    init = (jnp.full((tq, 1), NEG_INF, F32), jnp.zeros((tq, 128), F32))
    res = lax.fori_loop(0, nfull, functools.partial(step, masked=False), (init, init))
    res = step(nfull, res, True)
    acc = jnp.where(lane < HEAD_DIM, res[0][1], res[1][1])
    den = jnp.where(lane < HEAD_DIM, pltpu.roll(res[0][1], HEAD_DIM, 1), pltpu.roll(res[1][1], HEAD_DIM, 1))
    o_ref[...] = acc / den


def mla_flash(q3, k3, v3, tq=256, tk=512):
    b, t, _ = q3.shape
    assert tk % tq == 0 and t % tk == 0
    return pl.pallas_call(
        functools.partial(_mla_flash_kernel, tq=tq, tk=tk),
        grid=(b, 2, t // tq),
        in_specs=[pl.BlockSpec((None, tq, 256), lambda b_, p, i: (b_, i, p)),
                  pl.BlockSpec((None, t, 256), lambda b_, p, i: (b_, 0, p)),
                  pl.BlockSpec((None, t, 256), lambda b_, p, i: (b_, 0, p))],
        out_specs=pl.BlockSpec((None, tq, 128), lambda b_, p, i: (b_, i, p)),
        out_shape=jax.ShapeDtypeStruct((b, t, 256), F32),
        compiler_params=_cparams(("parallel", "parallel", "arbitrary")),
        name="mla_flash",
    )(q3, k3, v3)


DILATED_PATTERNS = ((128, 1), (512, 4), (2048, 16))
DIL_BLOCK = 128
DIL_UNROLL = 4


def _rotate_half_heads(x):
    lane = lax.broadcasted_iota(jnp.int32, x.shape, 1)
    width = x.shape[1]
    half = HEAD_DIM // 2
    return jnp.where((lane & (HEAD_DIM - 1)) < half, -pltpu.roll(x, width - half, 1), pltpu.roll(x, half, 1))


def _dil_prep_kernel(q_ref, k_ref, cd_ref, sd_ref, qo_ref, ko_ref):
    cd = cd_ref[...]
    sd = sd_ref[...]
    q = q_ref[...]
    k = k_ref[...]
    qo_ref[...] = (q * cd + _rotate_half_heads(q) * sd) * (HEAD_DIM ** -0.5)
    ko_ref[...] = k * cd + _rotate_half_heads(k) * sd


def dil_prep(cols, cd, sd, tm=1024):
    n = cols.shape[0]
    b0 = C_DIL // 256
    spec = lambda j: pl.BlockSpec((tm, 256), lambda i: (i, j))
    return pl.pallas_call(
        _dil_prep_kernel,
        grid=(n // tm,),
        in_specs=[spec(b0), spec(b0 + 1), spec(0), spec(0)],
        out_specs=[spec(0), spec(0)],
        out_shape=[jax.ShapeDtypeStruct((n, 256), F32), jax.ShapeDtypeStruct((n, 256), F32)],
        compiler_params=_cparams(("parallel",)),
        name="dil_prep",
    )(cols, cols, cd, sd)


def _dil_block(qb, kb, vb, mask, out):
    nq = qb.shape[0]
    lane = lax.broadcasted_iota(jnp.int32, (nq, 128), 1)
    kbb = kb.astype(BF16)
    vbb = vb.astype(BF16)
    hms = [(lane >> 6) == h for h in range(2)]
    ss = [jnp.where(mask, _dot_nt(jnp.where(hm, qb, 0.0).astype(BF16), kbb), NEG_INF) for hm in hms]
    yield
    ms = [jnp.max(s, axis=-1, keepdims=True) for s in ss]
    ps = [jnp.exp(s - m) for s, m in zip(ss, ms)]
    yield
    ls = [jnp.sum(p, axis=-1, keepdims=True) for p in ps]
    ohs = [_dot(p.astype(BF16), vbb) for p in ps]
    yield
    out[0] = jnp.where(hms[0], ohs[0] / ls[0], ohs[1] / ls[1])
    out[1] = jnp.where(hms[0], ms[0] + jnp.log(ls[0]), ms[1] + jnp.log(ls[1]))


def _dil_kernel(q_ref, k_ref, v_ref, o_ref, m_ref, z_ref, acc_ref, *, t):
    blk = DIL_BLOCK
    for pi, (window, dil) in enumerate(DILATED_PATTERNS):
        assert window // dil == blk
        length = t // dil
        nb = length // blk
        nk = 2 * blk if nb > 1 else blk
        uq = lax.broadcasted_iota(jnp.int32, (blk, nk), 0)
        uk = lax.broadcasted_iota(jnp.int32, (blk, nk), 1)

        def one(it, dil=dil, nb=nb, nk=nk, uq=uq, uk=uk, first=(pi == 0)):
            r = it // nb
            jb = it % nb
            kb0 = jnp.maximum(jb - 1, 0) * blk
            qs = r + dil * blk * jb
            ks = r + dil * kb0
            if dil == 1:
                qidx = pl.ds(pl.multiple_of(qs, blk), blk)
                kidx = pl.ds(pl.multiple_of(ks, blk), nk)
            else:
                qidx = pl.ds(qs, blk, stride=dil)
                kidx = pl.ds(ks, nk, stride=dil)
            dist = (jb * blk + uq) - (kb0 + uk)
            mask = (dist >= 0) & (dist <= blk)
            out = [None, None]
            yield from _dil_block(q_ref[qidx, :], k_ref[kidx, :], v_ref[kidx, :], mask, out)
            o, lse = out
            if first:
                m_ref[qidx, :] = lse
                z_ref[qidx, :] = jnp.ones_like(lse)
                acc_ref[qidx, :] = o
            else:
                m_old = m_ref[qidx, :]
                m_new = jnp.maximum(m_old, lse)
                a = jnp.exp(m_old - m_new)
                b = jnp.exp(lse - m_new)
                m_ref[qidx, :] = m_new
                z_ref[qidx, :] = z_ref[qidx, :] * a + b
                acc_ref[qidx, :] = acc_ref[qidx, :] * a + o * b

        def body(it, carry, one=one):
            _round_robin([one(it * DIL_UNROLL + u) for u in range(DIL_UNROLL)])
            return carry

        assert (dil * nb) % DIL_UNROLL == 0
        lax.fori_loop(0, dil * nb // DIL_UNROLL, body, 0)
    o_ref[...] = acc_ref[...] / z_ref[...]


def dilated_attention(q3, k3, cols3):
    b, t, _ = q3.shape
    assert t % (DIL_BLOCK * DILATED_PATTERNS[-1][1]) == 0
    vblk = C_DIL // 128 + 4
    return pl.pallas_call(
        functools.partial(_dil_kernel, t=t),
        grid=(b, 2),
        in_specs=[pl.BlockSpec((None, t, 128), lambda i, p: (i, 0, p)),
                  pl.BlockSpec((None, t, 128), lambda i, p: (i, 0, p)),
                  pl.BlockSpec((None, t, 128), lambda i, p: (i, 0, vblk + p))],
        out_specs=pl.BlockSpec((None, t, 128), lambda i, p: (i, 0, p)),
        out_shape=jax.ShapeDtypeStruct((b, t, 256), F32),
        scratch_shapes=[pltpu.VMEM((t, 128), F32)] * 3,
        compiler_params=_cparams(("parallel", "parallel")),
        name="dilated_attention",
    )(q3, k3, cols3)


CHUNK = 64
STACK = N_HEADS * CHUNK


def _stack_heads(x):
    lane = lax.broadcasted_iota(jnp.int32, x.shape, 1)
    return jnp.concatenate([jnp.where((lane >> 6) == h, x, 0.0) for h in range(N_HEADS)], axis=0)


def _unstack_heads(y):
    out = y[0:CHUNK]
    for h in range(1, N_HEADS):
        out = out + y[h * CHUNK:(h + 1) * CHUNK]
    return out


def _round_robin(stages):
    live = list(stages)
    while live:
        nxt = []
        for g in live:
            try:
                next(g)
                nxt.append(g)
            except StopIteration:
                pass
        live = nxt


def _head_mean_matrix():
    r = lax.broadcasted_iota(jnp.int32, (256, 256), 0)
    c = lax.broadcasted_iota(jnp.int32, (256, 256), 1)
    return jnp.where((r >> 6) == (c >> 6), 1.0 / HEAD_DIM, 0.0).astype(BF16)


HGRN_HALVES = (32, 16, 8, 4, 2, 1)


def _hgrn_sum_matrix():
    t = lax.broadcasted_iota(jnp.int32, (CHUNK, CHUNK), 0)
    j = lax.broadcasted_iota(jnp.int32, (CHUNK, CHUNK), 1)
    tri = jnp.where(j <= t, 1.0, 0.0)
    blocks = [tri]
    for half in HGRN_HALVES:
        ref = (t // (2 * half)) * (2 * half) + half - 1
        blocks.append(tri - jnp.where(j <= ref, 1.0, 0.0))
    return jnp.concatenate(blocks, axis=0).astype(BF16)


def _hgrn_kernel(q_ref, f_ref, i_ref, g_ref, lbl_ref, gn_ref, o_ref, st_ref, *, layer, nchunk, nb):
    @pl.when(pl.program_id(1) == 0)
    def _():
        st_ref[...] = jnp.zeros_like(st_ref)

    logits = lbl_ref[...]
    e = jnp.exp(logits - jnp.max(logits, axis=0, keepdims=True))
    sm = e / jnp.sum(e, axis=0, keepdims=True)
    lb = jnp.zeros((1, 256), F32)
    for l in range(1, layer + 1):
        lb = lb + sm[l:l + 1, :]
    gn = gn_ref[...]
    summat = _hgrn_sum_matrix()
    hmean = _head_mean_matrix()
    r = lax.broadcasted_iota(jnp.int32, (STACK, STACK), 0)
    c = lax.broadcasted_iota(jnp.int32, (STACK, STACK), 1)
    same_head = (r >> 6) == (c >> 6)
    rc = r & (CHUNK - 1)
    cc = c & (CHUNK - 1)

    def one(bi, rows):
        qv = q_ref[bi, rows, :]
        qq = qv * jax.nn.sigmoid(qv)
        forget = lb + (1.0 - lb) * jax.nn.sigmoid(f_ref[bi, rows, :])
        logf = jnp.log(forget)
        kk = 1.0 - forget
        v = i_ref[bi, rows, :]
        yield
        sums = _split_dot_lhs(summat, logf)
        b = sums[0:CHUNK]
        a = jnp.where(rc == cc, _dot_nt(_stack_heads(qq).astype(BF16), _stack_heads(kk).astype(BF16)), 0.0)
        yield
        for li, half in enumerate(HGRN_HALVES):
            d = sums[(li + 1) * CHUNK:(li + 2) * CHUNK]
            ql = qq * jnp.exp(jnp.minimum(d, 0.0))
            kl = kk * jnp.exp(jnp.minimum(-d, 0.0))
            m = ((rc // (2 * half)) == (cc // (2 * half))) & ((rc % (2 * half)) >= half) & ((cc % (2 * half)) < half)
            a = a + jnp.where(m, _dot_nt(_stack_heads(ql).astype(BF16), _stack_heads(kl).astype(BF16)), 0.0)
            yield
        o = _unstack_heads(_dot(a.astype(BF16), _stack_heads(v).astype(BF16)))
        st = st_ref[bi]
        o = o + _dot_nt((qq * jnp.exp(b)).astype(BF16), st.astype(BF16))
        yield
        b_end = b[CHUNK - 1:CHUNK, :]
        kbar = kk * jnp.exp(b_end - b)
        upd = lax.dot_general(v.astype(BF16), kbar.astype(BF16), (((0,), (0,)), ((), ())), preferred_element_type=F32)
        st_ref[bi] = st * jnp.exp(b_end) + jnp.where(same_head, upd, 0.0)
        ms = _split_dot(o * o, hmean)
        yield
        gv = g_ref[bi, rows, :]
        o_ref[bi, rows, :] = o * lax.rsqrt(ms + RMS_EPS) * gn * (gv * jax.nn.sigmoid(gv))

    def chunk(ci, carry):
        rows = pl.ds(pl.multiple_of(ci * CHUNK, CHUNK), CHUNK)
        _round_robin([one(bi, rows) for bi in range(nb)])
        return carry

    lax.fori_loop(0, nchunk, chunk, 0)


SEQ_PER_STEP = 4


def hgrn(cols3, lb_logits, g_norm, layer, tb=512):
    b, t, _ = cols3.shape
    nb = SEQ_PER_STEP if b % SEQ_PER_STEP == 0 else 1
    c0 = C_HGRN // 256
    spec = lambda j: pl.BlockSpec((nb, tb, 256), lambda b_, i: (b_, i, j))
    depth = lb_logits.shape[0]
    return pl.pallas_call(
        functools.partial(_hgrn_kernel, layer=layer, nchunk=tb // CHUNK, nb=nb),
        grid=(b // nb, t // tb),
        in_specs=[spec(c0), spec(c0 + 1), spec(c0 + 2), spec(c0 + 3),
                  pl.BlockSpec((depth, 256), lambda b_, i: (0, 0)),
                  pl.BlockSpec((1, 256), lambda b_, i: (0, 0))],
        out_specs=spec(0),
        out_shape=jax.ShapeDtypeStruct((b, t, 256), F32),
        scratch_shapes=[pltpu.VMEM((nb, STACK, STACK), F32)],
        compiler_params=_cparams(("parallel", "arbitrary")),
        name="hgrn",
    )(cols3, cols3, cols3, cols3, lb_logits, g_norm.reshape(1, 256))


def _head_sum_matrix():
    r = lax.broadcasted_iota(jnp.int32, (256, 256), 0)
    c = lax.broadcasted_iota(jnp.int32, (256, 256), 1)
    return jnp.where((r >> 6) == (c >> 6), 1.0, 0.0).astype(BF16)


def _rwkv_prep_kernel(c_ref, p_ref, mu_ref, w0_ref, w2_ref, a0_ref, a2_ref, g2_ref, kk_ref, ka_ref, rk_ref,
                      r_o, lw_o, k_o, v_o, kap_o, at_o, bonus_o, g_o):
    i = pl.program_id(1)
    a = c_ref[...]
    tm = a.shape[0]
    last_prev = jnp.where(i > 0, p_ref[7:8, :], 0.0)
    row = lax.broadcasted_iota(jnp.int32, a.shape, 0)
    prev = jnp.where(row == 0, last_prev, pltpu.roll(a, 1, 0))
    xs = a + (prev - a) * mu_ref[...]
    r = xs[:, 0:256]
    k = xs[:, 256:512]
    v = xs[:, 512:768]
    xwa = xs[:, 768:896]
    xg = xs[:, 896:1024]
    w_in = w0_ref[...] + _dot(jnp.tanh(xwa).astype(BF16), w2_ref[...])
    z = -w_in
    softplus = jnp.maximum(z, 0.0) + jnp.log(1.0 + jnp.exp(-jnp.abs(z)))
    lw_o[...] = -jnp.exp(-softplus - 0.5)
    lr = jax.nn.sigmoid(a0_ref[...] + _dot(xwa.astype(BF16), a2_ref[...]))
    g_o[...] = _dot(jax.nn.sigmoid(xg).astype(BF16), g2_ref[...])
    hsum = _head_sum_matrix()
    kk = k * kk_ref[...]
    norm = jnp.sqrt(_split_dot(kk * kk, hsum))
    kk = kk / jnp.maximum(norm, 1e-12)
    k2 = k * (1.0 + (lr - 1.0) * ka_ref[...])
    r_o[...] = r
    k_o[...] = k2
    v_o[...] = v
    kap_o[...] = kk
    at_o[...] = kk * lr
    bonus_o[...] = _split_dot(r * k2 * rk_ref[...], hsum) * v


def rwkv_prep(cols3, mu, w0, w2p, a0, a2p, g2, k_k, k_a, r_k, tm=512):
    b, t, _ = cols3.shape
    full = lambda shape: pl.BlockSpec(shape, lambda b_, i: (0, 0))
    ospec = pl.BlockSpec((None, tm, 256), lambda b_, i: (b_, i, 0))
    row = lambda x: x.reshape(1, -1)
    return pl.pallas_call(
        _rwkv_prep_kernel,
        grid=(b, t // tm),
        in_specs=[pl.BlockSpec((None, tm, 1024), lambda b_, i: (b_, i, 0)),
                  pl.BlockSpec((None, 8, 1024), lambda b_, i: (b_, jnp.maximum(i * (tm // 8) - 1, 0), 0)),
                  full((1, 1024)), full((1, 256)), full((128, 256)), full((1, 256)), full((128, 256)),
                  full((128, 256)), full((1, 256)), full((1, 256)), full((1, 256))],
        out_specs=[ospec] * 8,
        out_shape=[jax.ShapeDtypeStruct((b, t, 256), F32)] * 8,
        compiler_params=_cparams(("parallel", "parallel")),
        name="rwkv_prep",
    )(cols3, cols3, row(mu), row(w0), w2p, row(a0), a2p, g2, row(k_k), row(k_a), row(r_k))


RWKV_MERGE_HALVES = (2, 4, 8, 16, 32)


def _rwkv_kernel(r_ref, lw_ref, k_ref, v_ref, kap_ref, at_ref, bonus_ref, g_ref, lnw_ref, lnb_ref,
                 o_ref, ht_ref, *, nchunk, nb):
    @pl.when(pl.program_id(1) == 0)
    def _():
        ht_ref[...] = jnp.zeros_like(ht_ref)

    t = lax.broadcasted_iota(jnp.int32, (CHUNK, CHUNK), 0)
    j = lax.broadcasted_iota(jnp.int32, (CHUNK, CHUNK), 1)
    tri = jnp.where(j <= t, 1.0, 0.0).astype(BF16)
    hmean = _head_mean_matrix()
    rr = lax.broadcasted_iota(jnp.int32, (STACK, STACK), 0)
    cc = lax.broadcasted_iota(jnp.int32, (STACK, STACK), 1)
    same_head = (rr >> 6) == (cc >> 6)
    rc = rr & (CHUNK - 1)
    sc = cc & (CHUNK - 1)
    eye = jnp.where(rr == cc, 1.0, 0.0)
    lnw = lnw_ref[...]
    lnb = lnb_ref[...]

    def one(bi, rows):
        lw = lw_ref[bi, rows, :]
        lc = _split_dot_lhs(tri, lw)
        l_end = lc[CHUNK - 1:CHUNK, :]
        kap = kap_ref[bi, rows, :]
        at = at_ref[bi, rows, :]
        k2 = k_ref[bi, rows, :]
        v = v_ref[bi, rows, :]
        inv = jnp.exp(-lc)
        fwd = jnp.exp(l_end - lc)
        s_kap = _stack_heads(kap * jnp.exp(lc - lw)).astype(BF16)
        s_r32 = _stack_heads(r_ref[bi, rows, :] * jnp.exp(lc))
        s_r = s_r32.astype(BF16)
        s_a = _stack_heads(at * inv).astype(BF16)
        s_k = _stack_heads(k2 * inv).astype(BF16)
        s_v = _stack_heads(v).astype(BF16)
        strict = rc > sc
        incl = rc >= sc
        yield
        a_ka = jnp.where(strict, _dot_nt(s_kap, s_a), 0.0)
        a_kk = jnp.where(strict, _dot_nt(s_kap, s_k), 0.0).astype(BF16)
        m1 = ((rc >> 1) == (sc >> 1)) & strict
        tinv = eye - jnp.where(m1, a_ka, 0.0)
        yield
        a_ra = jnp.where(incl, _dot_nt(s_r, s_a), 0.0).astype(BF16)
        a_rk = jnp.where(incl, _dot_nt(s_r, s_k), 0.0).astype(BF16)
        for half in RWKV_MERGE_HALVES:
            m = ((rc // (2 * half)) == (sc // (2 * half))) & ((rc % (2 * half)) >= half) & ((sc % (2 * half)) < half)
            off = jnp.where(m, a_ka, 0.0).astype(BF16)
            tb = tinv.astype(BF16)
            x = _dot(off, tb).astype(BF16)
            yield
            tinv = tinv - _dot(tb, x)
            yield
        tb = tinv.astype(BF16)
        kt_st = _dot(tb, s_kap)
        w_st = _dot(a_kk, s_v).astype(BF16)
        yield
        vt_st = _dot(tb, w_st)
        rt = _unstack_heads(s_r32 - _dot(a_ra, kt_st.astype(BF16)))
        yield
        y0 = _unstack_heads(_dot(a_rk, s_v) - _dot(a_ra, vt_st.astype(BF16)))
        kt = _unstack_heads(kt_st)
        vt = _unstack_heads(vt_st)
        ht = ht_ref[bi]
        htb = ht.astype(BF16)
        yield
        y = y0 + _dot_nt(rt.astype(BF16), htb)
        u = vt + _dot_nt(kt.astype(BF16), htb)
        yield
        lhs = jnp.concatenate([v, u], axis=0).astype(BF16)
        rhs = jnp.concatenate([k2 * fwd, -(at * fwd)], axis=0).astype(BF16)
        upd = lax.dot_general(lhs, rhs, (((0,), (0,)), ((), ())), preferred_element_type=F32)
        ht_ref[bi] = ht * jnp.exp(l_end) + jnp.where(same_head, upd, 0.0)
        mean = _split_dot(y, hmean)
        yield
        yc = y - mean
        var = _split_dot(yc * yc, hmean)
        yield
        yn = yc * lax.rsqrt(var + RWKV_LN_EPS) * lnw + lnb
        o_ref[bi, rows, :] = (yn + bonus_ref[bi, rows, :]) * g_ref[bi, rows, :]

    def chunk(ci, carry):
        rows = pl.ds(pl.multiple_of(ci * CHUNK, CHUNK), CHUNK)
        _round_robin([one(bi, rows) for bi in range(nb)])
        return carry

    lax.fori_loop(0, nchunk, chunk, 0)


def rwkv_mix(prep, ln_w, ln_b, tb=512):
    b, t, _ = prep[0].shape
    nb = SEQ_PER_STEP if b % SEQ_PER_STEP == 0 else 1
    spec = pl.BlockSpec((nb, tb, 256), lambda b_, i: (b_, i, 0))
    full = pl.BlockSpec((1, 256), lambda b_, i: (0, 0))
    return pl.pallas_call(
        functools.partial(_rwkv_kernel, nchunk=tb // CHUNK, nb=nb),
        grid=(b // nb, t // tb),
        in_specs=[spec] * 8 + [full, full],
        out_specs=spec,
        out_shape=jax.ShapeDtypeStruct((b, t, 256), F32),
        scratch_shapes=[pltpu.VMEM((nb, STACK, STACK), F32)],
        compiler_params=_cparams(("parallel", "arbitrary")),
        name="rwkv_mix",
    )(*prep, ln_w.reshape(1, 256), ln_b.reshape(1, 256))


def _rot_cols(w, half):
    g = w.reshape(w.shape[0], -1, 2, half)
    return jnp.concatenate([-g[:, :, 1:2, :], g[:, :, 0:1, :]], axis=2).reshape(w.shape)


def pack_w_in(w):
    d = w.shape[0]
    rw = w[:, 0:1024]
    dq, dk, dv = w[:, 1024:1280], w[:, 1280:1536], w[:, 1536:1792]
    cq, ckv, kr = w[:, 1792:2048], w[:, 2048:2176], w[:, 2176:2208]
    hg = w[:, 2208:3232]
    z = lambda n: jnp.zeros((d, n), w.dtype)
    kra = jnp.concatenate([z(MLA_NOPE), kr, z(128 - MLA_NOPE - MLA_ROPE)], axis=1)
    krb = jnp.concatenate([z(MLA_NOPE), _rot_cols(kr, MLA_ROPE // 2), z(128 - MLA_NOPE - MLA_ROPE)], axis=1)
    packed = jnp.concatenate([rw, dq, dk, dv, hg, cq, ckv, kra, krb], axis=1)
    return packed.astype(BF16)


def pack_mla(w_uq, w_ukv):
    rq = w_uq.shape[0]
    q = w_uq.reshape(rq, N_HEADS, MLA_NOPE + MLA_ROPE)
    nope, rope = q[..., :MLA_NOPE], q[..., MLA_NOPE:]
    pad = jnp.zeros((rq, N_HEADS, 128 - MLA_NOPE - MLA_ROPE), w_uq.dtype)
    wq1 = jnp.concatenate([nope, rope, pad], axis=-1).reshape(rq, N_HEADS * 128)
    rrot = _rot_cols(rope.reshape(rq, N_HEADS * MLA_ROPE), MLA_ROPE // 2).reshape(rq, N_HEADS, MLA_ROPE)
    wq2 = jnp.concatenate([jnp.zeros_like(nope), rrot, pad], axis=-1).reshape(rq, N_HEADS * 128)
    rk = w_ukv.shape[0]
    kv = w_ukv.reshape(rk, N_HEADS, MLA_NOPE + HEAD_DIM)
    wk = jnp.concatenate([kv[..., :MLA_NOPE], jnp.zeros((rk, N_HEADS, 128 - MLA_NOPE), w_ukv.dtype)],
                         axis=-1).reshape(rk, N_HEADS * 128)
    vz = jnp.zeros((rk, HEAD_DIM), w_ukv.dtype)
    wv = jnp.concatenate([x for h in range(N_HEADS)
                          for x in ((kv[:, h, MLA_NOPE:], vz) if h % 2 == 0 else (vz, kv[:, h, MLA_NOPE:]))], axis=1)
    return wq1.astype(BF16), wq2.astype(BF16), wk.astype(BF16), wv.astype(BF16)


def kernel(x, positions, mix_norm, w_in, w_out, rwkv_mu, rwkv_w0, rwkv_w2, rwkv_a0, rwkv_a2, rwkv_g2, rwkv_k_k,
           rwkv_k_a, rwkv_r_k, rwkv_ln_w, rwkv_ln_b, mla_q_norm, mla_w_uq, mla_kv_norm, mla_w_ukv, hgrn_lb_logits,
           hgrn_g_norm, ffn_norm, ffn_w1, ffn_w3, ffn_w2, moe_router, moe_w1, moe_w3, moe_w2, final_norm):
    b, t, d = x.shape
    n = b * t
    depth = w_in.shape[0]
    x2 = x.reshape(n, d)
    cd, sd, cm, sm = rope_tables(positions.reshape(n, 1))
    lora_pad = jnp.zeros((rwkv_w2.shape[1], D_GROUP), F32)
    fused_final = False
    for layer in range(depth):
        cols = in_proj(x2, mix_norm[layer], pack_w_in(w_in[layer]))
        cols3 = cols.reshape(b, t, N_PACK)
        prep = rwkv_prep(cols3, rwkv_mu[layer], rwkv_w0[layer],
                         jnp.concatenate([rwkv_w2[layer], lora_pad], axis=0).astype(BF16), rwkv_a0[layer],
                         jnp.concatenate([lora_pad, rwkv_a2[layer]], axis=0).astype(BF16),
                         rwkv_g2[layer].astype(BF16), rwkv_k_k[layer], rwkv_k_a[layer], rwkv_r_k[layer])
        y_a = rwkv_mix(prep, rwkv_ln_w[layer], rwkv_ln_b[layer])
        q_d, k_d = dil_prep(cols, cd, sd)
        y_b = dilated_attention(q_d.reshape(b, t, D_GROUP), k_d.reshape(b, t, D_GROUP), cols3)
        q_m, k_m, v_m = mla_prep(cols, cm, sm, mla_q_norm[layer], mla_kv_norm[layer],
                                 *pack_mla(mla_w_uq[layer], mla_w_ukv[layer]))
        y_c = mla_flash(q_m.reshape(b, t, -1), k_m.reshape(b, t, -1), v_m.reshape(b, t, -1))
        y_d = hgrn(cols3, hgrn_lb_logits, hgrn_g_norm[layer], layer)
        ys = [y.reshape(n, D_GROUP) for y in (y_a, y_b, y_c, y_d)]
        x2, h = out_proj(x2, ys, w_out[layer].astype(BF16), ffn_norm[layer])
        j = layer // 2
        if layer % 2 == 0:
            x2 = ffn_dense(h, x2, ffn_w1[j].astype(BF16), ffn_w3[j].astype(BF16), ffn_w2[j].astype(BF16))
        else:
            route, hp = moe_route(x2, ffn_norm[layer], jnp.pad(moe_router[j], ((0, 0), (0, 128 - N_EXPERTS))))
            rows_pad = TOP_K * n + N_EXPERTS * MOE_TILE
            pos, tile_expert, tile_valid = moe_plan(route, rows_pad)
            xs = moe_scatter(pos, hp, rows_pad)
            ys = moe_experts(tile_expert, tile_valid, xs, moe_w1[j].astype(BF16), moe_w3[j].astype(BF16),
                             moe_w2[j].astype(BF16))
            fused_final = layer == depth - 1
            x2 = moe_combine(pos, x2, route, ys, final_norm, fused_final)
    if not fused_final:
        x2 = final_rms(x2, final_norm)
    return x2.reshape(b, t, d)
```

```python
import functools

import jax
import jax.numpy as jnp
from jax import lax
from jax.experimental import pallas as pl
from jax.experimental.pallas import tpu as pltpu

F32 = jnp.float32
BF16 = jnp.bfloat16

D_GROUP = 256
HEAD_DIM = 64
N_HEADS = 4
RMS_EPS = 1e-6
RWKV_LN_EPS = 64e-5
ROPE_THETA = 10000.0
NEG_INF = -1e30
N_EXPERTS = 8
MLA_ROPE = 32
MLA_NOPE = 64

VMEM_LIMIT = 56 * 1024 * 1024

C_RWKV = 0
C_DIL = 1024
C_HGRN = 1792
C_MLA = 2816
N_PACK = 3456


def _cparams(sem, vmem=VMEM_LIMIT):
    return pltpu.CompilerParams(dimension_semantics=sem, vmem_limit_bytes=vmem)


def _rms(x, w):
    return x * lax.rsqrt(jnp.mean(x * x, axis=-1, keepdims=True) + RMS_EPS) * w


def _dot(a, b):
    return jnp.dot(a, b, preferred_element_type=F32)


def _dot_nt(a, b):
    return lax.dot_general(a, b, (((1,), (1,)), ((), ())), preferred_element_type=F32)


def _split_dot(a_f32, b_bf16):
    hi = a_f32.astype(BF16)
    lo = (a_f32 - hi.astype(F32)).astype(BF16)
    return _dot(hi, b_bf16) + _dot(lo, b_bf16)


def _split_dot_lhs(a_bf16, b_f32):
    hi = b_f32.astype(BF16)
    lo = (b_f32 - hi.astype(F32)).astype(BF16)
    return _dot(a_bf16, hi) + _dot(a_bf16, lo)


def _rope_kernel(pos_ref, cd_ref, sd_ref, cm_ref, sm_ref):
    pos = pos_ref[...].astype(F32)
    tm = pos.shape[0]
    lane = lax.broadcasted_iota(jnp.int32, (tm, 128), 1)
    j = (lane & 31).astype(F32)
    inv = jnp.exp(j * (-2.0 / HEAD_DIM * jnp.log(ROPE_THETA)))
    ang = pos * inv
    cd_ref[...] = jnp.cos(ang)
    sd_ref[...] = jnp.sin(ang)
    jm = (lane & 15).astype(F32)
    invm = jnp.exp(jm * (-2.0 / MLA_ROPE * jnp.log(ROPE_THETA)))
    angm = pos * invm
    rope_lane = (lane >= MLA_NOPE) & (lane < MLA_NOPE + MLA_ROPE)
    cm_ref[...] = jnp.where(rope_lane, jnp.cos(angm), jnp.where(lane < MLA_NOPE, 1.0, 0.0))
    sm_ref[...] = jnp.where(rope_lane, jnp.sin(angm), 0.0)


def rope_tables(pos_col, tm=512):
    n = pos_col.shape[0]
    return pl.pallas_call(
        _rope_kernel,
        grid=(n // tm,),
        in_specs=[pl.BlockSpec((tm, 1), lambda i: (i, 0))],
        out_specs=[pl.BlockSpec((tm, 128), lambda i: (i, 0))] * 4,
        out_shape=[jax.ShapeDtypeStruct((n, 128), F32)] * 4,
        compiler_params=_cparams(("parallel",)),
        name="rope_tables",
    )(pos_col)


def _in_proj_kernel(x_ref, nw_ref, w_ref, o_ref):
    h = _rms(x_ref[...], nw_ref[...]).astype(BF16)
    o_ref[...] = _dot(h, w_ref[...])


def in_proj(x2, norm_w, w_packed, tm=512):
    n, d = x2.shape
    nc = w_packed.shape[1]
    return pl.pallas_call(
        _in_proj_kernel,
        grid=(n // tm,),
        in_specs=[pl.BlockSpec((tm, d), lambda i: (i, 0)),
                  pl.BlockSpec((1, d), lambda i: (0, 0)),
                  pl.BlockSpec((d, nc), lambda i: (0, 0))],
        out_specs=pl.BlockSpec((tm, nc), lambda i: (i, 0)),
        out_shape=jax.ShapeDtypeStruct((n, nc), F32),
        compiler_params=_cparams(("parallel",)),
        name="in_proj",
    )(x2, norm_w.reshape(1, d), w_packed)


def _out_proj_kernel(x_ref, ya_ref, yb_ref, yc_ref, yd_ref, w_ref, nw_ref, xo_ref, h_ref):
    acc = x_ref[...]
    for g, y_ref in enumerate((ya_ref, yb_ref, yc_ref, yd_ref)):
        acc = acc + _dot(y_ref[...].astype(BF16), w_ref[g * D_GROUP:(g + 1) * D_GROUP, :])
    xo_ref[...] = acc
    h_ref[...] = _rms(acc, nw_ref[...]).astype(BF16)


def out_proj(x2, ys, w_out_bf16, norm_w, tm=512):
    n, d = x2.shape
    yspec = pl.BlockSpec((tm, D_GROUP), lambda i: (i, 0))
    return pl.pallas_call(
        _out_proj_kernel,
        grid=(n // tm,),
        in_specs=[pl.BlockSpec((tm, d), lambda i: (i, 0)), yspec, yspec, yspec, yspec,
                  pl.BlockSpec((d, d), lambda i: (0, 0)),
                  pl.BlockSpec((1, d), lambda i: (0, 0))],
        out_specs=[pl.BlockSpec((tm, d), lambda i: (i, 0)), pl.BlockSpec((tm, d), lambda i: (i, 0))],
        out_shape=[jax.ShapeDtypeStruct((n, d), F32), jax.ShapeDtypeStruct((n, d), BF16)],
        compiler_params=_cparams(("parallel",)),
        name="out_proj",
    )(x2, *ys, w_out_bf16, norm_w.reshape(1, d))


def _ffn_kernel(h_ref, x_ref, w1_ref, w3_ref, w2_ref, o_ref, acc_ref):
    j = pl.program_id(1)

    @pl.when(j == 0)
    def _():
        acc_ref[...] = x_ref[...]

    h = h_ref[...]
    a = _dot(h, w1_ref[...])
    b = _dot(h, w3_ref[...])
    act = (a * jax.nn.sigmoid(a) * b).astype(BF16)
    acc_ref[...] += _dot(act, w2_ref[...])

    @pl.when(j == pl.num_programs(1) - 1)
    def _():
        o_ref[...] = acc_ref[...]


def ffn_dense(h_bf16, x2, w1, w3, w2, tm=512, tf=1408):
    n, d = x2.shape
    f = w1.shape[1]
    return pl.pallas_call(
        _ffn_kernel,
        grid=(n // tm, f // tf),
        in_specs=[pl.BlockSpec((tm, d), lambda i, j: (i, 0)),
                  pl.BlockSpec((tm, d), lambda i, j: (i, 0)),
                  pl.BlockSpec((d, tf), lambda i, j: (0, j)),
                  pl.BlockSpec((d, tf), lambda i, j: (0, j)),
                  pl.BlockSpec((tf, d), lambda i, j: (j, 0))],
        out_specs=pl.BlockSpec((tm, d), lambda i, j: (i, 0)),
        out_shape=jax.ShapeDtypeStruct((n, d), F32),
        scratch_shapes=[pltpu.VMEM((tm, d), F32)],
        compiler_params=_cparams(("parallel", "arbitrary")),
        name="ffn_dense",
    )(h_bf16, x2, w1, w3, w2)


MOE_TILE = 512
TOP_K = 2


def _pack_bf16_pairs(x):
    bits = pltpu.bitcast(x.astype(BF16).astype(F32), jnp.uint32)
    half = bits.shape[1] // 2
    return (bits[:, :half] >> 16) | (bits[:, half:] & jnp.uint32(0xFFFF0000))


def _unpack_bf16_pairs(p):
    lo = pltpu.bitcast(p << 16, F32)
    hi = pltpu.bitcast(p & jnp.uint32(0xFFFF0000), F32)
    return jnp.concatenate([lo, hi], axis=1)


def _router_kernel(x_ref, nw_ref, wr_ref, route_ref, hp_ref):
    h = _rms(x_ref[...], nw_ref[...])
    logits = jnp.dot(h, wr_ref[...], preferred_element_type=F32, precision=lax.Precision.HIGHEST)
    lane = lax.broadcasted_iota(jnp.int32, logits.shape, 1)
    logits = jnp.where(lane < N_EXPERTS, logits, -jnp.inf)
    m1 = jnp.max(logits, axis=-1, keepdims=True)
    i1 = jnp.min(jnp.where(logits == m1, lane, 128), axis=-1, keepdims=True)
    rest = jnp.where(lane == i1, -jnp.inf, logits)
    m2 = jnp.max(rest, axis=-1, keepdims=True)
    i2 = jnp.min(jnp.where(rest == m2, lane, 128), axis=-1, keepdims=True)
    e2 = jnp.exp(m2 - m1)
    g1 = 1.0 / (1.0 + e2)
    g2 = e2 * g1
    route_ref[...] = jnp.where(lane == 0, i1.astype(F32), jnp.where(lane == 1, i2.astype(F32),
                               jnp.where(lane == 2, g1, jnp.where(lane == 3, g2, 0.0))))
    hp_ref[...] = _pack_bf16_pairs(h)


def moe_route(x2, norm_w, w_router_pad, tm=512):
    n, d = x2.shape
    return pl.pallas_call(
        _router_kernel,
        grid=(n // tm,),
        in_specs=[pl.BlockSpec((tm, d), lambda i: (i, 0)),
                  pl.BlockSpec((1, d), lambda i: (0, 0)),
                  pl.BlockSpec((d, 128), lambda i: (0, 0))],
        out_specs=[pl.BlockSpec((tm, 128), lambda i: (i, 0)), pl.BlockSpec((tm, d // 2), lambda i: (i, 0))],
        out_shape=[jax.ShapeDtypeStruct((n, 128), F32), jax.ShapeDtypeStruct((n, d // 2), jnp.uint32)],
        compiler_params=_cparams(("parallel",)),
        name="moe_router",
    )(x2, norm_w.reshape(1, d), w_router_pad)


def moe_plan(route, n_rows_pad):
    e = route[:, :TOP_K].astype(jnp.int32)
    onehot = (e.reshape(-1, 1) == jnp.arange(N_EXPERTS, dtype=jnp.int32)[None, :]).astype(jnp.int32)
    csum = jnp.cumsum(onehot, axis=0)
    counts = csum[-1]
    rank = jnp.sum((csum - onehot) * onehot, axis=1)
    gsz = ((counts + MOE_TILE - 1) // MOE_TILE) * MOE_TILE
    gend = jnp.cumsum(gsz)
    goff = gend - gsz
    pos = (goff[e.reshape(-1)] + rank).astype(jnp.int32)
    tile_start = jnp.arange(n_rows_pad // MOE_TILE, dtype=jnp.int32) * MOE_TILE
    tile_expert = jnp.minimum(jnp.sum((tile_start[:, None] >= gend[None, :]).astype(jnp.int32), axis=1),
                              N_EXPERTS - 1).astype(jnp.int32)
    tile_valid = (tile_start < gend[-1]).astype(jnp.int32)
    return pos, tile_expert, tile_valid


def _moe_scatter_kernel(pos_ref, hp_ref, xs_in_ref, xs_ref, sem):
    del xs_in_ref
    i = pl.program_id(0)
    ts = hp_ref.shape[0]

    def copy(r, slot):
        dst = pos_ref[(i * ts + r) * TOP_K + slot]
        return pltpu.make_async_copy(hp_ref.at[pl.ds(r, 1), :], xs_ref.at[pl.ds(dst, 1), :], sem)

    def issue(r, carry):
        for slot in range(TOP_K):
            copy(r, slot).start(priority=slot)
        return carry

    lax.fori_loop(0, ts, issue, 0, unroll=8)
    for slot in range(TOP_K):
        pltpu.make_async_copy(hp_ref, xs_ref.at[pl.ds(0, ts), :], sem).wait()


def moe_scatter(pos, hp, n_rows_pad, ts=256):
    n, dh = hp.shape
    xs0 = jnp.zeros((n_rows_pad, dh), jnp.uint32)
    return pl.pallas_call(
        _moe_scatter_kernel,
        grid_spec=pltpu.PrefetchScalarGridSpec(
            num_scalar_prefetch=1,
            grid=(n // ts,),
            in_specs=[pl.BlockSpec((ts, dh), lambda i, pos_: (i, 0)),
                      pl.BlockSpec(memory_space=pl.ANY)],
            out_specs=pl.BlockSpec(memory_space=pl.ANY),
            scratch_shapes=[pltpu.SemaphoreType.DMA],
        ),
        out_shape=jax.ShapeDtypeStruct((n_rows_pad, dh), jnp.uint32),
        input_output_aliases={2: 0},
        compiler_params=_cparams(("arbitrary",)),
        name="moe_scatter",
    )(pos, hp, xs0)


def _moe_expert_kernel(te_ref, tv_ref, xs_ref, w1_ref, w3_ref, w2_ref, ys_ref, h_ref, acc_ref):
    i = pl.program_id(0)
    j = pl.program_id(1)

    @pl.when(tv_ref[i] > 0)
    def _():
        @pl.when(j == 0)
        def _():
            h_ref[...] = _unpack_bf16_pairs(xs_ref[...]).astype(BF16)
            acc_ref[...] = jnp.zeros_like(acc_ref)

        h = h_ref[...]
        a = _dot(h, w1_ref[...])
        b = _dot(h, w3_ref[...])
        act = (a * jax.nn.sigmoid(a) * b).astype(BF16)
        acc_ref[...] += _dot(act, w2_ref[...])

        @pl.when(j == pl.num_programs(1) - 1)
        def _():
            ys_ref[...] = _pack_bf16_pairs(acc_ref[...])

    @pl.when((tv_ref[i] == 0) & (j == pl.num_programs(1) - 1))
    def _():
        ys_ref[...] = jnp.zeros_like(ys_ref)


def moe_experts(tile_expert, tile_valid, xs, w1, w3, w2, tf=1792):
    rows, dh = xs.shape
    d = 2 * dh
    f = w1.shape[2]
    tm = MOE_TILE
    return pl.pallas_call(
        _moe_expert_kernel,
        grid_spec=pltpu.PrefetchScalarGridSpec(
            num_scalar_prefetch=2,
            grid=(rows // tm, f // tf),
            in_specs=[pl.BlockSpec((tm, dh), lambda i, j, te, tv: (i, 0)),
                      pl.BlockSpec((None, d, tf), lambda i, j, te, tv: (te[i], 0, j * tv[i])),
                      pl.BlockSpec((None, d, tf), lambda i, j, te, tv: (te[i], 0, j * tv[i])),
                      pl.BlockSpec((None, tf, d), lambda i, j, te, tv: (te[i], j * tv[i], 0))],
            out_specs=pl.BlockSpec((tm, dh), lambda i, j, te, tv: (i, 0)),
            scratch_shapes=[pltpu.VMEM((tm, d), BF16), pltpu.VMEM((tm, d), F32)],
        ),
        out_shape=jax.ShapeDtypeStruct((rows, dh), jnp.uint32),
        compiler_params=_cparams(("arbitrary", "arbitrary")),
        name="moe_experts",
    )(tile_expert, tile_valid, xs, w1, w3, w2)


def _moe_combine_kernel(pos_ref, x_ref, route_ref, nw_ref, ys_ref, o_ref, buf_ref, sem, *, final):
    i = pl.program_id(0)
    tc = x_ref.shape[0]

    def copy(r, slot):
        src = pos_ref[(i * tc + r) * TOP_K + slot]
        return pltpu.make_async_copy(ys_ref.at[pl.ds(src, 1), :], buf_ref.at[slot, pl.ds(r, 1), :], sem)

    def issue(r, carry):
        for slot in range(TOP_K):
            copy(r, slot).start(priority=slot)
        return carry

    lax.fori_loop(0, tc, issue, 0, unroll=8)
    for slot in range(TOP_K):
        pltpu.make_async_copy(ys_ref.at[pl.ds(0, tc), :], buf_ref.at[slot], sem).wait()
    route = route_ref[...]
    g1 = route[:, 2:3]
    g2 = route[:, 3:4]
    out = x_ref[...] + g1 * _unpack_bf16_pairs(buf_ref[0]) + g2 * _unpack_bf16_pairs(buf_ref[1])
    if final:
        out = _rms(out, nw_ref[...])
    o_ref[...] = out


def moe_combine(pos, x2, route, ys, norm_w, final, tc=256):
    n, d = x2.shape
    return pl.pallas_call(
        functools.partial(_moe_combine_kernel, final=final),
        grid_spec=pltpu.PrefetchScalarGridSpec(
            num_scalar_prefetch=1,
            grid=(n // tc,),
            in_specs=[pl.BlockSpec((tc, d), lambda i, pos_: (i, 0)),
                      pl.BlockSpec((tc, 128), lambda i, pos_: (i, 0)),
                      pl.BlockSpec((1, d), lambda i, pos_: (0, 0)),
                      pl.BlockSpec(memory_space=pl.ANY)],
            out_specs=pl.BlockSpec((tc, d), lambda i, pos_: (i, 0)),
            scratch_shapes=[pltpu.VMEM((TOP_K, tc, d // 2), jnp.uint32), pltpu.SemaphoreType.DMA],
        ),
        out_shape=jax.ShapeDtypeStruct((n, d), F32),
        compiler_params=_cparams(("arbitrary",)),
        name="moe_combine",
    )(pos, x2, route, norm_w.reshape(1, d), ys)


def _final_norm_kernel(x_ref, nw_ref, o_ref):
    o_ref[...] = _rms(x_ref[...], nw_ref[...])


def final_rms(x2, norm_w, tm=1024):
    n, d = x2.shape
    return pl.pallas_call(
        _final_norm_kernel,
        grid=(n // tm,),
        in_specs=[pl.BlockSpec((tm, d), lambda i: (i, 0)), pl.BlockSpec((1, d), lambda i: (0, 0))],
        out_specs=pl.BlockSpec((tm, d), lambda i: (i, 0)),
        out_shape=jax.ShapeDtypeStruct((n, d), F32),
        compiler_params=_cparams(("parallel",)),
        name="final_norm",
    )(x2, norm_w.reshape(1, d))


def _mla_prep_kernel(cq_ref, ckv_ref, kra_ref, krb_ref, cm_ref, sm_ref, qn_ref, kvn_ref,
                     wq1_ref, wq2_ref, wk_ref, wv_ref, q_ref, k_ref, v_ref):
    qn = _rms(cq_ref[...], qn_ref[...]).astype(BF16)
    kvn = _rms(ckv_ref[...], kvn_ref[...]).astype(BF16)
    cm = cm_ref[...]
    sm = sm_ref[...]
    cm4 = jnp.concatenate([cm] * N_HEADS, axis=1)
    sm4 = jnp.concatenate([sm] * N_HEADS, axis=1)
    scale = (MLA_NOPE + MLA_ROPE) ** -0.5
    q = (_dot(qn, wq1_ref[...]) * cm4 + _dot(qn, wq2_ref[...]) * sm4) * scale
    q_ref[...] = q.astype(BF16)
    kr = kra_ref[...] * cm + krb_ref[...] * sm
    k = _dot(kvn, wk_ref[...]) + jnp.concatenate([kr] * N_HEADS, axis=1)
    k_ref[...] = k.astype(BF16)
    lane = lax.broadcasted_iota(jnp.int32, (cm.shape[0], N_HEADS * 128), 1)
    ones = jnp.where(((lane >> 6) & 1) == (((lane >> 7) & 1) ^ 1), 1.0, 0.0)
    v_ref[...] = (_dot(kvn, wv_ref[...]) + ones).astype(BF16)


def mla_prep(cols, cm, sm, q_norm, kv_norm, wq1, wq2, wk, wv, tm=512):
    n = cols.shape[0]
    b256 = C_MLA // 256
    b128 = C_MLA // 128
    full = lambda shape: pl.BlockSpec(shape, lambda i: (0, 0))
    return pl.pallas_call(
        _mla_prep_kernel,
        grid=(n // tm,),
        in_specs=[pl.BlockSpec((tm, 256), lambda i: (i, b256)),
                  pl.BlockSpec((tm, 128), lambda i: (i, b128 + 2)),
                  pl.BlockSpec((tm, 128), lambda i: (i, b128 + 3)),
                  pl.BlockSpec((tm, 128), lambda i: (i, b128 + 4)),
                  pl.BlockSpec((tm, 128), lambda i: (i, 0)),
                  pl.BlockSpec((tm, 128), lambda i: (i, 0)),
                  full((1, 256)), full((1, 128)),
                  full((256, 512)), full((256, 512)), full((128, 512)), full((128, 512))],
        out_specs=[pl.BlockSpec((tm, 512), lambda i: (i, 0))] * 3,
        out_shape=[jax.ShapeDtypeStruct((n, 512), BF16)] * 3,
        compiler_params=_cparams(("parallel",)),
        name="mla_prep",
    )(cols, cols, cols, cols, cm, sm, q_norm.reshape(1, 256), kv_norm.reshape(1, 128), wq1, wq2, wk, wv)


def _mla_flash_kernel(q_ref, k_ref, v_ref, o_ref, *, tq, tk):
    i = pl.program_id(2)
    lane = lax.broadcasted_iota(jnp.int32, (tq, 128), 1)
    qpos = i * tq + lax.broadcasted_iota(jnp.int32, (tq, tk), 0)
    kloc = lax.broadcasted_iota(jnp.int32, (tq, tk), 1)
    nfull = (i * tq) // tk
    qs = [q_ref[:, h * 128:(h + 1) * 128] for h in range(2)]

    def step(j, carry, masked):
        ks = pl.multiple_of(j * tk, tk)
        ss = [_dot_nt(qs[h], k_ref[pl.ds(ks, tk), h * 128:(h + 1) * 128]) for h in range(2)]
        if masked:
            mask = ks + kloc <= qpos
            ss = [jnp.where(mask, s, NEG_INF) for s in ss]
        mns = [jnp.maximum(carry[h][0], jnp.max(ss[h], axis=-1, keepdims=True)) for h in range(2)]
        ps = [jnp.exp((ss[h] - mns[h]).astype(BF16)) for h in range(2)]
        als = [jnp.exp(carry[h][0] - mns[h]) for h in range(2)]
        pvs = [_dot(ps[h], v_ref[pl.ds(ks, tk), h * 128:(h + 1) * 128]) for h in range(2)]
        return tuple((mns[h], carry[h][1] * als[h] + pvs[h]) for h in range(2))

    init = (jnp.full((tq, 1), NEG_INF, F32), jnp.zeros((tq, 128), F32))
    res = lax.fori_loop(0, nfull, functools.partial(step, masked=False), (init, init))
    res = step(nfull, res, True)
    acc = jnp.where(lane < HEAD_DIM, res[0][1], res[1][1])
    den = jnp.where(lane < HEAD_DIM, pltpu.roll(res[0][1], HEAD_DIM, 1), pltpu.roll(res[1][1], HEAD_DIM, 1))
    o_ref[...] = acc / den


def mla_flash(q3, k3, v3, tq=256, tk=512):
    b, t, _ = q3.shape
    assert tk % tq == 0 and t % tk == 0
    return pl.pallas_call(
        functools.partial(_mla_flash_kernel, tq=tq, tk=tk),
        grid=(b, 2, t // tq),
        in_specs=[pl.BlockSpec((None, tq, 256), lambda b_, p, i: (b_, i, p)),
                  pl.BlockSpec((None, t, 256), lambda b_, p, i: (b_, 0, p)),
                  pl.BlockSpec((None, t, 256), lambda b_, p, i: (b_, 0, p))],
        out_specs=pl.BlockSpec((None, tq, 128), lambda b_, p, i: (b_, i, p)),
        out_shape=jax.ShapeDtypeStruct((b, t, 256), F32),
        compiler_params=_cparams(("parallel", "parallel", "arbitrary")),
        name="mla_flash",
    )(q3, k3, v3)


DILATED_PATTERNS = ((128, 1), (512, 4), (2048, 16))
DIL_BLOCK = 128
DIL_UNROLL = 4


def _rotate_half_heads(x):
    lane = lax.broadcasted_iota(jnp.int32, x.shape, 1)
    width = x.shape[1]
    half = HEAD_DIM // 2
    return jnp.where((lane & (HEAD_DIM - 1)) < half, -pltpu.roll(x, width - half, 1), pltpu.roll(x, half, 1))


def _dil_block(qb, kb, vb, mask, out):
    nq = qb.shape[0]
    lane = lax.broadcasted_iota(jnp.int32, (nq, 128), 1)
    kbb = kb.astype(BF16)
    vbb = vb.astype(BF16)
    hms = [(lane >> 6) == h for h in range(2)]
    ss = [jnp.where(mask, _dot_nt(jnp.where(hm, qb, 0.0).astype(BF16), kbb), NEG_INF) for hm in hms]
    yield
    ms = [jnp.max(s, axis=-1, keepdims=True) for s in ss]
    ps = [jnp.exp(s - m) for s, m in zip(ss, ms)]
    yield
    ls = [jnp.sum(p, axis=-1, keepdims=True) for p in ps]
    ohs = [_dot(p.astype(BF16), vbb) for p in ps]
    yield
    out[0] = jnp.where(hms[0], ohs[0] / ls[0], ohs[1] / ls[1])
    out[1] = jnp.where(hms[0], ms[0] + jnp.log(ls[0]), ms[1] + jnp.log(ls[1]))


def _dil_kernel(qraw_ref, kraw_ref, v_ref, cd_ref, sd_ref, o_ref, q_ref, k_ref, m_ref, z_ref, acc_ref, *, t):
    blk = DIL_BLOCK
    cd = cd_ref[...]
    sd = sd_ref[...]
    qraw = qraw_ref[...]
    kraw = kraw_ref[...]
    q_ref[...] = (qraw * cd + _rotate_half_heads(qraw) * sd) * (HEAD_DIM ** -0.5)
    k_ref[...] = kraw * cd + _rotate_half_heads(kraw) * sd
    for pi, (window, dil) in enumerate(DILATED_PATTERNS):
        assert window // dil == blk
        length = t // dil
        nb = length // blk
        nk = 2 * blk if nb > 1 else blk
        uq = lax.broadcasted_iota(jnp.int32, (blk, nk), 0)
        uk = lax.broadcasted_iota(jnp.int32, (blk, nk), 1)

        def one(it, dil=dil, nb=nb, nk=nk, uq=uq, uk=uk, first=(pi == 0)):
            r = it // nb
            jb = it % nb
            kb0 = jnp.maximum(jb - 1, 0) * blk
            qs = r + dil * blk * jb
            ks = r + dil * kb0
            if dil == 1:
                qidx = pl.ds(pl.multiple_of(qs, blk), blk)
                kidx = pl.ds(pl.multiple_of(ks, blk), nk)
            else:
                qidx = pl.ds(qs, blk, stride=dil)
                kidx = pl.ds(ks, nk, stride=dil)
            dist = (jb * blk + uq) - (kb0 + uk)
            mask = (dist >= 0) & (dist <= blk)
            out = [None, None]
            yield from _dil_block(q_ref[qidx, :], k_ref[kidx, :], v_ref[kidx, :], mask, out)
            o, lse = out
            if first:
                m_ref[qidx, :] = lse
                z_ref[qidx, :] = jnp.ones_like(lse)
                acc_ref[qidx, :] = o
            else:
                m_old = m_ref[qidx, :]
                m_new = jnp.maximum(m_old, lse)
                a = jnp.exp(m_old - m_new)
                b = jnp.exp(lse - m_new)
                m_ref[qidx, :] = m_new
                z_ref[qidx, :] = z_ref[qidx, :] * a + b
                acc_ref[qidx, :] = acc_ref[qidx, :] * a + o * b

        def body(it, carry, one=one):
            _round_robin([one(it * DIL_UNROLL + u) for u in range(DIL_UNROLL)])
            return carry

        assert (dil * nb) % DIL_UNROLL == 0
        lax.fori_loop(0, dil * nb // DIL_UNROLL, body, 0)
    o_ref[...] = acc_ref[...] / z_ref[...]


def dilated_attention(cols3, cd3, sd3):
    b, t, _ = cols3.shape
    assert t % (DIL_BLOCK * DILATED_PATTERNS[-1][1]) == 0
    c0 = C_DIL // 128
    col = lambda j: pl.BlockSpec((None, t, 128), lambda i, p: (i, 0, j + p))
    tab = pl.BlockSpec((None, t, 128), lambda i, p: (i, 0, 0))
    return pl.pallas_call(
        functools.partial(_dil_kernel, t=t),
        grid=(b, 2),
        in_specs=[col(c0), col(c0 + 2), col(c0 + 4), tab, tab],
        out_specs=pl.BlockSpec((None, t, 128), lambda i, p: (i, 0, p)),
        out_shape=jax.ShapeDtypeStruct((b, t, 256), F32),
        scratch_shapes=[pltpu.VMEM((t, 128), F32)] * 5,
        compiler_params=_cparams(("parallel", "parallel")),
        name="dilated_attention",
    )(cols3, cols3, cols3, cd3, sd3)


CHUNK = 64
STACK = N_HEADS * CHUNK


def _stack_heads(x):
    lane = lax.broadcasted_iota(jnp.int32, x.shape, 1)
    return jnp.concatenate([jnp.where((lane >> 6) == h, x, 0.0) for h in range(N_HEADS)], axis=0)


def _unstack_heads(y):
    out = y[0:CHUNK]
    for h in range(1, N_HEADS):
        out = out + y[h * CHUNK:(h + 1) * CHUNK]
    return out


def _round_robin(stages):
    live = list(stages)
    while live:
        nxt = []
        for g in live:
            try:
                next(g)
                nxt.append(g)
            except StopIteration:
                pass
        live = nxt


def _head_mean_matrix():
    r = lax.broadcasted_iota(jnp.int32, (256, 256), 0)
    c = lax.broadcasted_iota(jnp.int32, (256, 256), 1)
    return jnp.where((r >> 6) == (c >> 6), 1.0 / HEAD_DIM, 0.0).astype(BF16)


HGRN_HALVES = (32, 16, 8, 4, 2, 1)


def _hgrn_sum_matrix():
    t = lax.broadcasted_iota(jnp.int32, (CHUNK, CHUNK), 0)
    j = lax.broadcasted_iota(jnp.int32, (CHUNK, CHUNK), 1)
    tri = jnp.where(j <= t, 1.0, 0.0)
    blocks = [tri]
    for half in HGRN_HALVES:
        ref = (t // (2 * half)) * (2 * half) + half - 1
        blocks.append(tri - jnp.where(j <= ref, 1.0, 0.0))
    return jnp.concatenate(blocks, axis=0).astype(BF16)


def _hgrn_kernel(q_ref, f_ref, i_ref, g_ref, lbl_ref, gn_ref, o_ref, st_ref, *, layer, nchunk, nb):
    @pl.when(pl.program_id(1) == 0)
    def _():
        st_ref[...] = jnp.zeros_like(st_ref)

    logits = lbl_ref[...]
    e = jnp.exp(logits - jnp.max(logits, axis=0, keepdims=True))
    sm = e / jnp.sum(e, axis=0, keepdims=True)
    lb = jnp.zeros((1, 256), F32)
    for l in range(1, layer + 1):
        lb = lb + sm[l:l + 1, :]
    gn = gn_ref[...]
    summat = _hgrn_sum_matrix()
    hmean = _head_mean_matrix()
    r = lax.broadcasted_iota(jnp.int32, (STACK, STACK), 0)
    c = lax.broadcasted_iota(jnp.int32, (STACK, STACK), 1)
    same_head = (r >> 6) == (c >> 6)
    rc = r & (CHUNK - 1)
    cc = c & (CHUNK - 1)

    def one(bi, rows):
        qv = q_ref[bi, rows, :]
        qq = qv * jax.nn.sigmoid(qv)
        forget = lb + (1.0 - lb) * jax.nn.sigmoid(f_ref[bi, rows, :])
        logf = jnp.log(forget)
        kk = 1.0 - forget
        v = i_ref[bi, rows, :]
        yield
        sums = _split_dot_lhs(summat, logf)
        b = sums[0:CHUNK]
        a = jnp.where(rc == cc, _dot_nt(_stack_heads(qq).astype(BF16), _stack_heads(kk).astype(BF16)), 0.0)
        yield
        for li, half in enumerate(HGRN_HALVES):
            d = sums[(li + 1) * CHUNK:(li + 2) * CHUNK]
            ql = qq * jnp.exp(jnp.minimum(d, 0.0))
            kl = kk * jnp.exp(jnp.minimum(-d, 0.0))
            m = ((rc // (2 * half)) == (cc // (2 * half))) & ((rc % (2 * half)) >= half) & ((cc % (2 * half)) < half)
            a = a + jnp.where(m, _dot_nt(_stack_heads(ql).astype(BF16), _stack_heads(kl).astype(BF16)), 0.0)
            yield
        o = _unstack_heads(_dot(a.astype(BF16), _stack_heads(v).astype(BF16)))
        st = st_ref[bi]
        o = o + _dot_nt((qq * jnp.exp(b)).astype(BF16), st.astype(BF16))
        yield
        b_end = b[CHUNK - 1:CHUNK, :]
        kbar = kk * jnp.exp(b_end - b)
        upd = lax.dot_general(v.astype(BF16), kbar.astype(BF16), (((0,), (0,)), ((), ())), preferred_element_type=F32)
        st_ref[bi] = st * jnp.exp(b_end) + jnp.where(same_head, upd, 0.0)
        ms = _split_dot(o * o, hmean)
        yield
        gv = g_ref[bi, rows, :]
        o_ref[bi, rows, :] = o * lax.rsqrt(ms + RMS_EPS) * gn * (gv * jax.nn.sigmoid(gv))

    def chunk(ci, carry):
        rows = pl.ds(pl.multiple_of(ci * CHUNK, CHUNK), CHUNK)
        _round_robin([one(bi, rows) for bi in range(nb)])
        return carry

    lax.fori_loop(0, nchunk, chunk, 0)


SEQ_PER_STEP = 4


def hgrn(cols3, lb_logits, g_norm, layer, tb=512):
    b, t, _ = cols3.shape
    nb = SEQ_PER_STEP if b % SEQ_PER_STEP == 0 else 1
    c0 = C_HGRN // 256
    spec = lambda j: pl.BlockSpec((nb, tb, 256), lambda b_, i: (b_, i, j))
    depth = lb_logits.shape[0]
    return pl.pallas_call(
        functools.partial(_hgrn_kernel, layer=layer, nchunk=tb // CHUNK, nb=nb),
        grid=(b // nb, t // tb),
        in_specs=[spec(c0), spec(c0 + 1), spec(c0 + 2), spec(c0 + 3),
                  pl.BlockSpec((depth, 256), lambda b_, i: (0, 0)),
                  pl.BlockSpec((1, 256), lambda b_, i: (0, 0))],
        out_specs=spec(0),
        out_shape=jax.ShapeDtypeStruct((b, t, 256), F32),
        scratch_shapes=[pltpu.VMEM((nb, STACK, STACK), F32)],
        compiler_params=_cparams(("parallel", "arbitrary")),
        name="hgrn",
    )(cols3, cols3, cols3, cols3, lb_logits, g_norm.reshape(1, 256))


def _head_sum_matrix():
    r = lax.broadcasted_iota(jnp.int32, (256, 256), 0)
    c = lax.broadcasted_iota(jnp.int32, (256, 256), 1)
    return jnp.where((r >> 6) == (c >> 6), 1.0, 0.0).astype(BF16)


def _rwkv_prep_tile(a, last_prev, mu, w0, w2, a0, a2, g2, k_k, k_a, r_k):
    row = lax.broadcasted_iota(jnp.int32, a.shape, 0)
    prev = jnp.where(row == 0, last_prev, pltpu.roll(a, 1, 0))
    xs = a + (prev - a) * mu
    r = xs[:, 0:256]
    k = xs[:, 256:512]
    v = xs[:, 512:768]
    xwa = xs[:, 768:896]
    xg = xs[:, 896:1024]
    z = -(w0 + _dot(jnp.tanh(xwa).astype(BF16), w2))
    softplus = jnp.maximum(z, 0.0) + jnp.log(1.0 + jnp.exp(-jnp.abs(z)))
    lw = -jnp.exp(-softplus - 0.5)
    lr = jax.nn.sigmoid(a0 + _dot(xwa.astype(BF16), a2))
    g = _dot(jax.nn.sigmoid(xg).astype(BF16), g2)
    hsum = _head_sum_matrix()
    kk = k * k_k
    kk = kk / jnp.maximum(jnp.sqrt(_split_dot(kk * kk, hsum)), 1e-12)
    k2 = k * (1.0 + (lr - 1.0) * k_a)
    bonus = _split_dot(r * k2 * r_k, hsum) * v
    return r, lw, k2, v, kk, kk * lr, bonus, g


RWKV_MERGE_HALVES = (2, 4, 8, 16, 32)


def _rwkv_kernel(c_ref, p_ref, mu_ref, w0_ref, w2_ref, a0_ref, a2_ref, g2_ref, kk_ref, ka_ref, rk_ref,
                 lnw_ref, lnb_ref, o_ref,
                 r_ref, lw_ref, k_ref, v_ref, kap_ref, at_ref, bonus_ref, g_ref, ht_ref, *, nchunk, nb):
    first = pl.program_id(1) == 0

    @pl.when(first)
    def _():
        ht_ref[...] = jnp.zeros_like(ht_ref)

    for bi in range(nb):
        outs = _rwkv_prep_tile(c_ref[bi], jnp.where(first, 0.0, p_ref[bi, 7:8, :]), mu_ref[...], w0_ref[...],
                               w2_ref[...], a0_ref[...], a2_ref[...], g2_ref[...], kk_ref[...], ka_ref[...],
                               rk_ref[...])
        for ref, val in zip((r_ref, lw_ref, k_ref, v_ref, kap_ref, at_ref, bonus_ref, g_ref), outs):
            ref[bi] = val

    t = lax.broadcasted_iota(jnp.int32, (CHUNK, CHUNK), 0)
    j = lax.broadcasted_iota(jnp.int32, (CHUNK, CHUNK), 1)
    tri = jnp.where(j <= t, 1.0, 0.0).astype(BF16)
    hmean = _head_mean_matrix()
    rr = lax.broadcasted_iota(jnp.int32, (STACK, STACK), 0)
    cc = lax.broadcasted_iota(jnp.int32, (STACK, STACK), 1)
    same_head = (rr >> 6) == (cc >> 6)
    rc = rr & (CHUNK - 1)
    sc = cc & (CHUNK - 1)
    eye = jnp.where(rr == cc, 1.0, 0.0)
    lnw = lnw_ref[...]
    lnb = lnb_ref[...]

    def one(bi, rows):
        lw = lw_ref[bi, rows, :]
        lc = _split_dot_lhs(tri, lw)
        l_end = lc[CHUNK - 1:CHUNK, :]
        kap = kap_ref[bi, rows, :]
        at = at_ref[bi, rows, :]
        k2 = k_ref[bi, rows, :]
        v = v_ref[bi, rows, :]
        inv = jnp.exp(-lc)
        fwd = jnp.exp(l_end - lc)
        s_kap = _stack_heads(kap * jnp.exp(lc - lw)).astype(BF16)
        s_r32 = _stack_heads(r_ref[bi, rows, :] * jnp.exp(lc))
        s_r = s_r32.astype(BF16)
        s_a = _stack_heads(at * inv).astype(BF16)
        s_k = _stack_heads(k2 * inv).astype(BF16)
        s_v = _stack_heads(v).astype(BF16)
        strict = rc > sc
        incl = rc >= sc
        yield
        a_ka = jnp.where(strict, _dot_nt(s_kap, s_a), 0.0)
        a_kk = jnp.where(strict, _dot_nt(s_kap, s_k), 0.0).astype(BF16)
        m1 = ((rc >> 1) == (sc >> 1)) & strict
        tinv = eye - jnp.where(m1, a_ka, 0.0)
        yield
        a_ra = jnp.where(incl, _dot_nt(s_r, s_a), 0.0).astype(BF16)
        a_rk = jnp.where(incl, _dot_nt(s_r, s_k), 0.0).astype(BF16)
        for half in RWKV_MERGE_HALVES:
            m = ((rc // (2 * half)) == (sc // (2 * half))) & ((rc % (2 * half)) >= half) & ((sc % (2 * half)) < half)
            off = jnp.where(m, a_ka, 0.0).astype(BF16)
            tb = tinv.astype(BF16)
            x = _dot(off, tb).astype(BF16)
            yield
            tinv = tinv - _dot(tb, x)
            yield
        tb = tinv.astype(BF16)
        kt_st = _dot(tb, s_kap)
        w_st = _dot(a_kk, s_v).astype(BF16)
        yield
        vt_st = _dot(tb, w_st)
        rt = _unstack_heads(s_r32 - _dot(a_ra, kt_st.astype(BF16)))
        yield
        y0 = _unstack_heads(_dot(a_rk, s_v) - _dot(a_ra, vt_st.astype(BF16)))
        kt = _unstack_heads(kt_st)
        vt = _unstack_heads(vt_st)
        ht = ht_ref[bi]
        htb = ht.astype(BF16)
        yield
        y = y0 + _dot_nt(rt.astype(BF16), htb)
        u = vt + _dot_nt(kt.astype(BF16), htb)
        yield
        lhs = jnp.concatenate([v, u], axis=0).astype(BF16)
        rhs = jnp.concatenate([k2 * fwd, -(at * fwd)], axis=0).astype(BF16)
        upd = lax.dot_general(lhs, rhs, (((0,), (0,)), ((), ())), preferred_element_type=F32)
        ht_ref[bi] = ht * jnp.exp(l_end) + jnp.where(same_head, upd, 0.0)
        mean = _split_dot(y, hmean)
        yield
        yc = y - mean
        var = _split_dot(yc * yc, hmean)
        yield
        yn = yc * lax.rsqrt(var + RWKV_LN_EPS) * lnw + lnb
        o_ref[bi, rows, :] = (yn + bonus_ref[bi, rows, :]) * g_ref[bi, rows, :]

    def chunk(ci, carry):
        rows = pl.ds(pl.multiple_of(ci * CHUNK, CHUNK), CHUNK)
        _round_robin([one(bi, rows) for bi in range(nb)])
        return carry

    lax.fori_loop(0, nchunk, chunk, 0)


def rwkv_mix(cols3, mu, w0, w2p, a0, a2p, g2, k_k, k_a, r_k, ln_w, ln_b, tb=512):
    b, t, _ = cols3.shape
    nb = SEQ_PER_STEP if b % SEQ_PER_STEP == 0 else 1
    spec = pl.BlockSpec((nb, tb, 256), lambda b_, i: (b_, i, 0))
    full = lambda shape: pl.BlockSpec(shape, lambda b_, i: (0, 0))
    row = lambda x: x.reshape(1, -1)
    return pl.pallas_call(
        functools.partial(_rwkv_kernel, nchunk=tb // CHUNK, nb=nb),
        grid=(b // nb, t // tb),
        in_specs=[pl.BlockSpec((nb, tb, 1024), lambda b_, i: (b_, i, C_RWKV // 1024)),
                  pl.BlockSpec((nb, 8, 1024), lambda b_, i: (b_, jnp.maximum(i * (tb // 8) - 1, 0), C_RWKV // 1024)),
                  full((1, 1024)), full((1, 256)), full((128, 256)), full((1, 256)), full((128, 256)),
                  full((128, 256)), full((1, 256)), full((1, 256)), full((1, 256)), full((1, 256)), full((1, 256))],
        out_specs=spec,
        out_shape=jax.ShapeDtypeStruct((b, t, 256), F32),
        scratch_shapes=[pltpu.VMEM((nb, tb, 256), F32)] * 8 + [pltpu.VMEM((nb, STACK, STACK), F32)],
        compiler_params=_cparams(("parallel", "arbitrary")),
        name="rwkv_mix",
    )(cols3, cols3, row(mu), row(w0), w2p, row(a0), a2p, g2, row(k_k), row(k_a), row(r_k), row(ln_w), row(ln_b))


def _rot_cols(w, half):
    g = w.reshape(w.shape[0], -1, 2, half)
    return jnp.concatenate([-g[:, :, 1:2, :], g[:, :, 0:1, :]], axis=2).reshape(w.shape)


def pack_w_in(w):
    d = w.shape[0]
    rw = w[:, 0:1024]
    dq, dk, dv = w[:, 1024:1280], w[:, 1280:1536], w[:, 1536:1792]
    cq, ckv, kr = w[:, 1792:2048], w[:, 2048:2176], w[:, 2176:2208]
    hg = w[:, 2208:3232]
    z = lambda n: jnp.zeros((d, n), w.dtype)
    kra = jnp.concatenate([z(MLA_NOPE), kr, z(128 - MLA_NOPE - MLA_ROPE)], axis=1)
    krb = jnp.concatenate([z(MLA_NOPE), _rot_cols(kr, MLA_ROPE // 2), z(128 - MLA_NOPE - MLA_ROPE)], axis=1)
    packed = jnp.concatenate([rw, dq, dk, dv, hg, cq, ckv, kra, krb], axis=1)
    return packed.astype(BF16)


def pack_mla(w_uq, w_ukv):
    rq = w_uq.shape[0]
    q = w_uq.reshape(rq, N_HEADS, MLA_NOPE + MLA_ROPE)
    nope, rope = q[..., :MLA_NOPE], q[..., MLA_NOPE:]
    pad = jnp.zeros((rq, N_HEADS, 128 - MLA_NOPE - MLA_ROPE), w_uq.dtype)
    wq1 = jnp.concatenate([nope, rope, pad], axis=-1).reshape(rq, N_HEADS * 128)
    rrot = _rot_cols(rope.reshape(rq, N_HEADS * MLA_ROPE), MLA_ROPE // 2).reshape(rq, N_HEADS, MLA_ROPE)
    wq2 = jnp.concatenate([jnp.zeros_like(nope), rrot, pad], axis=-1).reshape(rq, N_HEADS * 128)
    rk = w_ukv.shape[0]
    kv = w_ukv.reshape(rk, N_HEADS, MLA_NOPE + HEAD_DIM)
    wk = jnp.concatenate([kv[..., :MLA_NOPE], jnp.zeros((rk, N_HEADS, 128 - MLA_NOPE), w_ukv.dtype)],
                         axis=-1).reshape(rk, N_HEADS * 128)
    vz = jnp.zeros((rk, HEAD_DIM), w_ukv.dtype)
    wv = jnp.concatenate([x for h in range(N_HEADS)
                          for x in ((kv[:, h, MLA_NOPE:], vz) if h % 2 == 0 else (vz, kv[:, h, MLA_NOPE:]))], axis=1)
    return wq1.astype(BF16), wq2.astype(BF16), wk.astype(BF16), wv.astype(BF16)


def kernel(x, positions, mix_norm, w_in, w_out, rwkv_mu, rwkv_w0, rwkv_w2, rwkv_a0, rwkv_a2, rwkv_g2, rwkv_k_k,
           rwkv_k_a, rwkv_r_k, rwkv_ln_w, rwkv_ln_b, mla_q_norm, mla_w_uq, mla_kv_norm, mla_w_ukv, hgrn_lb_logits,
           hgrn_g_norm, ffn_norm, ffn_w1, ffn_w3, ffn_w2, moe_router, moe_w1, moe_w3, moe_w2, final_norm):
    b, t, d = x.shape
    n = b * t
    depth = w_in.shape[0]
    x2 = x.reshape(n, d)
    cd, sd, cm, sm = rope_tables(positions.reshape(n, 1))
    lora_pad = jnp.zeros((rwkv_w2.shape[1], D_GROUP), F32)
    fused_final = False
    for layer in range(depth):
        cols = in_proj(x2, mix_norm[layer], pack_w_in(w_in[layer]))
        cols3 = cols.reshape(b, t, N_PACK)
        y_a = rwkv_mix(cols3, rwkv_mu[layer], rwkv_w0[layer],
                       jnp.concatenate([rwkv_w2[layer], lora_pad], axis=0).astype(BF16), rwkv_a0[layer],
                       jnp.concatenate([lora_pad, rwkv_a2[layer]], axis=0).astype(BF16),
                       rwkv_g2[layer].astype(BF16), rwkv_k_k[layer], rwkv_k_a[layer], rwkv_r_k[layer],
                       rwkv_ln_w[layer], rwkv_ln_b[layer])
        y_b = dilated_attention(cols3, cd.reshape(b, t, -1), sd.reshape(b, t, -1))
        q_m, k_m, v_m = mla_prep(cols, cm, sm, mla_q_norm[layer], mla_kv_norm[layer],
                                 *pack_mla(mla_w_uq[layer], mla_w_ukv[layer]))
        y_c = mla_flash(q_m.reshape(b, t, -1), k_m.reshape(b, t, -1), v_m.reshape(b, t, -1))
        y_d = hgrn(cols3, hgrn_lb_logits, hgrn_g_norm[layer], layer)
        ys = [y.reshape(n, D_GROUP) for y in (y_a, y_b, y_c, y_d)]
        x2, h = out_proj(x2, ys, w_out[layer].astype(BF16), ffn_norm[layer])
        j = layer // 2
        if layer % 2 == 0:
            x2 = ffn_dense(h, x2, ffn_w1[j].astype(BF16), ffn_w3[j].astype(BF16), ffn_w2[j].astype(BF16))
        else:
            route, hp = moe_route(x2, ffn_norm[layer], jnp.pad(moe_router[j], ((0, 0), (0, 128 - N_EXPERTS))))
            rows_pad = TOP_K * n + N_EXPERTS * MOE_TILE
            pos, tile_expert, tile_valid = moe_plan(route, rows_pad)
            xs = moe_scatter(pos, hp, rows_pad)
            ys = moe_experts(tile_expert, tile_valid, xs, moe_w1[j].astype(BF16), moe_w3[j].astype(BF16),
                             moe_w2[j].astype(BF16))
            fused_final = layer == depth - 1
            x2 = moe_combine(pos, x2, route, ys, final_norm, fused_final)
    if not fused_final:
        x2 = final_rms(x2, final_norm)
    return x2.reshape(b, t, d)
```

```python
import functools

import jax
import jax.numpy as jnp
from jax import lax
from jax.experimental import pallas as pl
from jax.experimental.pallas import tpu as pltpu

F32 = jnp.float32
BF16 = jnp.bfloat16

D_GROUP = 256
HEAD_DIM = 64
N_HEADS = 4
RMS_EPS = 1e-6
RWKV_LN_EPS = 64e-5
ROPE_THETA = 10000.0
NEG_INF = -1e30
N_EXPERTS = 8
MLA_ROPE = 32
MLA_NOPE = 64

VMEM_LIMIT = 56 * 1024 * 1024

C_RWKV = 0
C_DIL = 1024
C_HGRN = 1792
C_MLA = 2816
N_PACK = 3456


def _cparams(sem, vmem=VMEM_LIMIT):
    return pltpu.CompilerParams(dimension_semantics=sem, vmem_limit_bytes=vmem)


def _rms(x, w):
    return x * lax.rsqrt(jnp.mean(x * x, axis=-1, keepdims=True) + RMS_EPS) * w


def _dot(a, b):
    return jnp.dot(a, b, preferred_element_type=F32)


def _dot_nt(a, b):
    return lax.dot_general(a, b, (((1,), (1,)), ((), ())), preferred_element_type=F32)


def _split_dot(a_f32, b_bf16):
    hi = a_f32.astype(BF16)
    lo = (a_f32 - hi.astype(F32)).astype(BF16)
    return _dot(hi, b_bf16) + _dot(lo, b_bf16)


def _split_dot_lhs(a_bf16, b_f32):
    hi = b_f32.astype(BF16)
    lo = (b_f32 - hi.astype(F32)).astype(BF16)
    return _dot(a_bf16, hi) + _dot(a_bf16, lo)


def _rope_kernel(pos_ref, cd_ref, sd_ref, cm_ref, sm_ref):
    pos = pos_ref[...].astype(F32)
    tm = pos.shape[0]
    lane = lax.broadcasted_iota(jnp.int32, (tm, 128), 1)
    j = (lane & 31).astype(F32)
    inv = jnp.exp(j * (-2.0 / HEAD_DIM * jnp.log(ROPE_THETA)))
    ang = pos * inv
    cd_ref[...] = jnp.cos(ang)
    sd_ref[...] = jnp.sin(ang)
    jm = (lane & 15).astype(F32)
    invm = jnp.exp(jm * (-2.0 / MLA_ROPE * jnp.log(ROPE_THETA)))
    angm = pos * invm
    rope_lane = (lane >= MLA_NOPE) & (lane < MLA_NOPE + MLA_ROPE)
    cm_ref[...] = jnp.where(rope_lane, jnp.cos(angm), jnp.where(lane < MLA_NOPE, 1.0, 0.0))
    sm_ref[...] = jnp.where(rope_lane, jnp.sin(angm), 0.0)


def rope_tables(pos_col, tm=512):
    n = pos_col.shape[0]
    return pl.pallas_call(
        _rope_kernel,
        grid=(n // tm,),
        in_specs=[pl.BlockSpec((tm, 1), lambda i: (i, 0))],
        out_specs=[pl.BlockSpec((tm, 128), lambda i: (i, 0))] * 4,
        out_shape=[jax.ShapeDtypeStruct((n, 128), F32)] * 4,
        compiler_params=_cparams(("parallel",)),
        name="rope_tables",
    )(pos_col)


def _in_proj_kernel(x_ref, nw_ref, w_ref, o_ref):
    h = _rms(x_ref[...], nw_ref[...]).astype(BF16)
    o_ref[...] = _dot(h, w_ref[...])


def in_proj(x2, norm_w, w_packed, tm=512):
    n, d = x2.shape
    nc = w_packed.shape[1]
    return pl.pallas_call(
        _in_proj_kernel,
        grid=(n // tm,),
        in_specs=[pl.BlockSpec((tm, d), lambda i: (i, 0)),
                  pl.BlockSpec((1, d), lambda i: (0, 0)),
                  pl.BlockSpec((d, nc), lambda i: (0, 0))],
        out_specs=pl.BlockSpec((tm, nc), lambda i: (i, 0)),
        out_shape=jax.ShapeDtypeStruct((n, nc), F32),
        compiler_params=_cparams(("parallel",)),
        name="in_proj",
    )(x2, norm_w.reshape(1, d), w_packed)


def _out_proj_kernel(x_ref, ya_ref, yb_ref, yc_ref, yd_ref, w_ref, nw_ref, xo_ref, h_ref):
    acc = x_ref[...]
    for g, y_ref in enumerate((ya_ref, yb_ref, yc_ref, yd_ref)):
        acc = acc + _dot(y_ref[...].astype(BF16), w_ref[g * D_GROUP:(g + 1) * D_GROUP, :])
    xo_ref[...] = acc
    h_ref[...] = _rms(acc, nw_ref[...]).astype(BF16)


def out_proj(x2, ys, w_out_bf16, norm_w, tm=512):
    n, d = x2.shape
    yspec = pl.BlockSpec((tm, D_GROUP), lambda i: (i, 0))
    return pl.pallas_call(
        _out_proj_kernel,
        grid=(n // tm,),
        in_specs=[pl.BlockSpec((tm, d), lambda i: (i, 0)), yspec, yspec, yspec, yspec,
                  pl.BlockSpec((d, d), lambda i: (0, 0)),
                  pl.BlockSpec((1, d), lambda i: (0, 0))],
        out_specs=[pl.BlockSpec((tm, d), lambda i: (i, 0)), pl.BlockSpec((tm, d), lambda i: (i, 0))],
        out_shape=[jax.ShapeDtypeStruct((n, d), F32), jax.ShapeDtypeStruct((n, d), BF16)],
        compiler_params=_cparams(("parallel",)),
        name="out_proj",
    )(x2, *ys, w_out_bf16, norm_w.reshape(1, d))


def _ffn_kernel(h_ref, x_ref, w1_ref, w3_ref, w2_ref, o_ref, acc_ref):
    j = pl.program_id(1)

    @pl.when(j == 0)
    def _():
        acc_ref[...] = x_ref[...]

    h = h_ref[...]
    a = _dot(h, w1_ref[...])
    b = _dot(h, w3_ref[...])
    act = (a * jax.nn.sigmoid(a) * b).astype(BF16)
    acc_ref[...] += _dot(act, w2_ref[...])

    @pl.when(j == pl.num_programs(1) - 1)
    def _():
        o_ref[...] = acc_ref[...]


def ffn_dense(h_bf16, x2, w1, w3, w2, tm=512, tf=1408):
    n, d = x2.shape
    f = w1.shape[1]
    return pl.pallas_call(
        _ffn_kernel,
        grid=(n // tm, f // tf),
        in_specs=[pl.BlockSpec((tm, d), lambda i, j: (i, 0)),
                  pl.BlockSpec((tm, d), lambda i, j: (i, 0)),
                  pl.BlockSpec((d, tf), lambda i, j: (0, j)),
                  pl.BlockSpec((d, tf), lambda i, j: (0, j)),
                  pl.BlockSpec((tf, d), lambda i, j: (j, 0))],
        out_specs=pl.BlockSpec((tm, d), lambda i, j: (i, 0)),
        out_shape=jax.ShapeDtypeStruct((n, d), F32),
        scratch_shapes=[pltpu.VMEM((tm, d), F32)],
        compiler_params=_cparams(("parallel", "arbitrary")),
        name="ffn_dense",
    )(h_bf16, x2, w1, w3, w2)


MOE_TILE = 512
TOP_K = 2
MOE_DMA_TOKENS = 1024


def _pack_bf16_pairs(x):
    bits = pltpu.bitcast(x.astype(BF16).astype(F32), jnp.uint32)
    half = bits.shape[1] // 2
    return (bits[:, :half] >> 16) | (bits[:, half:] & jnp.uint32(0xFFFF0000))


def _unpack_bf16_pairs(p):
    lo = pltpu.bitcast(p << 16, F32)
    hi = pltpu.bitcast(p & jnp.uint32(0xFFFF0000), F32)
    return jnp.concatenate([lo, hi], axis=1)


def _router_kernel(x_ref, nw_ref, wr_ref, route_ref, hp_ref):
    h = _rms(x_ref[...], nw_ref[...])
    logits = jnp.dot(h, wr_ref[...], preferred_element_type=F32, precision=lax.Precision.HIGHEST)
    lane = lax.broadcasted_iota(jnp.int32, logits.shape, 1)
    logits = jnp.where(lane < N_EXPERTS, logits, -jnp.inf)
    m1 = jnp.max(logits, axis=-1, keepdims=True)
    i1 = jnp.min(jnp.where(logits == m1, lane, 128), axis=-1, keepdims=True)
    rest = jnp.where(lane == i1, -jnp.inf, logits)
    m2 = jnp.max(rest, axis=-1, keepdims=True)
    i2 = jnp.min(jnp.where(rest == m2, lane, 128), axis=-1, keepdims=True)
    e2 = jnp.exp(m2 - m1)
    g1 = 1.0 / (1.0 + e2)
    g2 = e2 * g1
    route_ref[...] = jnp.where(lane == 0, i1.astype(F32), jnp.where(lane == 1, i2.astype(F32),
                               jnp.where(lane == 2, g1, jnp.where(lane == 3, g2, 0.0))))
    hp_ref[...] = _pack_bf16_pairs(h)


def moe_route(x2, norm_w, w_router_pad, tm=512):
    n, d = x2.shape
    return pl.pallas_call(
        _router_kernel,
        grid=(n // tm,),
        in_specs=[pl.BlockSpec((tm, d), lambda i: (i, 0)),
                  pl.BlockSpec((1, d), lambda i: (0, 0)),
                  pl.BlockSpec((d, 128), lambda i: (0, 0))],
        out_specs=[pl.BlockSpec((tm, 128), lambda i: (i, 0)), pl.BlockSpec((tm, d // 2), lambda i: (i, 0))],
        out_shape=[jax.ShapeDtypeStruct((n, 128), F32), jax.ShapeDtypeStruct((n, d // 2), jnp.uint32)],
        compiler_params=_cparams(("parallel",)),
        name="moe_router",
    )(x2, norm_w.reshape(1, d), w_router_pad)


def moe_plan(route, n_rows_pad):
    e = route[:, :TOP_K].astype(jnp.int32)
    onehot = (e.reshape(-1, 1) == jnp.arange(N_EXPERTS, dtype=jnp.int32)[None, :]).astype(jnp.int32)
    csum = jnp.cumsum(onehot, axis=0)
    counts = csum[-1]
    rank = jnp.sum((csum - onehot) * onehot, axis=1)
    gsz = ((counts + MOE_TILE - 1) // MOE_TILE) * MOE_TILE
    gend = jnp.cumsum(gsz)
    goff = gend - gsz
    pos = (goff[e.reshape(-1)] + rank).astype(jnp.int32)
    tile_start = jnp.arange(n_rows_pad // MOE_TILE, dtype=jnp.int32) * MOE_TILE
    tile_expert = jnp.minimum(jnp.sum((tile_start[:, None] >= gend[None, :]).astype(jnp.int32), axis=1),
                              N_EXPERTS - 1).astype(jnp.int32)
    tile_valid = (tile_start < gend[-1]).astype(jnp.int32)
    return pos, tile_expert, tile_valid


def _moe_scatter_kernel(pos_ref, hp_ref, xs_in_ref, xs_ref, sem):
    del xs_in_ref
    i = pl.program_id(0)
    ts = hp_ref.shape[0]

    def copy(r, slot):
        dst = pos_ref[(i * ts + r) * TOP_K + slot]
        return pltpu.make_async_copy(hp_ref.at[pl.ds(r, 1), :], xs_ref.at[pl.ds(dst, 1), :], sem)

    def issue(r, carry):
        for slot in range(TOP_K):
            copy(r, slot).start(priority=slot)
        return carry

    lax.fori_loop(0, ts, issue, 0, unroll=8)
    for slot in range(TOP_K):
        pltpu.make_async_copy(hp_ref, xs_ref.at[pl.ds(0, ts), :], sem).wait()


def moe_scatter(pos, hp, n_rows_pad, ts=MOE_DMA_TOKENS):
    n, dh = hp.shape
    xs0 = jnp.zeros((n_rows_pad, dh), jnp.uint32)
    return pl.pallas_call(
        _moe_scatter_kernel,
        grid_spec=pltpu.PrefetchScalarGridSpec(
            num_scalar_prefetch=1,
            grid=(n // ts,),
            in_specs=[pl.BlockSpec((ts, dh), lambda i, pos_: (i, 0)),
                      pl.BlockSpec(memory_space=pl.ANY)],
            out_specs=pl.BlockSpec(memory_space=pl.ANY),
            scratch_shapes=[pltpu.SemaphoreType.DMA],
        ),
        out_shape=jax.ShapeDtypeStruct((n_rows_pad, dh), jnp.uint32),
        input_output_aliases={2: 0},
        compiler_params=_cparams(("arbitrary",)),
        name="moe_scatter",
    )(pos, hp, xs0)


def _moe_expert_kernel(te_ref, tv_ref, xs_ref, w1_ref, w3_ref, w2_ref, ys_ref, h_ref, acc_ref):
    i = pl.program_id(0)
    j = pl.program_id(1)

    @pl.when(tv_ref[i] > 0)
    def _():
        @pl.when(j == 0)
        def _():
            h_ref[...] = _unpack_bf16_pairs(xs_ref[...]).astype(BF16)
            acc_ref[...] = jnp.zeros_like(acc_ref)

        h = h_ref[...]
        a = _dot(h, w1_ref[...])
        b = _dot(h, w3_ref[...])
        act = (a * jax.nn.sigmoid(a) * b).astype(BF16)
        acc_ref[...] += _dot(act, w2_ref[...])

        @pl.when(j == pl.num_programs(1) - 1)
        def _():
            ys_ref[...] = _pack_bf16_pairs(acc_ref[...])

    @pl.when((tv_ref[i] == 0) & (j == pl.num_programs(1) - 1))
    def _():
        ys_ref[...] = jnp.zeros_like(ys_ref)


def moe_experts(tile_expert, tile_valid, xs, w1, w3, w2, tf=1792):
    rows, dh = xs.shape
    d = 2 * dh
    f = w1.shape[2]
    tm = MOE_TILE
    return pl.pallas_call(
        _moe_expert_kernel,
        grid_spec=pltpu.PrefetchScalarGridSpec(
            num_scalar_prefetch=2,
            grid=(rows // tm, f // tf),
            in_specs=[pl.BlockSpec((tm, dh), lambda i, j, te, tv: (i, 0)),
                      pl.BlockSpec((None, d, tf), lambda i, j, te, tv: (te[i], 0, j * tv[i])),
                      pl.BlockSpec((None, d, tf), lambda i, j, te, tv: (te[i], 0, j * tv[i])),
                      pl.BlockSpec((None, tf, d), lambda i, j, te, tv: (te[i], j * tv[i], 0))],
            out_specs=pl.BlockSpec((tm, dh), lambda i, j, te, tv: (i, 0)),
            scratch_shapes=[pltpu.VMEM((tm, d), BF16), pltpu.VMEM((tm, d), F32)],
        ),
        out_shape=jax.ShapeDtypeStruct((rows, dh), jnp.uint32),
        compiler_params=_cparams(("arbitrary", "arbitrary")),
        name="moe_experts",
    )(tile_expert, tile_valid, xs, w1, w3, w2)


def _moe_combine_kernel(pos_ref, x_ref, route_ref, nw_ref, ys_ref, o_ref, buf_ref, sem, *, final):
    i = pl.program_id(0)
    tc = x_ref.shape[0]

    def copy(r, slot):
        src = pos_ref[(i * tc + r) * TOP_K + slot]
        return pltpu.make_async_copy(ys_ref.at[pl.ds(src, 1), :], buf_ref.at[slot, pl.ds(r, 1), :], sem)

    def issue(r, carry):
        for slot in range(TOP_K):
            copy(r, slot).start(priority=slot)
        return carry

    lax.fori_loop(0, tc, issue, 0, unroll=8)
    for slot in range(TOP_K):
        pltpu.make_async_copy(ys_ref.at[pl.ds(0, tc), :], buf_ref.at[slot], sem).wait()
    route = route_ref[...]
    g1 = route[:, 2:3]
    g2 = route[:, 3:4]
    out = x_ref[...] + g1 * _unpack_bf16_pairs(buf_ref[0]) + g2 * _unpack_bf16_pairs(buf_ref[1])
    if final:
        out = _rms(out, nw_ref[...])
    o_ref[...] = out


def moe_combine(pos, x2, route, ys, norm_w, final, tc=MOE_DMA_TOKENS):
    n, d = x2.shape
    return pl.pallas_call(
        functools.partial(_moe_combine_kernel, final=final),
        grid_spec=pltpu.PrefetchScalarGridSpec(
            num_scalar_prefetch=1,
            grid=(n // tc,),
            in_specs=[pl.BlockSpec((tc, d), lambda i, pos_: (i, 0)),
                      pl.BlockSpec((tc, 128), lambda i, pos_: (i, 0)),
                      pl.BlockSpec((1, d), lambda i, pos_: (0, 0)),
                      pl.BlockSpec(memory_space=pl.ANY)],
            out_specs=pl.BlockSpec((tc, d), lambda i, pos_: (i, 0)),
            scratch_shapes=[pltpu.VMEM((TOP_K, tc, d // 2), jnp.uint32), pltpu.SemaphoreType.DMA],
        ),
        out_shape=jax.ShapeDtypeStruct((n, d), F32),
        compiler_params=_cparams(("arbitrary",)),
        name="moe_combine",
    )(pos, x2, route, norm_w.reshape(1, d), ys)


def _final_norm_kernel(x_ref, nw_ref, o_ref):
    o_ref[...] = _rms(x_ref[...], nw_ref[...])


def final_rms(x2, norm_w, tm=1024):
    n, d = x2.shape
    return pl.pallas_call(
        _final_norm_kernel,
        grid=(n // tm,),
        in_specs=[pl.BlockSpec((tm, d), lambda i: (i, 0)), pl.BlockSpec((1, d), lambda i: (0, 0))],
        out_specs=pl.BlockSpec((tm, d), lambda i: (i, 0)),
        out_shape=jax.ShapeDtypeStruct((n, d), F32),
        compiler_params=_cparams(("parallel",)),
        name="final_norm",
    )(x2, norm_w.reshape(1, d))


def _mla_prep_kernel(cq_ref, ckv_ref, kra_ref, krb_ref, cm_ref, sm_ref, qn_ref, kvn_ref,
                     wq1_ref, wq2_ref, wk_ref, wv_ref, q_ref, k_ref, v_ref):
    qn = _rms(cq_ref[...], qn_ref[...]).astype(BF16)
    kvn = _rms(ckv_ref[...], kvn_ref[...]).astype(BF16)
    cm = cm_ref[...]
    sm = sm_ref[...]
    cm4 = jnp.concatenate([cm] * N_HEADS, axis=1)
    sm4 = jnp.concatenate([sm] * N_HEADS, axis=1)
    scale = (MLA_NOPE + MLA_ROPE) ** -0.5
    q = (_dot(qn, wq1_ref[...]) * cm4 + _dot(qn, wq2_ref[...]) * sm4) * scale
    q_ref[...] = q.astype(BF16)
    kr = kra_ref[...] * cm + krb_ref[...] * sm
    k = _dot(kvn, wk_ref[...]) + jnp.concatenate([kr] * N_HEADS, axis=1)
    k_ref[...] = k.astype(BF16)
    lane = lax.broadcasted_iota(jnp.int32, (cm.shape[0], N_HEADS * 128), 1)
    ones = jnp.where(((lane >> 6) & 1) == (((lane >> 7) & 1) ^ 1), 1.0, 0.0)
    v_ref[...] = (_dot(kvn, wv_ref[...]) + ones).astype(BF16)


def mla_prep(cols, cm, sm, q_norm, kv_norm, wq1, wq2, wk, wv, tm=512):
    n = cols.shape[0]
    b256 = C_MLA // 256
    b128 = C_MLA // 128
    full = lambda shape: pl.BlockSpec(shape, lambda i: (0, 0))
    return pl.pallas_call(
        _mla_prep_kernel,
        grid=(n // tm,),
        in_specs=[pl.BlockSpec((tm, 256), lambda i: (i, b256)),
                  pl.BlockSpec((tm, 128), lambda i: (i, b128 + 2)),
                  pl.BlockSpec((tm, 128), lambda i: (i, b128 + 3)),
                  pl.BlockSpec((tm, 128), lambda i: (i, b128 + 4)),
                  pl.BlockSpec((tm, 128), lambda i: (i, 0)),
                  pl.BlockSpec((tm, 128), lambda i: (i, 0)),
                  full((1, 256)), full((1, 128)),
                  full((256, 512)), full((256, 512)), full((128, 512)), full((128, 512))],
        out_specs=[pl.BlockSpec((tm, 512), lambda i: (i, 0))] * 3,
        out_shape=[jax.ShapeDtypeStruct((n, 512), BF16)] * 3,
        compiler_params=_cparams(("parallel",)),
        name="mla_prep",
    )(cols, cols, cols, cols, cm, sm, q_norm.reshape(1, 256), kv_norm.reshape(1, 128), wq1, wq2, wk, wv)


def _mla_flash_kernel(q_ref, k_ref, v_ref, o_ref, *, tq, tk):
    i = pl.program_id(2)
    lane = lax.broadcasted_iota(jnp.int32, (tq, 128), 1)
    qpos = i * tq + lax.broadcasted_iota(jnp.int32, (tq, tk), 0)
    kloc = lax.broadcasted_iota(jnp.int32, (tq, tk), 1)
    nfull = (i * tq) // tk
    qs = [q_ref[:, h * 128:(h + 1) * 128] for h in range(2)]

    def step(j, carry, masked):
        ks = pl.multiple_of(j * tk, tk)
        ss = [_dot_nt(qs[h], k_ref[pl.ds(ks, tk), h * 128:(h + 1) * 128]) for h in range(2)]
        if masked:
            mask = ks + kloc <= qpos
            ss = [jnp.where(mask, s, NEG_INF) for s in ss]
        mns = [jnp.maximum(carry[h][0], jnp.max(ss[h], axis=-1, keepdims=True)) for h in range(2)]
        ps = [jnp.exp((ss[h] - mns[h]).astype(BF16)) for h in range(2)]
        als = [jnp.exp(carry[h][0] - mns[h]) for h in range(2)]
        pvs = [_dot(ps[h], v_ref[pl.ds(ks, tk), h * 128:(h + 1) * 128]) for h in range(2)]
        return tuple((mns[h], carry[h][1] * als[h] + pvs[h]) for h in range(2))

    init = (jnp.full((tq, 1), NEG_INF, F32), jnp.zeros((tq, 128), F32))
    res = lax.fori_loop(0, nfull, functools.partial(step, masked=False), (init, init))
    res = step(nfull, res, True)
    acc = jnp.where(lane < HEAD_DIM, res[0][1], res[1][1])
    den = jnp.where(lane < HEAD_DIM, pltpu.roll(res[0][1], HEAD_DIM, 1), pltpu.roll(res[1][1], HEAD_DIM, 1))
    o_ref[...] = acc / den


def mla_flash(q3, k3, v3, tq=256, tk=512):
    b, t, _ = q3.shape
    assert tk % tq == 0 and t % tk == 0
    return pl.pallas_call(
        functools.partial(_mla_flash_kernel, tq=tq, tk=tk),
        grid=(b, 2, t // tq),
        in_specs=[pl.BlockSpec((None, tq, 256), lambda b_, p, i: (b_, i, p)),
                  pl.BlockSpec((None, t, 256), lambda b_, p, i: (b_, 0, p)),
                  pl.BlockSpec((None, t, 256), lambda b_, p, i: (b_, 0, p))],
        out_specs=pl.BlockSpec((None, tq, 128), lambda b_, p, i: (b_, i, p)),
        out_shape=jax.ShapeDtypeStruct((b, t, 256), F32),
        compiler_params=_cparams(("parallel", "parallel", "arbitrary")),
        name="mla_flash",
    )(q3, k3, v3)


DILATED_PATTERNS = ((128, 1), (512, 4), (2048, 16))
DIL_BLOCK = 128
DIL_UNROLL = 4


def _rotate_half_heads(x):
    lane = lax.broadcasted_iota(jnp.int32, x.shape, 1)
    width = x.shape[1]
    half = HEAD_DIM // 2
    return jnp.where((lane & (HEAD_DIM - 1)) < half, -pltpu.roll(x, width - half, 1), pltpu.roll(x, half, 1))


def _dil_block(qb, kb, vb, mask, out):
    nq = qb.shape[0]
    lane = lax.broadcasted_iota(jnp.int32, (nq, 128), 1)
    kbb = kb.astype(BF16)
    vbb = vb.astype(BF16)
    hms = [(lane >> 6) == h for h in range(2)]
    ss = [jnp.where(mask, _dot_nt(jnp.where(hm, qb, 0.0).astype(BF16), kbb), NEG_INF) for hm in hms]
    yield
    ms = [jnp.max(s, axis=-1, keepdims=True) for s in ss]
    ps = [jnp.exp(s - m) for s, m in zip(ss, ms)]
    yield
    ls = [jnp.sum(p, axis=-1, keepdims=True) for p in ps]
    ohs = [_dot(p.astype(BF16), vbb) for p in ps]
    yield
    out[0] = jnp.where(hms[0], ohs[0] / ls[0], ohs[1] / ls[1])
    out[1] = jnp.where(hms[0], ms[0] + jnp.log(ls[0]), ms[1] + jnp.log(ls[1]))


def _dil_kernel(qraw_ref, kraw_ref, v_ref, cd_ref, sd_ref, o_ref, q_ref, k_ref, m_ref, z_ref, acc_ref, *, t):
    blk = DIL_BLOCK
    cd = cd_ref[...]
    sd = sd_ref[...]
    qraw = qraw_ref[...]
    kraw = kraw_ref[...]
    q_ref[...] = (qraw * cd + _rotate_half_heads(qraw) * sd) * (HEAD_DIM ** -0.5)
    k_ref[...] = kraw * cd + _rotate_half_heads(kraw) * sd
    for pi, (window, dil) in enumerate(DILATED_PATTERNS):
        assert window // dil == blk
        length = t // dil
        nb = length // blk
        nk = 2 * blk if nb > 1 else blk
        uq = lax.broadcasted_iota(jnp.int32, (blk, nk), 0)
        uk = lax.broadcasted_iota(jnp.int32, (blk, nk), 1)

        def one(it, dil=dil, nb=nb, nk=nk, uq=uq, uk=uk, first=(pi == 0)):
            r = it // nb
            jb = it % nb
            kb0 = jnp.maximum(jb - 1, 0) * blk
            qs = r + dil * blk * jb
            ks = r + dil * kb0
            if dil == 1:
                qidx = pl.ds(pl.multiple_of(qs, blk), blk)
                kidx = pl.ds(pl.multiple_of(ks, blk), nk)
            else:
                qidx = pl.ds(qs, blk, stride=dil)
                kidx = pl.ds(ks, nk, stride=dil)
            dist = (jb * blk + uq) - (kb0 + uk)
            mask = (dist >= 0) & (dist <= blk)
            out = [None, None]
            yield from _dil_block(q_ref[qidx, :], k_ref[kidx, :], v_ref[kidx, :], mask, out)
            o, lse = out
            if first:
                m_ref[qidx, :] = lse
                z_ref[qidx, :] = jnp.ones_like(lse)
                acc_ref[qidx, :] = o
            else:
                m_old = m_ref[qidx, :]
                m_new = jnp.maximum(m_old, lse)
                a = jnp.exp(m_old - m_new)
                b = jnp.exp(lse - m_new)
                m_ref[qidx, :] = m_new
                z_ref[qidx, :] = z_ref[qidx, :] * a + b
                acc_ref[qidx, :] = acc_ref[qidx, :] * a + o * b

        def body(it, carry, one=one):
            _round_robin([one(it * DIL_UNROLL + u) for u in range(DIL_UNROLL)])
            return carry

        assert (dil * nb) % DIL_UNROLL == 0
        lax.fori_loop(0, dil * nb // DIL_UNROLL, body, 0)
    o_ref[...] = acc_ref[...] / z_ref[...]


def dilated_attention(cols3, cd3, sd3):
    b, t, _ = cols3.shape
    assert t % (DIL_BLOCK * DILATED_PATTERNS[-1][1]) == 0
    c0 = C_DIL // 128
    col = lambda j: pl.BlockSpec((None, t, 128), lambda i, p: (i, 0, j + p))
    tab = pl.BlockSpec((None, t, 128), lambda i, p: (i, 0, 0))
    return pl.pallas_call(
        functools.partial(_dil_kernel, t=t),
        grid=(b, 2),
        in_specs=[col(c0), col(c0 + 2), col(c0 + 4), tab, tab],
        out_specs=pl.BlockSpec((None, t, 128), lambda i, p: (i, 0, p)),
        out_shape=jax.ShapeDtypeStruct((b, t, 256), F32),
        scratch_shapes=[pltpu.VMEM((t, 128), F32)] * 5,
        compiler_params=_cparams(("parallel", "parallel")),
        name="dilated_attention",
    )(cols3, cols3, cols3, cd3, sd3)


CHUNK = 64
STACK = N_HEADS * CHUNK


def _stack_heads(x):
    lane = lax.broadcasted_iota(jnp.int32, x.shape, 1)
    return jnp.concatenate([jnp.where((lane >> 6) == h, x, 0.0) for h in range(N_HEADS)], axis=0)


def _unstack_heads(y):
    out = y[0:CHUNK]
    for h in range(1, N_HEADS):
        out = out + y[h * CHUNK:(h + 1) * CHUNK]
    return out


def _round_robin(stages):
    live = list(stages)
    while live:
        nxt = []
        for g in live:
            try:
                next(g)
                nxt.append(g)
            except StopIteration:
                pass
        live = nxt


def _head_mean_matrix():
    r = lax.broadcasted_iota(jnp.int32, (256, 256), 0)
    c = lax.broadcasted_iota(jnp.int32, (256, 256), 1)
    return jnp.where((r >> 6) == (c >> 6), 1.0 / HEAD_DIM, 0.0).astype(BF16)


HGRN_HALVES = (32, 16, 8, 4, 2, 1)


def _hgrn_sum_matrix():
    t = lax.broadcasted_iota(jnp.int32, (CHUNK, CHUNK), 0)
    j = lax.broadcasted_iota(jnp.int32, (CHUNK, CHUNK), 1)
    tri = jnp.where(j <= t, 1.0, 0.0)
    blocks = [tri]
    for half in HGRN_HALVES:
        ref = (t // (2 * half)) * (2 * half) + half - 1
        blocks.append(tri - jnp.where(j <= ref, 1.0, 0.0))
    return jnp.concatenate(blocks, axis=0).astype(BF16)


def _hgrn_kernel(q_ref, f_ref, i_ref, g_ref, lbl_ref, gn_ref, o_ref, st_ref, *, layer, nchunk, nb):
    @pl.when(pl.program_id(1) == 0)
    def _():
        st_ref[...] = jnp.zeros_like(st_ref)

    logits = lbl_ref[...]
    e = jnp.exp(logits - jnp.max(logits, axis=0, keepdims=True))
    sm = e / jnp.sum(e, axis=0, keepdims=True)
    lb = jnp.zeros((1, 256), F32)
    for l in range(1, layer + 1):
        lb = lb + sm[l:l + 1, :]
    gn = gn_ref[...]
    summat = _hgrn_sum_matrix()
    hmean = _head_mean_matrix()
    r = lax.broadcasted_iota(jnp.int32, (STACK, STACK), 0)
    c = lax.broadcasted_iota(jnp.int32, (STACK, STACK), 1)
    same_head = (r >> 6) == (c >> 6)
    rc = r & (CHUNK - 1)
    cc = c & (CHUNK - 1)

    def one(bi, rows):
        qv = q_ref[bi, rows, :]
        qq = qv * jax.nn.sigmoid(qv)
        forget = lb + (1.0 - lb) * jax.nn.sigmoid(f_ref[bi, rows, :])
        logf = jnp.log(forget)
        kk = 1.0 - forget
        v = i_ref[bi, rows, :]
        yield
        sums = _split_dot_lhs(summat, logf)
        b = sums[0:CHUNK]
        a = jnp.where(rc == cc, _dot_nt(_stack_heads(qq).astype(BF16), _stack_heads(kk).astype(BF16)), 0.0)
        yield
        for li, half in enumerate(HGRN_HALVES):
            d = sums[(li + 1) * CHUNK:(li + 2) * CHUNK]
            ql = qq * jnp.exp(jnp.minimum(d, 0.0))
            kl = kk * jnp.exp(jnp.minimum(-d, 0.0))
            m = ((rc // (2 * half)) == (cc // (2 * half))) & ((rc % (2 * half)) >= half) & ((cc % (2 * half)) < half)
            a = a + jnp.where(m, _dot_nt(_stack_heads(ql).astype(BF16), _stack_heads(kl).astype(BF16)), 0.0)
            yield
        o = _unstack_heads(_dot(a.astype(BF16), _stack_heads(v).astype(BF16)))
        st = st_ref[bi]
        o = o + _dot_nt((qq * jnp.exp(b)).astype(BF16), st.astype(BF16))
        yield
        b_end = b[CHUNK - 1:CHUNK, :]
        kbar = kk * jnp.exp(b_end - b)
        upd = lax.dot_general(v.astype(BF16), kbar.astype(BF16), (((0,), (0,)), ((), ())), preferred_element_type=F32)
        st_ref[bi] = st * jnp.exp(b_end) + jnp.where(same_head, upd, 0.0)
        ms = _split_dot(o * o, hmean)
        yield
        gv = g_ref[bi, rows, :]
        o_ref[bi, rows, :] = o * lax.rsqrt(ms + RMS_EPS) * gn * (gv * jax.nn.sigmoid(gv))

    def chunk(ci, carry):
        rows = pl.ds(pl.multiple_of(ci * CHUNK, CHUNK), CHUNK)
        _round_robin([one(bi, rows) for bi in range(nb)])
        return carry

    lax.fori_loop(0, nchunk, chunk, 0)


SEQ_PER_STEP = 4


def hgrn(cols3, lb_logits, g_norm, layer, tb=512):
    b, t, _ = cols3.shape
    nb = SEQ_PER_STEP if b % SEQ_PER_STEP == 0 else 1
    c0 = C_HGRN // 256
    spec = lambda j: pl.BlockSpec((nb, tb, 256), lambda b_, i: (b_, i, j))
    depth = lb_logits.shape[0]
    return pl.pallas_call(
        functools.partial(_hgrn_kernel, layer=layer, nchunk=tb // CHUNK, nb=nb),
        grid=(b // nb, t // tb),
        in_specs=[spec(c0), spec(c0 + 1), spec(c0 + 2), spec(c0 + 3),
                  pl.BlockSpec((depth, 256), lambda b_, i: (0, 0)),
                  pl.BlockSpec((1, 256), lambda b_, i: (0, 0))],
        out_specs=spec(0),
        out_shape=jax.ShapeDtypeStruct((b, t, 256), F32),
        scratch_shapes=[pltpu.VMEM((nb, STACK, STACK), F32)],
        compiler_params=_cparams(("parallel", "arbitrary")),
        name="hgrn",
    )(cols3, cols3, cols3, cols3, lb_logits, g_norm.reshape(1, 256))


def _head_sum_matrix():
    r = lax.broadcasted_iota(jnp.int32, (256, 256), 0)
    c = lax.broadcasted_iota(jnp.int32, (256, 256), 1)
    return jnp.where((r >> 6) == (c >> 6), 1.0, 0.0).astype(BF16)


def _rwkv_prep_tile(a, last_prev, mu, w0, w2, a0, a2, g2, k_k, k_a, r_k):
    row = lax.broadcasted_iota(jnp.int32, a.shape, 0)
    prev = jnp.where(row == 0, last_prev, pltpu.roll(a, 1, 0))
    xs = a + (prev - a) * mu
    r = xs[:, 0:256]
    k = xs[:, 256:512]
    v = xs[:, 512:768]
    xwa = xs[:, 768:896]
    xg = xs[:, 896:1024]
    z = -(w0 + _dot(jnp.tanh(xwa).astype(BF16), w2))
    softplus = jnp.maximum(z, 0.0) + jnp.log(1.0 + jnp.exp(-jnp.abs(z)))
    lw = -jnp.exp(-softplus - 0.5)
    lr = jax.nn.sigmoid(a0 + _dot(xwa.astype(BF16), a2))
    g = _dot(jax.nn.sigmoid(xg).astype(BF16), g2)
    hsum = _head_sum_matrix()
    kk = k * k_k
    kk = kk / jnp.maximum(jnp.sqrt(_split_dot(kk * kk, hsum)), 1e-12)
    k2 = k * (1.0 + (lr - 1.0) * k_a)
    bonus = _split_dot(r * k2 * r_k, hsum) * v
    return r, lw, k2, v, kk, kk * lr, bonus, g


RWKV_MERGE_HALVES = (2, 4, 8, 16, 32)


def _compact_upper(x, half):
    return jnp.concatenate([x[s + half:s + 2 * half] for s in range(0, STACK, 2 * half)], axis=0)


def _expand_upper(c, half):
    zero = jnp.zeros((half, c.shape[1]), c.dtype)
    return jnp.concatenate([p for i in range(STACK // (2 * half)) for p in (zero, c[i * half:(i + 1) * half])], axis=0)


def _rwkv_kernel(c_ref, p_ref, mu_ref, w0_ref, w2_ref, a0_ref, a2_ref, g2_ref, kk_ref, ka_ref, rk_ref,
                 lnw_ref, lnb_ref, o_ref,
                 r_ref, lw_ref, k_ref, v_ref, kap_ref, at_ref, bonus_ref, g_ref, ht_ref, *, nchunk, nb):
    first = pl.program_id(1) == 0

    @pl.when(first)
    def _():
        ht_ref[...] = jnp.zeros_like(ht_ref)

    for bi in range(nb):
        outs = _rwkv_prep_tile(c_ref[bi], jnp.where(first, 0.0, p_ref[bi, 7:8, :]), mu_ref[...], w0_ref[...],
                               w2_ref[...], a0_ref[...], a2_ref[...], g2_ref[...], kk_ref[...], ka_ref[...],
                               rk_ref[...])
        for ref, val in zip((r_ref, lw_ref, k_ref, v_ref, kap_ref, at_ref, bonus_ref, g_ref), outs):
            ref[bi] = val

    t = lax.broadcasted_iota(jnp.int32, (CHUNK, CHUNK), 0)
    j = lax.broadcasted_iota(jnp.int32, (CHUNK, CHUNK), 1)
    tri = jnp.where(j <= t, 1.0, 0.0).astype(BF16)
    hmean = _head_mean_matrix()
    rr = lax.broadcasted_iota(jnp.int32, (STACK, STACK), 0)
    cc = lax.broadcasted_iota(jnp.int32, (STACK, STACK), 1)
    same_head = (rr >> 6) == (cc >> 6)
    rc = rr & (CHUNK - 1)
    sc = cc & (CHUNK - 1)
    eye = jnp.where(rr == cc, 1.0, 0.0)
    lnw = lnw_ref[...]
    lnb = lnb_ref[...]

    def one(bi, rows):
        lw = lw_ref[bi, rows, :]
        lc = _split_dot_lhs(tri, lw)
        l_end = lc[CHUNK - 1:CHUNK, :]
        kap = kap_ref[bi, rows, :]
        at = at_ref[bi, rows, :]
        k2 = k_ref[bi, rows, :]
        v = v_ref[bi, rows, :]
        inv = jnp.exp(-lc)
        fwd = jnp.exp(l_end - lc)
        s_kap = _stack_heads(kap * jnp.exp(lc - lw)).astype(BF16)
        s_r32 = _stack_heads(r_ref[bi, rows, :] * jnp.exp(lc))
        s_r = s_r32.astype(BF16)
        s_a = _stack_heads(at * inv).astype(BF16)
        s_k = _stack_heads(k2 * inv).astype(BF16)
        s_v = _stack_heads(v).astype(BF16)
        strict = rc > sc
        incl = rc >= sc
        yield
        a_ka = jnp.where(strict, _dot_nt(s_kap, s_a), 0.0)
        a_kk = jnp.where(strict, _dot_nt(s_kap, s_k), 0.0).astype(BF16)
        m1 = ((rc >> 1) == (sc >> 1)) & strict
        tinv = eye - jnp.where(m1, a_ka, 0.0)
        yield
        a_ra = jnp.where(incl, _dot_nt(s_r, s_a), 0.0).astype(BF16)
        a_rk = jnp.where(incl, _dot_nt(s_r, s_k), 0.0).astype(BF16)
        for half in RWKV_MERGE_HALVES:
            m = ((rc // (2 * half)) == (sc // (2 * half))) & ((rc % (2 * half)) >= half) & ((sc % (2 * half)) < half)
            off = jnp.where(m, a_ka, 0.0)
            tb = tinv.astype(BF16)
            if half % 8 == 0:
                x = _expand_upper(_dot(_compact_upper(off, half).astype(BF16), tb), half).astype(BF16)
                yield
                tinv = tinv - _expand_upper(_dot(_compact_upper(tinv, half).astype(BF16), x), half)
            else:
                x = _dot(off.astype(BF16), tb).astype(BF16)
                yield
                tinv = tinv - _dot(tb, x)
            yield
        tb = tinv.astype(BF16)
        kt_st = _dot(tb, s_kap)
        w_st = _dot(a_kk, s_v).astype(BF16)
        yield
        vt_st = _dot(tb, w_st)
        rt = _unstack_heads(s_r32 - _dot(a_ra, kt_st.astype(BF16)))
        yield
        y0 = _unstack_heads(_dot(a_rk, s_v) - _dot(a_ra, vt_st.astype(BF16)))
        kt = _unstack_heads(kt_st)
        vt = _unstack_heads(vt_st)
        ht = ht_ref[bi]
        htb = ht.astype(BF16)
        yield
        y = y0 + _dot_nt(rt.astype(BF16), htb)
        u = vt + _dot_nt(kt.astype(BF16), htb)
        yield
        lhs = jnp.concatenate([v, u], axis=0).astype(BF16)
        rhs = jnp.concatenate([k2 * fwd, -(at * fwd)], axis=0).astype(BF16)
        upd = lax.dot_general(lhs, rhs, (((0,), (0,)), ((), ())), preferred_element_type=F32)
        ht_ref[bi] = ht * jnp.exp(l_end) + jnp.where(same_head, upd, 0.0)
        mean = _split_dot(y, hmean)
        yield
        yc = y - mean
        var = _split_dot(yc * yc, hmean)
        yield
        yn = yc * lax.rsqrt(var + RWKV_LN_EPS) * lnw + lnb
        o_ref[bi, rows, :] = (yn + bonus_ref[bi, rows, :]) * g_ref[bi, rows, :]

    def chunk(ci, carry):
        rows = pl.ds(pl.multiple_of(ci * CHUNK, CHUNK), CHUNK)
        _round_robin([one(bi, rows) for bi in range(nb)])
        return carry

    lax.fori_loop(0, nchunk, chunk, 0)


def rwkv_mix(cols3, mu, w0, w2p, a0, a2p, g2, k_k, k_a, r_k, ln_w, ln_b, tb=512):
    b, t, _ = cols3.shape
    nb = SEQ_PER_STEP if b % SEQ_PER_STEP == 0 else 1
    spec = pl.BlockSpec((nb, tb, 256), lambda b_, i: (b_, i, 0))
    full = lambda shape: pl.BlockSpec(shape, lambda b_, i: (0, 0))
    row = lambda x: x.reshape(1, -1)
    return pl.pallas_call(
        functools.partial(_rwkv_kernel, nchunk=tb // CHUNK, nb=nb),
        grid=(b // nb, t // tb),
        in_specs=[pl.BlockSpec((nb, tb, 1024), lambda b_, i: (b_, i, C_RWKV // 1024)),
                  pl.BlockSpec((nb, 8, 1024), lambda b_, i: (b_, jnp.maximum(i * (tb // 8) - 1, 0), C_RWKV // 1024)),
                  full((1, 1024)), full((1, 256)), full((128, 256)), full((1, 256)), full((128, 256)),
                  full((128, 256)), full((1, 256)), full((1, 256)), full((1, 256)), full((1, 256)), full((1, 256))],
        out_specs=spec,
        out_shape=jax.ShapeDtypeStruct((b, t, 256), F32),
        scratch_shapes=[pltpu.VMEM((nb, tb, 256), F32)] * 8 + [pltpu.VMEM((nb, STACK, STACK), F32)],
        compiler_params=_cparams(("parallel", "arbitrary")),
        name="rwkv_mix",
    )(cols3, cols3, row(mu), row(w0), w2p, row(a0), a2p, g2, row(k_k), row(k_a), row(r_k), row(ln_w), row(ln_b))


def _rot_cols(w, half):
    g = w.reshape(w.shape[0], -1, 2, half)
    return jnp.concatenate([-g[:, :, 1:2, :], g[:, :, 0:1, :]], axis=2).reshape(w.shape)


def pack_w_in(w):
    d = w.shape[0]
    rw = w[:, 0:1024]
    dq, dk, dv = w[:, 1024:1280], w[:, 1280:1536], w[:, 1536:1792]
    cq, ckv, kr = w[:, 1792:2048], w[:, 2048:2176], w[:, 2176:2208]
    hg = w[:, 2208:3232]
    z = lambda n: jnp.zeros((d, n), w.dtype)
    kra = jnp.concatenate([z(MLA_NOPE), kr, z(128 - MLA_NOPE - MLA_ROPE)], axis=1)
    krb = jnp.concatenate([z(MLA_NOPE), _rot_cols(kr, MLA_ROPE // 2), z(128 - MLA_NOPE - MLA_ROPE)], axis=1)
    packed = jnp.concatenate([rw, dq, dk, dv, hg, cq, ckv, kra, krb], axis=1)
    return packed.astype(BF16)


def pack_mla(w_uq, w_ukv):
    rq = w_uq.shape[0]
    q = w_uq.reshape(rq, N_HEADS, MLA_NOPE + MLA_ROPE)
    nope, rope = q[..., :MLA_NOPE], q[..., MLA_NOPE:]
    pad = jnp.zeros((rq, N_HEADS, 128 - MLA_NOPE - MLA_ROPE), w_uq.dtype)
    wq1 = jnp.concatenate([nope, rope, pad], axis=-1).reshape(rq, N_HEADS * 128)
    rrot = _rot_cols(rope.reshape(rq, N_HEADS * MLA_ROPE), MLA_ROPE // 2).reshape(rq, N_HEADS, MLA_ROPE)
    wq2 = jnp.concatenate([jnp.zeros_like(nope), rrot, pad], axis=-1).reshape(rq, N_HEADS * 128)
    rk = w_ukv.shape[0]
    kv = w_ukv.reshape(rk, N_HEADS, MLA_NOPE + HEAD_DIM)
    wk = jnp.concatenate([kv[..., :MLA_NOPE], jnp.zeros((rk, N_HEADS, 128 - MLA_NOPE), w_ukv.dtype)],
                         axis=-1).reshape(rk, N_HEADS * 128)
    vz = jnp.zeros((rk, HEAD_DIM), w_ukv.dtype)
    wv = jnp.concatenate([x for h in range(N_HEADS)
                          for x in ((kv[:, h, MLA_NOPE:], vz) if h % 2 == 0 else (vz, kv[:, h, MLA_NOPE:]))], axis=1)
    return wq1.astype(BF16), wq2.astype(BF16), wk.astype(BF16), wv.astype(BF16)


def kernel(x, positions, mix_norm, w_in, w_out, rwkv_mu, rwkv_w0, rwkv_w2, rwkv_a0, rwkv_a2, rwkv_g2, rwkv_k_k,
           rwkv_k_a, rwkv_r_k, rwkv_ln_w, rwkv_ln_b, mla_q_norm, mla_w_uq, mla_kv_norm, mla_w_ukv, hgrn_lb_logits,
           hgrn_g_norm, ffn_norm, ffn_w1, ffn_w3, ffn_w2, moe_router, moe_w1, moe_w3, moe_w2, final_norm):
    b, t, d = x.shape
    n = b * t
    depth = w_in.shape[0]
    x2 = x.reshape(n, d)
    cd, sd, cm, sm = rope_tables(positions.reshape(n, 1))
    lora_pad = jnp.zeros((rwkv_w2.shape[1], D_GROUP), F32)
    fused_final = False
    for layer in range(depth):
        cols = in_proj(x2, mix_norm[layer], pack_w_in(w_in[layer]))
        cols3 = cols.reshape(b, t, N_PACK)
        y_a = rwkv_mix(cols3, rwkv_mu[layer], rwkv_w0[layer],
                       jnp.concatenate([rwkv_w2[layer], lora_pad], axis=0).astype(BF16), rwkv_a0[layer],
                       jnp.concatenate([lora_pad, rwkv_a2[layer]], axis=0).astype(BF16),
                       rwkv_g2[layer].astype(BF16), rwkv_k_k[layer], rwkv_k_a[layer], rwkv_r_k[layer],
                       rwkv_ln_w[layer], rwkv_ln_b[layer])
        y_b = dilated_attention(cols3, cd.reshape(b, t, -1), sd.reshape(b, t, -1))
        q_m, k_m, v_m = mla_prep(cols, cm, sm, mla_q_norm[layer], mla_kv_norm[layer],
                                 *pack_mla(mla_w_uq[layer], mla_w_ukv[layer]))
        y_c = mla_flash(q_m.reshape(b, t, -1), k_m.reshape(b, t, -1), v_m.reshape(b, t, -1))
        y_d = hgrn(cols3, hgrn_lb_logits, hgrn_g_norm[layer], layer)
        ys = [y.reshape(n, D_GROUP) for y in (y_a, y_b, y_c, y_d)]
        x2, h = out_proj(x2, ys, w_out[layer].astype(BF16), ffn_norm[layer])
        j = layer // 2
        if layer % 2 == 0:
            x2 = ffn_dense(h, x2, ffn_w1[j].astype(BF16), ffn_w3[j].astype(BF16), ffn_w2[j].astype(BF16))
        else:
            route, hp = moe_route(x2, ffn_norm[layer], jnp.pad(moe_router[j], ((0, 0), (0, 128 - N_EXPERTS))))
            rows_pad = TOP_K * n + N_EXPERTS * MOE_TILE
            pos, tile_expert, tile_valid = moe_plan(route, rows_pad)
            xs = moe_scatter(pos, hp, rows_pad)
            ys = moe_experts(tile_expert, tile_valid, xs, moe_w1[j].astype(BF16), moe_w3[j].astype(BF16),
                             moe_w2[j].astype(BF16))
            fused_final = layer == depth - 1
            x2 = moe_combine(pos, x2, route, ys, final_norm, fused_final)
    if not fused_final:
        x2 = final_rms(x2, final_norm)
    return x2.reshape(b, t, d)
```

```python
import functools

import jax
import jax.numpy as jnp
from jax import lax
from jax.experimental import pallas as pl
from jax.experimental.pallas import tpu as pltpu

F32 = jnp.float32
BF16 = jnp.bfloat16

D_GROUP = 256
HEAD_DIM = 64
N_HEADS = 4
RMS_EPS = 1e-6
RWKV_LN_EPS = 64e-5
ROPE_THETA = 10000.0
NEG_INF = -1e30
N_EXPERTS = 8
MLA_ROPE = 32
MLA_NOPE = 64

VMEM_LIMIT = 56 * 1024 * 1024

C_RWKV = 0
C_DIL = 1024
C_HGRN = 1792
C_MLA = 2816
N_PACK = 3456


def _cparams(sem, vmem=VMEM_LIMIT):
    return pltpu.CompilerParams(dimension_semantics=sem, vmem_limit_bytes=vmem)


def _rms(x, w):
    return x * lax.rsqrt(jnp.mean(x * x, axis=-1, keepdims=True) + RMS_EPS) * w


def _dot(a, b):
    return jnp.dot(a, b, preferred_element_type=F32)


def _dot_nt(a, b):
    return lax.dot_general(a, b, (((1,), (1,)), ((), ())), preferred_element_type=F32)


def _split_dot(a_f32, b_bf16):
    hi = a_f32.astype(BF16)
    lo = (a_f32 - hi.astype(F32)).astype(BF16)
    return _dot(hi, b_bf16) + _dot(lo, b_bf16)


def _split_dot_lhs(a_bf16, b_f32):
    hi = b_f32.astype(BF16)
    lo = (b_f32 - hi.astype(F32)).astype(BF16)
    return _dot(a_bf16, hi) + _dot(a_bf16, lo)


def _rope_kernel(pos_ref, cd_ref, sd_ref, cm_ref, sm_ref):
    pos = pos_ref[...].astype(F32)
    tm = pos.shape[0]
    lane = lax.broadcasted_iota(jnp.int32, (tm, 128), 1)
    j = (lane & 31).astype(F32)
    inv = jnp.exp(j * (-2.0 / HEAD_DIM * jnp.log(ROPE_THETA)))
    ang = pos * inv
    cd_ref[...] = jnp.cos(ang)
    sd_ref[...] = jnp.sin(ang)
    jm = (lane & 15).astype(F32)
    invm = jnp.exp(jm * (-2.0 / MLA_ROPE * jnp.log(ROPE_THETA)))
    angm = pos * invm
    rope_lane = (lane >= MLA_NOPE) & (lane < MLA_NOPE + MLA_ROPE)
    cm_ref[...] = jnp.where(rope_lane, jnp.cos(angm), jnp.where(lane < MLA_NOPE, 1.0, 0.0))
    sm_ref[...] = jnp.where(rope_lane, jnp.sin(angm), 0.0)


def rope_tables(pos_col, tm=512):
    n = pos_col.shape[0]
    return pl.pallas_call(
        _rope_kernel,
        grid=(n // tm,),
        in_specs=[pl.BlockSpec((tm, 1), lambda i: (i, 0))],
        out_specs=[pl.BlockSpec((tm, 128), lambda i: (i, 0))] * 4,
        out_shape=[jax.ShapeDtypeStruct((n, 128), F32)] * 4,
        compiler_params=_cparams(("parallel",)),
        name="rope_tables",
    )(pos_col)


def _in_proj_kernel(x_ref, nw_ref, w_ref, o_ref):
    h = _rms(x_ref[...], nw_ref[...]).astype(BF16)
    o_ref[...] = _dot(h, w_ref[...])


def in_proj(x2, norm_w, w_packed, tm=512):
    n, d = x2.shape
    nc = w_packed.shape[1]
    return pl.pallas_call(
        _in_proj_kernel,
        grid=(n // tm,),
        in_specs=[pl.BlockSpec((tm, d), lambda i: (i, 0)),
                  pl.BlockSpec((1, d), lambda i: (0, 0)),
                  pl.BlockSpec((d, nc), lambda i: (0, 0))],
        out_specs=pl.BlockSpec((tm, nc), lambda i: (i, 0)),
        out_shape=jax.ShapeDtypeStruct((n, nc), F32),
        compiler_params=_cparams(("parallel",)),
        name="in_proj",
    )(x2, norm_w.reshape(1, d), w_packed)


def _out_proj_kernel(x_ref, ya_ref, yb_ref, yc_ref, yd_ref, w_ref, nw_ref, xo_ref, h_ref):
    acc = x_ref[...]
    for g, y_ref in enumerate((ya_ref, yb_ref, yc_ref, yd_ref)):
        acc = acc + _dot(y_ref[...].astype(BF16), w_ref[g * D_GROUP:(g + 1) * D_GROUP, :])
    xo_ref[...] = acc
    h_ref[...] = _rms(acc, nw_ref[...]).astype(BF16)


def out_proj(x2, ys, w_out_bf16, norm_w, tm=512):
    n, d = x2.shape
    yspec = pl.BlockSpec((tm, D_GROUP), lambda i: (i, 0))
    return pl.pallas_call(
        _out_proj_kernel,
        grid=(n // tm,),
        in_specs=[pl.BlockSpec((tm, d), lambda i: (i, 0)), yspec, yspec, yspec, yspec,
                  pl.BlockSpec((d, d), lambda i: (0, 0)),
                  pl.BlockSpec((1, d), lambda i: (0, 0))],
        out_specs=[pl.BlockSpec((tm, d), lambda i: (i, 0)), pl.BlockSpec((tm, d), lambda i: (i, 0))],
        out_shape=[jax.ShapeDtypeStruct((n, d), F32), jax.ShapeDtypeStruct((n, d), BF16)],
        compiler_params=_cparams(("parallel",)),
        name="out_proj",
    )(x2, *ys, w_out_bf16, norm_w.reshape(1, d))


def _ffn_kernel(h_ref, x_ref, w1_ref, w3_ref, w2_ref, o_ref, acc_ref):
    j = pl.program_id(1)

    @pl.when(j == 0)
    def _():
        acc_ref[...] = x_ref[...]

    h = h_ref[...]
    a = _dot(h, w1_ref[...])
    b = _dot(h, w3_ref[...])
    act = (a * jax.nn.sigmoid(a) * b).astype(BF16)
    acc_ref[...] += _dot(act, w2_ref[...])

    @pl.when(j == pl.num_programs(1) - 1)
    def _():
        o_ref[...] = acc_ref[...]


def ffn_dense(h_bf16, x2, w1, w3, w2, tm=512, tf=1408):
    n, d = x2.shape
    f = w1.shape[1]
    return pl.pallas_call(
        _ffn_kernel,
        grid=(n // tm, f // tf),
        in_specs=[pl.BlockSpec((tm, d), lambda i, j: (i, 0)),
                  pl.BlockSpec((tm, d), lambda i, j: (i, 0)),
                  pl.BlockSpec((d, tf), lambda i, j: (0, j)),
                  pl.BlockSpec((d, tf), lambda i, j: (0, j)),
                  pl.BlockSpec((tf, d), lambda i, j: (j, 0))],
        out_specs=pl.BlockSpec((tm, d), lambda i, j: (i, 0)),
        out_shape=jax.ShapeDtypeStruct((n, d), F32),
        scratch_shapes=[pltpu.VMEM((tm, d), F32)],
        compiler_params=_cparams(("parallel", "arbitrary")),
        name="ffn_dense",
    )(h_bf16, x2, w1, w3, w2)


MOE_TILE = 512
TOP_K = 2
MOE_DMA_TOKENS = 1024


def _pack_bf16_pairs(x):
    bits = pltpu.bitcast(x.astype(BF16).astype(F32), jnp.uint32)
    half = bits.shape[1] // 2
    return (bits[:, :half] >> 16) | (bits[:, half:] & jnp.uint32(0xFFFF0000))


def _unpack_bf16_pairs(p):
    lo = pltpu.bitcast(p << 16, F32)
    hi = pltpu.bitcast(p & jnp.uint32(0xFFFF0000), F32)
    return jnp.concatenate([lo, hi], axis=1)


def _router_kernel(x_ref, nw_ref, wr_ref, route_ref, hp_ref):
    h = _rms(x_ref[...], nw_ref[...])
    logits = jnp.dot(h, wr_ref[...], preferred_element_type=F32, precision=lax.Precision.HIGHEST)
    lane = lax.broadcasted_iota(jnp.int32, logits.shape, 1)
    logits = jnp.where(lane < N_EXPERTS, logits, -jnp.inf)
    m1 = jnp.max(logits, axis=-1, keepdims=True)
    i1 = jnp.min(jnp.where(logits == m1, lane, 128), axis=-1, keepdims=True)
    rest = jnp.where(lane == i1, -jnp.inf, logits)
    m2 = jnp.max(rest, axis=-1, keepdims=True)
    i2 = jnp.min(jnp.where(rest == m2, lane, 128), axis=-1, keepdims=True)
    e2 = jnp.exp(m2 - m1)
    g1 = 1.0 / (1.0 + e2)
    g2 = e2 * g1
    route_ref[...] = jnp.where(lane == 0, i1.astype(F32), jnp.where(lane == 1, i2.astype(F32),
                               jnp.where(lane == 2, g1, jnp.where(lane == 3, g2, 0.0))))
    hp_ref[...] = _pack_bf16_pairs(h)


def moe_route(x2, norm_w, w_router_pad, tm=512):
    n, d = x2.shape
    return pl.pallas_call(
        _router_kernel,
        grid=(n // tm,),
        in_specs=[pl.BlockSpec((tm, d), lambda i: (i, 0)),
                  pl.BlockSpec((1, d), lambda i: (0, 0)),
                  pl.BlockSpec((d, 128), lambda i: (0, 0))],
        out_specs=[pl.BlockSpec((tm, 128), lambda i: (i, 0)), pl.BlockSpec((tm, d // 2), lambda i: (i, 0))],
        out_shape=[jax.ShapeDtypeStruct((n, 128), F32), jax.ShapeDtypeStruct((n, d // 2), jnp.uint32)],
        compiler_params=_cparams(("parallel",)),
        name="moe_router",
    )(x2, norm_w.reshape(1, d), w_router_pad)


def moe_plan(route, n_rows_pad):
    e = route[:, :TOP_K].astype(jnp.int32)
    onehot = (e.reshape(-1, 1) == jnp.arange(N_EXPERTS, dtype=jnp.int32)[None, :]).astype(jnp.int32)
    csum = jnp.cumsum(onehot, axis=0)
    counts = csum[-1]
    rank = jnp.sum((csum - onehot) * onehot, axis=1)
    gsz = ((counts + MOE_TILE - 1) // MOE_TILE) * MOE_TILE
    gend = jnp.cumsum(gsz)
    goff = gend - gsz
    pos = (goff[e.reshape(-1)] + rank).astype(jnp.int32)
    tile_start = jnp.arange(n_rows_pad // MOE_TILE, dtype=jnp.int32) * MOE_TILE
    tile_expert = jnp.minimum(jnp.sum((tile_start[:, None] >= gend[None, :]).astype(jnp.int32), axis=1),
                              N_EXPERTS - 1).astype(jnp.int32)
    tile_valid = (tile_start < gend[-1]).astype(jnp.int32)
    return pos, tile_expert, tile_valid


def _moe_scatter_kernel(pos_ref, hp_ref, xs_in_ref, xs_ref, sem):
    del xs_in_ref
    i = pl.program_id(0)
    ts = hp_ref.shape[0]

    def copy(r, slot):
        dst = pos_ref[(i * ts + r) * TOP_K + slot]
        return pltpu.make_async_copy(hp_ref.at[pl.ds(r, 1), :], xs_ref.at[pl.ds(dst, 1), :], sem)

    def issue(r, carry):
        for slot in range(TOP_K):
            copy(r, slot).start(priority=slot)
        return carry

    lax.fori_loop(0, ts, issue, 0, unroll=8)
    for slot in range(TOP_K):
        pltpu.make_async_copy(hp_ref, xs_ref.at[pl.ds(0, ts), :], sem).wait()


def moe_scatter(pos, hp, n_rows_pad, ts=MOE_DMA_TOKENS):
    n, dh = hp.shape
    xs0 = jnp.zeros((n_rows_pad, dh), jnp.uint32)
    return pl.pallas_call(
        _moe_scatter_kernel,
        grid_spec=pltpu.PrefetchScalarGridSpec(
            num_scalar_prefetch=1,
            grid=(n // ts,),
            in_specs=[pl.BlockSpec((ts, dh), lambda i, pos_: (i, 0)),
                      pl.BlockSpec(memory_space=pl.ANY)],
            out_specs=pl.BlockSpec(memory_space=pl.ANY),
            scratch_shapes=[pltpu.SemaphoreType.DMA],
        ),
        out_shape=jax.ShapeDtypeStruct((n_rows_pad, dh), jnp.uint32),
        input_output_aliases={2: 0},
        compiler_params=_cparams(("arbitrary",)),
        name="moe_scatter",
    )(pos, hp, xs0)


def _moe_expert_kernel(te_ref, tv_ref, xs_ref, w1_ref, w3_ref, w2_ref, ys_ref, h_ref, acc_ref):
    i = pl.program_id(0)
    j = pl.program_id(1)

    @pl.when(tv_ref[i] > 0)
    def _():
        @pl.when(j == 0)
        def _():
            h_ref[...] = _unpack_bf16_pairs(xs_ref[...]).astype(BF16)
            acc_ref[...] = jnp.zeros_like(acc_ref)

        h = h_ref[...]
        a = _dot(h, w1_ref[...])
        b = _dot(h, w3_ref[...])
        act = (a * jax.nn.sigmoid(a) * b).astype(BF16)
        acc_ref[...] += _dot(act, w2_ref[...])

        @pl.when(j == pl.num_programs(1) - 1)
        def _():
            ys_ref[...] = _pack_bf16_pairs(acc_ref[...])

    @pl.when((tv_ref[i] == 0) & (j == pl.num_programs(1) - 1))
    def _():
        ys_ref[...] = jnp.zeros_like(ys_ref)


def moe_experts(tile_expert, tile_valid, xs, w1, w3, w2, tf=1792):
    rows, dh = xs.shape
    d = 2 * dh
    f = w1.shape[2]
    tm = MOE_TILE
    return pl.pallas_call(
        _moe_expert_kernel,
        grid_spec=pltpu.PrefetchScalarGridSpec(
            num_scalar_prefetch=2,
            grid=(rows // tm, f // tf),
            in_specs=[pl.BlockSpec((tm, dh), lambda i, j, te, tv: (i, 0)),
                      pl.BlockSpec((None, d, tf), lambda i, j, te, tv: (te[i], 0, j * tv[i])),
                      pl.BlockSpec((None, d, tf), lambda i, j, te, tv: (te[i], 0, j * tv[i])),
                      pl.BlockSpec((None, tf, d), lambda i, j, te, tv: (te[i], j * tv[i], 0))],
            out_specs=pl.BlockSpec((tm, dh), lambda i, j, te, tv: (i, 0)),
            scratch_shapes=[pltpu.VMEM((tm, d), BF16), pltpu.VMEM((tm, d), F32)],
        ),
        out_shape=jax.ShapeDtypeStruct((rows, dh), jnp.uint32),
        compiler_params=_cparams(("arbitrary", "arbitrary")),
        name="moe_experts",
    )(tile_expert, tile_valid, xs, w1, w3, w2)


def _moe_combine_kernel(pos_ref, x_ref, route_ref, nw_ref, ys_ref, o_ref, buf_ref, sem, *, final):
    i = pl.program_id(0)
    tc = x_ref.shape[0]

    def copy(r, slot):
        src = pos_ref[(i * tc + r) * TOP_K + slot]
        return pltpu.make_async_copy(ys_ref.at[pl.ds(src, 1), :], buf_ref.at[slot, pl.ds(r, 1), :], sem)

    def issue(r, carry):
        for slot in range(TOP_K):
            copy(r, slot).start(priority=slot)
        return carry

    lax.fori_loop(0, tc, issue, 0, unroll=8)
    for slot in range(TOP_K):
        pltpu.make_async_copy(ys_ref.at[pl.ds(0, tc), :], buf_ref.at[slot], sem).wait()
    route = route_ref[...]
    g1 = route[:, 2:3]
    g2 = route[:, 3:4]
    out = x_ref[...] + g1 * _unpack_bf16_pairs(buf_ref[0]) + g2 * _unpack_bf16_pairs(buf_ref[1])
    if final:
        out = _rms(out, nw_ref[...])
    o_ref[...] = out


def moe_combine(pos, x2, route, ys, norm_w, final, tc=MOE_DMA_TOKENS):
    n, d = x2.shape
    return pl.pallas_call(
        functools.partial(_moe_combine_kernel, final=final),
        grid_spec=pltpu.PrefetchScalarGridSpec(
            num_scalar_prefetch=1,
            grid=(n // tc,),
            in_specs=[pl.BlockSpec((tc, d), lambda i, pos_: (i, 0)),
                      pl.BlockSpec((tc, 128), lambda i, pos_: (i, 0)),
                      pl.BlockSpec((1, d), lambda i, pos_: (0, 0)),
                      pl.BlockSpec(memory_space=pl.ANY)],
            out_specs=pl.BlockSpec((tc, d), lambda i, pos_: (i, 0)),
            scratch_shapes=[pltpu.VMEM((TOP_K, tc, d // 2), jnp.uint32), pltpu.SemaphoreType.DMA],
        ),
        out_shape=jax.ShapeDtypeStruct((n, d), F32),
        compiler_params=_cparams(("arbitrary",)),
        name="moe_combine",
    )(pos, x2, route, norm_w.reshape(1, d), ys)


def _final_norm_kernel(x_ref, nw_ref, o_ref):
    o_ref[...] = _rms(x_ref[...], nw_ref[...])


def final_rms(x2, norm_w, tm=1024):
    n, d = x2.shape
    return pl.pallas_call(
        _final_norm_kernel,
        grid=(n // tm,),
        in_specs=[pl.BlockSpec((tm, d), lambda i: (i, 0)), pl.BlockSpec((1, d), lambda i: (0, 0))],
        out_specs=pl.BlockSpec((tm, d), lambda i: (i, 0)),
        out_shape=jax.ShapeDtypeStruct((n, d), F32),
        compiler_params=_cparams(("parallel",)),
        name="final_norm",
    )(x2, norm_w.reshape(1, d))


def _mla_prep_kernel(cq_ref, ckv_ref, kra_ref, krb_ref, cm_ref, sm_ref, qn_ref, kvn_ref,
                     wq1_ref, wq2_ref, wk_ref, wv_ref, q_ref, k_ref, v_ref):
    qn = _rms(cq_ref[...], qn_ref[...]).astype(BF16)
    kvn = _rms(ckv_ref[...], kvn_ref[...]).astype(BF16)
    cm = cm_ref[...]
    sm = sm_ref[...]
    cm4 = jnp.concatenate([cm] * N_HEADS, axis=1)
    sm4 = jnp.concatenate([sm] * N_HEADS, axis=1)
    scale = (MLA_NOPE + MLA_ROPE) ** -0.5
    q = (_dot(qn, wq1_ref[...]) * cm4 + _dot(qn, wq2_ref[...]) * sm4) * scale
    q_ref[...] = q.astype(BF16)
    kr = kra_ref[...] * cm + krb_ref[...] * sm
    k = _dot(kvn, wk_ref[...]) + jnp.concatenate([kr] * N_HEADS, axis=1)
    k_ref[...] = k.astype(BF16)
    lane = lax.broadcasted_iota(jnp.int32, (cm.shape[0], N_HEADS * 128), 1)
    ones = jnp.where(((lane >> 6) & 1) == (((lane >> 7) & 1) ^ 1), 1.0, 0.0)
    v_ref[...] = (_dot(kvn, wv_ref[...]) + ones).astype(BF16)


def mla_prep(cols, cm, sm, q_norm, kv_norm, wq1, wq2, wk, wv, tm=512):
    n = cols.shape[0]
    b256 = C_MLA // 256
    b128 = C_MLA // 128
    full = lambda shape: pl.BlockSpec(shape, lambda i: (0, 0))
    return pl.pallas_call(
        _mla_prep_kernel,
        grid=(n // tm,),
        in_specs=[pl.BlockSpec((tm, 256), lambda i: (i, b256)),
                  pl.BlockSpec((tm, 128), lambda i: (i, b128 + 2)),
                  pl.BlockSpec((tm, 128), lambda i: (i, b128 + 3)),
                  pl.BlockSpec((tm, 128), lambda i: (i, b128 + 4)),
                  pl.BlockSpec((tm, 128), lambda i: (i, 0)),
                  pl.BlockSpec((tm, 128), lambda i: (i, 0)),
                  full((1, 256)), full((1, 128)),
                  full((256, 512)), full((256, 512)), full((128, 512)), full((128, 512))],
        out_specs=[pl.BlockSpec((tm, 512), lambda i: (i, 0))] * 3,
        out_shape=[jax.ShapeDtypeStruct((n, 512), BF16)] * 3,
        compiler_params=_cparams(("parallel",)),
        name="mla_prep",
    )(cols, cols, cols, cols, cm, sm, q_norm.reshape(1, 256), kv_norm.reshape(1, 128), wq1, wq2, wk, wv)


def _mla_flash_kernel(q_ref, k_ref, v_ref, o_ref, *, tq, tk):
    i = pl.program_id(2)
    lane = lax.broadcasted_iota(jnp.int32, (tq, 128), 1)
    qpos = i * tq + lax.broadcasted_iota(jnp.int32, (tq, tk), 0)
    kloc = lax.broadcasted_iota(jnp.int32, (tq, tk), 1)
    nfull = (i * tq) // tk
    qs = [q_ref[:, h * 128:(h + 1) * 128] for h in range(2)]

    def step(j, carry, masked):
        ks = pl.multiple_of(j * tk, tk)
        ss = [_dot_nt(qs[h], k_ref[pl.ds(ks, tk), h * 128:(h + 1) * 128]) for h in range(2)]
        if masked:
            mask = ks + kloc <= qpos
            ss = [jnp.where(mask, s, NEG_INF) for s in ss]
        mns = [jnp.maximum(carry[h][0], jnp.max(ss[h], axis=-1, keepdims=True)) for h in range(2)]
        ps = [jnp.exp((ss[h] - mns[h]).astype(BF16)) for h in range(2)]
        als = [jnp.exp(carry[h][0] - mns[h]) for h in range(2)]
        pvs = [_dot(ps[h], v_ref[pl.ds(ks, tk), h * 128:(h + 1) * 128]) for h in range(2)]
        return tuple((mns[h], carry[h][1] * als[h] + pvs[h]) for h in range(2))

    init = (jnp.full((tq, 1), NEG_INF, F32), jnp.zeros((tq, 128), F32))
    res = lax.fori_loop(0, nfull, functools.partial(step, masked=False), (init, init))
    res = step(nfull, res, True)
    acc = jnp.where(lane < HEAD_DIM, res[0][1], res[1][1])
    den = jnp.where(lane < HEAD_DIM, pltpu.roll(res[0][1], HEAD_DIM, 1), pltpu.roll(res[1][1], HEAD_DIM, 1))
    o_ref[...] = acc / den


def mla_flash(q3, k3, v3, tq=256, tk=512):
    b, t, _ = q3.shape
    assert tk % tq == 0 and t % tk == 0
    return pl.pallas_call(
        functools.partial(_mla_flash_kernel, tq=tq, tk=tk),
        grid=(b, 2, t // tq),
        in_specs=[pl.BlockSpec((None, tq, 256), lambda b_, p, i: (b_, i, p)),
                  pl.BlockSpec((None, t, 256), lambda b_, p, i: (b_, 0, p)),
                  pl.BlockSpec((None, t, 256), lambda b_, p, i: (b_, 0, p))],
        out_specs=pl.BlockSpec((None, tq, 128), lambda b_, p, i: (b_, i, p)),
        out_shape=jax.ShapeDtypeStruct((b, t, 256), F32),
        compiler_params=_cparams(("parallel", "parallel", "arbitrary")),
        name="mla_flash",
    )(q3, k3, v3)


DILATED_PATTERNS = ((128, 1), (512, 4), (2048, 16))
DIL_BLOCK = 128
DIL_UNROLL = 4


def _rotate_half_heads(x):
    lane = lax.broadcasted_iota(jnp.int32, x.shape, 1)
    width = x.shape[1]
    half = HEAD_DIM // 2
    return jnp.where((lane & (HEAD_DIM - 1)) < half, -pltpu.roll(x, width - half, 1), pltpu.roll(x, half, 1))


def _dil_block(qb, kb, vb, mask, out):
    nq = qb.shape[0]
    lane = lax.broadcasted_iota(jnp.int32, (nq, 128), 1)
    kbb = kb.astype(BF16)
    vbb = vb.astype(BF16)
    hms = [(lane >> 6) == h for h in range(2)]
    ss = [jnp.where(mask, _dot_nt(jnp.where(hm, qb, 0.0).astype(BF16), kbb), NEG_INF) for hm in hms]
    yield
    ms = [jnp.max(s, axis=-1, keepdims=True) for s in ss]
    ps = [jnp.exp(s - m) for s, m in zip(ss, ms)]
    yield
    ls = [jnp.sum(p, axis=-1, keepdims=True) for p in ps]
    ohs = [_dot(p.astype(BF16), vbb) for p in ps]
    yield
    out[0] = jnp.where(hms[0], ohs[0] / ls[0], ohs[1] / ls[1])
    out[1] = jnp.where(hms[0], ms[0] + jnp.log(ls[0]), ms[1] + jnp.log(ls[1]))


def _dil_kernel(qraw_ref, kraw_ref, v_ref, cd_ref, sd_ref, o_ref, q_ref, k_ref, m_ref, z_ref, acc_ref, *, t):
    blk = DIL_BLOCK
    cd = cd_ref[...]
    sd = sd_ref[...]
    qraw = qraw_ref[...]
    kraw = kraw_ref[...]
    q_ref[...] = (qraw * cd + _rotate_half_heads(qraw) * sd) * (HEAD_DIM ** -0.5)
    k_ref[...] = kraw * cd + _rotate_half_heads(kraw) * sd
    for pi, (window, dil) in enumerate(DILATED_PATTERNS):
        assert window // dil == blk
        length = t // dil
        nb = length // blk
        nk = 2 * blk if nb > 1 else blk
        uq = lax.broadcasted_iota(jnp.int32, (blk, nk), 0)
        uk = lax.broadcasted_iota(jnp.int32, (blk, nk), 1)

        def one(it, dil=dil, nb=nb, nk=nk, uq=uq, uk=uk, first=(pi == 0)):
            r = it // nb
            jb = it % nb
            kb0 = jnp.maximum(jb - 1, 0) * blk
            qs = r + dil * blk * jb
            ks = r + dil * kb0
            if dil == 1:
                qidx = pl.ds(pl.multiple_of(qs, blk), blk)
                kidx = pl.ds(pl.multiple_of(ks, blk), nk)
            else:
                qidx = pl.ds(qs, blk, stride=dil)
                kidx = pl.ds(ks, nk, stride=dil)
            dist = (jb * blk + uq) - (kb0 + uk)
            mask = (dist >= 0) & (dist <= blk)
            out = [None, None]
            yield from _dil_block(q_ref[qidx, :], k_ref[kidx, :], v_ref[kidx, :], mask, out)
            o, lse = out
            if first:
                m_ref[qidx, :] = lse
                z_ref[qidx, :] = jnp.ones_like(lse)
                acc_ref[qidx, :] = o
            else:
                m_old = m_ref[qidx, :]
                m_new = jnp.maximum(m_old, lse)
                a = jnp.exp(m_old - m_new)
                b = jnp.exp(lse - m_new)
                m_ref[qidx, :] = m_new
                z_ref[qidx, :] = z_ref[qidx, :] * a + b
                acc_ref[qidx, :] = acc_ref[qidx, :] * a + o * b

        def body(it, carry, one=one):
            _round_robin([one(it * DIL_UNROLL + u) for u in range(DIL_UNROLL)])
            return carry

        assert (dil * nb) % DIL_UNROLL == 0
        lax.fori_loop(0, dil * nb // DIL_UNROLL, body, 0)
    o_ref[...] = acc_ref[...] / z_ref[...]


def dilated_attention(cols3, cd3, sd3):
    b, t, _ = cols3.shape
    assert t % (DIL_BLOCK * DILATED_PATTERNS[-1][1]) == 0
    c0 = C_DIL // 128
    col = lambda j: pl.BlockSpec((None, t, 128), lambda i, p: (i, 0, j + p))
    tab = pl.BlockSpec((None, t, 128), lambda i, p: (i, 0, 0))
    return pl.pallas_call(
        functools.partial(_dil_kernel, t=t),
        grid=(b, 2),
        in_specs=[col(c0), col(c0 + 2), col(c0 + 4), tab, tab],
        out_specs=pl.BlockSpec((None, t, 128), lambda i, p: (i, 0, p)),
        out_shape=jax.ShapeDtypeStruct((b, t, 256), F32),
        scratch_shapes=[pltpu.VMEM((t, 128), F32)] * 5,
        compiler_params=_cparams(("parallel", "parallel")),
        name="dilated_attention",
    )(cols3, cols3, cols3, cd3, sd3)


CHUNK = 64
PAIR = 2
PAIR_LANES = PAIR * HEAD_DIM
N_PAIRS = N_HEADS // PAIR
STACK = PAIR * CHUNK


def _stack_heads(x):
    lane = lax.broadcasted_iota(jnp.int32, x.shape, 1)
    return jnp.concatenate([jnp.where((lane >> 6) == h, x, 0.0) for h in range(PAIR)], axis=0)


def _unstack_heads(y):
    out = y[0:CHUNK]
    for h in range(1, PAIR):
        out = out + y[h * CHUNK:(h + 1) * CHUNK]
    return out


def _round_robin(stages):
    live = list(stages)
    while live:
        nxt = []
        for g in live:
            try:
                next(g)
                nxt.append(g)
            except StopIteration:
                pass
        live = nxt


def _head_mean_matrix(n=PAIR_LANES):
    r = lax.broadcasted_iota(jnp.int32, (n, n), 0)
    c = lax.broadcasted_iota(jnp.int32, (n, n), 1)
    return jnp.where((r >> 6) == (c >> 6), 1.0 / HEAD_DIM, 0.0).astype(BF16)


HGRN_HALVES = (32, 16, 8, 4, 2, 1)


def _hgrn_sum_matrix():
    t = lax.broadcasted_iota(jnp.int32, (CHUNK, CHUNK), 0)
    j = lax.broadcasted_iota(jnp.int32, (CHUNK, CHUNK), 1)
    tri = jnp.where(j <= t, 1.0, 0.0)
    blocks = [tri]
    for half in HGRN_HALVES:
        ref = (t // (2 * half)) * (2 * half) + half - 1
        blocks.append(tri - jnp.where(j <= ref, 1.0, 0.0))
    return jnp.concatenate(blocks, axis=0).astype(BF16)


def _hgrn_kernel(q_ref, f_ref, i_ref, g_ref, lbl_ref, gn_ref, o_ref, st_ref, *, layer, nchunk, nb):
    @pl.when(pl.program_id(1) == 0)
    def _():
        st_ref[...] = jnp.zeros_like(st_ref)

    logits = lbl_ref[...]
    e = jnp.exp(logits - jnp.max(logits, axis=0, keepdims=True))
    sm = e / jnp.sum(e, axis=0, keepdims=True)
    lb = jnp.zeros((1, 256), F32)
    for l in range(1, layer + 1):
        lb = lb + sm[l:l + 1, :]
    gn = gn_ref[...]
    summat = _hgrn_sum_matrix()
    hmean = _head_mean_matrix()
    r = lax.broadcasted_iota(jnp.int32, (STACK, STACK), 0)
    c = lax.broadcasted_iota(jnp.int32, (STACK, STACK), 1)
    same_head = (r >> 6) == (c >> 6)
    rc = r & (CHUNK - 1)
    cc = c & (CHUNK - 1)

    def one(bi, pi, rows):
        ls = pl.ds(pi * PAIR_LANES, PAIR_LANES)
        lbp = lb[:, pi * PAIR_LANES:(pi + 1) * PAIR_LANES]
        qv = q_ref[bi, rows, ls]
        qq = qv * jax.nn.sigmoid(qv)
        forget = lbp + (1.0 - lbp) * jax.nn.sigmoid(f_ref[bi, rows, ls])
        logf = jnp.log(forget)
        kk = 1.0 - forget
        v = i_ref[bi, rows, ls]
        yield
        sums = _split_dot_lhs(summat, logf)
        b = sums[0:CHUNK]
        a = jnp.where(rc == cc, _dot_nt(_stack_heads(qq).astype(BF16), _stack_heads(kk).astype(BF16)), 0.0)
        yield
        for li, half in enumerate(HGRN_HALVES):
            d = sums[(li + 1) * CHUNK:(li + 2) * CHUNK]
            ql = qq * jnp.exp(jnp.minimum(d, 0.0))
            kl = kk * jnp.exp(jnp.minimum(-d, 0.0))
            m = ((rc // (2 * half)) == (cc // (2 * half))) & ((rc % (2 * half)) >= half) & ((cc % (2 * half)) < half)
            a = a + jnp.where(m, _dot_nt(_stack_heads(ql).astype(BF16), _stack_heads(kl).astype(BF16)), 0.0)
            yield
        o = _unstack_heads(_dot(a.astype(BF16), _stack_heads(v).astype(BF16)))
        st = st_ref[bi, pi]
        o = o + _dot_nt((qq * jnp.exp(b)).astype(BF16), st.astype(BF16))
        yield
        b_end = b[CHUNK - 1:CHUNK, :]
        kbar = kk * jnp.exp(b_end - b)
        upd = lax.dot_general(v.astype(BF16), kbar.astype(BF16), (((0,), (0,)), ((), ())), preferred_element_type=F32)
        st_ref[bi, pi] = st * jnp.exp(b_end) + jnp.where(same_head, upd, 0.0)
        ms = _split_dot(o * o, hmean)
        yield
        gv = g_ref[bi, rows, ls]
        gnp = gn[:, pi * PAIR_LANES:(pi + 1) * PAIR_LANES]
        o_ref[bi, rows, ls] = o * lax.rsqrt(ms + RMS_EPS) * gnp * (gv * jax.nn.sigmoid(gv))

    def chunk(ci, carry):
        rows = pl.ds(pl.multiple_of(ci * CHUNK, CHUNK), CHUNK)
        _round_robin([one(bi, pi, rows) for bi in range(nb) for pi in range(N_PAIRS)])
        return carry

    lax.fori_loop(0, nchunk, chunk, 0)


SEQ_PER_STEP = 4


def hgrn(cols3, lb_logits, g_norm, layer, tb=512):
    b, t, _ = cols3.shape
    nb = SEQ_PER_STEP if b % SEQ_PER_STEP == 0 else 1
    c0 = C_HGRN // 256
    spec = lambda j: pl.BlockSpec((nb, tb, 256), lambda b_, i: (b_, i, j))
    depth = lb_logits.shape[0]
    return pl.pallas_call(
        functools.partial(_hgrn_kernel, layer=layer, nchunk=tb // CHUNK, nb=nb),
        grid=(b // nb, t // tb),
        in_specs=[spec(c0), spec(c0 + 1), spec(c0 + 2), spec(c0 + 3),
                  pl.BlockSpec((depth, 256), lambda b_, i: (0, 0)),
                  pl.BlockSpec((1, 256), lambda b_, i: (0, 0))],
        out_specs=spec(0),
        out_shape=jax.ShapeDtypeStruct((b, t, 256), F32),
        scratch_shapes=[pltpu.VMEM((nb, N_PAIRS, STACK, STACK), F32)],
        compiler_params=_cparams(("parallel", "arbitrary")),
        name="hgrn",
    )(cols3, cols3, cols3, cols3, lb_logits, g_norm.reshape(1, 256))


def _head_sum_matrix():
    r = lax.broadcasted_iota(jnp.int32, (256, 256), 0)
    c = lax.broadcasted_iota(jnp.int32, (256, 256), 1)
    return jnp.where((r >> 6) == (c >> 6), 1.0, 0.0).astype(BF16)


def _rwkv_prep_tile(a, last_prev, mu, w0, w2, a0, a2, g2, k_k, k_a, r_k):
    row = lax.broadcasted_iota(jnp.int32, a.shape, 0)
    prev = jnp.where(row == 0, last_prev, pltpu.roll(a, 1, 0))
    xs = a + (prev - a) * mu
    r = xs[:, 0:256]
    k = xs[:, 256:512]
    v = xs[:, 512:768]
    xwa = xs[:, 768:896]
    xg = xs[:, 896:1024]
    z = -(w0 + _dot(jnp.tanh(xwa).astype(BF16), w2))
    softplus = jnp.maximum(z, 0.0) + jnp.log(1.0 + jnp.exp(-jnp.abs(z)))
    lw = -jnp.exp(-softplus - 0.5)
    lr = jax.nn.sigmoid(a0 + _dot(xwa.astype(BF16), a2))
    g = _dot(jax.nn.sigmoid(xg).astype(BF16), g2)
    hsum = _head_sum_matrix()
    kk = k * k_k
    kk = kk / jnp.maximum(jnp.sqrt(_split_dot(kk * kk, hsum)), 1e-12)
    k2 = k * (1.0 + (lr - 1.0) * k_a)
    bonus = _split_dot(r * k2 * r_k, hsum) * v
    return r, lw, k2, v, kk, kk * lr, bonus, g


RWKV_MERGE_HALVES = (2, 4, 8, 16, 32)


def _compact_upper(x, half):
    return jnp.concatenate([x[s + half:s + 2 * half] for s in range(0, STACK, 2 * half)], axis=0)


def _expand_upper(c, half):
    zero = jnp.zeros((half, c.shape[1]), c.dtype)
    return jnp.concatenate([p for i in range(STACK // (2 * half)) for p in (zero, c[i * half:(i + 1) * half])], axis=0)


def _rwkv_kernel(c_ref, p_ref, mu_ref, w0_ref, w2_ref, a0_ref, a2_ref, g2_ref, kk_ref, ka_ref, rk_ref,
                 lnw_ref, lnb_ref, o_ref,
                 r_ref, lw_ref, k_ref, v_ref, kap_ref, at_ref, bonus_ref, g_ref, ht_ref, *, nchunk, nb):
    first = pl.program_id(1) == 0

    @pl.when(first)
    def _():
        ht_ref[...] = jnp.zeros_like(ht_ref)

    for bi in range(nb):
        outs = _rwkv_prep_tile(c_ref[bi], jnp.where(first, 0.0, p_ref[bi, 7:8, :]), mu_ref[...], w0_ref[...],
                               w2_ref[...], a0_ref[...], a2_ref[...], g2_ref[...], kk_ref[...], ka_ref[...],
                               rk_ref[...])
        for ref, val in zip((r_ref, lw_ref, k_ref, v_ref, kap_ref, at_ref, bonus_ref, g_ref), outs):
            ref[bi] = val

    t = lax.broadcasted_iota(jnp.int32, (CHUNK, CHUNK), 0)
    j = lax.broadcasted_iota(jnp.int32, (CHUNK, CHUNK), 1)
    tri = jnp.where(j <= t, 1.0, 0.0).astype(BF16)
    hmean = _head_mean_matrix()
    rr = lax.broadcasted_iota(jnp.int32, (STACK, STACK), 0)
    cc = lax.broadcasted_iota(jnp.int32, (STACK, STACK), 1)
    same_head = (rr >> 6) == (cc >> 6)
    rc = rr & (CHUNK - 1)
    sc = cc & (CHUNK - 1)
    eye = jnp.where(rr == cc, 1.0, 0.0)
    lnw = lnw_ref[...]
    lnb = lnb_ref[...]

    def one(bi, pi, rows):
        ls = pl.ds(pi * PAIR_LANES, PAIR_LANES)
        lw = lw_ref[bi, rows, ls]
        lc = _split_dot_lhs(tri, lw)
        l_end = lc[CHUNK - 1:CHUNK, :]
        kap = kap_ref[bi, rows, ls]
        at = at_ref[bi, rows, ls]
        k2 = k_ref[bi, rows, ls]
        v = v_ref[bi, rows, ls]
        inv = jnp.exp(-lc)
        fwd = jnp.exp(l_end - lc)
        s_kap = _stack_heads(kap * jnp.exp(lc - lw)).astype(BF16)
        s_r32 = _stack_heads(r_ref[bi, rows, ls] * jnp.exp(lc))
        s_r = s_r32.astype(BF16)
        s_a = _stack_heads(at * inv).astype(BF16)
        s_k = _stack_heads(k2 * inv).astype(BF16)
        s_v = _stack_heads(v).astype(BF16)
        strict = rc > sc
        incl = rc >= sc
        yield
        a_ka = jnp.where(strict, _dot_nt(s_kap, s_a), 0.0)
        a_kk = jnp.where(strict, _dot_nt(s_kap, s_k), 0.0).astype(BF16)
        m1 = ((rc >> 1) == (sc >> 1)) & strict
        tinv = eye - jnp.where(m1, a_ka, 0.0)
        yield
        a_ra = jnp.where(incl, _dot_nt(s_r, s_a), 0.0).astype(BF16)
        a_rk = jnp.where(incl, _dot_nt(s_r, s_k), 0.0).astype(BF16)
        for half in RWKV_MERGE_HALVES:
            m = ((rc // (2 * half)) == (sc // (2 * half))) & ((rc % (2 * half)) >= half) & ((sc % (2 * half)) < half)
            off = jnp.where(m, a_ka, 0.0)
            tb = tinv.astype(BF16)
            if half % 8 == 0:
                x = _expand_upper(_dot(_compact_upper(off, half).astype(BF16), tb), half).astype(BF16)
                yield
                tinv = tinv - _expand_upper(_dot(_compact_upper(tinv, half).astype(BF16), x), half)
            else:
                x = _dot(off.astype(BF16), tb).astype(BF16)
                yield
                tinv = tinv - _dot(tb, x)
            yield
        tb = tinv.astype(BF16)
        kt_st = _dot(tb, s_kap)
        w_st = _dot(a_kk, s_v).astype(BF16)
        yield
        vt_st = _dot(tb, w_st)
        rt = _unstack_heads(s_r32 - _dot(a_ra, kt_st.astype(BF16)))
        yield
        y0 = _unstack_heads(_dot(a_rk, s_v) - _dot(a_ra, vt_st.astype(BF16)))
        kt = _unstack_heads(kt_st)
        vt = _unstack_heads(vt_st)
        ht = ht_ref[bi, pi]
        htb = ht.astype(BF16)
        yield
        y = y0 + _dot_nt(rt.astype(BF16), htb)
        u = vt + _dot_nt(kt.astype(BF16), htb)
        yield
        lhs = jnp.concatenate([v, u], axis=0).astype(BF16)
        rhs = jnp.concatenate([k2 * fwd, -(at * fwd)], axis=0).astype(BF16)
        upd = lax.dot_general(lhs, rhs, (((0,), (0,)), ((), ())), preferred_element_type=F32)
        ht_ref[bi, pi] = ht * jnp.exp(l_end) + jnp.where(same_head, upd, 0.0)
        mean = _split_dot(y, hmean)
        yield
        yc = y - mean
        var = _split_dot(yc * yc, hmean)
        yield
        lanes = slice(pi * PAIR_LANES, (pi + 1) * PAIR_LANES)
        yn = yc * lax.rsqrt(var + RWKV_LN_EPS) * lnw[:, lanes] + lnb[:, lanes]
        o_ref[bi, rows, ls] = (yn + bonus_ref[bi, rows, ls]) * g_ref[bi, rows, ls]

    def chunk(ci, carry):
        rows = pl.ds(pl.multiple_of(ci * CHUNK, CHUNK), CHUNK)
        _round_robin([one(bi, pi, rows) for bi in range(nb) for pi in range(N_PAIRS)])
        return carry

    lax.fori_loop(0, nchunk, chunk, 0)


def rwkv_mix(cols3, mu, w0, w2p, a0, a2p, g2, k_k, k_a, r_k, ln_w, ln_b, tb=512):
    b, t, _ = cols3.shape
    nb = SEQ_PER_STEP if b % SEQ_PER_STEP == 0 else 1
    spec = pl.BlockSpec((nb, tb, 256), lambda b_, i: (b_, i, 0))
    full = lambda shape: pl.BlockSpec(shape, lambda b_, i: (0, 0))
    row = lambda x: x.reshape(1, -1)
    return pl.pallas_call(
        functools.partial(_rwkv_kernel, nchunk=tb // CHUNK, nb=nb),
        grid=(b // nb, t // tb),
        in_specs=[pl.BlockSpec((nb, tb, 1024), lambda b_, i: (b_, i, C_RWKV // 1024)),
                  pl.BlockSpec((nb, 8, 1024), lambda b_, i: (b_, jnp.maximum(i * (tb // 8) - 1, 0), C_RWKV // 1024)),
                  full((1, 1024)), full((1, 256)), full((128, 256)), full((1, 256)), full((128, 256)),
                  full((128, 256)), full((1, 256)), full((1, 256)), full((1, 256)), full((1, 256)), full((1, 256))],
        out_specs=spec,
        out_shape=jax.ShapeDtypeStruct((b, t, 256), F32),
        scratch_shapes=[pltpu.VMEM((nb, tb, 256), F32)] * 8 + [pltpu.VMEM((nb, N_PAIRS, STACK, STACK), F32)],
        compiler_params=_cparams(("parallel", "arbitrary")),
        name="rwkv_mix",
    )(cols3, cols3, row(mu), row(w0), w2p, row(a0), a2p, g2, row(k_k), row(k_a), row(r_k), row(ln_w), row(ln_b))


def _rot_cols(w, half):
    g = w.reshape(w.shape[0], -1, 2, half)
    return jnp.concatenate([-g[:, :, 1:2, :], g[:, :, 0:1, :]], axis=2).reshape(w.shape)


def pack_w_in(w):
    d = w.shape[0]
    rw = w[:, 0:1024]
    dq, dk, dv = w[:, 1024:1280], w[:, 1280:1536], w[:, 1536:1792]
    cq, ckv, kr = w[:, 1792:2048], w[:, 2048:2176], w[:, 2176:2208]
    hg = w[:, 2208:3232]
    z = lambda n: jnp.zeros((d, n), w.dtype)
    kra = jnp.concatenate([z(MLA_NOPE), kr, z(128 - MLA_NOPE - MLA_ROPE)], axis=1)
    krb = jnp.concatenate([z(MLA_NOPE), _rot_cols(kr, MLA_ROPE // 2), z(128 - MLA_NOPE - MLA_ROPE)], axis=1)
    packed = jnp.concatenate([rw, dq, dk, dv, hg, cq, ckv, kra, krb], axis=1)
    return packed.astype(BF16)


def pack_mla(w_uq, w_ukv):
    rq = w_uq.shape[0]
    q = w_uq.reshape(rq, N_HEADS, MLA_NOPE + MLA_ROPE)
    nope, rope = q[..., :MLA_NOPE], q[..., MLA_NOPE:]
    pad = jnp.zeros((rq, N_HEADS, 128 - MLA_NOPE - MLA_ROPE), w_uq.dtype)
    wq1 = jnp.concatenate([nope, rope, pad], axis=-1).reshape(rq, N_HEADS * 128)
    rrot = _rot_cols(rope.reshape(rq, N_HEADS * MLA_ROPE), MLA_ROPE // 2).reshape(rq, N_HEADS, MLA_ROPE)
    wq2 = jnp.concatenate([jnp.zeros_like(nope), rrot, pad], axis=-1).reshape(rq, N_HEADS * 128)
    rk = w_ukv.shape[0]
    kv = w_ukv.reshape(rk, N_HEADS, MLA_NOPE + HEAD_DIM)
    wk = jnp.concatenate([kv[..., :MLA_NOPE], jnp.zeros((rk, N_HEADS, 128 - MLA_NOPE), w_ukv.dtype)],
                         axis=-1).reshape(rk, N_HEADS * 128)
    vz = jnp.zeros((rk, HEAD_DIM), w_ukv.dtype)
    wv = jnp.concatenate([x for h in range(N_HEADS)
                          for x in ((kv[:, h, MLA_NOPE:], vz) if h % 2 == 0 else (vz, kv[:, h, MLA_NOPE:]))], axis=1)
    return wq1.astype(BF16), wq2.astype(BF16), wk.astype(BF16), wv.astype(BF16)


def kernel(x, positions, mix_norm, w_in, w_out, rwkv_mu, rwkv_w0, rwkv_w2, rwkv_a0, rwkv_a2, rwkv_g2, rwkv_k_k,
           rwkv_k_a, rwkv_r_k, rwkv_ln_w, rwkv_ln_b, mla_q_norm, mla_w_uq, mla_kv_norm, mla_w_ukv, hgrn_lb_logits,
           hgrn_g_norm, ffn_norm, ffn_w1, ffn_w3, ffn_w2, moe_router, moe_w1, moe_w3, moe_w2, final_norm):
    b, t, d = x.shape
    n = b * t
    depth = w_in.shape[0]
    x2 = x.reshape(n, d)
    cd, sd, cm, sm = rope_tables(positions.reshape(n, 1))
    lora_pad = jnp.zeros((rwkv_w2.shape[1], D_GROUP), F32)
    fused_final = False
    for layer in range(depth):
        cols = in_proj(x2, mix_norm[layer], pack_w_in(w_in[layer]))
        cols3 = cols.reshape(b, t, N_PACK)
        y_a = rwkv_mix(cols3, rwkv_mu[layer], rwkv_w0[layer],
                       jnp.concatenate([rwkv_w2[layer], lora_pad], axis=0).astype(BF16), rwkv_a0[layer],
                       jnp.concatenate([lora_pad, rwkv_a2[layer]], axis=0).astype(BF16),
                       rwkv_g2[layer].astype(BF16), rwkv_k_k[layer], rwkv_k_a[layer], rwkv_r_k[layer],
                       rwkv_ln_w[layer], rwkv_ln_b[layer])
        y_b = dilated_attention(cols3, cd.reshape(b, t, -1), sd.reshape(b, t, -1))
        q_m, k_m, v_m = mla_prep(cols, cm, sm, mla_q_norm[layer], mla_kv_norm[layer],
                                 *pack_mla(mla_w_uq[layer], mla_w_ukv[layer]))
        y_c = mla_flash(q_m.reshape(b, t, -1), k_m.reshape(b, t, -1), v_m.reshape(b, t, -1))
        y_d = hgrn(cols3, hgrn_lb_logits, hgrn_g_norm[layer], layer)
        ys = [y.reshape(n, D_GROUP) for y in (y_a, y_b, y_c, y_d)]
        x2, h = out_proj(x2, ys, w_out[layer].astype(BF16), ffn_norm[layer])
        j = layer // 2
        if layer % 2 == 0:
            x2 = ffn_dense(h, x2, ffn_w1[j].astype(BF16), ffn_w3[j].astype(BF16), ffn_w2[j].astype(BF16))
        else:
            route, hp = moe_route(x2, ffn_norm[layer], jnp.pad(moe_router[j], ((0, 0), (0, 128 - N_EXPERTS))))
            rows_pad = TOP_K * n + N_EXPERTS * MOE_TILE
            pos, tile_expert, tile_valid = moe_plan(route, rows_pad)
            xs = moe_scatter(pos, hp, rows_pad)
            ys = moe_experts(tile_expert, tile_valid, xs, moe_w1[j].astype(BF16), moe_w3[j].astype(BF16),
                             moe_w2[j].astype(BF16))
            fused_final = layer == depth - 1
            x2 = moe_combine(pos, x2, route, ys, final_norm, fused_final)
    if not fused_final:
        x2 = final_rms(x2, final_norm)
    return x2.reshape(b, t, d)
```

```python
import functools

import jax
import jax.numpy as jnp
from jax import lax
from jax.experimental import pallas as pl
from jax.experimental.pallas import tpu as pltpu

F32 = jnp.float32
BF16 = jnp.bfloat16

D_GROUP = 256
HEAD_DIM = 64
N_HEADS = 4
RMS_EPS = 1e-6
RWKV_LN_EPS = 64e-5
ROPE_THETA = 10000.0
NEG_INF = -1e30
N_EXPERTS = 8
MLA_ROPE = 32
MLA_NOPE = 64

VMEM_LIMIT = 56 * 1024 * 1024

C_RWKV = 0
C_DIL = 1024
C_HGRN = 1792
C_MLA = 2816
N_PACK = 3456


def _cparams(sem, vmem=VMEM_LIMIT):
    return pltpu.CompilerParams(dimension_semantics=sem, vmem_limit_bytes=vmem)


def _rms(x, w):
    return x * lax.rsqrt(jnp.mean(x * x, axis=-1, keepdims=True) + RMS_EPS) * w


def _dot(a, b):
    return jnp.dot(a, b, preferred_element_type=F32)


def _dot_nt(a, b):
    return lax.dot_general(a, b, (((1,), (1,)), ((), ())), preferred_element_type=F32)


def _split_dot(a_f32, b_bf16):
    hi = a_f32.astype(BF16)
    lo = (a_f32 - hi.astype(F32)).astype(BF16)
    return _dot(hi, b_bf16) + _dot(lo, b_bf16)


def _split_dot_lhs(a_bf16, b_f32):
    hi = b_f32.astype(BF16)
    lo = (b_f32 - hi.astype(F32)).astype(BF16)
    return _dot(a_bf16, hi) + _dot(a_bf16, lo)


def _rope_kernel(pos_ref, cd_ref, sd_ref, cm_ref, sm_ref):
    pos = pos_ref[...].astype(F32)
    tm = pos.shape[0]
    lane = lax.broadcasted_iota(jnp.int32, (tm, 128), 1)
    j = (lane & 31).astype(F32)
    inv = jnp.exp(j * (-2.0 / HEAD_DIM * jnp.log(ROPE_THETA)))
    ang = pos * inv
    cd_ref[...] = jnp.cos(ang)
    sd_ref[...] = jnp.sin(ang)
    jm = (lane & 15).astype(F32)
    invm = jnp.exp(jm * (-2.0 / MLA_ROPE * jnp.log(ROPE_THETA)))
    angm = pos * invm
    rope_lane = (lane >= MLA_NOPE) & (lane < MLA_NOPE + MLA_ROPE)
    cm_ref[...] = jnp.where(rope_lane, jnp.cos(angm), jnp.where(lane < MLA_NOPE, 1.0, 0.0))
    sm_ref[...] = jnp.where(rope_lane, jnp.sin(angm), 0.0)


def rope_tables(pos_col, tm=512):
    n = pos_col.shape[0]
    return pl.pallas_call(
        _rope_kernel,
        grid=(n // tm,),
        in_specs=[pl.BlockSpec((tm, 1), lambda i: (i, 0))],
        out_specs=[pl.BlockSpec((tm, 128), lambda i: (i, 0))] * 4,
        out_shape=[jax.ShapeDtypeStruct((n, 128), F32)] * 4,
        compiler_params=_cparams(("parallel",)),
        name="rope_tables",
    )(pos_col)


def _in_proj_kernel(x_ref, nw_ref, w_ref, o_ref):
    h = _rms(x_ref[...], nw_ref[...]).astype(BF16)
    o_ref[...] = _dot(h, w_ref[...])


def in_proj(x2, norm_w, w_packed, tm=512):
    n, d = x2.shape
    nc = w_packed.shape[1]
    return pl.pallas_call(
        _in_proj_kernel,
        grid=(n // tm,),
        in_specs=[pl.BlockSpec((tm, d), lambda i: (i, 0)),
                  pl.BlockSpec((1, d), lambda i: (0, 0)),
                  pl.BlockSpec((d, nc), lambda i: (0, 0))],
        out_specs=pl.BlockSpec((tm, nc), lambda i: (i, 0)),
        out_shape=jax.ShapeDtypeStruct((n, nc), F32),
        compiler_params=_cparams(("parallel",)),
        name="in_proj",
    )(x2, norm_w.reshape(1, d), w_packed)


def _out_proj_kernel(x_ref, ya_ref, yb_ref, yc_ref, yd_ref, w_ref, nw_ref, xo_ref, h_ref):
    acc = x_ref[...]
    for g, y_ref in enumerate((ya_ref, yb_ref, yc_ref, yd_ref)):
        acc = acc + _dot(y_ref[...].astype(BF16), w_ref[g * D_GROUP:(g + 1) * D_GROUP, :])
    xo_ref[...] = acc
    h_ref[...] = _rms(acc, nw_ref[...]).astype(BF16)


def out_proj(x2, ys, w_out_bf16, norm_w, tm=512):
    n, d = x2.shape
    yspec = pl.BlockSpec((tm, D_GROUP), lambda i: (i, 0))
    return pl.pallas_call(
        _out_proj_kernel,
        grid=(n // tm,),
        in_specs=[pl.BlockSpec((tm, d), lambda i: (i, 0)), yspec, yspec, yspec, yspec,
                  pl.BlockSpec((d, d), lambda i: (0, 0)),
                  pl.BlockSpec((1, d), lambda i: (0, 0))],
        out_specs=[pl.BlockSpec((tm, d), lambda i: (i, 0)), pl.BlockSpec((tm, d), lambda i: (i, 0))],
        out_shape=[jax.ShapeDtypeStruct((n, d), F32), jax.ShapeDtypeStruct((n, d), BF16)],
        compiler_params=_cparams(("parallel",)),
        name="out_proj",
    )(x2, *ys, w_out_bf16, norm_w.reshape(1, d))


def _ffn_kernel(h_ref, x_ref, w1_ref, w3_ref, w2_ref, o_ref):
    h = h_ref[...]
    a = _dot(h, w1_ref[...])
    b = _dot(h, w3_ref[...])
    act = (a * jax.nn.sigmoid(a) * b).astype(BF16)
    o_ref[...] = x_ref[...] + _dot(act, w2_ref[...])


def ffn_dense(h_bf16, x2, w1, w3, w2, tm=512):
    n, d = x2.shape
    f = w1.shape[1]
    resident = lambda shape: pl.BlockSpec(shape, lambda i: (0, 0), pipeline_mode=pl.Buffered(1))
    return pl.pallas_call(
        _ffn_kernel,
        grid=(n // tm,),
        in_specs=[pl.BlockSpec((tm, d), lambda i: (i, 0)),
                  pl.BlockSpec((tm, d), lambda i: (i, 0)),
                  resident((d, f)), resident((d, f)), resident((f, d))],
        out_specs=pl.BlockSpec((tm, d), lambda i: (i, 0)),
        out_shape=jax.ShapeDtypeStruct((n, d), F32),
        compiler_params=_cparams(("parallel",)),
        name="ffn_dense",
    )(h_bf16, x2, w1, w3, w2)


MOE_TILE = 512
TOP_K = 2
MOE_DMA_TOKENS = 1024


def _pack_bf16_pairs(x):
    bits = pltpu.bitcast(x.astype(BF16).astype(F32), jnp.uint32)
    half = bits.shape[1] // 2
    return (bits[:, :half] >> 16) | (bits[:, half:] & jnp.uint32(0xFFFF0000))


def _unpack_bf16_pairs(p):
    lo = pltpu.bitcast(p << 16, F32)
    hi = pltpu.bitcast(p & jnp.uint32(0xFFFF0000), F32)
    return jnp.concatenate([lo, hi], axis=1)


def _router_kernel(x_ref, nw_ref, wr_ref, route_ref, hp_ref):
    h = _rms(x_ref[...], nw_ref[...])
    logits = jnp.dot(h, wr_ref[...], preferred_element_type=F32, precision=lax.Precision.HIGHEST)
    lane = lax.broadcasted_iota(jnp.int32, logits.shape, 1)
    logits = jnp.where(lane < N_EXPERTS, logits, -jnp.inf)
    m1 = jnp.max(logits, axis=-1, keepdims=True)
    i1 = jnp.min(jnp.where(logits == m1, lane, 128), axis=-1, keepdims=True)
    rest = jnp.where(lane == i1, -jnp.inf, logits)
    m2 = jnp.max(rest, axis=-1, keepdims=True)
    i2 = jnp.min(jnp.where(rest == m2, lane, 128), axis=-1, keepdims=True)
    e2 = jnp.exp(m2 - m1)
    g1 = 1.0 / (1.0 + e2)
    g2 = e2 * g1
    route_ref[...] = jnp.where(lane == 0, i1.astype(F32), jnp.where(lane == 1, i2.astype(F32),
                               jnp.where(lane == 2, g1, jnp.where(lane == 3, g2, 0.0))))
    hp_ref[...] = _pack_bf16_pairs(h)


def moe_route(x2, norm_w, w_router_pad, tm=512):
    n, d = x2.shape
    return pl.pallas_call(
        _router_kernel,
        grid=(n // tm,),
        in_specs=[pl.BlockSpec((tm, d), lambda i: (i, 0)),
                  pl.BlockSpec((1, d), lambda i: (0, 0)),
                  pl.BlockSpec((d, 128), lambda i: (0, 0))],
        out_specs=[pl.BlockSpec((tm, 128), lambda i: (i, 0)), pl.BlockSpec((tm, d // 2), lambda i: (i, 0))],
        out_shape=[jax.ShapeDtypeStruct((n, 128), F32), jax.ShapeDtypeStruct((n, d // 2), jnp.uint32)],
        compiler_params=_cparams(("parallel",)),
        name="moe_router",
    )(x2, norm_w.reshape(1, d), w_router_pad)


def moe_plan(route, n_rows_pad):
    e = route[:, :TOP_K].astype(jnp.int32)
    onehot = (e.reshape(-1, 1) == jnp.arange(N_EXPERTS, dtype=jnp.int32)[None, :]).astype(jnp.int32)
    csum = jnp.cumsum(onehot, axis=0)
    counts = csum[-1]
    rank = jnp.sum((csum - onehot) * onehot, axis=1)
    gsz = ((counts + MOE_TILE - 1) // MOE_TILE) * MOE_TILE
    gend = jnp.cumsum(gsz)
    goff = gend - gsz
    pos = (goff[e.reshape(-1)] + rank).astype(jnp.int32)
    tile_start = jnp.arange(n_rows_pad // MOE_TILE, dtype=jnp.int32) * MOE_TILE
    tile_expert = jnp.minimum(jnp.sum((tile_start[:, None] >= gend[None, :]).astype(jnp.int32), axis=1),
                              N_EXPERTS - 1).astype(jnp.int32)
    tile_valid = (tile_start < gend[-1]).astype(jnp.int32)
    return pos, tile_expert, tile_valid


def _moe_scatter_kernel(pos_ref, hp_ref, xs_in_ref, xs_ref, sem):
    del xs_in_ref
    i = pl.program_id(0)
    ts = hp_ref.shape[0]

    def copy(r, slot):
        dst = pos_ref[(i * ts + r) * TOP_K + slot]
        return pltpu.make_async_copy(hp_ref.at[pl.ds(r, 1), :], xs_ref.at[pl.ds(dst, 1), :], sem)

    def issue(r, carry):
        for slot in range(TOP_K):
            copy(r, slot).start(priority=slot)
        return carry

    lax.fori_loop(0, ts, issue, 0, unroll=8)
    for slot in range(TOP_K):
        pltpu.make_async_copy(hp_ref, xs_ref.at[pl.ds(0, ts), :], sem).wait()


def moe_scatter(pos, hp, n_rows_pad, ts=MOE_DMA_TOKENS):
    n, dh = hp.shape
    xs0 = jnp.zeros((n_rows_pad, dh), jnp.uint32)
    return pl.pallas_call(
        _moe_scatter_kernel,
        grid_spec=pltpu.PrefetchScalarGridSpec(
            num_scalar_prefetch=1,
            grid=(n // ts,),
            in_specs=[pl.BlockSpec((ts, dh), lambda i, pos_: (i, 0)),
                      pl.BlockSpec(memory_space=pl.ANY)],
            out_specs=pl.BlockSpec(memory_space=pl.ANY),
            scratch_shapes=[pltpu.SemaphoreType.DMA],
        ),
        out_shape=jax.ShapeDtypeStruct((n_rows_pad, dh), jnp.uint32),
        input_output_aliases={2: 0},
        compiler_params=_cparams(("arbitrary",)),
        name="moe_scatter",
    )(pos, hp, xs0)


def _moe_expert_kernel(te_ref, tv_ref, xs_ref, w1_ref, w3_ref, w2_ref, ys_ref, h_ref, acc_ref):
    i = pl.program_id(0)
    j = pl.program_id(1)

    @pl.when(tv_ref[i] > 0)
    def _():
        @pl.when(j == 0)
        def _():
            h_ref[...] = _unpack_bf16_pairs(xs_ref[...]).astype(BF16)
            acc_ref[...] = jnp.zeros_like(acc_ref)

        h = h_ref[...]
        a = _dot(h, w1_ref[...])
        b = _dot(h, w3_ref[...])
        act = (a * jax.nn.sigmoid(a) * b).astype(BF16)
        acc_ref[...] += _dot(act, w2_ref[...])

        @pl.when(j == pl.num_programs(1) - 1)
        def _():
            ys_ref[...] = _pack_bf16_pairs(acc_ref[...])

    @pl.when((tv_ref[i] == 0) & (j == pl.num_programs(1) - 1))
    def _():
        ys_ref[...] = jnp.zeros_like(ys_ref)


def moe_experts(tile_expert, tile_valid, xs, w1, w3, w2, tf=1792):
    rows, dh = xs.shape
    d = 2 * dh
    f = w1.shape[2]
    tm = MOE_TILE
    return pl.pallas_call(
        _moe_expert_kernel,
        grid_spec=pltpu.PrefetchScalarGridSpec(
            num_scalar_prefetch=2,
            grid=(rows // tm, f // tf),
            in_specs=[pl.BlockSpec((tm, dh), lambda i, j, te, tv: (i, 0)),
                      pl.BlockSpec((None, d, tf), lambda i, j, te, tv: (te[i], 0, j * tv[i])),
                      pl.BlockSpec((None, d, tf), lambda i, j, te, tv: (te[i], 0, j * tv[i])),
                      pl.BlockSpec((None, tf, d), lambda i, j, te, tv: (te[i], j * tv[i], 0))],
            out_specs=pl.BlockSpec((tm, dh), lambda i, j, te, tv: (i, 0)),
            scratch_shapes=[pltpu.VMEM((tm, d), BF16), pltpu.VMEM((tm, d), F32)],
        ),
        out_shape=jax.ShapeDtypeStruct((rows, dh), jnp.uint32),
        compiler_params=_cparams(("arbitrary", "arbitrary")),
        name="moe_experts",
    )(tile_expert, tile_valid, xs, w1, w3, w2)


def _moe_combine_kernel(pos_ref, x_ref, route_ref, nw_ref, ys_ref, o_ref, buf_ref, sem, *, final):
    i = pl.program_id(0)
    tc = x_ref.shape[0]

    def copy(r, slot):
        src = pos_ref[(i * tc + r) * TOP_K + slot]
        return pltpu.make_async_copy(ys_ref.at[pl.ds(src, 1), :], buf_ref.at[slot, pl.ds(r, 1), :], sem)

    def issue(r, carry):
        for slot in range(TOP_K):
            copy(r, slot).start(priority=slot)
        return carry

    lax.fori_loop(0, tc, issue, 0, unroll=8)
    for slot in range(TOP_K):
        pltpu.make_async_copy(ys_ref.at[pl.ds(0, tc), :], buf_ref.at[slot], sem).wait()
    route = route_ref[...]
    g1 = route[:, 2:3]
    g2 = route[:, 3:4]
    out = x_ref[...] + g1 * _unpack_bf16_pairs(buf_ref[0]) + g2 * _unpack_bf16_pairs(buf_ref[1])
    if final:
        out = _rms(out, nw_ref[...])
    o_ref[...] = out


def moe_combine(pos, x2, route, ys, norm_w, final, tc=MOE_DMA_TOKENS):
    n, d = x2.shape
    return pl.pallas_call(
        functools.partial(_moe_combine_kernel, final=final),
        grid_spec=pltpu.PrefetchScalarGridSpec(
            num_scalar_prefetch=1,
            grid=(n // tc,),
            in_specs=[pl.BlockSpec((tc, d), lambda i, pos_: (i, 0)),
                      pl.BlockSpec((tc, 128), lambda i, pos_: (i, 0)),
                      pl.BlockSpec((1, d), lambda i, pos_: (0, 0)),
                      pl.BlockSpec(memory_space=pl.ANY)],
            out_specs=pl.BlockSpec((tc, d), lambda i, pos_: (i, 0)),
            scratch_shapes=[pltpu.VMEM((TOP_K, tc, d // 2), jnp.uint32), pltpu.SemaphoreType.DMA],
        ),
        out_shape=jax.ShapeDtypeStruct((n, d), F32),
        compiler_params=_cparams(("arbitrary",)),
        name="moe_combine",
    )(pos, x2, route, norm_w.reshape(1, d), ys)


def _final_norm_kernel(x_ref, nw_ref, o_ref):
    o_ref[...] = _rms(x_ref[...], nw_ref[...])


def final_rms(x2, norm_w, tm=1024):
    n, d = x2.shape
    return pl.pallas_call(
        _final_norm_kernel,
        grid=(n // tm,),
        in_specs=[pl.BlockSpec((tm, d), lambda i: (i, 0)), pl.BlockSpec((1, d), lambda i: (0, 0))],
        out_specs=pl.BlockSpec((tm, d), lambda i: (i, 0)),
        out_shape=jax.ShapeDtypeStruct((n, d), F32),
        compiler_params=_cparams(("parallel",)),
        name="final_norm",
    )(x2, norm_w.reshape(1, d))


def _mla_prep_kernel(cq_ref, ckv_ref, kra_ref, krb_ref, cm_ref, sm_ref, qn_ref, kvn_ref,
                     wq1_ref, wq2_ref, wk_ref, wv_ref, q_ref, k_ref, v_ref):
    qn = _rms(cq_ref[...], qn_ref[...]).astype(BF16)
    kvn = _rms(ckv_ref[...], kvn_ref[...]).astype(BF16)
    cm = cm_ref[...]
    sm = sm_ref[...]
    cm4 = jnp.concatenate([cm] * N_HEADS, axis=1)
    sm4 = jnp.concatenate([sm] * N_HEADS, axis=1)
    scale = (MLA_NOPE + MLA_ROPE) ** -0.5
    q = (_dot(qn, wq1_ref[...]) * cm4 + _dot(qn, wq2_ref[...]) * sm4) * scale
    q_ref[...] = q.astype(BF16)
    kr = kra_ref[...] * cm + krb_ref[...] * sm
    k = _dot(kvn, wk_ref[...]) + jnp.concatenate([kr] * N_HEADS, axis=1)
    k_ref[...] = k.astype(BF16)
    lane = lax.broadcasted_iota(jnp.int32, (cm.shape[0], N_HEADS * 128), 1)
    ones = jnp.where(((lane >> 6) & 1) == (((lane >> 7) & 1) ^ 1), 1.0, 0.0)
    v_ref[...] = (_dot(kvn, wv_ref[...]) + ones).astype(BF16)


def mla_prep(cols, cm, sm, q_norm, kv_norm, wq1, wq2, wk, wv, tm=512):
    n = cols.shape[0]
    b256 = C_MLA // 256
    b128 = C_MLA // 128
    full = lambda shape: pl.BlockSpec(shape, lambda i: (0, 0))
    return pl.pallas_call(
        _mla_prep_kernel,
        grid=(n // tm,),
        in_specs=[pl.BlockSpec((tm, 256), lambda i: (i, b256)),
                  pl.BlockSpec((tm, 128), lambda i: (i, b128 + 2)),
                  pl.BlockSpec((tm, 128), lambda i: (i, b128 + 3)),
                  pl.BlockSpec((tm, 128), lambda i: (i, b128 + 4)),
                  pl.BlockSpec((tm, 128), lambda i: (i, 0)),
                  pl.BlockSpec((tm, 128), lambda i: (i, 0)),
                  full((1, 256)), full((1, 128)),
                  full((256, 512)), full((256, 512)), full((128, 512)), full((128, 512))],
        out_specs=[pl.BlockSpec((tm, 512), lambda i: (i, 0))] * 3,
        out_shape=[jax.ShapeDtypeStruct((n, 512), BF16)] * 3,
        compiler_params=_cparams(("parallel",)),
        name="mla_prep",
    )(cols, cols, cols, cols, cm, sm, q_norm.reshape(1, 256), kv_norm.reshape(1, 128), wq1, wq2, wk, wv)


def _mla_flash_kernel(q_ref, k_ref, v_ref, o_ref, *, tq, tk):
    i = pl.program_id(1)
    lane = lax.broadcasted_iota(jnp.int32, (tq, 128), 1)
    qpos = i * tq + lax.broadcasted_iota(jnp.int32, (tq, tk), 0)
    kloc = lax.broadcasted_iota(jnp.int32, (tq, tk), 1)
    nfull = (i * tq) // tk
    heads = range(N_HEADS)
    qs = [q_ref[:, h * 128:(h + 1) * 128] for h in heads]

    def step(j, carry, masked):
        ks = pl.multiple_of(j * tk, tk)
        ss = [_dot_nt(qs[h], k_ref[pl.ds(ks, tk), h * 128:(h + 1) * 128]) for h in heads]
        if masked:
            mask = ks + kloc <= qpos
            ss = [jnp.where(mask, s, NEG_INF) for s in ss]
        mns = [jnp.maximum(carry[h][0], jnp.max(ss[h], axis=-1, keepdims=True)) for h in heads]
        ps = [jnp.exp((ss[h] - mns[h]).astype(BF16)) for h in heads]
        als = [jnp.exp(carry[h][0] - mns[h]) for h in heads]
        pvs = [_dot(ps[h], v_ref[pl.ds(ks, tk), h * 128:(h + 1) * 128]) for h in heads]
        return tuple((mns[h], carry[h][1] * als[h] + pvs[h]) for h in heads)

    init = (jnp.full((tq, 1), NEG_INF, F32), jnp.zeros((tq, 128), F32))
    res = lax.fori_loop(0, nfull, functools.partial(step, masked=False), (init,) * N_HEADS)
    res = step(nfull, res, True)
    for p in range(N_PAIRS):
        a0, a1 = res[2 * p][1], res[2 * p + 1][1]
        acc = jnp.where(lane < HEAD_DIM, a0, a1)
        den = jnp.where(lane < HEAD_DIM, pltpu.roll(a0, HEAD_DIM, 1), pltpu.roll(a1, HEAD_DIM, 1))
        o_ref[:, p * 128:(p + 1) * 128] = acc / den


def mla_flash(q3, k3, v3, tq=256, tk=512):
    b, t, _ = q3.shape
    assert tk % tq == 0 and t % tk == 0
    w = N_HEADS * 128
    return pl.pallas_call(
        functools.partial(_mla_flash_kernel, tq=tq, tk=tk),
        grid=(b, t // tq),
        in_specs=[pl.BlockSpec((None, tq, w), lambda b_, i: (b_, i, 0)),
                  pl.BlockSpec((None, t, w), lambda b_, i: (b_, 0, 0)),
                  pl.BlockSpec((None, t, w), lambda b_, i: (b_, 0, 0))],
        out_specs=pl.BlockSpec((None, tq, D_GROUP), lambda b_, i: (b_, i, 0)),
        out_shape=jax.ShapeDtypeStruct((b, t, D_GROUP), F32),
        compiler_params=_cparams(("parallel", "arbitrary")),
        name="mla_flash",
    )(q3, k3, v3)


DILATED_PATTERNS = ((128, 1), (512, 4), (2048, 16))
DIL_BLOCK = 128
DIL_UNROLL = 4


def _rotate_half_heads(x):
    lane = lax.broadcasted_iota(jnp.int32, x.shape, 1)
    width = x.shape[1]
    half = HEAD_DIM // 2
    return jnp.where((lane & (HEAD_DIM - 1)) < half, -pltpu.roll(x, width - half, 1), pltpu.roll(x, half, 1))


def _dil_block(qb, kb, vb, mask, out):
    nq = qb.shape[0]
    lane = lax.broadcasted_iota(jnp.int32, (nq, 128), 1)
    kbb = kb.astype(BF16)
    vbb = vb.astype(BF16)
    hms = [(lane >> 6) == h for h in range(2)]
    ss = [jnp.where(mask, _dot_nt(jnp.where(hm, qb, 0.0).astype(BF16), kbb), NEG_INF) for hm in hms]
    yield
    ms = [jnp.max(s, axis=-1, keepdims=True) for s in ss]
    ps = [jnp.exp(s - m) for s, m in zip(ss, ms)]
    yield
    ls = [jnp.sum(p, axis=-1, keepdims=True) for p in ps]
    ohs = [_dot(p.astype(BF16), vbb) for p in ps]
    yield
    out[0] = jnp.where(hms[0], ohs[0] / ls[0], ohs[1] / ls[1])
    out[1] = jnp.where(hms[0], ms[0] + jnp.log(ls[0]), ms[1] + jnp.log(ls[1]))


def _dil_kernel(qraw_ref, kraw_ref, v_ref, cd_ref, sd_ref, o_ref, q_ref, k_ref, m_ref, z_ref, acc_ref, *, t):
    blk = DIL_BLOCK
    cd = cd_ref[...]
    sd = sd_ref[...]
    qraw = qraw_ref[...]
    kraw = kraw_ref[...]
    q_ref[...] = (qraw * cd + _rotate_half_heads(qraw) * sd) * (HEAD_DIM ** -0.5)
    k_ref[...] = kraw * cd + _rotate_half_heads(kraw) * sd
    for pi, (window, dil) in enumerate(DILATED_PATTERNS):
        assert window // dil == blk
        length = t // dil
        nb = length // blk
        nk = 2 * blk if nb > 1 else blk
        uq = lax.broadcasted_iota(jnp.int32, (blk, nk), 0)
        uk = lax.broadcasted_iota(jnp.int32, (blk, nk), 1)

        def one(it, dil=dil, nb=nb, nk=nk, uq=uq, uk=uk, first=(pi == 0)):
            r = it // nb
            jb = it % nb
            kb0 = jnp.maximum(jb - 1, 0) * blk
            qs = r + dil * blk * jb
            ks = r + dil * kb0
            if dil == 1:
                qidx = pl.ds(pl.multiple_of(qs, blk), blk)
                kidx = pl.ds(pl.multiple_of(ks, blk), nk)
            else:
                qidx = pl.ds(qs, blk, stride=dil)
                kidx = pl.ds(ks, nk, stride=dil)
            dist = (jb * blk + uq) - (kb0 + uk)
            mask = (dist >= 0) & (dist <= blk)
            out = [None, None]
            yield from _dil_block(q_ref[qidx, :], k_ref[kidx, :], v_ref[kidx, :], mask, out)
            o, lse = out
            if first:
                m_ref[qidx, :] = lse
                z_ref[qidx, :] = jnp.ones_like(lse)
                acc_ref[qidx, :] = o
            else:
                m_old = m_ref[qidx, :]
                m_new = jnp.maximum(m_old, lse)
                a = jnp.exp(m_old - m_new)
                b = jnp.exp(lse - m_new)
                m_ref[qidx, :] = m_new
                z_ref[qidx, :] = z_ref[qidx, :] * a + b
                acc_ref[qidx, :] = acc_ref[qidx, :] * a + o * b

        def body(it, carry, one=one):
            _round_robin([one(it * DIL_UNROLL + u) for u in range(DIL_UNROLL)])
            return carry

        assert (dil * nb) % DIL_UNROLL == 0
        lax.fori_loop(0, dil * nb // DIL_UNROLL, body, 0)
    o_ref[...] = acc_ref[...] / z_ref[...]


def dilated_attention(cols3, cd3, sd3):
    b, t, _ = cols3.shape
    assert t % (DIL_BLOCK * DILATED_PATTERNS[-1][1]) == 0
    c0 = C_DIL // 128
    col = lambda j: pl.BlockSpec((None, t, 128), lambda i, p: (i, 0, j + p))
    tab = pl.BlockSpec((None, t, 128), lambda i, p: (i, 0, 0))
    return pl.pallas_call(
        functools.partial(_dil_kernel, t=t),
        grid=(b, 2),
        in_specs=[col(c0), col(c0 + 2), col(c0 + 4), tab, tab],
        out_specs=pl.BlockSpec((None, t, 128), lambda i, p: (i, 0, p)),
        out_shape=jax.ShapeDtypeStruct((b, t, 256), F32),
        scratch_shapes=[pltpu.VMEM((t, 128), F32)] * 5,
        compiler_params=_cparams(("parallel", "parallel")),
        name="dilated_attention",
    )(cols3, cols3, cols3, cd3, sd3)


CHUNK = 64
PAIR = 2
PAIR_LANES = PAIR * HEAD_DIM
N_PAIRS = N_HEADS // PAIR
STACK = PAIR * CHUNK


def _stack_heads(x):
    lane = lax.broadcasted_iota(jnp.int32, x.shape, 1)
    return jnp.concatenate([jnp.where((lane >> 6) == h, x, 0.0) for h in range(PAIR)], axis=0)


def _unstack_heads(y):
    out = y[0:CHUNK]
    for h in range(1, PAIR):
        out = out + y[h * CHUNK:(h + 1) * CHUNK]
    return out


def _round_robin(stages):
    live = list(stages)
    while live:
        nxt = []
        for g in live:
            try:
                next(g)
                nxt.append(g)
            except StopIteration:
                pass
        live = nxt


def _head_mean_matrix(n=PAIR_LANES):
    r = lax.broadcasted_iota(jnp.int32, (n, n), 0)
    c = lax.broadcasted_iota(jnp.int32, (n, n), 1)
    return jnp.where((r >> 6) == (c >> 6), 1.0 / HEAD_DIM, 0.0).astype(BF16)


HGRN_HALVES = (32, 16, 8, 4, 2, 1)


def _hgrn_sum_matrix():
    t = lax.broadcasted_iota(jnp.int32, (CHUNK, CHUNK), 0)
    j = lax.broadcasted_iota(jnp.int32, (CHUNK, CHUNK), 1)
    tri = jnp.where(j <= t, 1.0, 0.0)
    blocks = [tri]
    for half in HGRN_HALVES:
        ref = (t // (2 * half)) * (2 * half) + half - 1
        blocks.append(tri - jnp.where(j <= ref, 1.0, 0.0))
    return jnp.concatenate(blocks, axis=0).astype(BF16)


def _hgrn_kernel(q_ref, f_ref, i_ref, g_ref, lbl_ref, gn_ref, o_ref, st_ref, *, layer, nchunk, nb):
    @pl.when(pl.program_id(1) == 0)
    def _():
        st_ref[...] = jnp.zeros_like(st_ref)

    logits = lbl_ref[...]
    e = jnp.exp(logits - jnp.max(logits, axis=0, keepdims=True))
    sm = e / jnp.sum(e, axis=0, keepdims=True)
    lb = jnp.zeros((1, 256), F32)
    for l in range(1, layer + 1):
        lb = lb + sm[l:l + 1, :]
    gn = gn_ref[...]
    summat = _hgrn_sum_matrix()
    hmean = _head_mean_matrix()
    r = lax.broadcasted_iota(jnp.int32, (STACK, STACK), 0)
    c = lax.broadcasted_iota(jnp.int32, (STACK, STACK), 1)
    same_head = (r >> 6) == (c >> 6)
    rc = r & (CHUNK - 1)
    cc = c & (CHUNK - 1)

    def one(bi, pi, rows):
        ls = pl.ds(pi * PAIR_LANES, PAIR_LANES)
        lbp = lb[:, pi * PAIR_LANES:(pi + 1) * PAIR_LANES]
        qv = q_ref[bi, rows, ls]
        qq = qv * jax.nn.sigmoid(qv)
        forget = lbp + (1.0 - lbp) * jax.nn.sigmoid(f_ref[bi, rows, ls])
        logf = jnp.log(forget)
        kk = 1.0 - forget
        v = i_ref[bi, rows, ls]
        yield
        sums = _split_dot_lhs(summat, logf)
        b = sums[0:CHUNK]
        a = jnp.where(rc == cc, _dot_nt(_stack_heads(qq).astype(BF16), _stack_heads(kk).astype(BF16)), 0.0)
        yield
        for li, half in enumerate(HGRN_HALVES):
            d = sums[(li + 1) * CHUNK:(li + 2) * CHUNK]
            ql = qq * jnp.exp(jnp.minimum(d, 0.0))
            kl = kk * jnp.exp(jnp.minimum(-d, 0.0))
            m = ((rc // (2 * half)) == (cc // (2 * half))) & ((rc % (2 * half)) >= half) & ((cc % (2 * half)) < half)
            a = a + jnp.where(m, _dot_nt(_stack_heads(ql).astype(BF16), _stack_heads(kl).astype(BF16)), 0.0)
            yield
        o = _unstack_heads(_dot(a.astype(BF16), _stack_heads(v).astype(BF16)))
        st = st_ref[bi, pi]
        o = o + _dot_nt((qq * jnp.exp(b)).astype(BF16), st.astype(BF16))
        yield
        b_end = b[CHUNK - 1:CHUNK, :]
        kbar = kk * jnp.exp(b_end - b)
        upd = lax.dot_general(v.astype(BF16), kbar.astype(BF16), (((0,), (0,)), ((), ())), preferred_element_type=F32)
        st_ref[bi, pi] = st * jnp.exp(b_end) + jnp.where(same_head, upd, 0.0)
        ms = _split_dot(o * o, hmean)
        yield
        gv = g_ref[bi, rows, ls]
        gnp = gn[:, pi * PAIR_LANES:(pi + 1) * PAIR_LANES]
        o_ref[bi, rows, ls] = o * lax.rsqrt(ms + RMS_EPS) * gnp * (gv * jax.nn.sigmoid(gv))

    def chunk(ci, carry):
        rows = pl.ds(pl.multiple_of(ci * CHUNK, CHUNK), CHUNK)
        _round_robin([one(bi, pi, rows) for bi in range(nb) for pi in range(N_PAIRS)])
        return carry

    lax.fori_loop(0, nchunk, chunk, 0)


SEQ_PER_STEP = 4


def hgrn(cols3, lb_logits, g_norm, layer, tb=512):
    b, t, _ = cols3.shape
    nb = SEQ_PER_STEP if b % SEQ_PER_STEP == 0 else 1
    c0 = C_HGRN // 256
    spec = lambda j: pl.BlockSpec((nb, tb, 256), lambda b_, i: (b_, i, j))
    depth = lb_logits.shape[0]
    return pl.pallas_call(
        functools.partial(_hgrn_kernel, layer=layer, nchunk=tb // CHUNK, nb=nb),
        grid=(b // nb, t // tb),
        in_specs=[spec(c0), spec(c0 + 1), spec(c0 + 2), spec(c0 + 3),
                  pl.BlockSpec((depth, 256), lambda b_, i: (0, 0)),
                  pl.BlockSpec((1, 256), lambda b_, i: (0, 0))],
        out_specs=spec(0),
        out_shape=jax.ShapeDtypeStruct((b, t, 256), F32),
        scratch_shapes=[pltpu.VMEM((nb, N_PAIRS, STACK, STACK), F32)],
        compiler_params=_cparams(("parallel", "arbitrary")),
        name="hgrn",
    )(cols3, cols3, cols3, cols3, lb_logits, g_norm.reshape(1, 256))


def _head_sum_matrix():
    r = lax.broadcasted_iota(jnp.int32, (256, 256), 0)
    c = lax.broadcasted_iota(jnp.int32, (256, 256), 1)
    return jnp.where((r >> 6) == (c >> 6), 1.0, 0.0).astype(BF16)


def _rwkv_prep_tile(a, last_prev, mu, w0, w2, a0, a2, g2, k_k, k_a, r_k):
    row = lax.broadcasted_iota(jnp.int32, a.shape, 0)
    prev = jnp.where(row == 0, last_prev, pltpu.roll(a, 1, 0))
    xs = a + (prev - a) * mu
    r = xs[:, 0:256]
    k = xs[:, 256:512]
    v = xs[:, 512:768]
    xwa = xs[:, 768:896]
    xg = xs[:, 896:1024]
    z = -(w0 + _dot(jnp.tanh(xwa).astype(BF16), w2))
    softplus = jnp.maximum(z, 0.0) + jnp.log(1.0 + jnp.exp(-jnp.abs(z)))
    lw = -jnp.exp(-softplus - 0.5)
    lr = jax.nn.sigmoid(a0 + _dot(xwa.astype(BF16), a2))
    g = _dot(jax.nn.sigmoid(xg).astype(BF16), g2)
    hsum = _head_sum_matrix()
    kk = k * k_k
    kk = kk / jnp.maximum(jnp.sqrt(_split_dot(kk * kk, hsum)), 1e-12)
    k2 = k * (1.0 + (lr - 1.0) * k_a)
    bonus = _split_dot(r * k2 * r_k, hsum) * v
    return r, lw, k2, v, kk, kk * lr, bonus, g


RWKV_MERGE_HALVES = (2, 4, 8, 16, 32)


def _compact_upper(x, half):
    return jnp.concatenate([x[s + half:s + 2 * half] for s in range(0, STACK, 2 * half)], axis=0)


def _expand_upper(c, half):
    zero = jnp.zeros((half, c.shape[1]), c.dtype)
    return jnp.concatenate([p for i in range(STACK // (2 * half)) for p in (zero, c[i * half:(i + 1) * half])], axis=0)


def _rwkv_kernel(c_ref, p_ref, mu_ref, w0_ref, w2_ref, a0_ref, a2_ref, g2_ref, kk_ref, ka_ref, rk_ref,
                 lnw_ref, lnb_ref, o_ref,
                 r_ref, lw_ref, k_ref, v_ref, kap_ref, at_ref, bonus_ref, g_ref, ht_ref, *, nchunk, nb):
    first = pl.program_id(1) == 0

    @pl.when(first)
    def _():
        ht_ref[...] = jnp.zeros_like(ht_ref)

    for bi in range(nb):
        outs = _rwkv_prep_tile(c_ref[bi], jnp.where(first, 0.0, p_ref[bi, 7:8, :]), mu_ref[...], w0_ref[...],
                               w2_ref[...], a0_ref[...], a2_ref[...], g2_ref[...], kk_ref[...], ka_ref[...],
                               rk_ref[...])
        for ref, val in zip((r_ref, lw_ref, k_ref, v_ref, kap_ref, at_ref, bonus_ref, g_ref), outs):
            ref[bi] = val

    t = lax.broadcasted_iota(jnp.int32, (CHUNK, CHUNK), 0)
    j = lax.broadcasted_iota(jnp.int32, (CHUNK, CHUNK), 1)
    tri = jnp.where(j <= t, 1.0, 0.0).astype(BF16)
    hmean = _head_mean_matrix()
    rr = lax.broadcasted_iota(jnp.int32, (STACK, STACK), 0)
    cc = lax.broadcasted_iota(jnp.int32, (STACK, STACK), 1)
    same_head = (rr >> 6) == (cc >> 6)
    rc = rr & (CHUNK - 1)
    sc = cc & (CHUNK - 1)
    eye = jnp.where(rr == cc, 1.0, 0.0)
    lnw = lnw_ref[...]
    lnb = lnb_ref[...]

    def one(bi, pi, rows):
        ls = pl.ds(pi * PAIR_LANES, PAIR_LANES)
        lw = lw_ref[bi, rows, ls]
        lc = _split_dot_lhs(tri, lw)
        l_end = lc[CHUNK - 1:CHUNK, :]
        kap = kap_ref[bi, rows, ls]
        at = at_ref[bi, rows, ls]
        k2 = k_ref[bi, rows, ls]
        v = v_ref[bi, rows, ls]
        inv = jnp.exp(-lc)
        fwd = jnp.exp(l_end - lc)
        s_kap = _stack_heads(kap * jnp.exp(lc - lw)).astype(BF16)
        s_r32 = _stack_heads(r_ref[bi, rows, ls] * jnp.exp(lc))
        s_r = s_r32.astype(BF16)
        s_a = _stack_heads(at * inv).astype(BF16)
        s_k = _stack_heads(k2 * inv).astype(BF16)
        s_v = _stack_heads(v).astype(BF16)
        strict = rc > sc
        incl = rc >= sc
        yield
        a_ka = jnp.where(strict, _dot_nt(s_kap, s_a), 0.0)
        a_kk = jnp.where(strict, _dot_nt(s_kap, s_k), 0.0).astype(BF16)
        m1 = ((rc >> 1) == (sc >> 1)) & strict
        tinv = eye - jnp.where(m1, a_ka, 0.0)
        yield
        a_ra = jnp.where(incl, _dot_nt(s_r, s_a), 0.0).astype(BF16)
        a_rk = jnp.where(incl, _dot_nt(s_r, s_k), 0.0).astype(BF16)
        for half in RWKV_MERGE_HALVES:
            m = ((rc // (2 * half)) == (sc // (2 * half))) & ((rc % (2 * half)) >= half) & ((sc % (2 * half)) < half)
            off = jnp.where(m, a_ka, 0.0)
            tb = tinv.astype(BF16)
            if half % 8 == 0:
                x = _expand_upper(_dot(_compact_upper(off, half).astype(BF16), tb), half).astype(BF16)
                yield
                tinv = tinv - _expand_upper(_dot(_compact_upper(tinv, half).astype(BF16), x), half)
            else:
                x = _dot(off.astype(BF16), tb).astype(BF16)
                yield
                tinv = tinv - _dot(tb, x)
            yield
        tb = tinv.astype(BF16)
        kt_st = _dot(tb, s_kap)
        w_st = _dot(a_kk, s_v).astype(BF16)
        yield
        vt_st = _dot(tb, w_st)
        rt = _unstack_heads(s_r32 - _dot(a_ra, kt_st.astype(BF16)))
        yield
        y0 = _unstack_heads(_dot(a_rk, s_v) - _dot(a_ra, vt_st.astype(BF16)))
        kt = _unstack_heads(kt_st)
        vt = _unstack_heads(vt_st)
        ht = ht_ref[bi, pi]
        htb = ht.astype(BF16)
        yield
        y = y0 + _dot_nt(rt.astype(BF16), htb)
        u = vt + _dot_nt(kt.astype(BF16), htb)
        yield
        lhs = jnp.concatenate([v, u], axis=0).astype(BF16)
        rhs = jnp.concatenate([k2 * fwd, -(at * fwd)], axis=0).astype(BF16)
        upd = lax.dot_general(lhs, rhs, (((0,), (0,)), ((), ())), preferred_element_type=F32)
        ht_ref[bi, pi] = ht * jnp.exp(l_end) + jnp.where(same_head, upd, 0.0)
        mean = _split_dot(y, hmean)
        yield
        yc = y - mean
        var = _split_dot(yc * yc, hmean)
        yield
        lanes = slice(pi * PAIR_LANES, (pi + 1) * PAIR_LANES)
        yn = yc * lax.rsqrt(var + RWKV_LN_EPS) * lnw[:, lanes] + lnb[:, lanes]
        o_ref[bi, rows, ls] = (yn + bonus_ref[bi, rows, ls]) * g_ref[bi, rows, ls]

    def chunk(ci, carry):
        rows = pl.ds(pl.multiple_of(ci * CHUNK, CHUNK), CHUNK)
        _round_robin([one(bi, pi, rows) for bi in range(nb) for pi in range(N_PAIRS)])
        return carry

    lax.fori_loop(0, nchunk, chunk, 0)


def rwkv_mix(cols3, mu, w0, w2p, a0, a2p, g2, k_k, k_a, r_k, ln_w, ln_b, tb=512):
    b, t, _ = cols3.shape
    nb = SEQ_PER_STEP if b % SEQ_PER_STEP == 0 else 1
    spec = pl.BlockSpec((nb, tb, 256), lambda b_, i: (b_, i, 0))
    full = lambda shape: pl.BlockSpec(shape, lambda b_, i: (0, 0))
    row = lambda x: x.reshape(1, -1)
    return pl.pallas_call(
        functools.partial(_rwkv_kernel, nchunk=tb // CHUNK, nb=nb),
        grid=(b // nb, t // tb),
        in_specs=[pl.BlockSpec((nb, tb, 1024), lambda b_, i: (b_, i, C_RWKV // 1024)),
                  pl.BlockSpec((nb, 8, 1024), lambda b_, i: (b_, jnp.maximum(i * (tb // 8) - 1, 0), C_RWKV // 1024)),
                  full((1, 1024)), full((1, 256)), full((128, 256)), full((1, 256)), full((128, 256)),
                  full((128, 256)), full((1, 256)), full((1, 256)), full((1, 256)), full((1, 256)), full((1, 256))],
        out_specs=spec,
        out_shape=jax.ShapeDtypeStruct((b, t, 256), F32),
        scratch_shapes=[pltpu.VMEM((nb, tb, 256), F32)] * 8 + [pltpu.VMEM((nb, N_PAIRS, STACK, STACK), F32)],
        compiler_params=_cparams(("parallel", "arbitrary")),
        name="rwkv_mix",
    )(cols3, cols3, row(mu), row(w0), w2p, row(a0), a2p, g2, row(k_k), row(k_a), row(r_k), row(ln_w), row(ln_b))


def _rot_cols(w, half):
    g = w.reshape(w.shape[0], -1, 2, half)
    return jnp.concatenate([-g[:, :, 1:2, :], g[:, :, 0:1, :]], axis=2).reshape(w.shape)


def pack_w_in(w):
    d = w.shape[0]
    rw = w[:, 0:1024]
    dq, dk, dv = w[:, 1024:1280], w[:, 1280:1536], w[:, 1536:1792]
    cq, ckv, kr = w[:, 1792:2048], w[:, 2048:2176], w[:, 2176:2208]
    hg = w[:, 2208:3232]
    z = lambda n: jnp.zeros((d, n), w.dtype)
    kra = jnp.concatenate([z(MLA_NOPE), kr, z(128 - MLA_NOPE - MLA_ROPE)], axis=1)
    krb = jnp.concatenate([z(MLA_NOPE), _rot_cols(kr, MLA_ROPE // 2), z(128 - MLA_NOPE - MLA_ROPE)], axis=1)
    packed = jnp.concatenate([rw, dq, dk, dv, hg, cq, ckv, kra, krb], axis=1)
    return packed.astype(BF16)


def pack_mla(w_uq, w_ukv):
    rq = w_uq.shape[0]
    q = w_uq.reshape(rq, N_HEADS, MLA_NOPE + MLA_ROPE)
    nope, rope = q[..., :MLA_NOPE], q[..., MLA_NOPE:]
    pad = jnp.zeros((rq, N_HEADS, 128 - MLA_NOPE - MLA_ROPE), w_uq.dtype)
    wq1 = jnp.concatenate([nope, rope, pad], axis=-1).reshape(rq, N_HEADS * 128)
    rrot = _rot_cols(rope.reshape(rq, N_HEADS * MLA_ROPE), MLA_ROPE // 2).reshape(rq, N_HEADS, MLA_ROPE)
    wq2 = jnp.concatenate([jnp.zeros_like(nope), rrot, pad], axis=-1).reshape(rq, N_HEADS * 128)
    rk = w_ukv.shape[0]
    kv = w_ukv.reshape(rk, N_HEADS, MLA_NOPE + HEAD_DIM)
    wk = jnp.concatenate([kv[..., :MLA_NOPE], jnp.zeros((rk, N_HEADS, 128 - MLA_NOPE), w_ukv.dtype)],
                         axis=-1).reshape(rk, N_HEADS * 128)
    vz = jnp.zeros((rk, HEAD_DIM), w_ukv.dtype)
    wv = jnp.concatenate([x for h in range(N_HEADS)
                          for x in ((kv[:, h, MLA_NOPE:], vz) if h % 2 == 0 else (vz, kv[:, h, MLA_NOPE:]))], axis=1)
    return wq1.astype(BF16), wq2.astype(BF16), wk.astype(BF16), wv.astype(BF16)


def kernel(x, positions, mix_norm, w_in, w_out, rwkv_mu, rwkv_w0, rwkv_w2, rwkv_a0, rwkv_a2, rwkv_g2, rwkv_k_k,
           rwkv_k_a, rwkv_r_k, rwkv_ln_w, rwkv_ln_b, mla_q_norm, mla_w_uq, mla_kv_norm, mla_w_ukv, hgrn_lb_logits,
           hgrn_g_norm, ffn_norm, ffn_w1, ffn_w3, ffn_w2, moe_router, moe_w1, moe_w3, moe_w2, final_norm):
    b, t, d = x.shape
    n = b * t
    depth = w_in.shape[0]
    x2 = x.reshape(n, d)
    cd, sd, cm, sm = rope_tables(positions.reshape(n, 1))
    lora_pad = jnp.zeros((rwkv_w2.shape[1], D_GROUP), F32)
    fused_final = False
    for layer in range(depth):
        cols = in_proj(x2, mix_norm[layer], pack_w_in(w_in[layer]))
        cols3 = cols.reshape(b, t, N_PACK)
        y_a = rwkv_mix(cols3, rwkv_mu[layer], rwkv_w0[layer],
                       jnp.concatenate([rwkv_w2[layer], lora_pad], axis=0).astype(BF16), rwkv_a0[layer],
                       jnp.concatenate([lora_pad, rwkv_a2[layer]], axis=0).astype(BF16),
                       rwkv_g2[layer].astype(BF16), rwkv_k_k[layer], rwkv_k_a[layer], rwkv_r_k[layer],
                       rwkv_ln_w[layer], rwkv_ln_b[layer])
        y_b = dilated_attention(cols3, cd.reshape(b, t, -1), sd.reshape(b, t, -1))
        q_m, k_m, v_m = mla_prep(cols, cm, sm, mla_q_norm[layer], mla_kv_norm[layer],
                                 *pack_mla(mla_w_uq[layer], mla_w_ukv[layer]))
        y_c = mla_flash(q_m.reshape(b, t, -1), k_m.reshape(b, t, -1), v_m.reshape(b, t, -1))
        y_d = hgrn(cols3, hgrn_lb_logits, hgrn_g_norm[layer], layer)
        ys = [y.reshape(n, D_GROUP) for y in (y_a, y_b, y_c, y_d)]
        x2, h = out_proj(x2, ys, w_out[layer].astype(BF16), ffn_norm[layer])
        j = layer // 2
        if layer % 2 == 0:
            x2 = ffn_dense(h, x2, ffn_w1[j].astype(BF16), ffn_w3[j].astype(BF16), ffn_w2[j].astype(BF16))
        else:
            route, hp = moe_route(x2, ffn_norm[layer], jnp.pad(moe_router[j], ((0, 0), (0, 128 - N_EXPERTS))))
            rows_pad = TOP_K * n + N_EXPERTS * MOE_TILE
            pos, tile_expert, tile_valid = moe_plan(route, rows_pad)
            xs = moe_scatter(pos, hp, rows_pad)
            ys = moe_experts(tile_expert, tile_valid, xs, moe_w1[j].astype(BF16), moe_w3[j].astype(BF16),
                             moe_w2[j].astype(BF16))
            fused_final = layer == depth - 1
            x2 = moe_combine(pos, x2, route, ys, final_norm, fused_final)
    if not fused_final:
        x2 = final_rms(x2, final_norm)
    return x2.reshape(b, t, d)
```

```python
import functools

import jax
import jax.numpy as jnp
from jax import lax
from jax.experimental import pallas as pl
from jax.experimental.pallas import tpu as pltpu

F32 = jnp.float32
BF16 = jnp.bfloat16

D_GROUP = 256
HEAD_DIM = 64
N_HEADS = 4
RMS_EPS = 1e-6
RWKV_LN_EPS = 64e-5
ROPE_THETA = 10000.0
NEG_INF = -1e30
N_EXPERTS = 8
MLA_ROPE = 32
MLA_NOPE = 64

VMEM_LIMIT = 56 * 1024 * 1024

C_RWKV = 0
C_DIL = 1024
C_HGRN = 1792
C_MLA = 2816
N_PACK = 3456


def _cparams(sem, vmem=VMEM_LIMIT):
    return pltpu.CompilerParams(dimension_semantics=sem, vmem_limit_bytes=vmem)


def _rms(x, w):
    return x * lax.rsqrt(jnp.mean(x * x, axis=-1, keepdims=True) + RMS_EPS) * w


def _dot(a, b):
    return jnp.dot(a, b, preferred_element_type=F32)


def _dot_nt(a, b):
    return lax.dot_general(a, b, (((1,), (1,)), ((), ())), preferred_element_type=F32)


def _split_dot(a_f32, b_bf16):
    hi = a_f32.astype(BF16)
    lo = (a_f32 - hi.astype(F32)).astype(BF16)
    return _dot(hi, b_bf16) + _dot(lo, b_bf16)


def _split_dot_lhs(a_bf16, b_f32):
    hi = b_f32.astype(BF16)
    lo = (b_f32 - hi.astype(F32)).astype(BF16)
    return _dot(a_bf16, hi) + _dot(a_bf16, lo)


def _rope_kernel(pos_ref, cd_ref, sd_ref, cm_ref, sm_ref):
    pos = pos_ref[...].astype(F32)
    tm = pos.shape[0]
    lane = lax.broadcasted_iota(jnp.int32, (tm, 128), 1)
    nd = HEAD_DIM // 2
    nm = MLA_ROPE // 2
    is_d = lane < nd
    is_m = (lane >= nd) & (lane < nd + nm)
    expo = jnp.where(is_d, lane.astype(F32) * (-2.0 / HEAD_DIM), (lane - nd).astype(F32) * (-2.0 / MLA_ROPE))
    ang = pos * jnp.exp(expo * jnp.log(ROPE_THETA))
    cos = jnp.cos(ang)
    sin = jnp.sin(ang)
    roll = lambda x, s: pltpu.roll(x, s, 1)

    def tile_d(x):
        x0 = jnp.where(is_d, x, 0.0)
        return x0 + roll(x0, nd) + roll(x0, 2 * nd) + roll(x0, 3 * nd)

    def place_m(x):
        x0 = jnp.where(is_m, x, 0.0)
        return roll(x0, MLA_NOPE - nd) + roll(x0, MLA_NOPE - nd + nm)

    cd_ref[...] = tile_d(cos)
    sd_ref[...] = tile_d(sin)
    cm_ref[...] = place_m(cos) + jnp.where(lane < MLA_NOPE, 1.0, 0.0)
    sm_ref[...] = place_m(sin)


def rope_tables(pos_col, tm=512):
    n = pos_col.shape[0]
    return pl.pallas_call(
        _rope_kernel,
        grid=(n // tm,),
        in_specs=[pl.BlockSpec((tm, 1), lambda i: (i, 0))],
        out_specs=[pl.BlockSpec((tm, 128), lambda i: (i, 0))] * 4,
        out_shape=[jax.ShapeDtypeStruct((n, 128), F32)] * 4,
        compiler_params=_cparams(("parallel",)),
        name="rope_tables",
    )(pos_col)


def _in_proj_kernel(x_ref, nw_ref, w_ref, o_ref):
    h = _rms(x_ref[...], nw_ref[...]).astype(BF16)
    o_ref[...] = _dot(h, w_ref[...])


def in_proj(x2, norm_w, w_packed, tm=512):
    n, d = x2.shape
    nc = w_packed.shape[1]
    return pl.pallas_call(
        _in_proj_kernel,
        grid=(n // tm,),
        in_specs=[pl.BlockSpec((tm, d), lambda i: (i, 0)),
                  pl.BlockSpec((1, d), lambda i: (0, 0)),
                  pl.BlockSpec((d, nc), lambda i: (0, 0))],
        out_specs=pl.BlockSpec((tm, nc), lambda i: (i, 0)),
        out_shape=jax.ShapeDtypeStruct((n, nc), F32),
        compiler_params=_cparams(("parallel",)),
        name="in_proj",
    )(x2, norm_w.reshape(1, d), w_packed)


def _out_proj_kernel(x_ref, ya_ref, yb_ref, yc_ref, yd_ref, w_ref, nw_ref, xo_ref, h_ref):
    acc = x_ref[...]
    for g, y_ref in enumerate((ya_ref, yb_ref, yc_ref, yd_ref)):
        acc = acc + _dot(y_ref[...].astype(BF16), w_ref[g * D_GROUP:(g + 1) * D_GROUP, :])
    xo_ref[...] = acc
    h_ref[...] = _rms(acc, nw_ref[...]).astype(BF16)


def out_proj(x2, ys, w_out_bf16, norm_w, tm=512):
    n, d = x2.shape
    yspec = pl.BlockSpec((tm, D_GROUP), lambda i: (i, 0))
    return pl.pallas_call(
        _out_proj_kernel,
        grid=(n // tm,),
        in_specs=[pl.BlockSpec((tm, d), lambda i: (i, 0)), yspec, yspec, yspec, yspec,
                  pl.BlockSpec((d, d), lambda i: (0, 0)),
                  pl.BlockSpec((1, d), lambda i: (0, 0))],
        out_specs=[pl.BlockSpec((tm, d), lambda i: (i, 0)), pl.BlockSpec((tm, d), lambda i: (i, 0))],
        out_shape=[jax.ShapeDtypeStruct((n, d), F32), jax.ShapeDtypeStruct((n, d), BF16)],
        compiler_params=_cparams(("parallel",)),
        name="out_proj",
    )(x2, *ys, w_out_bf16, norm_w.reshape(1, d))


def _ffn_kernel(h_ref, x_ref, w1_ref, w3_ref, w2_ref, o_ref):
    h = h_ref[...]
    a = _dot(h, w1_ref[...])
    b = _dot(h, w3_ref[...])
    act = (a * jax.nn.sigmoid(a) * b).astype(BF16)
    o_ref[...] = x_ref[...] + _dot(act, w2_ref[...])


def ffn_dense(h_bf16, x2, w1, w3, w2, tm=512):
    n, d = x2.shape
    f = w1.shape[1]
    resident = lambda shape: pl.BlockSpec(shape, lambda i: (0, 0), pipeline_mode=pl.Buffered(1))
    return pl.pallas_call(
        _ffn_kernel,
        grid=(n // tm,),
        in_specs=[pl.BlockSpec((tm, d), lambda i: (i, 0)),
                  pl.BlockSpec((tm, d), lambda i: (i, 0)),
                  resident((d, f)), resident((d, f)), resident((f, d))],
        out_specs=pl.BlockSpec((tm, d), lambda i: (i, 0)),
        out_shape=jax.ShapeDtypeStruct((n, d), F32),
        compiler_params=_cparams(("parallel",)),
        name="ffn_dense",
    )(h_bf16, x2, w1, w3, w2)


MOE_TILE = 512
TOP_K = 2
MOE_DMA_TOKENS = 1024


def _pack_bf16_pairs(x):
    bits = pltpu.bitcast(x.astype(BF16).astype(F32), jnp.uint32)
    half = bits.shape[1] // 2
    return (bits[:, :half] >> 16) | (bits[:, half:] & jnp.uint32(0xFFFF0000))


def _unpack_bf16_pairs(p):
    lo = pltpu.bitcast(p << 16, F32)
    hi = pltpu.bitcast(p & jnp.uint32(0xFFFF0000), F32)
    return jnp.concatenate([lo, hi], axis=1)


def _route_top2(h, w_router):
    logits = jnp.dot(h, w_router, preferred_element_type=F32, precision=lax.Precision.HIGHEST)
    lane = lax.broadcasted_iota(jnp.int32, logits.shape, 1)
    logits = jnp.where(lane < N_EXPERTS, logits, -jnp.inf)
    m1 = jnp.max(logits, axis=-1, keepdims=True)
    i1 = jnp.min(jnp.where(logits == m1, lane, 128), axis=-1, keepdims=True)
    rest = jnp.where(lane == i1, -jnp.inf, logits)
    m2 = jnp.max(rest, axis=-1, keepdims=True)
    i2 = jnp.min(jnp.where(rest == m2, lane, 128), axis=-1, keepdims=True)
    e2 = jnp.exp(m2 - m1)
    g1 = 1.0 / (1.0 + e2)
    g2 = e2 * g1
    return jnp.where(lane == 0, i1.astype(F32), jnp.where(lane == 1, i2.astype(F32),
                     jnp.where(lane == 2, g1, jnp.where(lane == 3, g2, 0.0))))


def _out_proj_route_kernel(x_ref, ya_ref, yb_ref, yc_ref, yd_ref, w_ref, nw_ref, wr_ref, xo_ref, route_ref, hp_ref):
    acc = x_ref[...]
    for g, y_ref in enumerate((ya_ref, yb_ref, yc_ref, yd_ref)):
        acc = acc + _dot(y_ref[...].astype(BF16), w_ref[g * D_GROUP:(g + 1) * D_GROUP, :])
    xo_ref[...] = acc
    h = _rms(acc, nw_ref[...])
    route_ref[...] = _route_top2(h, wr_ref[...])
    hp_ref[...] = _pack_bf16_pairs(h)


def out_proj_route(x2, ys, w_out_bf16, norm_w, w_router_pad, tm=512):
    n, d = x2.shape
    yspec = pl.BlockSpec((tm, D_GROUP), lambda i: (i, 0))
    return pl.pallas_call(
        _out_proj_route_kernel,
        grid=(n // tm,),
        in_specs=[pl.BlockSpec((tm, d), lambda i: (i, 0)), yspec, yspec, yspec, yspec,
                  pl.BlockSpec((d, d), lambda i: (0, 0)),
                  pl.BlockSpec((1, d), lambda i: (0, 0)),
                  pl.BlockSpec((d, 128), lambda i: (0, 0))],
        out_specs=[pl.BlockSpec((tm, d), lambda i: (i, 0)), pl.BlockSpec((tm, 128), lambda i: (i, 0)),
                   pl.BlockSpec((tm, d // 2), lambda i: (i, 0))],
        out_shape=[jax.ShapeDtypeStruct((n, d), F32), jax.ShapeDtypeStruct((n, 128), F32),
                   jax.ShapeDtypeStruct((n, d // 2), jnp.uint32)],
        compiler_params=_cparams(("parallel",)),
        name="out_proj_route",
    )(x2, *ys, w_out_bf16, norm_w.reshape(1, d), w_router_pad)


def moe_plan(route, n_rows_pad):
    e = route[:, :TOP_K].astype(jnp.int32)
    onehot = (e.reshape(-1, 1) == jnp.arange(N_EXPERTS, dtype=jnp.int32)[None, :]).astype(jnp.int32)
    csum = jnp.cumsum(onehot, axis=0)
    counts = csum[-1]
    rank = jnp.sum((csum - onehot) * onehot, axis=1)
    gsz = ((counts + MOE_TILE - 1) // MOE_TILE) * MOE_TILE
    gend = jnp.cumsum(gsz)
    goff = gend - gsz
    pos = (goff[e.reshape(-1)] + rank).astype(jnp.int32)
    tile_start = jnp.arange(n_rows_pad // MOE_TILE, dtype=jnp.int32) * MOE_TILE
    tile_expert = jnp.minimum(jnp.sum((tile_start[:, None] >= gend[None, :]).astype(jnp.int32), axis=1),
                              N_EXPERTS - 1).astype(jnp.int32)
    tile_valid = (tile_start < gend[-1]).astype(jnp.int32)
    return pos, tile_expert, tile_valid


def _moe_scatter_kernel(pos_ref, hp_ref, xs_in_ref, xs_ref, sem):
    del xs_in_ref
    i = pl.program_id(0)
    ts = hp_ref.shape[0]

    def copy(r, slot):
        dst = pos_ref[(i * ts + r) * TOP_K + slot]
        return pltpu.make_async_copy(hp_ref.at[pl.ds(r, 1), :], xs_ref.at[pl.ds(dst, 1), :], sem)

    def issue(r, carry):
        for slot in range(TOP_K):
            copy(r, slot).start(priority=slot)
        return carry

    lax.fori_loop(0, ts, issue, 0, unroll=8)
    for slot in range(TOP_K):
        pltpu.make_async_copy(hp_ref, xs_ref.at[pl.ds(0, ts), :], sem).wait()


def moe_scatter(pos, hp, n_rows_pad, ts=MOE_DMA_TOKENS):
    n, dh = hp.shape
    xs0 = jnp.zeros((n_rows_pad, dh), jnp.uint32)
    return pl.pallas_call(
        _moe_scatter_kernel,
        grid_spec=pltpu.PrefetchScalarGridSpec(
            num_scalar_prefetch=1,
            grid=(n // ts,),
            in_specs=[pl.BlockSpec((ts, dh), lambda i, pos_: (i, 0)),
                      pl.BlockSpec(memory_space=pl.ANY)],
            out_specs=pl.BlockSpec(memory_space=pl.ANY),
            scratch_shapes=[pltpu.SemaphoreType.DMA],
        ),
        out_shape=jax.ShapeDtypeStruct((n_rows_pad, dh), jnp.uint32),
        input_output_aliases={2: 0},
        compiler_params=_cparams(("arbitrary",)),
        name="moe_scatter",
    )(pos, hp, xs0)


def _moe_expert_kernel(te_ref, tv_ref, xs_ref, w1_ref, w3_ref, w2_ref, ys_ref, h_ref, acc_ref):
    i = pl.program_id(0)
    j = pl.program_id(1)

    @pl.when(tv_ref[i] > 0)
    def _():
        @pl.when(j == 0)
        def _():
            h_ref[...] = _unpack_bf16_pairs(xs_ref[...]).astype(BF16)
            acc_ref[...] = jnp.zeros_like(acc_ref)

        h = h_ref[...]
        a = _dot(h, w1_ref[...])
        b = _dot(h, w3_ref[...])
        act = (a * jax.nn.sigmoid(a) * b).astype(BF16)
        acc_ref[...] += _dot(act, w2_ref[...])

        @pl.when(j == pl.num_programs(1) - 1)
        def _():
            ys_ref[...] = _pack_bf16_pairs(acc_ref[...])

    @pl.when((tv_ref[i] == 0) & (j == pl.num_programs(1) - 1))
    def _():
        ys_ref[...] = jnp.zeros_like(ys_ref)


def moe_experts(tile_expert, tile_valid, xs, w1, w3, w2, tf=1792):
    rows, dh = xs.shape
    d = 2 * dh
    f = w1.shape[2]
    tm = MOE_TILE
    return pl.pallas_call(
        _moe_expert_kernel,
        grid_spec=pltpu.PrefetchScalarGridSpec(
            num_scalar_prefetch=2,
            grid=(rows // tm, f // tf),
            in_specs=[pl.BlockSpec((tm, dh), lambda i, j, te, tv: (i, 0)),
                      pl.BlockSpec((None, d, tf), lambda i, j, te, tv: (te[i], 0, j * tv[i])),
                      pl.BlockSpec((None, d, tf), lambda i, j, te, tv: (te[i], 0, j * tv[i])),
                      pl.BlockSpec((None, tf, d), lambda i, j, te, tv: (te[i], j * tv[i], 0))],
            out_specs=pl.BlockSpec((tm, dh), lambda i, j, te, tv: (i, 0)),
            scratch_shapes=[pltpu.VMEM((tm, d), BF16), pltpu.VMEM((tm, d), F32)],
        ),
        out_shape=jax.ShapeDtypeStruct((rows, dh), jnp.uint32),
        compiler_params=_cparams(("arbitrary", "arbitrary")),
        name="moe_experts",
    )(tile_expert, tile_valid, xs, w1, w3, w2)


def _moe_combine_kernel(pos_ref, x_ref, route_ref, nw_ref, ys_ref, o_ref, buf_ref, sem, *, final):
    i = pl.program_id(0)
    tc = x_ref.shape[0]

    def copy(r, slot):
        src = pos_ref[(i * tc + r) * TOP_K + slot]
        return pltpu.make_async_copy(ys_ref.at[pl.ds(src, 1), :], buf_ref.at[slot, pl.ds(r, 1), :], sem)

    def issue(r, carry):
        for slot in range(TOP_K):
            copy(r, slot).start(priority=slot)
        return carry

    lax.fori_loop(0, tc, issue, 0, unroll=8)
    for slot in range(TOP_K):
        pltpu.make_async_copy(ys_ref.at[pl.ds(0, tc), :], buf_ref.at[slot], sem).wait()
    route = route_ref[...]
    g1 = route[:, 2:3]
    g2 = route[:, 3:4]
    out = x_ref[...] + g1 * _unpack_bf16_pairs(buf_ref[0]) + g2 * _unpack_bf16_pairs(buf_ref[1])
    if final:
        out = _rms(out, nw_ref[...])
    o_ref[...] = out


def moe_combine(pos, x2, route, ys, norm_w, final, tc=MOE_DMA_TOKENS):
    n, d = x2.shape
    return pl.pallas_call(
        functools.partial(_moe_combine_kernel, final=final),
        grid_spec=pltpu.PrefetchScalarGridSpec(
            num_scalar_prefetch=1,
            grid=(n // tc,),
            in_specs=[pl.BlockSpec((tc, d), lambda i, pos_: (i, 0)),
                      pl.BlockSpec((tc, 128), lambda i, pos_: (i, 0)),
                      pl.BlockSpec((1, d), lambda i, pos_: (0, 0)),
                      pl.BlockSpec(memory_space=pl.ANY)],
            out_specs=pl.BlockSpec((tc, d), lambda i, pos_: (i, 0)),
            scratch_shapes=[pltpu.VMEM((TOP_K, tc, d // 2), jnp.uint32), pltpu.SemaphoreType.DMA],
        ),
        out_shape=jax.ShapeDtypeStruct((n, d), F32),
        compiler_params=_cparams(("arbitrary",)),
        name="moe_combine",
    )(pos, x2, route, norm_w.reshape(1, d), ys)


def _final_norm_kernel(x_ref, nw_ref, o_ref):
    o_ref[...] = _rms(x_ref[...], nw_ref[...])


def final_rms(x2, norm_w, tm=1024):
    n, d = x2.shape
    return pl.pallas_call(
        _final_norm_kernel,
        grid=(n // tm,),
        in_specs=[pl.BlockSpec((tm, d), lambda i: (i, 0)), pl.BlockSpec((1, d), lambda i: (0, 0))],
        out_specs=pl.BlockSpec((tm, d), lambda i: (i, 0)),
        out_shape=jax.ShapeDtypeStruct((n, d), F32),
        compiler_params=_cparams(("parallel",)),
        name="final_norm",
    )(x2, norm_w.reshape(1, d))


MLA_KV_ROWS = 512


def _mla_kernel(cq_ref, ckv_ref, kra_ref, krb_ref, cm_ref, sm_ref, qn_ref, kvn_ref,
                wq1_ref, wq2_ref, wk_ref, wv_ref, o_ref, k_ref, v_ref, *, tq, tk):
    i = pl.program_id(1)
    t = k_ref.shape[0]

    @pl.when(i == 0)
    def _():
        lane = lax.broadcasted_iota(jnp.int32, (MLA_KV_ROWS, N_HEADS * 128), 1)
        ones = jnp.where(((lane >> 6) & 1) == (((lane >> 7) & 1) ^ 1), 1.0, 0.0)

        def build(c, carry):
            rows = pl.ds(pl.multiple_of(c * MLA_KV_ROWS, MLA_KV_ROWS), MLA_KV_ROWS)
            kvn = _rms(ckv_ref[rows, :], kvn_ref[...]).astype(BF16)
            kr = kra_ref[rows, :] * cm_ref[rows, :] + krb_ref[rows, :] * sm_ref[rows, :]
            k_ref[rows, :] = (_dot(kvn, wk_ref[...]) + jnp.concatenate([kr] * N_HEADS, axis=1)).astype(BF16)
            v_ref[rows, :] = (_dot(kvn, wv_ref[...]) + ones).astype(BF16)
            return carry

        lax.fori_loop(0, t // MLA_KV_ROWS, build, 0)

    qrows = pl.ds(pl.multiple_of(i * tq, tq), tq)
    qn = _rms(cq_ref[...], qn_ref[...]).astype(BF16)
    cm4 = jnp.concatenate([cm_ref[qrows, :]] * N_HEADS, axis=1)
    sm4 = jnp.concatenate([sm_ref[qrows, :]] * N_HEADS, axis=1)
    scale = (MLA_NOPE + MLA_ROPE) ** -0.5
    q = ((_dot(qn, wq1_ref[...]) * cm4 + _dot(qn, wq2_ref[...]) * sm4) * scale).astype(BF16)

    lane = lax.broadcasted_iota(jnp.int32, (tq, 128), 1)
    qpos = i * tq + lax.broadcasted_iota(jnp.int32, (tq, tk), 0)
    kloc = lax.broadcasted_iota(jnp.int32, (tq, tk), 1)
    nfull = (i * tq) // tk
    heads = range(N_HEADS)
    qs = [q[:, h * 128:(h + 1) * 128] for h in heads]

    def step(j, carry, masked):
        ks = pl.multiple_of(j * tk, tk)
        ss = [_dot_nt(qs[h], k_ref[pl.ds(ks, tk), h * 128:(h + 1) * 128]) for h in heads]
        if masked:
            mask = ks + kloc <= qpos
            ss = [jnp.where(mask, s, NEG_INF) for s in ss]
        mns = [jnp.maximum(carry[h][0], jnp.max(ss[h], axis=-1, keepdims=True)) for h in heads]
        ps = [jnp.exp((ss[h] - mns[h]).astype(BF16)) for h in heads]
        als = [jnp.exp(carry[h][0] - mns[h]) for h in heads]
        pvs = [_dot(ps[h], v_ref[pl.ds(ks, tk), h * 128:(h + 1) * 128]) for h in heads]
        return tuple((mns[h], carry[h][1] * als[h] + pvs[h]) for h in heads)

    init = (jnp.full((tq, 1), NEG_INF, F32), jnp.zeros((tq, 128), F32))
    res = lax.fori_loop(0, nfull, functools.partial(step, masked=False), (init,) * N_HEADS)
    res = step(nfull, res, True)
    for p in range(N_PAIRS):
        a0, a1 = res[2 * p][1], res[2 * p + 1][1]
        acc = jnp.where(lane < HEAD_DIM, a0, a1)
        den = jnp.where(lane < HEAD_DIM, pltpu.roll(a0, HEAD_DIM, 1), pltpu.roll(a1, HEAD_DIM, 1))
        o_ref[:, p * 128:(p + 1) * 128] = acc / den


def mla_attention(cols3, cm3, sm3, q_norm, kv_norm, wq1, wq2, wk, wv, tq=256, tk=512):
    b, t, _ = cols3.shape
    assert tk % tq == 0 and t % tk == 0 and t % MLA_KV_ROWS == 0
    w = N_HEADS * 128
    b256 = C_MLA // 256
    b128 = C_MLA // 128
    seq = lambda j: pl.BlockSpec((None, t, 128), lambda b_, i: (b_, 0, j))
    full = lambda shape: pl.BlockSpec(shape, lambda b_, i: (0, 0))
    return pl.pallas_call(
        functools.partial(_mla_kernel, tq=tq, tk=tk),
        grid=(b, t // tq),
        in_specs=[pl.BlockSpec((None, tq, 256), lambda b_, i: (b_, i, b256)),
                  seq(b128 + 2), seq(b128 + 3), seq(b128 + 4), seq(0), seq(0),
                  full((1, 256)), full((1, 128)), full((256, w)), full((256, w)), full((128, w)), full((128, w))],
        out_specs=pl.BlockSpec((None, tq, D_GROUP), lambda b_, i: (b_, i, 0)),
        out_shape=jax.ShapeDtypeStruct((b, t, D_GROUP), F32),
        scratch_shapes=[pltpu.VMEM((t, w), BF16), pltpu.VMEM((t, w), BF16)],
        compiler_params=_cparams(("parallel", "arbitrary")),
        name="mla_attention",
    )(cols3, cols3, cols3, cols3, cm3, sm3, q_norm.reshape(1, 256), kv_norm.reshape(1, 128), wq1, wq2, wk, wv)


DILATED_PATTERNS = ((128, 1), (512, 4), (2048, 16))
DIL_BLOCK = 128
DIL_UNROLL = 4


def _rotate_half_heads(x):
    lane = lax.broadcasted_iota(jnp.int32, x.shape, 1)
    width = x.shape[1]
    half = HEAD_DIM // 2
    return jnp.where((lane & (HEAD_DIM - 1)) < half, -pltpu.roll(x, width - half, 1), pltpu.roll(x, half, 1))


def _dil_block(qb, kb, vb, mask, out):
    nq = qb.shape[0]
    lane = lax.broadcasted_iota(jnp.int32, (nq, 128), 1)
    kbb = kb.astype(BF16)
    vbb = vb.astype(BF16)
    hms = [(lane >> 6) == h for h in range(2)]
    ss = [jnp.where(mask, _dot_nt(jnp.where(hm, qb, 0.0).astype(BF16), kbb), NEG_INF) for hm in hms]
    yield
    ms = [jnp.max(s, axis=-1, keepdims=True) for s in ss]
    ps = [jnp.exp(s - m) for s, m in zip(ss, ms)]
    yield
    ls = [jnp.sum(p, axis=-1, keepdims=True) for p in ps]
    ohs = [_dot(p.astype(BF16), vbb) for p in ps]
    yield
    out[0] = jnp.where(hms[0], ohs[0] / ls[0], ohs[1] / ls[1])
    out[1] = jnp.where(hms[0], ms[0] + jnp.log(ls[0]), ms[1] + jnp.log(ls[1]))


def _dil_kernel(qraw_ref, kraw_ref, v_ref, cd_ref, sd_ref, o_ref, q_ref, k_ref, m_ref, z_ref, acc_ref, *, t):
    blk = DIL_BLOCK
    cd = cd_ref[...]
    sd = sd_ref[...]
    qraw = qraw_ref[...]
    kraw = kraw_ref[...]
    q_ref[...] = (qraw * cd + _rotate_half_heads(qraw) * sd) * (HEAD_DIM ** -0.5)
    k_ref[...] = kraw * cd + _rotate_half_heads(kraw) * sd
    for pi, (window, dil) in enumerate(DILATED_PATTERNS):
        assert window // dil == blk
        length = t // dil
        nb = length // blk
        nk = 2 * blk if nb > 1 else blk
        uq = lax.broadcasted_iota(jnp.int32, (blk, nk), 0)
        uk = lax.broadcasted_iota(jnp.int32, (blk, nk), 1)

        def one(it, dil=dil, nb=nb, nk=nk, uq=uq, uk=uk, first=(pi == 0)):
            r = it // nb
            jb = it % nb
            kb0 = jnp.maximum(jb - 1, 0) * blk
            qs = r + dil * blk * jb
            ks = r + dil * kb0
            if dil == 1:
                qidx = pl.ds(pl.multiple_of(qs, blk), blk)
                kidx = pl.ds(pl.multiple_of(ks, blk), nk)
            else:
                qidx = pl.ds(qs, blk, stride=dil)
                kidx = pl.ds(ks, nk, stride=dil)
            dist = (jb * blk + uq) - (kb0 + uk)
            mask = (dist >= 0) & (dist <= blk)
            out = [None, None]
            yield from _dil_block(q_ref[qidx, :], k_ref[kidx, :], v_ref[kidx, :], mask, out)
            o, lse = out
            if first:
                m_ref[qidx, :] = lse
                z_ref[qidx, :] = jnp.ones_like(lse)
                acc_ref[qidx, :] = o
            else:
                m_old = m_ref[qidx, :]
                m_new = jnp.maximum(m_old, lse)
                a = jnp.exp(m_old - m_new)
                b = jnp.exp(lse - m_new)
                m_ref[qidx, :] = m_new
                z_ref[qidx, :] = z_ref[qidx, :] * a + b
                acc_ref[qidx, :] = acc_ref[qidx, :] * a + o * b

        def body(it, carry, one=one):
            _round_robin([one(it * DIL_UNROLL + u) for u in range(DIL_UNROLL)])
            return carry

        assert (dil * nb) % DIL_UNROLL == 0
        lax.fori_loop(0, dil * nb // DIL_UNROLL, body, 0)
    o_ref[...] = acc_ref[...] / z_ref[...]


def dilated_attention(cols3, cd3, sd3):
    b, t, _ = cols3.shape
    assert t % (DIL_BLOCK * DILATED_PATTERNS[-1][1]) == 0
    c0 = C_DIL // 128
    col = lambda j: pl.BlockSpec((None, t, 128), lambda i, p: (i, 0, j + p))
    tab = pl.BlockSpec((None, t, 128), lambda i, p: (i, 0, 0))
    return pl.pallas_call(
        functools.partial(_dil_kernel, t=t),
        grid=(b, 2),
        in_specs=[col(c0), col(c0 + 2), col(c0 + 4), tab, tab],
        out_specs=pl.BlockSpec((None, t, 128), lambda i, p: (i, 0, p)),
        out_shape=jax.ShapeDtypeStruct((b, t, 256), F32),
        scratch_shapes=[pltpu.VMEM((t, 128), F32)] * 5,
        compiler_params=_cparams(("parallel", "parallel")),
        name="dilated_attention",
    )(cols3, cols3, cols3, cd3, sd3)


CHUNK = 64
PAIR = 2
PAIR_LANES = PAIR * HEAD_DIM
N_PAIRS = N_HEADS // PAIR
STACK = PAIR * CHUNK


def _stack_heads(x):
    lane = lax.broadcasted_iota(jnp.int32, x.shape, 1)
    return jnp.concatenate([jnp.where((lane >> 6) == h, x, 0.0) for h in range(PAIR)], axis=0)


def _unstack_heads(y):
    out = y[0:CHUNK]
    for h in range(1, PAIR):
        out = out + y[h * CHUNK:(h + 1) * CHUNK]
    return out


def _round_robin(stages):
    live = list(stages)
    while live:
        nxt = []
        for g in live:
            try:
                next(g)
                nxt.append(g)
            except StopIteration:
                pass
        live = nxt


def _head_mean_matrix(n=PAIR_LANES):
    r = lax.broadcasted_iota(jnp.int32, (n, n), 0)
    c = lax.broadcasted_iota(jnp.int32, (n, n), 1)
    return jnp.where((r >> 6) == (c >> 6), 1.0 / HEAD_DIM, 0.0).astype(BF16)


HGRN_HALVES = (32, 16, 8, 4, 2, 1)


def _hgrn_sum_matrix():
    t = lax.broadcasted_iota(jnp.int32, (CHUNK, CHUNK), 0)
    j = lax.broadcasted_iota(jnp.int32, (CHUNK, CHUNK), 1)
    tri = jnp.where(j <= t, 1.0, 0.0)
    blocks = [tri]
    for half in HGRN_HALVES:
        ref = (t // (2 * half)) * (2 * half) + half - 1
        blocks.append(tri - jnp.where(j <= ref, 1.0, 0.0))
    return jnp.concatenate(blocks, axis=0).astype(BF16)


def _hgrn_kernel(q_ref, f_ref, i_ref, g_ref, lbl_ref, gn_ref, o_ref, st_ref, *, layer, nchunk, nb):
    @pl.when(pl.program_id(1) == 0)
    def _():
        st_ref[...] = jnp.zeros_like(st_ref)

    logits = lbl_ref[...]
    e = jnp.exp(logits - jnp.max(logits, axis=0, keepdims=True))
    sm = e / jnp.sum(e, axis=0, keepdims=True)
    lb = jnp.zeros((1, 256), F32)
    for l in range(1, layer + 1):
        lb = lb + sm[l:l + 1, :]
    gn = gn_ref[...]
    summat = _hgrn_sum_matrix()
    hmean = _head_mean_matrix()
    r = lax.broadcasted_iota(jnp.int32, (STACK, STACK), 0)
    c = lax.broadcasted_iota(jnp.int32, (STACK, STACK), 1)
    same_head = (r >> 6) == (c >> 6)
    rc = r & (CHUNK - 1)
    cc = c & (CHUNK - 1)

    def one(bi, pi, rows):
        ls = pl.ds(pi * PAIR_LANES, PAIR_LANES)
        lbp = lb[:, pi * PAIR_LANES:(pi + 1) * PAIR_LANES]
        qv = q_ref[bi, rows, ls]
        qq = qv * jax.nn.sigmoid(qv)
        forget = lbp + (1.0 - lbp) * jax.nn.sigmoid(f_ref[bi, rows, ls])
        logf = jnp.log(forget)
        kk = 1.0 - forget
        v = i_ref[bi, rows, ls]
        yield
        sums = _split_dot_lhs(summat, logf)
        b = sums[0:CHUNK]
        a = jnp.where(rc == cc, _dot_nt(_stack_heads(qq).astype(BF16), _stack_heads(kk).astype(BF16)), 0.0)
        yield
        for li, half in enumerate(HGRN_HALVES):
            d = sums[(li + 1) * CHUNK:(li + 2) * CHUNK]
            ql = qq * jnp.exp(jnp.minimum(d, 0.0))
            kl = kk * jnp.exp(jnp.minimum(-d, 0.0))
            m = ((rc // (2 * half)) == (cc // (2 * half))) & ((rc % (2 * half)) >= half) & ((cc % (2 * half)) < half)
            a = a + jnp.where(m, _dot_nt(_stack_heads(ql).astype(BF16), _stack_heads(kl).astype(BF16)), 0.0)
            yield
        o = _unstack_heads(_dot(a.astype(BF16), _stack_heads(v).astype(BF16)))
        st = st_ref[bi, pi]
        o = o + _dot_nt((qq * jnp.exp(b)).astype(BF16), st.astype(BF16))
        yield
        b_end = b[CHUNK - 1:CHUNK, :]
        kbar = kk * jnp.exp(b_end - b)
        upd = lax.dot_general(v.astype(BF16), kbar.astype(BF16), (((0,), (0,)), ((), ())), preferred_element_type=F32)
        st_ref[bi, pi] = st * jnp.exp(b_end) + jnp.where(same_head, upd, 0.0)
        ms = _split_dot(o * o, hmean)
        yield
        gv = g_ref[bi, rows, ls]
        gnp = gn[:, pi * PAIR_LANES:(pi + 1) * PAIR_LANES]
        o_ref[bi, rows, ls] = o * lax.rsqrt(ms + RMS_EPS) * gnp * (gv * jax.nn.sigmoid(gv))

    def chunk(ci, carry):
        rows = pl.ds(pl.multiple_of(ci * CHUNK, CHUNK), CHUNK)
        _round_robin([one(bi, pi, rows) for bi in range(nb) for pi in range(N_PAIRS)])
        return carry

    lax.fori_loop(0, nchunk, chunk, 0)


SEQ_PER_STEP = 4


def hgrn(cols3, lb_logits, g_norm, layer, tb=512):
    b, t, _ = cols3.shape
    nb = SEQ_PER_STEP if b % SEQ_PER_STEP == 0 else 1
    c0 = C_HGRN // 256
    spec = lambda j: pl.BlockSpec((nb, tb, 256), lambda b_, i: (b_, i, j))
    depth = lb_logits.shape[0]
    return pl.pallas_call(
        functools.partial(_hgrn_kernel, layer=layer, nchunk=tb // CHUNK, nb=nb),
        grid=(b // nb, t // tb),
        in_specs=[spec(c0), spec(c0 + 1), spec(c0 + 2), spec(c0 + 3),
                  pl.BlockSpec((depth, 256), lambda b_, i: (0, 0)),
                  pl.BlockSpec((1, 256), lambda b_, i: (0, 0))],
        out_specs=spec(0),
        out_shape=jax.ShapeDtypeStruct((b, t, 256), F32),
        scratch_shapes=[pltpu.VMEM((nb, N_PAIRS, STACK, STACK), F32)],
        compiler_params=_cparams(("parallel", "arbitrary")),
        name="hgrn",
    )(cols3, cols3, cols3, cols3, lb_logits, g_norm.reshape(1, 256))


def _head_sum_matrix():
    r = lax.broadcasted_iota(jnp.int32, (256, 256), 0)
    c = lax.broadcasted_iota(jnp.int32, (256, 256), 1)
    return jnp.where((r >> 6) == (c >> 6), 1.0, 0.0).astype(BF16)


def _rwkv_prep_tile(a, last_prev, mu, w0, w2, a0, a2, g2, k_k, k_a, r_k):
    row = lax.broadcasted_iota(jnp.int32, a.shape, 0)
    prev = jnp.where(row == 0, last_prev, pltpu.roll(a, 1, 0))
    xs = a + (prev - a) * mu
    r = xs[:, 0:256]
    k = xs[:, 256:512]
    v = xs[:, 512:768]
    xwa = xs[:, 768:896]
    xg = xs[:, 896:1024]
    z = -(w0 + _dot(jnp.tanh(xwa).astype(BF16), w2))
    softplus = jnp.maximum(z, 0.0) + jnp.log(1.0 + jnp.exp(-jnp.abs(z)))
    lw = -jnp.exp(-softplus - 0.5)
    lr = jax.nn.sigmoid(a0 + _dot(xwa.astype(BF16), a2))
    g = _dot(jax.nn.sigmoid(xg).astype(BF16), g2)
    hsum = _head_sum_matrix()
    kk = k * k_k
    kk = kk / jnp.maximum(jnp.sqrt(_split_dot(kk * kk, hsum)), 1e-12)
    k2 = k * (1.0 + (lr - 1.0) * k_a)
    bonus = _split_dot(r * k2 * r_k, hsum) * v
    return r, lw, k2, v, kk, kk * lr, bonus, g


RWKV_MERGE_HALVES = (2, 4, 8, 16, 32)


def _compact_upper(x, half):
    return jnp.concatenate([x[s + half:s + 2 * half] for s in range(0, STACK, 2 * half)], axis=0)


def _expand_upper(c, half):
    zero = jnp.zeros((half, c.shape[1]), c.dtype)
    return jnp.concatenate([p for i in range(STACK // (2 * half)) for p in (zero, c[i * half:(i + 1) * half])], axis=0)


def _rwkv_kernel(c_ref, p_ref, mu_ref, w0_ref, w2_ref, a0_ref, a2_ref, g2_ref, kk_ref, ka_ref, rk_ref,
                 lnw_ref, lnb_ref, o_ref,
                 r_ref, lw_ref, k_ref, v_ref, kap_ref, at_ref, bonus_ref, g_ref, ht_ref, *, nchunk, nb):
    first = pl.program_id(1) == 0

    @pl.when(first)
    def _():
        ht_ref[...] = jnp.zeros_like(ht_ref)

    for bi in range(nb):
        outs = _rwkv_prep_tile(c_ref[bi], jnp.where(first, 0.0, p_ref[bi, 7:8, :]), mu_ref[...], w0_ref[...],
                               w2_ref[...], a0_ref[...], a2_ref[...], g2_ref[...], kk_ref[...], ka_ref[...],
                               rk_ref[...])
        for ref, val in zip((r_ref, lw_ref, k_ref, v_ref, kap_ref, at_ref, bonus_ref, g_ref), outs):
            ref[bi] = val

    t = lax.broadcasted_iota(jnp.int32, (CHUNK, CHUNK), 0)
    j = lax.broadcasted_iota(jnp.int32, (CHUNK, CHUNK), 1)
    tri = jnp.where(j <= t, 1.0, 0.0).astype(BF16)
    hmean = _head_mean_matrix()
    rr = lax.broadcasted_iota(jnp.int32, (STACK, STACK), 0)
    cc = lax.broadcasted_iota(jnp.int32, (STACK, STACK), 1)
    same_head = (rr >> 6) == (cc >> 6)
    rc = rr & (CHUNK - 1)
    sc = cc & (CHUNK - 1)
    eye = jnp.where(rr == cc, 1.0, 0.0)
    lnw = lnw_ref[...]
    lnb = lnb_ref[...]

    def one(bi, pi, rows):
        ls = pl.ds(pi * PAIR_LANES, PAIR_LANES)
        lw = lw_ref[bi, rows, ls]
        lc = _split_dot_lhs(tri, lw)
        l_end = lc[CHUNK - 1:CHUNK, :]
        kap = kap_ref[bi, rows, ls]
        at = at_ref[bi, rows, ls]
        k2 = k_ref[bi, rows, ls]
        v = v_ref[bi, rows, ls]
        inv = jnp.exp(-lc)
        fwd = jnp.exp(l_end - lc)
        s_kap = _stack_heads(kap * jnp.exp(lc - lw)).astype(BF16)
        s_r32 = _stack_heads(r_ref[bi, rows, ls] * jnp.exp(lc))
        s_r = s_r32.astype(BF16)
        s_a = _stack_heads(at * inv).astype(BF16)
        s_k = _stack_heads(k2 * inv).astype(BF16)
        s_v = _stack_heads(v).astype(BF16)
        strict = rc > sc
        incl = rc >= sc
        yield
        a_ka = jnp.where(strict, _dot_nt(s_kap, s_a), 0.0)
        a_kk = jnp.where(strict, _dot_nt(s_kap, s_k), 0.0).astype(BF16)
        m1 = ((rc >> 1) == (sc >> 1)) & strict
        tinv = eye - jnp.where(m1, a_ka, 0.0)
        yield
        a_ra = jnp.where(incl, _dot_nt(s_r, s_a), 0.0).astype(BF16)
        a_rk = jnp.where(incl, _dot_nt(s_r, s_k), 0.0).astype(BF16)
        for half in RWKV_MERGE_HALVES:
            m = ((rc // (2 * half)) == (sc // (2 * half))) & ((rc % (2 * half)) >= half) & ((sc % (2 * half)) < half)
            off = jnp.where(m, a_ka, 0.0)
            tb = tinv.astype(BF16)
            if half % 8 == 0:
                x = _expand_upper(_dot(_compact_upper(off, half).astype(BF16), tb), half).astype(BF16)
                yield
                tinv = tinv - _expand_upper(_dot(_compact_upper(tinv, half).astype(BF16), x), half)
            else:
                x = _dot(off.astype(BF16), tb).astype(BF16)
                yield
                tinv = tinv - _dot(tb, x)
            yield
        tb = tinv.astype(BF16)
        kt_st = _dot(tb, s_kap)
        w_st = _dot(a_kk, s_v).astype(BF16)
        yield
        vt_st = _dot(tb, w_st)
        rt = _unstack_heads(s_r32 - _dot(a_ra, kt_st.astype(BF16)))
        yield
        y0 = _unstack_heads(_dot(a_rk, s_v) - _dot(a_ra, vt_st.astype(BF16)))
        kt = _unstack_heads(kt_st)
        vt = _unstack_heads(vt_st)
        ht = ht_ref[bi, pi]
        htb = ht.astype(BF16)
        yield
        y = y0 + _dot_nt(rt.astype(BF16), htb)
        u = vt + _dot_nt(kt.astype(BF16), htb)
        yield
        lhs = jnp.concatenate([v, u], axis=0).astype(BF16)
        rhs = jnp.concatenate([k2 * fwd, -(at * fwd)], axis=0).astype(BF16)
        upd = lax.dot_general(lhs, rhs, (((0,), (0,)), ((), ())), preferred_element_type=F32)
        ht_ref[bi, pi] = ht * jnp.exp(l_end) + jnp.where(same_head, upd, 0.0)
        mean = _split_dot(y, hmean)
        yield
        yc = y - mean
        var = _split_dot(yc * yc, hmean)
        yield
        lanes = slice(pi * PAIR_LANES, (pi + 1) * PAIR_LANES)
        yn = yc * lax.rsqrt(var + RWKV_LN_EPS) * lnw[:, lanes] + lnb[:, lanes]
        o_ref[bi, rows, ls] = (yn + bonus_ref[bi, rows, ls]) * g_ref[bi, rows, ls]

    def chunk(ci, carry):
        rows = pl.ds(pl.multiple_of(ci * CHUNK, CHUNK), CHUNK)
        _round_robin([one(bi, pi, rows) for bi in range(nb) for pi in range(N_PAIRS)])
        return carry

    lax.fori_loop(0, nchunk, chunk, 0)


def rwkv_mix(cols3, mu, w0, w2p, a0, a2p, g2, k_k, k_a, r_k, ln_w, ln_b, tb=512):
    b, t, _ = cols3.shape
    nb = SEQ_PER_STEP if b % SEQ_PER_STEP == 0 else 1
    spec = pl.BlockSpec((nb, tb, 256), lambda b_, i: (b_, i, 0))
    full = lambda shape: pl.BlockSpec(shape, lambda b_, i: (0, 0))
    row = lambda x: x.reshape(1, -1)
    return pl.pallas_call(
        functools.partial(_rwkv_kernel, nchunk=tb // CHUNK, nb=nb),
        grid=(b // nb, t // tb),
        in_specs=[pl.BlockSpec((nb, tb, 1024), lambda b_, i: (b_, i, C_RWKV // 1024)),
                  pl.BlockSpec((nb, 8, 1024), lambda b_, i: (b_, jnp.maximum(i * (tb // 8) - 1, 0), C_RWKV // 1024)),
                  full((1, 1024)), full((1, 256)), full((128, 256)), full((1, 256)), full((128, 256)),
                  full((128, 256)), full((1, 256)), full((1, 256)), full((1, 256)), full((1, 256)), full((1, 256))],
        out_specs=spec,
        out_shape=jax.ShapeDtypeStruct((b, t, 256), F32),
        scratch_shapes=[pltpu.VMEM((nb, tb, 256), F32)] * 8 + [pltpu.VMEM((nb, N_PAIRS, STACK, STACK), F32)],
        compiler_params=_cparams(("parallel", "arbitrary")),
        name="rwkv_mix",
    )(cols3, cols3, row(mu), row(w0), w2p, row(a0), a2p, g2, row(k_k), row(k_a), row(r_k), row(ln_w), row(ln_b))


def _rot_cols(w, half):
    g = w.reshape(w.shape[0], -1, 2, half)
    return jnp.concatenate([-g[:, :, 1:2, :], g[:, :, 0:1, :]], axis=2).reshape(w.shape)


def pack_w_in(w):
    d = w.shape[0]
    rw = w[:, 0:1024]
    dq, dk, dv = w[:, 1024:1280], w[:, 1280:1536], w[:, 1536:1792]
    cq, ckv, kr = w[:, 1792:2048], w[:, 2048:2176], w[:, 2176:2208]
    hg = w[:, 2208:3232]
    z = lambda n: jnp.zeros((d, n), w.dtype)
    kra = jnp.concatenate([z(MLA_NOPE), kr, z(128 - MLA_NOPE - MLA_ROPE)], axis=1)
    krb = jnp.concatenate([z(MLA_NOPE), _rot_cols(kr, MLA_ROPE // 2), z(128 - MLA_NOPE - MLA_ROPE)], axis=1)
    packed = jnp.concatenate([rw, dq, dk, dv, hg, cq, ckv, kra, krb], axis=1)
    return packed.astype(BF16)


def pack_mla(w_uq, w_ukv):
    rq = w_uq.shape[0]
    q = w_uq.reshape(rq, N_HEADS, MLA_NOPE + MLA_ROPE)
    nope, rope = q[..., :MLA_NOPE], q[..., MLA_NOPE:]
    pad = jnp.zeros((rq, N_HEADS, 128 - MLA_NOPE - MLA_ROPE), w_uq.dtype)
    wq1 = jnp.concatenate([nope, rope, pad], axis=-1).reshape(rq, N_HEADS * 128)
    rrot = _rot_cols(rope.reshape(rq, N_HEADS * MLA_ROPE), MLA_ROPE // 2).reshape(rq, N_HEADS, MLA_ROPE)
    wq2 = jnp.concatenate([jnp.zeros_like(nope), rrot, pad], axis=-1).reshape(rq, N_HEADS * 128)
    rk = w_ukv.shape[0]
    kv = w_ukv.reshape(rk, N_HEADS, MLA_NOPE + HEAD_DIM)
    wk = jnp.concatenate([kv[..., :MLA_NOPE], jnp.zeros((rk, N_HEADS, 128 - MLA_NOPE), w_ukv.dtype)],
                         axis=-1).reshape(rk, N_HEADS * 128)
    vz = jnp.zeros((rk, HEAD_DIM), w_ukv.dtype)
    wv = jnp.concatenate([x for h in range(N_HEADS)
                          for x in ((kv[:, h, MLA_NOPE:], vz) if h % 2 == 0 else (vz, kv[:, h, MLA_NOPE:]))], axis=1)
    return wq1.astype(BF16), wq2.astype(BF16), wk.astype(BF16), wv.astype(BF16)


def kernel(x, positions, mix_norm, w_in, w_out, rwkv_mu, rwkv_w0, rwkv_w2, rwkv_a0, rwkv_a2, rwkv_g2, rwkv_k_k,
           rwkv_k_a, rwkv_r_k, rwkv_ln_w, rwkv_ln_b, mla_q_norm, mla_w_uq, mla_kv_norm, mla_w_ukv, hgrn_lb_logits,
           hgrn_g_norm, ffn_norm, ffn_w1, ffn_w3, ffn_w2, moe_router, moe_w1, moe_w3, moe_w2, final_norm):
    b, t, d = x.shape
    n = b * t
    depth = w_in.shape[0]
    x2 = x.reshape(n, d)
    cd, sd, cm, sm = rope_tables(positions.reshape(n, 1))
    lora_pad = jnp.zeros((rwkv_w2.shape[1], D_GROUP), F32)
    fused_final = False
    for layer in range(depth):
        cols = in_proj(x2, mix_norm[layer], pack_w_in(w_in[layer]))
        cols3 = cols.reshape(b, t, N_PACK)
        y_a = rwkv_mix(cols3, rwkv_mu[layer], rwkv_w0[layer],
                       jnp.concatenate([rwkv_w2[layer], lora_pad], axis=0).astype(BF16), rwkv_a0[layer],
                       jnp.concatenate([lora_pad, rwkv_a2[layer]], axis=0).astype(BF16),
                       rwkv_g2[layer].astype(BF16), rwkv_k_k[layer], rwkv_k_a[layer], rwkv_r_k[layer],
                       rwkv_ln_w[layer], rwkv_ln_b[layer])
        y_b = dilated_attention(cols3, cd.reshape(b, t, -1), sd.reshape(b, t, -1))
        y_c = mla_attention(cols3, cm.reshape(b, t, -1), sm.reshape(b, t, -1), mla_q_norm[layer], mla_kv_norm[layer],
                            *pack_mla(mla_w_uq[layer], mla_w_ukv[layer]))
        y_d = hgrn(cols3, hgrn_lb_logits, hgrn_g_norm[layer], layer)
        ys = [y.reshape(n, D_GROUP) for y in (y_a, y_b, y_c, y_d)]
        j = layer // 2
        if layer % 2 == 0:
            x2, h = out_proj(x2, ys, w_out[layer].astype(BF16), ffn_norm[layer])
            x2 = ffn_dense(h, x2, ffn_w1[j].astype(BF16), ffn_w3[j].astype(BF16), ffn_w2[j].astype(BF16))
        else:
            x2, route, hp = out_proj_route(x2, ys, w_out[layer].astype(BF16), ffn_norm[layer],
                                           jnp.pad(moe_router[j], ((0, 0), (0, 128 - N_EXPERTS))))
            rows_pad = TOP_K * n + N_EXPERTS * MOE_TILE
            pos, tile_expert, tile_valid = moe_plan(route, rows_pad)
            xs = moe_scatter(pos, hp, rows_pad)
            ys = moe_experts(tile_expert, tile_valid, xs, moe_w1[j].astype(BF16), moe_w3[j].astype(BF16),
                             moe_w2[j].astype(BF16))
            fused_final = layer == depth - 1
            x2 = moe_combine(pos, x2, route, ys, final_norm, fused_final)
    if not fused_final:
        x2 = final_rms(x2, final_norm)
    return x2.reshape(b, t, d)
```

```python
import functools

import jax
import jax.numpy as jnp
from jax import lax
from jax.experimental import pallas as pl
from jax.experimental.pallas import tpu as pltpu

F32 = jnp.float32
BF16 = jnp.bfloat16

D_GROUP = 256
HEAD_DIM = 64
N_HEADS = 4
RMS_EPS = 1e-6
RWKV_LN_EPS = 64e-5
ROPE_THETA = 10000.0
NEG_INF = -1e30
N_EXPERTS = 8
MLA_ROPE = 32
MLA_NOPE = 64

VMEM_LIMIT = 56 * 1024 * 1024

C_RWKV = 0
C_DIL = 1024
C_HGRN = 1792
C_MLA = 2816
N_PACK = 3456


def _cparams(sem, vmem=VMEM_LIMIT):
    return pltpu.CompilerParams(dimension_semantics=sem, vmem_limit_bytes=vmem)


def _rms(x, w):
    return x * lax.rsqrt(jnp.mean(x * x, axis=-1, keepdims=True) + RMS_EPS) * w


def _dot(a, b):
    return jnp.dot(a, b, preferred_element_type=F32)


def _dot_nt(a, b):
    return lax.dot_general(a, b, (((1,), (1,)), ((), ())), preferred_element_type=F32)


def _split_dot(a_f32, b_bf16):
    hi = a_f32.astype(BF16)
    lo = (a_f32 - hi.astype(F32)).astype(BF16)
    return _dot(hi, b_bf16) + _dot(lo, b_bf16)


def _split_dot_lhs(a_bf16, b_f32):
    hi = b_f32.astype(BF16)
    lo = (b_f32 - hi.astype(F32)).astype(BF16)
    return _dot(a_bf16, hi) + _dot(a_bf16, lo)


def _rope_kernel(pos_ref, cd_ref, sd_ref, cm_ref, sm_ref):
    pos = pos_ref[...].astype(F32)
    tm = pos.shape[0]
    lane = lax.broadcasted_iota(jnp.int32, (tm, 128), 1)
    j = (lane & 31).astype(F32)
    inv = jnp.exp(j * (-2.0 / HEAD_DIM * jnp.log(ROPE_THETA)))
    ang = pos * inv
    cd_ref[...] = jnp.cos(ang)
    sd_ref[...] = jnp.sin(ang)
    jm = (lane & 15).astype(F32)
    invm = jnp.exp(jm * (-2.0 / MLA_ROPE * jnp.log(ROPE_THETA)))
    angm = pos * invm
    rope_lane = (lane >= MLA_NOPE) & (lane < MLA_NOPE + MLA_ROPE)
    cm_ref[...] = jnp.where(rope_lane, jnp.cos(angm), jnp.where(lane < MLA_NOPE, 1.0, 0.0))
    sm_ref[...] = jnp.where(rope_lane, jnp.sin(angm), 0.0)


def rope_tables(pos_col, tm=512):
    n = pos_col.shape[0]
    return pl.pallas_call(
        _rope_kernel,
        grid=(n // tm,),
        in_specs=[pl.BlockSpec((tm, 1), lambda i: (i, 0))],
        out_specs=[pl.BlockSpec((tm, 128), lambda i: (i, 0))] * 4,
        out_shape=[jax.ShapeDtypeStruct((n, 128), F32)] * 4,
        compiler_params=_cparams(("parallel",)),
        name="rope_tables",
    )(pos_col)


def _in_proj_kernel(x_ref, nw_ref, w_ref, o_ref):
    h = _rms(x_ref[...], nw_ref[...]).astype(BF16)
    o_ref[...] = _dot(h, w_ref[...])


def in_proj(x2, norm_w, w_packed, tm=512):
    n, d = x2.shape
    nc = w_packed.shape[1]
    return pl.pallas_call(
        _in_proj_kernel,
        grid=(n // tm,),
        in_specs=[pl.BlockSpec((tm, d), lambda i: (i, 0)),
                  pl.BlockSpec((1, d), lambda i: (0, 0)),
                  pl.BlockSpec((d, nc), lambda i: (0, 0))],
        out_specs=pl.BlockSpec((tm, nc), lambda i: (i, 0)),
        out_shape=jax.ShapeDtypeStruct((n, nc), F32),
        compiler_params=_cparams(("parallel",)),
        name="in_proj",
    )(x2, norm_w.reshape(1, d), w_packed)


def _out_proj_kernel(x_ref, ya_ref, yb_ref, yc_ref, yd_ref, w_ref, nw_ref, xo_ref, h_ref):
    acc = x_ref[...]
    for g, y_ref in enumerate((ya_ref, yb_ref, yc_ref, yd_ref)):
        acc = acc + _dot(y_ref[...].astype(BF16), w_ref[g * D_GROUP:(g + 1) * D_GROUP, :])
    xo_ref[...] = acc
    h_ref[...] = _rms(acc, nw_ref[...]).astype(BF16)


def out_proj(x2, ys, w_out_bf16, norm_w, tm=512):
    n, d = x2.shape
    yspec = pl.BlockSpec((tm, D_GROUP), lambda i: (i, 0))
    return pl.pallas_call(
        _out_proj_kernel,
        grid=(n // tm,),
        in_specs=[pl.BlockSpec((tm, d), lambda i: (i, 0)), yspec, yspec, yspec, yspec,
                  pl.BlockSpec((d, d), lambda i: (0, 0)),
                  pl.BlockSpec((1, d), lambda i: (0, 0))],
        out_specs=[pl.BlockSpec((tm, d), lambda i: (i, 0)), pl.BlockSpec((tm, d), lambda i: (i, 0))],
        out_shape=[jax.ShapeDtypeStruct((n, d), F32), jax.ShapeDtypeStruct((n, d), BF16)],
        compiler_params=_cparams(("parallel",)),
        name="out_proj",
    )(x2, *ys, w_out_bf16, norm_w.reshape(1, d))


def _ffn_kernel(h_ref, x_ref, w1_ref, w3_ref, w2_ref, o_ref):
    h = h_ref[...]
    a = _dot(h, w1_ref[...])
    b = _dot(h, w3_ref[...])
    act = (a * jax.nn.sigmoid(a) * b).astype(BF16)
    o_ref[...] = x_ref[...] + _dot(act, w2_ref[...])


def ffn_dense(h_bf16, x2, w1, w3, w2, tm=512):
    n, d = x2.shape
    f = w1.shape[1]
    resident = lambda shape: pl.BlockSpec(shape, lambda i: (0, 0), pipeline_mode=pl.Buffered(1))
    return pl.pallas_call(
        _ffn_kernel,
        grid=(n // tm,),
        in_specs=[pl.BlockSpec((tm, d), lambda i: (i, 0)),
                  pl.BlockSpec((tm, d), lambda i: (i, 0)),
                  resident((d, f)), resident((d, f)), resident((f, d))],
        out_specs=pl.BlockSpec((tm, d), lambda i: (i, 0)),
        out_shape=jax.ShapeDtypeStruct((n, d), F32),
        compiler_params=_cparams(("parallel",)),
        name="ffn_dense",
    )(h_bf16, x2, w1, w3, w2)


MOE_TILE = 512
TOP_K = 2
MOE_DMA_TOKENS = 1024


def _router_kernel(x_ref, nw_ref, wr_ref, route_ref, hp_ref):
    h = _rms(x_ref[...], nw_ref[...])
    logits = jnp.dot(h, wr_ref[...], preferred_element_type=F32, precision=lax.Precision.HIGHEST)
    lane = lax.broadcasted_iota(jnp.int32, logits.shape, 1)
    logits = jnp.where(lane < N_EXPERTS, logits, -jnp.inf)
    m1 = jnp.max(logits, axis=-1, keepdims=True)
    i1 = jnp.min(jnp.where(logits == m1, lane, 128), axis=-1, keepdims=True)
    rest = jnp.where(lane == i1, -jnp.inf, logits)
    m2 = jnp.max(rest, axis=-1, keepdims=True)
    i2 = jnp.min(jnp.where(rest == m2, lane, 128), axis=-1, keepdims=True)
    e2 = jnp.exp(m2 - m1)
    g1 = 1.0 / (1.0 + e2)
    g2 = e2 * g1
    route_ref[...] = jnp.where(lane == 0, i1.astype(F32), jnp.where(lane == 1, i2.astype(F32),
                               jnp.where(lane == 2, g1, jnp.where(lane == 3, g2, 0.0))))
    hp_ref[...] = h


def moe_route(x2, norm_w, w_router_pad, tm=512):
    n, d = x2.shape
    return pl.pallas_call(
        _router_kernel,
        grid=(n // tm,),
        in_specs=[pl.BlockSpec((tm, d), lambda i: (i, 0)),
                  pl.BlockSpec((1, d), lambda i: (0, 0)),
                  pl.BlockSpec((d, 128), lambda i: (0, 0))],
        out_specs=[pl.BlockSpec((tm, 128), lambda i: (i, 0)), pl.BlockSpec((tm, d), lambda i: (i, 0))],
        out_shape=[jax.ShapeDtypeStruct((n, 128), F32), jax.ShapeDtypeStruct((n, d), F32)],
        compiler_params=_cparams(("parallel",)),
        name="moe_router",
    )(x2, norm_w.reshape(1, d), w_router_pad)


def moe_plan(route, n_rows_pad):
    e = route[:, :TOP_K].astype(jnp.int32)
    onehot = (e.reshape(-1, 1) == jnp.arange(N_EXPERTS, dtype=jnp.int32)[None, :]).astype(jnp.int32)
    csum = jnp.cumsum(onehot, axis=0)
    counts = csum[-1]
    rank = jnp.sum((csum - onehot) * onehot, axis=1)
    gsz = ((counts + MOE_TILE - 1) // MOE_TILE) * MOE_TILE
    gend = jnp.cumsum(gsz)
    goff = gend - gsz
    pos = (goff[e.reshape(-1)] + rank).astype(jnp.int32)
    tile_start = jnp.arange(n_rows_pad // MOE_TILE, dtype=jnp.int32) * MOE_TILE
    tile_expert = jnp.minimum(jnp.sum((tile_start[:, None] >= gend[None, :]).astype(jnp.int32), axis=1),
                              N_EXPERTS - 1).astype(jnp.int32)
    tile_valid = (tile_start < gend[-1]).astype(jnp.int32)
    return pos, tile_expert, tile_valid


def _moe_scatter_kernel(pos_ref, hp_ref, xs_in_ref, xs_ref, sem):
    del xs_in_ref
    i = pl.program_id(0)
    ts = hp_ref.shape[0]

    def copy(r, slot):
        dst = pos_ref[(i * ts + r) * TOP_K + slot]
        return pltpu.make_async_copy(hp_ref.at[pl.ds(r, 1), :], xs_ref.at[pl.ds(dst, 1), :], sem)

    def issue(r, carry):
        for slot in range(TOP_K):
            copy(r, slot).start(priority=slot)
        return carry

    lax.fori_loop(0, ts, issue, 0, unroll=8)
    for slot in range(TOP_K):
        pltpu.make_async_copy(hp_ref, xs_ref.at[pl.ds(0, ts), :], sem).wait()


def moe_scatter(pos, hp, n_rows_pad, ts=MOE_DMA_TOKENS):
    n, dh = hp.shape
    xs0 = jnp.zeros((n_rows_pad, dh), hp.dtype)
    return pl.pallas_call(
        _moe_scatter_kernel,
        grid_spec=pltpu.PrefetchScalarGridSpec(
            num_scalar_prefetch=1,
            grid=(n // ts,),
            in_specs=[pl.BlockSpec((ts, dh), lambda i, pos_: (i, 0)),
                      pl.BlockSpec(memory_space=pl.ANY)],
            out_specs=pl.BlockSpec(memory_space=pl.ANY),
            scratch_shapes=[pltpu.SemaphoreType.DMA],
        ),
        out_shape=jax.ShapeDtypeStruct((n_rows_pad, dh), hp.dtype),
        input_output_aliases={2: 0},
        compiler_params=_cparams(("arbitrary",)),
        name="moe_scatter",
    )(pos, hp, xs0)


def _moe_expert_kernel(te_ref, tv_ref, xs_ref, w1_ref, w3_ref, w2_ref, ys_ref, h_ref, acc_ref):
    i = pl.program_id(0)
    j = pl.program_id(1)

    @pl.when(tv_ref[i] > 0)
    def _():
        @pl.when(j == 0)
        def _():
            h_ref[...] = xs_ref[...].astype(BF16)
            acc_ref[...] = jnp.zeros_like(acc_ref)

        h = h_ref[...]
        a = _dot(h, w1_ref[...])
        b = _dot(h, w3_ref[...])
        act = (a * jax.nn.sigmoid(a) * b).astype(BF16)
        acc_ref[...] += _dot(act, w2_ref[...])

        @pl.when(j == pl.num_programs(1) - 1)
        def _():
            ys_ref[...] = acc_ref[...]

    @pl.when((tv_ref[i] == 0) & (j == pl.num_programs(1) - 1))
    def _():
        ys_ref[...] = jnp.zeros_like(ys_ref)


def moe_experts(tile_expert, tile_valid, xs, w1, w3, w2, tf=1792):
    rows, d = xs.shape
    f = w1.shape[2]
    tm = MOE_TILE
    return pl.pallas_call(
        _moe_expert_kernel,
        grid_spec=pltpu.PrefetchScalarGridSpec(
            num_scalar_prefetch=2,
            grid=(rows // tm, f // tf),
            in_specs=[pl.BlockSpec((tm, d), lambda i, j, te, tv: (i, 0)),
                      pl.BlockSpec((None, d, tf), lambda i, j, te, tv: (te[i], 0, j * tv[i])),
                      pl.BlockSpec((None, d, tf), lambda i, j, te, tv: (te[i], 0, j * tv[i])),
                      pl.BlockSpec((None, tf, d), lambda i, j, te, tv: (te[i], j * tv[i], 0))],
            out_specs=pl.BlockSpec((tm, d), lambda i, j, te, tv: (i, 0)),
            scratch_shapes=[pltpu.VMEM((tm, d), BF16), pltpu.VMEM((tm, d), F32)],
        ),
        out_shape=jax.ShapeDtypeStruct((rows, d), F32),
        compiler_params=_cparams(("arbitrary", "arbitrary")),
        name="moe_experts",
    )(tile_expert, tile_valid, xs, w1, w3, w2)


def _moe_combine_kernel(pos_ref, x_ref, route_ref, nw_ref, ys_ref, o_ref, buf_ref, sem, *, final):
    i = pl.program_id(0)
    tc = x_ref.shape[0]

    def copy(r, slot):
        src = pos_ref[(i * tc + r) * TOP_K + slot]
        return pltpu.make_async_copy(ys_ref.at[pl.ds(src, 1), :], buf_ref.at[slot, pl.ds(r, 1), :], sem)

    def issue(r, carry):
        for slot in range(TOP_K):
            copy(r, slot).start(priority=slot)
        return carry

    lax.fori_loop(0, tc, issue, 0, unroll=8)
    for slot in range(TOP_K):
        pltpu.make_async_copy(ys_ref.at[pl.ds(0, tc), :], buf_ref.at[slot], sem).wait()
    route = route_ref[...]
    g1 = route[:, 2:3]
    g2 = route[:, 3:4]
    out = x_ref[...] + g1 * buf_ref[0] + g2 * buf_ref[1]
    if final:
        out = _rms(out, nw_ref[...])
    o_ref[...] = out


def moe_combine(pos, x2, route, ys, norm_w, final, tc=MOE_DMA_TOKENS):
    n, d = x2.shape
    return pl.pallas_call(
        functools.partial(_moe_combine_kernel, final=final),
        grid_spec=pltpu.PrefetchScalarGridSpec(
            num_scalar_prefetch=1,
            grid=(n // tc,),
            in_specs=[pl.BlockSpec((tc, d), lambda i, pos_: (i, 0)),
                      pl.BlockSpec((tc, 128), lambda i, pos_: (i, 0)),
                      pl.BlockSpec((1, d), lambda i, pos_: (0, 0)),
                      pl.BlockSpec(memory_space=pl.ANY)],
            out_specs=pl.BlockSpec((tc, d), lambda i, pos_: (i, 0)),
            scratch_shapes=[pltpu.VMEM((TOP_K, tc, d), F32), pltpu.SemaphoreType.DMA],
        ),
        out_shape=jax.ShapeDtypeStruct((n, d), F32),
        compiler_params=_cparams(("arbitrary",)),
        name="moe_combine",
    )(pos, x2, route, norm_w.reshape(1, d), ys)


def _final_norm_kernel(x_ref, nw_ref, o_ref):
    o_ref[...] = _rms(x_ref[...], nw_ref[...])


def final_rms(x2, norm_w, tm=1024):
    n, d = x2.shape
    return pl.pallas_call(
        _final_norm_kernel,
        grid=(n // tm,),
        in_specs=[pl.BlockSpec((tm, d), lambda i: (i, 0)), pl.BlockSpec((1, d), lambda i: (0, 0))],
        out_specs=pl.BlockSpec((tm, d), lambda i: (i, 0)),
        out_shape=jax.ShapeDtypeStruct((n, d), F32),
        compiler_params=_cparams(("parallel",)),
        name="final_norm",
    )(x2, norm_w.reshape(1, d))


def _mla_prep_kernel(cq_ref, ckv_ref, kra_ref, krb_ref, cm_ref, sm_ref, qn_ref, kvn_ref,
                     wq1_ref, wq2_ref, wk_ref, wv_ref, q_ref, k_ref, v_ref):
    qn = _rms(cq_ref[...], qn_ref[...]).astype(BF16)
    kvn = _rms(ckv_ref[...], kvn_ref[...]).astype(BF16)
    cm = cm_ref[...]
    sm = sm_ref[...]
    cm4 = jnp.concatenate([cm] * N_HEADS, axis=1)
    sm4 = jnp.concatenate([sm] * N_HEADS, axis=1)
    scale = (MLA_NOPE + MLA_ROPE) ** -0.5
    q = (_dot(qn, wq1_ref[...]) * cm4 + _dot(qn, wq2_ref[...]) * sm4) * scale
    q_ref[...] = q.astype(BF16)
    kr = kra_ref[...] * cm + krb_ref[...] * sm
    k = _dot(kvn, wk_ref[...]) + jnp.concatenate([kr] * N_HEADS, axis=1)
    k_ref[...] = k.astype(BF16)
    lane = lax.broadcasted_iota(jnp.int32, (cm.shape[0], N_HEADS * 128), 1)
    ones = jnp.where(((lane >> 6) & 1) == (((lane >> 7) & 1) ^ 1), 1.0, 0.0)
    v_ref[...] = (_dot(kvn, wv_ref[...]) + ones).astype(BF16)


def mla_prep(cols, cm, sm, q_norm, kv_norm, wq1, wq2, wk, wv, tm=512):
    n = cols.shape[0]
    b256 = C_MLA // 256
    b128 = C_MLA // 128
    full = lambda shape: pl.BlockSpec(shape, lambda i: (0, 0))
    return pl.pallas_call(
        _mla_prep_kernel,
        grid=(n // tm,),
        in_specs=[pl.BlockSpec((tm, 256), lambda i: (i, b256)),
                  pl.BlockSpec((tm, 128), lambda i: (i, b128 + 2)),
                  pl.BlockSpec((tm, 128), lambda i: (i, b128 + 3)),
                  pl.BlockSpec((tm, 128), lambda i: (i, b128 + 4)),
                  pl.BlockSpec((tm, 128), lambda i: (i, 0)),
                  pl.BlockSpec((tm, 128), lambda i: (i, 0)),
                  full((1, 256)), full((1, 128)),
                  full((256, 512)), full((256, 512)), full((128, 512)), full((128, 512))],
        out_specs=[pl.BlockSpec((tm, 512), lambda i: (i, 0))] * 3,
        out_shape=[jax.ShapeDtypeStruct((n, 512), BF16)] * 3,
        compiler_params=_cparams(("parallel",)),
        name="mla_prep",
    )(cols, cols, cols, cols, cm, sm, q_norm.reshape(1, 256), kv_norm.reshape(1, 128), wq1, wq2, wk, wv)


def _mla_flash_kernel(q_ref, k_ref, v_ref, o_ref, *, tq, tk):
    i = pl.program_id(1)
    lane = lax.broadcasted_iota(jnp.int32, (tq, 128), 1)
    qpos = i * tq + lax.broadcasted_iota(jnp.int32, (tq, tk), 0)
    kloc = lax.broadcasted_iota(jnp.int32, (tq, tk), 1)
    nfull = (i * tq) // tk
    heads = range(N_HEADS)
    qs = [q_ref[:, h * 128:(h + 1) * 128] for h in heads]

    def step(j, carry, masked):
        ks = pl.multiple_of(j * tk, tk)
        ss = [_dot_nt(qs[h], k_ref[pl.ds(ks, tk), h * 128:(h + 1) * 128]) for h in heads]
        if masked:
            mask = ks + kloc <= qpos
            ss = [jnp.where(mask, s, NEG_INF) for s in ss]
        mns = [jnp.maximum(carry[h][0], jnp.max(ss[h], axis=-1, keepdims=True)) for h in heads]
        ps = [jnp.exp((ss[h] - mns[h]).astype(BF16)) for h in heads]
        als = [jnp.exp(carry[h][0] - mns[h]) for h in heads]
        pvs = [_dot(ps[h], v_ref[pl.ds(ks, tk), h * 128:(h + 1) * 128]) for h in heads]
        return tuple((mns[h], carry[h][1] * als[h] + pvs[h]) for h in heads)

    init = (jnp.full((tq, 1), NEG_INF, F32), jnp.zeros((tq, 128), F32))
    res = lax.fori_loop(0, nfull, functools.partial(step, masked=False), (init,) * N_HEADS)
    res = step(nfull, res, True)
    for p in range(N_PAIRS):
        a0, a1 = res[2 * p][1], res[2 * p + 1][1]
        acc = jnp.where(lane < HEAD_DIM, a0, a1)
        den = jnp.where(lane < HEAD_DIM, pltpu.roll(a0, HEAD_DIM, 1), pltpu.roll(a1, HEAD_DIM, 1))
        o_ref[:, p * 128:(p + 1) * 128] = acc / den


def mla_flash(q3, k3, v3, tq=256, tk=512):
    b, t, _ = q3.shape
    assert tk % tq == 0 and t % tk == 0
    w = N_HEADS * 128
    return pl.pallas_call(
        functools.partial(_mla_flash_kernel, tq=tq, tk=tk),
        grid=(b, t // tq),
        in_specs=[pl.BlockSpec((None, tq, w), lambda b_, i: (b_, i, 0)),
                  pl.BlockSpec((None, t, w), lambda b_, i: (b_, 0, 0)),
                  pl.BlockSpec((None, t, w), lambda b_, i: (b_, 0, 0))],
        out_specs=pl.BlockSpec((None, tq, D_GROUP), lambda b_, i: (b_, i, 0)),
        out_shape=jax.ShapeDtypeStruct((b, t, D_GROUP), F32),
        compiler_params=_cparams(("parallel", "arbitrary")),
        name="mla_flash",
    )(q3, k3, v3)


DILATED_PATTERNS = ((128, 1), (512, 4), (2048, 16))
DIL_BLOCK = 128
DIL_UNROLL = 4


def _rotate_half_heads(x):
    lane = lax.broadcasted_iota(jnp.int32, x.shape, 1)
    width = x.shape[1]
    half = HEAD_DIM // 2
    return jnp.where((lane & (HEAD_DIM - 1)) < half, -pltpu.roll(x, width - half, 1), pltpu.roll(x, half, 1))


def _dil_block(qb, kb, vb, mask, out):
    nq = qb.shape[0]
    lane = lax.broadcasted_iota(jnp.int32, (nq, 128), 1)
    kbb = kb.astype(BF16)
    vbb = vb.astype(BF16)
    hms = [(lane >> 6) == h for h in range(2)]
    ss = [jnp.where(mask, _dot_nt(jnp.where(hm, qb, 0.0).astype(BF16), kbb), NEG_INF) for hm in hms]
    yield
    ms = [jnp.max(s, axis=-1, keepdims=True) for s in ss]
    ps = [jnp.exp(s - m) for s, m in zip(ss, ms)]
    yield
    ls = [jnp.sum(p, axis=-1, keepdims=True) for p in ps]
    ohs = [_dot(p.astype(BF16), vbb) for p in ps]
    yield
    out[0] = jnp.where(hms[0], ohs[0] / ls[0], ohs[1] / ls[1])
    out[1] = jnp.where(hms[0], ms[0] + jnp.log(ls[0]), ms[1] + jnp.log(ls[1]))


def _dil_kernel(qraw_ref, kraw_ref, v_ref, cd_ref, sd_ref, o_ref, q_ref, k_ref, m_ref, z_ref, acc_ref, *, t):
    blk = DIL_BLOCK
    cd = cd_ref[...]
    sd = sd_ref[...]
    qraw = qraw_ref[...]
    kraw = kraw_ref[...]
    q_ref[...] = (qraw * cd + _rotate_half_heads(qraw) * sd) * (HEAD_DIM ** -0.5)
    k_ref[...] = kraw * cd + _rotate_half_heads(kraw) * sd
    for pi, (window, dil) in enumerate(DILATED_PATTERNS):
        assert window // dil == blk
        length = t // dil
        nb = length // blk
        nk = 2 * blk if nb > 1 else blk
        uq = lax.broadcasted_iota(jnp.int32, (blk, nk), 0)
        uk = lax.broadcasted_iota(jnp.int32, (blk, nk), 1)

        def one(it, dil=dil, nb=nb, nk=nk, uq=uq, uk=uk, first=(pi == 0)):
            r = it // nb
            jb = it % nb
            kb0 = jnp.maximum(jb - 1, 0) * blk
            qs = r + dil * blk * jb
            ks = r + dil * kb0
            if dil == 1:
                qidx = pl.ds(pl.multiple_of(qs, blk), blk)
                kidx = pl.ds(pl.multiple_of(ks, blk), nk)
            else:
                qidx = pl.ds(qs, blk, stride=dil)
                kidx = pl.ds(ks, nk, stride=dil)
            dist = (jb * blk + uq) - (kb0 + uk)
            mask = (dist >= 0) & (dist <= blk)
            out = [None, None]
            yield from _dil_block(q_ref[qidx, :], k_ref[kidx, :], v_ref[kidx, :], mask, out)
            o, lse = out
            if first:
                m_ref[qidx, :] = lse
                z_ref[qidx, :] = jnp.ones_like(lse)
                acc_ref[qidx, :] = o
            else:
                m_old = m_ref[qidx, :]
                m_new = jnp.maximum(m_old, lse)
                a = jnp.exp(m_old - m_new)
                b = jnp.exp(lse - m_new)
                m_ref[qidx, :] = m_new
                z_ref[qidx, :] = z_ref[qidx, :] * a + b
                acc_ref[qidx, :] = acc_ref[qidx, :] * a + o * b

        def body(it, carry, one=one):
            _round_robin([one(it * DIL_UNROLL + u) for u in range(DIL_UNROLL)])
            return carry

        assert (dil * nb) % DIL_UNROLL == 0
        lax.fori_loop(0, dil * nb // DIL_UNROLL, body, 0)
    o_ref[...] = acc_ref[...] / z_ref[...]


def dilated_attention(cols3, cd3, sd3):
    b, t, _ = cols3.shape
    assert t % (DIL_BLOCK * DILATED_PATTERNS[-1][1]) == 0
    c0 = C_DIL // 128
    col = lambda j: pl.BlockSpec((None, t, 128), lambda i, p: (i, 0, j + p))
    tab = pl.BlockSpec((None, t, 128), lambda i, p: (i, 0, 0))
    return pl.pallas_call(
        functools.partial(_dil_kernel, t=t),
        grid=(b, 2),
        in_specs=[col(c0), col(c0 + 2), col(c0 + 4), tab, tab],
        out_specs=pl.BlockSpec((None, t, 128), lambda i, p: (i, 0, p)),
        out_shape=jax.ShapeDtypeStruct((b, t, 256), F32),
        scratch_shapes=[pltpu.VMEM((t, 128), F32)] * 5,
        compiler_params=_cparams(("parallel", "parallel")),
        name="dilated_attention",
    )(cols3, cols3, cols3, cd3, sd3)


CHUNK = 64
PAIR = 2
PAIR_LANES = PAIR * HEAD_DIM
N_PAIRS = N_HEADS // PAIR
STACK = PAIR * CHUNK


def _stack_heads(x):
    lane = lax.broadcasted_iota(jnp.int32, x.shape, 1)
    return jnp.concatenate([jnp.where((lane >> 6) == h, x, 0.0) for h in range(PAIR)], axis=0)


def _unstack_heads(y):
    out = y[0:CHUNK]
    for h in range(1, PAIR):
        out = out + y[h * CHUNK:(h + 1) * CHUNK]
    return out


def _round_robin(stages):
    live = list(stages)
    while live:
        nxt = []
        for g in live:
            try:
                next(g)
                nxt.append(g)
            except StopIteration:
                pass
        live = nxt


def _head_mean_matrix(n=PAIR_LANES):
    r = lax.broadcasted_iota(jnp.int32, (n, n), 0)
    c = lax.broadcasted_iota(jnp.int32, (n, n), 1)
    return jnp.where((r >> 6) == (c >> 6), 1.0 / HEAD_DIM, 0.0).astype(BF16)


HGRN_HALVES = (32, 16, 8, 4, 2, 1)


def _hgrn_sum_matrix():
    t = lax.broadcasted_iota(jnp.int32, (CHUNK, CHUNK), 0)
    j = lax.broadcasted_iota(jnp.int32, (CHUNK, CHUNK), 1)
    tri = jnp.where(j <= t, 1.0, 0.0)
    blocks = [tri]
    for half in HGRN_HALVES:
        ref = (t // (2 * half)) * (2 * half) + half - 1
        blocks.append(tri - jnp.where(j <= ref, 1.0, 0.0))
    return jnp.concatenate(blocks, axis=0).astype(BF16)


def _hgrn_kernel(q_ref, f_ref, i_ref, g_ref, lbl_ref, gn_ref, o_ref, st_ref, *, layer, nchunk, nb):
    @pl.when(pl.program_id(1) == 0)
    def _():
        st_ref[...] = jnp.zeros_like(st_ref)

    logits = lbl_ref[...]
    e = jnp.exp(logits - jnp.max(logits, axis=0, keepdims=True))
    sm = e / jnp.sum(e, axis=0, keepdims=True)
    lb = jnp.zeros((1, 256), F32)
    for l in range(1, layer + 1):
        lb = lb + sm[l:l + 1, :]
    gn = gn_ref[...]
    summat = _hgrn_sum_matrix()
    hmean = _head_mean_matrix()
    r = lax.broadcasted_iota(jnp.int32, (STACK, STACK), 0)
    c = lax.broadcasted_iota(jnp.int32, (STACK, STACK), 1)
    same_head = (r >> 6) == (c >> 6)
    rc = r & (CHUNK - 1)
    cc = c & (CHUNK - 1)

    def one(bi, pi, rows):
        ls = pl.ds(pi * PAIR_LANES, PAIR_LANES)
        lbp = lb[:, pi * PAIR_LANES:(pi + 1) * PAIR_LANES]
        qv = q_ref[bi, rows, ls]
        qq = qv * jax.nn.sigmoid(qv)
        forget = lbp + (1.0 - lbp) * jax.nn.sigmoid(f_ref[bi, rows, ls])
        logf = jnp.log(forget)
        kk = 1.0 - forget
        v = i_ref[bi, rows, ls]
        yield
        sums = _split_dot_lhs(summat, logf)
        b = sums[0:CHUNK]
        a = jnp.where(rc == cc, _dot_nt(_stack_heads(qq).astype(BF16), _stack_heads(kk).astype(BF16)), 0.0)
        yield
        for li, half in enumerate(HGRN_HALVES):
            d = sums[(li + 1) * CHUNK:(li + 2) * CHUNK]
            ql = qq * jnp.exp(jnp.minimum(d, 0.0))
            kl = kk * jnp.exp(jnp.minimum(-d, 0.0))
            m = ((rc // (2 * half)) == (cc // (2 * half))) & ((rc % (2 * half)) >= half) & ((cc % (2 * half)) < half)
            a = a + jnp.where(m, _dot_nt(_stack_heads(ql).astype(BF16), _stack_heads(kl).astype(BF16)), 0.0)
            yield
        o = _unstack_heads(_dot(a.astype(BF16), _stack_heads(v).astype(BF16)))
        st = st_ref[bi, pi]
        o = o + _dot_nt((qq * jnp.exp(b)).astype(BF16), st.astype(BF16))
        yield
        b_end = b[CHUNK - 1:CHUNK, :]
        kbar = kk * jnp.exp(b_end - b)
        upd = lax.dot_general(v.astype(BF16), kbar.astype(BF16), (((0,), (0,)), ((), ())), preferred_element_type=F32)
        st_ref[bi, pi] = st * jnp.exp(b_end) + jnp.where(same_head, upd, 0.0)
        ms = _split_dot(o * o, hmean)
        yield
        gv = g_ref[bi, rows, ls]
        gnp = gn[:, pi * PAIR_LANES:(pi + 1) * PAIR_LANES]
        o_ref[bi, rows, ls] = o * lax.rsqrt(ms + RMS_EPS) * gnp * (gv * jax.nn.sigmoid(gv))

    def chunk(ci, carry):
        rows = pl.ds(pl.multiple_of(ci * CHUNK, CHUNK), CHUNK)
        _round_robin([one(bi, pi, rows) for bi in range(nb) for pi in range(N_PAIRS)])
        return carry

    lax.fori_loop(0, nchunk, chunk, 0)


SEQ_PER_STEP = 4


def hgrn(cols3, lb_logits, g_norm, layer, tb=512):
    b, t, _ = cols3.shape
    nb = SEQ_PER_STEP if b % SEQ_PER_STEP == 0 else 1
    c0 = C_HGRN // 256
    spec = lambda j: pl.BlockSpec((nb, tb, 256), lambda b_, i: (b_, i, j))
    depth = lb_logits.shape[0]
    return pl.pallas_call(
        functools.partial(_hgrn_kernel, layer=layer, nchunk=tb // CHUNK, nb=nb),
        grid=(b // nb, t // tb),
        in_specs=[spec(c0), spec(c0 + 1), spec(c0 + 2), spec(c0 + 3),
                  pl.BlockSpec((depth, 256), lambda b_, i: (0, 0)),
                  pl.BlockSpec((1, 256), lambda b_, i: (0, 0))],
        out_specs=spec(0),
        out_shape=jax.ShapeDtypeStruct((b, t, 256), F32),
        scratch_shapes=[pltpu.VMEM((nb, N_PAIRS, STACK, STACK), F32)],
        compiler_params=_cparams(("parallel", "arbitrary")),
        name="hgrn",
    )(cols3, cols3, cols3, cols3, lb_logits, g_norm.reshape(1, 256))


def _head_sum_matrix():
    r = lax.broadcasted_iota(jnp.int32, (256, 256), 0)
    c = lax.broadcasted_iota(jnp.int32, (256, 256), 1)
    return jnp.where((r >> 6) == (c >> 6), 1.0, 0.0).astype(BF16)


def _rwkv_prep_tile(a, last_prev, mu, w0, w2, a0, a2, g2, k_k, k_a, r_k):
    row = lax.broadcasted_iota(jnp.int32, a.shape, 0)
    prev = jnp.where(row == 0, last_prev, pltpu.roll(a, 1, 0))
    xs = a + (prev - a) * mu
    r = xs[:, 0:256]
    k = xs[:, 256:512]
    v = xs[:, 512:768]
    xwa = xs[:, 768:896]
    xg = xs[:, 896:1024]
    z = -(w0 + _dot(jnp.tanh(xwa).astype(BF16), w2))
    softplus = jnp.maximum(z, 0.0) + jnp.log(1.0 + jnp.exp(-jnp.abs(z)))
    lw = -jnp.exp(-softplus - 0.5)
    lr = jax.nn.sigmoid(a0 + _dot(xwa.astype(BF16), a2))
    g = _dot(jax.nn.sigmoid(xg).astype(BF16), g2)
    hsum = _head_sum_matrix()
    kk = k * k_k
    kk = kk / jnp.maximum(jnp.sqrt(_split_dot(kk * kk, hsum)), 1e-12)
    k2 = k * (1.0 + (lr - 1.0) * k_a)
    bonus = _split_dot(r * k2 * r_k, hsum) * v
    return r, lw, k2, v, kk, kk * lr, bonus, g


RWKV_MERGE_HALVES = (2, 4, 8, 16, 32)


def _compact_upper(x, half):
    return jnp.concatenate([x[s + half:s + 2 * half] for s in range(0, STACK, 2 * half)], axis=0)


def _expand_upper(c, half):
    zero = jnp.zeros((half, c.shape[1]), c.dtype)
    return jnp.concatenate([p for i in range(STACK // (2 * half)) for p in (zero, c[i * half:(i + 1) * half])], axis=0)


def _rwkv_kernel(c_ref, p_ref, mu_ref, w0_ref, w2_ref, a0_ref, a2_ref, g2_ref, kk_ref, ka_ref, rk_ref,
                 lnw_ref, lnb_ref, o_ref,
                 r_ref, lw_ref, k_ref, v_ref, kap_ref, at_ref, bonus_ref, g_ref, ht_ref, *, nchunk, nb):
    first = pl.program_id(1) == 0

    @pl.when(first)
    def _():
        ht_ref[...] = jnp.zeros_like(ht_ref)

    for bi in range(nb):
        outs = _rwkv_prep_tile(c_ref[bi], jnp.where(first, 0.0, p_ref[bi, 7:8, :]), mu_ref[...], w0_ref[...],
                               w2_ref[...], a0_ref[...], a2_ref[...], g2_ref[...], kk_ref[...], ka_ref[...],
                               rk_ref[...])
        for ref, val in zip((r_ref, lw_ref, k_ref, v_ref, kap_ref, at_ref, bonus_ref, g_ref), outs):
            ref[bi] = val

    t = lax.broadcasted_iota(jnp.int32, (CHUNK, CHUNK), 0)
    j = lax.broadcasted_iota(jnp.int32, (CHUNK, CHUNK), 1)
    tri = jnp.where(j <= t, 1.0, 0.0).astype(BF16)
    hmean = _head_mean_matrix()
    rr = lax.broadcasted_iota(jnp.int32, (STACK, STACK), 0)
    cc = lax.broadcasted_iota(jnp.int32, (STACK, STACK), 1)
    same_head = (rr >> 6) == (cc >> 6)
    rc = rr & (CHUNK - 1)
    sc = cc & (CHUNK - 1)
    eye = jnp.where(rr == cc, 1.0, 0.0)
    lnw = lnw_ref[...]
    lnb = lnb_ref[...]

    def one(bi, pi, rows):
        ls = pl.ds(pi * PAIR_LANES, PAIR_LANES)
        lw = lw_ref[bi, rows, ls]
        lc = _split_dot_lhs(tri, lw)
        l_end = lc[CHUNK - 1:CHUNK, :]
        kap = kap_ref[bi, rows, ls]
        at = at_ref[bi, rows, ls]
        k2 = k_ref[bi, rows, ls]
        v = v_ref[bi, rows, ls]
        inv = jnp.exp(-lc)
        fwd = jnp.exp(l_end - lc)
        s_kap = _stack_heads(kap * jnp.exp(lc - lw)).astype(BF16)
        s_r32 = _stack_heads(r_ref[bi, rows, ls] * jnp.exp(lc))
        s_r = s_r32.astype(BF16)
        s_a = _stack_heads(at * inv).astype(BF16)
        s_k = _stack_heads(k2 * inv).astype(BF16)
        s_v = _stack_heads(v).astype(BF16)
        strict = rc > sc
        incl = rc >= sc
        yield
        a_ka = jnp.where(strict, _dot_nt(s_kap, s_a), 0.0)
        a_kk = jnp.where(strict, _dot_nt(s_kap, s_k), 0.0).astype(BF16)
        m1 = ((rc >> 1) == (sc >> 1)) & strict
        tinv = eye - jnp.where(m1, a_ka, 0.0)
        yield
        a_ra = jnp.where(incl, _dot_nt(s_r, s_a), 0.0).astype(BF16)
        a_rk = jnp.where(incl, _dot_nt(s_r, s_k), 0.0).astype(BF16)
        for half in RWKV_MERGE_HALVES:
            m = ((rc // (2 * half)) == (sc // (2 * half))) & ((rc % (2 * half)) >= half) & ((sc % (2 * half)) < half)
            off = jnp.where(m, a_ka, 0.0)
            tb = tinv.astype(BF16)
            if half % 8 == 0:
                x = _expand_upper(_dot(_compact_upper(off, half).astype(BF16), tb), half).astype(BF16)
                yield
                tinv = tinv - _expand_upper(_dot(_compact_upper(tinv, half).astype(BF16), x), half)
            else:
                x = _dot(off.astype(BF16), tb).astype(BF16)
                yield
                tinv = tinv - _dot(tb, x)
            yield
        tb = tinv.astype(BF16)
        kt_st = _dot(tb, s_kap)
        w_st = _dot(a_kk, s_v).astype(BF16)
        yield
        vt_st = _dot(tb, w_st)
        rt = _unstack_heads(s_r32 - _dot(a_ra, kt_st.astype(BF16)))
        yield
        y0 = _unstack_heads(_dot(a_rk, s_v) - _dot(a_ra, vt_st.astype(BF16)))
        kt = _unstack_heads(kt_st)
        vt = _unstack_heads(vt_st)
        ht = ht_ref[bi, pi]
        htb = ht.astype(BF16)
        yield
        y = y0 + _dot_nt(rt.astype(BF16), htb)
        u = vt + _dot_nt(kt.astype(BF16), htb)
        yield
        lhs = jnp.concatenate([v, u], axis=0).astype(BF16)
        rhs = jnp.concatenate([k2 * fwd, -(at * fwd)], axis=0).astype(BF16)
        upd = lax.dot_general(lhs, rhs, (((0,), (0,)), ((), ())), preferred_element_type=F32)
        ht_ref[bi, pi] = ht * jnp.exp(l_end) + jnp.where(same_head, upd, 0.0)
        mean = _split_dot(y, hmean)
        yield
        yc = y - mean
        var = _split_dot(yc * yc, hmean)
        yield
        lanes = slice(pi * PAIR_LANES, (pi + 1) * PAIR_LANES)
        yn = yc * lax.rsqrt(var + RWKV_LN_EPS) * lnw[:, lanes] + lnb[:, lanes]
        o_ref[bi, rows, ls] = (yn + bonus_ref[bi, rows, ls]) * g_ref[bi, rows, ls]

    def chunk(ci, carry):
        rows = pl.ds(pl.multiple_of(ci * CHUNK, CHUNK), CHUNK)
        _round_robin([one(bi, pi, rows) for bi in range(nb) for pi in range(N_PAIRS)])
        return carry

    lax.fori_loop(0, nchunk, chunk, 0)


def rwkv_mix(cols3, mu, w0, w2p, a0, a2p, g2, k_k, k_a, r_k, ln_w, ln_b, tb=512):
    b, t, _ = cols3.shape
    nb = SEQ_PER_STEP if b % SEQ_PER_STEP == 0 else 1
    spec = pl.BlockSpec((nb, tb, 256), lambda b_, i: (b_, i, 0))
    full = lambda shape: pl.BlockSpec(shape, lambda b_, i: (0, 0))
    row = lambda x: x.reshape(1, -1)
    return pl.pallas_call(
        functools.partial(_rwkv_kernel, nchunk=tb // CHUNK, nb=nb),
        grid=(b // nb, t // tb),
        in_specs=[pl.BlockSpec((nb, tb, 1024), lambda b_, i: (b_, i, C_RWKV // 1024)),
                  pl.BlockSpec((nb, 8, 1024), lambda b_, i: (b_, jnp.maximum(i * (tb // 8) - 1, 0), C_RWKV // 1024)),
                  full((1, 1024)), full((1, 256)), full((128, 256)), full((1, 256)), full((128, 256)),
                  full((128, 256)), full((1, 256)), full((1, 256)), full((1, 256)), full((1, 256)), full((1, 256))],
        out_specs=spec,
        out_shape=jax.ShapeDtypeStruct((b, t, 256), F32),
        scratch_shapes=[pltpu.VMEM((nb, tb, 256), F32)] * 8 + [pltpu.VMEM((nb, N_PAIRS, STACK, STACK), F32)],
        compiler_params=_cparams(("parallel", "arbitrary")),
        name="rwkv_mix",
    )(cols3, cols3, row(mu), row(w0), w2p, row(a0), a2p, g2, row(k_k), row(k_a), row(r_k), row(ln_w), row(ln_b))


def _rot_cols(w, half):
    g = w.reshape(w.shape[0], -1, 2, half)
    return jnp.concatenate([-g[:, :, 1:2, :], g[:, :, 0:1, :]], axis=2).reshape(w.shape)


def pack_w_in(w):
    d = w.shape[0]
    rw = w[:, 0:1024]
    dq, dk, dv = w[:, 1024:1280], w[:, 1280:1536], w[:, 1536:1792]
    cq, ckv, kr = w[:, 1792:2048], w[:, 2048:2176], w[:, 2176:2208]
    hg = w[:, 2208:3232]
    z = lambda n: jnp.zeros((d, n), w.dtype)
    kra = jnp.concatenate([z(MLA_NOPE), kr, z(128 - MLA_NOPE - MLA_ROPE)], axis=1)
    krb = jnp.concatenate([z(MLA_NOPE), _rot_cols(kr, MLA_ROPE // 2), z(128 - MLA_NOPE - MLA_ROPE)], axis=1)
    packed = jnp.concatenate([rw, dq, dk, dv, hg, cq, ckv, kra, krb], axis=1)
    return packed.astype(BF16)


def pack_mla(w_uq, w_ukv):
    rq = w_uq.shape[0]
    q = w_uq.reshape(rq, N_HEADS, MLA_NOPE + MLA_ROPE)
    nope, rope = q[..., :MLA_NOPE], q[..., MLA_NOPE:]
    pad = jnp.zeros((rq, N_HEADS, 128 - MLA_NOPE - MLA_ROPE), w_uq.dtype)
    wq1 = jnp.concatenate([nope, rope, pad], axis=-1).reshape(rq, N_HEADS * 128)
    rrot = _rot_cols(rope.reshape(rq, N_HEADS * MLA_ROPE), MLA_ROPE // 2).reshape(rq, N_HEADS, MLA_ROPE)
    wq2 = jnp.concatenate([jnp.zeros_like(nope), rrot, pad], axis=-1).reshape(rq, N_HEADS * 128)
    rk = w_ukv.shape[0]
    kv = w_ukv.reshape(rk, N_HEADS, MLA_NOPE + HEAD_DIM)
    wk = jnp.concatenate([kv[..., :MLA_NOPE], jnp.zeros((rk, N_HEADS, 128 - MLA_NOPE), w_ukv.dtype)],
                         axis=-1).reshape(rk, N_HEADS * 128)
    vz = jnp.zeros((rk, HEAD_DIM), w_ukv.dtype)
    wv = jnp.concatenate([x for h in range(N_HEADS)
                          for x in ((kv[:, h, MLA_NOPE:], vz) if h % 2 == 0 else (vz, kv[:, h, MLA_NOPE:]))], axis=1)
    return wq1.astype(BF16), wq2.astype(BF16), wk.astype(BF16), wv.astype(BF16)


def kernel(x, positions, mix_norm, w_in, w_out, rwkv_mu, rwkv_w0, rwkv_w2, rwkv_a0, rwkv_a2, rwkv_g2, rwkv_k_k,
           rwkv_k_a, rwkv_r_k, rwkv_ln_w, rwkv_ln_b, mla_q_norm, mla_w_uq, mla_kv_norm, mla_w_ukv, hgrn_lb_logits,
           hgrn_g_norm, ffn_norm, ffn_w1, ffn_w3, ffn_w2, moe_router, moe_w1, moe_w3, moe_w2, final_norm):
    b, t, d = x.shape
    n = b * t
    depth = w_in.shape[0]
    x2 = x.reshape(n, d)
    cd, sd, cm, sm = rope_tables(positions.reshape(n, 1))
    lora_pad = jnp.zeros((rwkv_w2.shape[1], D_GROUP), F32)
    fused_final = False
    for layer in range(depth):
        cols = in_proj(x2, mix_norm[layer], pack_w_in(w_in[layer]))
        cols3 = cols.reshape(b, t, N_PACK)
        y_a = rwkv_mix(cols3, rwkv_mu[layer], rwkv_w0[layer],
                       jnp.concatenate([rwkv_w2[layer], lora_pad], axis=0).astype(BF16), rwkv_a0[layer],
                       jnp.concatenate([lora_pad, rwkv_a2[layer]], axis=0).astype(BF16),
                       rwkv_g2[layer].astype(BF16), rwkv_k_k[layer], rwkv_k_a[layer], rwkv_r_k[layer],
                       rwkv_ln_w[layer], rwkv_ln_b[layer])
        y_b = dilated_attention(cols3, cd.reshape(b, t, -1), sd.reshape(b, t, -1))
        q_m, k_m, v_m = mla_prep(cols, cm, sm, mla_q_norm[layer], mla_kv_norm[layer],
                                 *pack_mla(mla_w_uq[layer], mla_w_ukv[layer]))
        y_c = mla_flash(q_m.reshape(b, t, -1), k_m.reshape(b, t, -1), v_m.reshape(b, t, -1))
        y_d = hgrn(cols3, hgrn_lb_logits, hgrn_g_norm[layer], layer)
        ys = [y.reshape(n, D_GROUP) for y in (y_a, y_b, y_c, y_d)]
        x2, h = out_proj(x2, ys, w_out[layer].astype(BF16), ffn_norm[layer])
        j = layer // 2
        if layer % 2 == 0:
            x2 = ffn_dense(h, x2, ffn_w1[j].astype(BF16), ffn_w3[j].astype(BF16), ffn_w2[j].astype(BF16))
        else:
            route, hp = moe_route(x2, ffn_norm[layer], jnp.pad(moe_router[j], ((0, 0), (0, 128 - N_EXPERTS))))
            rows_pad = TOP_K * n + N_EXPERTS * MOE_TILE
            pos, tile_expert, tile_valid = moe_plan(route, rows_pad)
            xs = moe_scatter(pos, hp, rows_pad)
            ys = moe_experts(tile_expert, tile_valid, xs, moe_w1[j].astype(BF16), moe_w3[j].astype(BF16),
                             moe_w2[j].astype(BF16))
            fused_final = layer == depth - 1
            x2 = moe_combine(pos, x2, route, ys, final_norm, fused_final)
    if not fused_final:
        x2 = final_rms(x2, final_norm)
    return x2.reshape(b, t, d)
```

```python
import functools

import jax
import jax.numpy as jnp
from jax import lax
from jax.experimental import pallas as pl
from jax.experimental.pallas import tpu as pltpu

F32 = jnp.float32
BF16 = jnp.bfloat16

D_GROUP = 256
HEAD_DIM = 64
N_HEADS = 4
RMS_EPS = 1e-6
RWKV_LN_EPS = 64e-5
ROPE_THETA = 10000.0
NEG_INF = -1e30
N_EXPERTS = 8
MLA_ROPE = 32
MLA_NOPE = 64

VMEM_LIMIT = 56 * 1024 * 1024

C_RWKV = 0
C_DIL = 1024
C_HGRN = 1792
C_MLA = 2816
N_PACK = 3456


def _cparams(sem, vmem=VMEM_LIMIT):
    return pltpu.CompilerParams(dimension_semantics=sem, vmem_limit_bytes=vmem)


def _rms(x, w):
    return x * lax.rsqrt(jnp.mean(x * x, axis=-1, keepdims=True) + RMS_EPS) * w


def _dot(a, b):
    return jnp.dot(a, b, preferred_element_type=F32)


def _dot_nt(a, b):
    return lax.dot_general(a, b, (((1,), (1,)), ((), ())), preferred_element_type=F32)


def _split_dot(a_f32, b_bf16):
    hi = a_f32.astype(BF16)
    lo = (a_f32 - hi.astype(F32)).astype(BF16)
    return _dot(hi, b_bf16) + _dot(lo, b_bf16)


def _split_dot_lhs(a_bf16, b_f32):
    hi = b_f32.astype(BF16)
    lo = (b_f32 - hi.astype(F32)).astype(BF16)
    return _dot(a_bf16, hi) + _dot(a_bf16, lo)


def _rope_kernel(pos_ref, cd_ref, sd_ref, cm_ref, sm_ref):
    pos = pos_ref[...].astype(F32)
    tm = pos.shape[0]
    lane = lax.broadcasted_iota(jnp.int32, (tm, 128), 1)
    j = (lane & 31).astype(F32)
    inv = jnp.exp(j * (-2.0 / HEAD_DIM * jnp.log(ROPE_THETA)))
    ang = pos * inv
    cd_ref[...] = jnp.cos(ang)
    sd_ref[...] = jnp.sin(ang)
    jm = (lane & 15).astype(F32)
    invm = jnp.exp(jm * (-2.0 / MLA_ROPE * jnp.log(ROPE_THETA)))
    angm = pos * invm
    rope_lane = (lane >= MLA_NOPE) & (lane < MLA_NOPE + MLA_ROPE)
    cm_ref[...] = jnp.where(rope_lane, jnp.cos(angm), jnp.where(lane < MLA_NOPE, 1.0, 0.0))
    sm_ref[...] = jnp.where(rope_lane, jnp.sin(angm), 0.0)


def rope_tables(pos_col, tm=512):
    n = pos_col.shape[0]
    return pl.pallas_call(
        _rope_kernel,
        grid=(n // tm,),
        in_specs=[pl.BlockSpec((tm, 1), lambda i: (i, 0))],
        out_specs=[pl.BlockSpec((tm, 128), lambda i: (i, 0))] * 4,
        out_shape=[jax.ShapeDtypeStruct((n, 128), F32)] * 4,
        compiler_params=_cparams(("parallel",)),
        name="rope_tables",
    )(pos_col)


def _in_proj_kernel(x_ref, nw_ref, w_ref, o_ref):
    h = _rms(x_ref[...], nw_ref[...]).astype(BF16)
    o_ref[...] = _dot(h, w_ref[...])


def in_proj(x2, norm_w, w_packed, tm=512):
    n, d = x2.shape
    nc = w_packed.shape[1]
    return pl.pallas_call(
        _in_proj_kernel,
        grid=(n // tm,),
        in_specs=[pl.BlockSpec((tm, d), lambda i: (i, 0)),
                  pl.BlockSpec((1, d), lambda i: (0, 0)),
                  pl.BlockSpec((d, nc), lambda i: (0, 0))],
        out_specs=pl.BlockSpec((tm, nc), lambda i: (i, 0)),
        out_shape=jax.ShapeDtypeStruct((n, nc), F32),
        compiler_params=_cparams(("parallel",)),
        name="in_proj",
    )(x2, norm_w.reshape(1, d), w_packed)


def _out_proj_kernel(x_ref, ya_ref, yb_ref, yc_ref, yd_ref, w_ref, nw_ref, xo_ref, h_ref):
    acc = x_ref[...]
    for g, y_ref in enumerate((ya_ref, yb_ref, yc_ref, yd_ref)):
        acc = acc + _dot(y_ref[...].astype(BF16), w_ref[g * D_GROUP:(g + 1) * D_GROUP, :])
    xo_ref[...] = acc
    h_ref[...] = _rms(acc, nw_ref[...]).astype(BF16)


def out_proj(x2, ys, w_out_bf16, norm_w, tm=512):
    n, d = x2.shape
    yspec = pl.BlockSpec((tm, D_GROUP), lambda i: (i, 0))
    return pl.pallas_call(
        _out_proj_kernel,
        grid=(n // tm,),
        in_specs=[pl.BlockSpec((tm, d), lambda i: (i, 0)), yspec, yspec, yspec, yspec,
                  pl.BlockSpec((d, d), lambda i: (0, 0)),
                  pl.BlockSpec((1, d), lambda i: (0, 0))],
        out_specs=[pl.BlockSpec((tm, d), lambda i: (i, 0)), pl.BlockSpec((tm, d), lambda i: (i, 0))],
        out_shape=[jax.ShapeDtypeStruct((n, d), F32), jax.ShapeDtypeStruct((n, d), BF16)],
        compiler_params=_cparams(("parallel",)),
        name="out_proj",
    )(x2, *ys, w_out_bf16, norm_w.reshape(1, d))


def _ffn_kernel(h_ref, x_ref, w1_ref, w3_ref, w2_ref, o_ref):
    h = h_ref[...]
    a = _dot(h, w1_ref[...])
    b = _dot(h, w3_ref[...])
    act = (a * jax.nn.sigmoid(a) * b).astype(BF16)
    o_ref[...] = x_ref[...] + _dot(act, w2_ref[...])


def ffn_dense(h_bf16, x2, w1, w3, w2, tm=512):
    n, d = x2.shape
    f = w1.shape[1]
    resident = lambda shape: pl.BlockSpec(shape, lambda i: (0, 0), pipeline_mode=pl.Buffered(1))
    return pl.pallas_call(
        _ffn_kernel,
        grid=(n // tm,),
        in_specs=[pl.BlockSpec((tm, d), lambda i: (i, 0)),
                  pl.BlockSpec((tm, d), lambda i: (i, 0)),
                  resident((d, f)), resident((d, f)), resident((f, d))],
        out_specs=pl.BlockSpec((tm, d), lambda i: (i, 0)),
        out_shape=jax.ShapeDtypeStruct((n, d), F32),
        compiler_params=_cparams(("parallel",)),
        name="ffn_dense",
    )(h_bf16, x2, w1, w3, w2)


MOE_TILE = 512
TOP_K = 2
MOE_DMA_TOKENS = 1024


def _router_kernel(x_ref, nw_ref, wr_ref, route_ref, hp_ref):
    h = _rms(x_ref[...], nw_ref[...])
    logits = jnp.dot(h, wr_ref[...], preferred_element_type=F32, precision=lax.Precision.HIGHEST)
    lane = lax.broadcasted_iota(jnp.int32, logits.shape, 1)
    logits = jnp.where(lane < N_EXPERTS, logits, -jnp.inf)
    m1 = jnp.max(logits, axis=-1, keepdims=True)
    i1 = jnp.min(jnp.where(logits == m1, lane, 128), axis=-1, keepdims=True)
    rest = jnp.where(lane == i1, -jnp.inf, logits)
    m2 = jnp.max(rest, axis=-1, keepdims=True)
    i2 = jnp.min(jnp.where(rest == m2, lane, 128), axis=-1, keepdims=True)
    e2 = jnp.exp(m2 - m1)
    g1 = 1.0 / (1.0 + e2)
    g2 = e2 * g1
    route_ref[...] = jnp.where(lane == 0, i1.astype(F32), jnp.where(lane == 1, i2.astype(F32),
                               jnp.where(lane == 2, g1, jnp.where(lane == 3, g2, 0.0))))
    hp_ref[...] = h


def moe_route(x2, norm_w, w_router_pad, tm=512):
    n, d = x2.shape
    return pl.pallas_call(
        _router_kernel,
        grid=(n // tm,),
        in_specs=[pl.BlockSpec((tm, d), lambda i: (i, 0)),
                  pl.BlockSpec((1, d), lambda i: (0, 0)),
                  pl.BlockSpec((d, 128), lambda i: (0, 0))],
        out_specs=[pl.BlockSpec((tm, 128), lambda i: (i, 0)), pl.BlockSpec((tm, d), lambda i: (i, 0))],
        out_shape=[jax.ShapeDtypeStruct((n, 128), F32), jax.ShapeDtypeStruct((n, d), F32)],
        compiler_params=_cparams(("parallel",)),
        name="moe_router",
    )(x2, norm_w.reshape(1, d), w_router_pad)


def moe_plan(route, n_rows_pad):
    e = route[:, :TOP_K].astype(jnp.int32)
    onehot = (e.reshape(-1, 1) == jnp.arange(N_EXPERTS, dtype=jnp.int32)[None, :]).astype(jnp.int32)
    csum = jnp.cumsum(onehot, axis=0)
    counts = csum[-1]
    rank = jnp.sum((csum - onehot) * onehot, axis=1)
    gsz = ((counts + MOE_TILE - 1) // MOE_TILE) * MOE_TILE
    gend = jnp.cumsum(gsz)
    goff = gend - gsz
    pos = (goff[e.reshape(-1)] + rank).astype(jnp.int32)
    tile_start = jnp.arange(n_rows_pad // MOE_TILE, dtype=jnp.int32) * MOE_TILE
    tile_expert = jnp.minimum(jnp.sum((tile_start[:, None] >= gend[None, :]).astype(jnp.int32), axis=1),
                              N_EXPERTS - 1).astype(jnp.int32)
    tile_valid = (tile_start < gend[-1]).astype(jnp.int32)
    last_tile = jnp.where(gsz > 0, gend - MOE_TILE, -1)
    slack = gend[-1] + jnp.arange(N_EXPERTS, dtype=jnp.int32) * MOE_TILE
    clear_tiles = jnp.concatenate([last_tile, jnp.where(slack < n_rows_pad, slack, -1)]).astype(jnp.int32)
    return pos, tile_expert, tile_valid, clear_tiles


def _moe_scatter_kernel(pos_ref, last_ref, hp_ref, xs_ref, zero_ref, sem, zsem):
    i = pl.program_id(0)
    ts = hp_ref.shape[0]

    @pl.when(i == 0)
    def _():
        zero_ref[...] = jnp.zeros_like(zero_ref)

        def clear(e):
            start = pl.multiple_of(jnp.maximum(last_ref[e], 0), MOE_TILE)
            return pltpu.make_async_copy(zero_ref, xs_ref.at[pl.ds(start, MOE_TILE), :], zsem)

        for e in range(2 * N_EXPERTS):
            @pl.when(last_ref[e] >= 0)
            def _(e=e):
                clear(e).start()
        for e in range(2 * N_EXPERTS):
            @pl.when(last_ref[e] >= 0)
            def _(e=e):
                clear(e).wait()

    def copy(r, slot):
        dst = pos_ref[(i * ts + r) * TOP_K + slot]
        return pltpu.make_async_copy(hp_ref.at[pl.ds(r, 1), :], xs_ref.at[pl.ds(dst, 1), :], sem)

    def issue(r, carry):
        for slot in range(TOP_K):
            copy(r, slot).start(priority=slot)
        return carry

    lax.fori_loop(0, ts, issue, 0, unroll=8)
    for slot in range(TOP_K):
        pltpu.make_async_copy(hp_ref, xs_ref.at[pl.ds(0, ts), :], sem).wait()


def moe_scatter(pos, clear_tiles, hp, n_rows_pad, ts=MOE_DMA_TOKENS):
    n, dh = hp.shape
    return pl.pallas_call(
        _moe_scatter_kernel,
        grid_spec=pltpu.PrefetchScalarGridSpec(
            num_scalar_prefetch=2,
            grid=(n // ts,),
            in_specs=[pl.BlockSpec((ts, dh), lambda i, pos_, last_: (i, 0))],
            out_specs=pl.BlockSpec(memory_space=pl.ANY),
            scratch_shapes=[pltpu.VMEM((MOE_TILE, dh), hp.dtype), pltpu.SemaphoreType.DMA, pltpu.SemaphoreType.DMA],
        ),
        out_shape=jax.ShapeDtypeStruct((n_rows_pad, dh), hp.dtype),
        compiler_params=_cparams(("arbitrary",)),
        name="moe_scatter",
    )(pos, clear_tiles, hp)


def _moe_expert_kernel(te_ref, tv_ref, xs_ref, w1_ref, w3_ref, w2_ref, ys_ref, h_ref, acc_ref):
    i = pl.program_id(0)
    j = pl.program_id(1)

    @pl.when(tv_ref[i] > 0)
    def _():
        @pl.when(j == 0)
        def _():
            h_ref[...] = xs_ref[...].astype(BF16)
            acc_ref[...] = jnp.zeros_like(acc_ref)

        h = h_ref[...]
        a = _dot(h, w1_ref[...])
        b = _dot(h, w3_ref[...])
        act = (a * jax.nn.sigmoid(a) * b).astype(BF16)
        acc_ref[...] += _dot(act, w2_ref[...])

        @pl.when(j == pl.num_programs(1) - 1)
        def _():
            ys_ref[...] = acc_ref[...]

    @pl.when((tv_ref[i] == 0) & (j == pl.num_programs(1) - 1))
    def _():
        ys_ref[...] = jnp.zeros_like(ys_ref)


def moe_experts(tile_expert, tile_valid, xs, w1, w3, w2, tf=1792):
    rows, d = xs.shape
    f = w1.shape[2]
    tm = MOE_TILE
    return pl.pallas_call(
        _moe_expert_kernel,
        grid_spec=pltpu.PrefetchScalarGridSpec(
            num_scalar_prefetch=2,
            grid=(rows // tm, f // tf),
            in_specs=[pl.BlockSpec((tm, d), lambda i, j, te, tv: (i, 0)),
                      pl.BlockSpec((None, d, tf), lambda i, j, te, tv: (te[i], 0, j * tv[i])),
                      pl.BlockSpec((None, d, tf), lambda i, j, te, tv: (te[i], 0, j * tv[i])),
                      pl.BlockSpec((None, tf, d), lambda i, j, te, tv: (te[i], j * tv[i], 0))],
            out_specs=pl.BlockSpec((tm, d), lambda i, j, te, tv: (i, 0)),
            scratch_shapes=[pltpu.VMEM((tm, d), BF16), pltpu.VMEM((tm, d), F32)],
        ),
        out_shape=jax.ShapeDtypeStruct((rows, d), F32),
        compiler_params=_cparams(("arbitrary", "arbitrary")),
        name="moe_experts",
    )(tile_expert, tile_valid, xs, w1, w3, w2)


def _moe_combine_kernel(pos_ref, x_ref, route_ref, nw_ref, ys_ref, o_ref, buf_ref, sem, *, final):
    i = pl.program_id(0)
    tc = x_ref.shape[0]

    def copy(r, slot):
        src = pos_ref[(i * tc + r) * TOP_K + slot]
        return pltpu.make_async_copy(ys_ref.at[pl.ds(src, 1), :], buf_ref.at[slot, pl.ds(r, 1), :], sem)

    def issue(r, carry):
        for slot in range(TOP_K):
            copy(r, slot).start(priority=slot)
        return carry

    lax.fori_loop(0, tc, issue, 0, unroll=8)
    for slot in range(TOP_K):
        pltpu.make_async_copy(ys_ref.at[pl.ds(0, tc), :], buf_ref.at[slot], sem).wait()
    route = route_ref[...]
    g1 = route[:, 2:3]
    g2 = route[:, 3:4]
    out = x_ref[...] + g1 * buf_ref[0] + g2 * buf_ref[1]
    if final:
        out = _rms(out, nw_ref[...])
    o_ref[...] = out


def moe_combine(pos, x2, route, ys, norm_w, final, tc=MOE_DMA_TOKENS):
    n, d = x2.shape
    return pl.pallas_call(
        functools.partial(_moe_combine_kernel, final=final),
        grid_spec=pltpu.PrefetchScalarGridSpec(
            num_scalar_prefetch=1,
            grid=(n // tc,),
            in_specs=[pl.BlockSpec((tc, d), lambda i, pos_: (i, 0)),
                      pl.BlockSpec((tc, 128), lambda i, pos_: (i, 0)),
                      pl.BlockSpec((1, d), lambda i, pos_: (0, 0)),
                      pl.BlockSpec(memory_space=pl.ANY)],
            out_specs=pl.BlockSpec((tc, d), lambda i, pos_: (i, 0)),
            scratch_shapes=[pltpu.VMEM((TOP_K, tc, d), F32), pltpu.SemaphoreType.DMA],
        ),
        out_shape=jax.ShapeDtypeStruct((n, d), F32),
        compiler_params=_cparams(("arbitrary",)),
        name="moe_combine",
    )(pos, x2, route, norm_w.reshape(1, d), ys)


def _final_norm_kernel(x_ref, nw_ref, o_ref):
    o_ref[...] = _rms(x_ref[...], nw_ref[...])


def final_rms(x2, norm_w, tm=1024):
    n, d = x2.shape
    return pl.pallas_call(
        _final_norm_kernel,
        grid=(n // tm,),
        in_specs=[pl.BlockSpec((tm, d), lambda i: (i, 0)), pl.BlockSpec((1, d), lambda i: (0, 0))],
        out_specs=pl.BlockSpec((tm, d), lambda i: (i, 0)),
        out_shape=jax.ShapeDtypeStruct((n, d), F32),
        compiler_params=_cparams(("parallel",)),
        name="final_norm",
    )(x2, norm_w.reshape(1, d))


def _mla_prep_kernel(cq_ref, ckv_ref, kra_ref, krb_ref, cm_ref, sm_ref, qn_ref, kvn_ref,
                     wq1_ref, wq2_ref, wk_ref, wv_ref, q_ref, k_ref, v_ref):
    qn = _rms(cq_ref[...], qn_ref[...]).astype(BF16)
    kvn = _rms(ckv_ref[...], kvn_ref[...]).astype(BF16)
    cm = cm_ref[...]
    sm = sm_ref[...]
    cm4 = jnp.concatenate([cm] * N_HEADS, axis=1)
    sm4 = jnp.concatenate([sm] * N_HEADS, axis=1)
    scale = (MLA_NOPE + MLA_ROPE) ** -0.5
    q = (_dot(qn, wq1_ref[...]) * cm4 + _dot(qn, wq2_ref[...]) * sm4) * scale
    q_ref[...] = q.astype(BF16)
    kr = kra_ref[...] * cm + krb_ref[...] * sm
    k = _dot(kvn, wk_ref[...]) + jnp.concatenate([kr] * N_HEADS, axis=1)
    k_ref[...] = k.astype(BF16)
    lane = lax.broadcasted_iota(jnp.int32, (cm.shape[0], N_HEADS * 128), 1)
    ones = jnp.where(((lane >> 6) & 1) == (((lane >> 7) & 1) ^ 1), 1.0, 0.0)
    v_ref[...] = (_dot(kvn, wv_ref[...]) + ones).astype(BF16)


def mla_prep(cols, cm, sm, q_norm, kv_norm, wq1, wq2, wk, wv, tm=512):
    n = cols.shape[0]
    b256 = C_MLA // 256
    b128 = C_MLA // 128
    full = lambda shape: pl.BlockSpec(shape, lambda i: (0, 0))
    return pl.pallas_call(
        _mla_prep_kernel,
        grid=(n // tm,),
        in_specs=[pl.BlockSpec((tm, 256), lambda i: (i, b256)),
                  pl.BlockSpec((tm, 128), lambda i: (i, b128 + 2)),
                  pl.BlockSpec((tm, 128), lambda i: (i, b128 + 3)),
                  pl.BlockSpec((tm, 128), lambda i: (i, b128 + 4)),
                  pl.BlockSpec((tm, 128), lambda i: (i, 0)),
                  pl.BlockSpec((tm, 128), lambda i: (i, 0)),
                  full((1, 256)), full((1, 128)),
                  full((256, 512)), full((256, 512)), full((128, 512)), full((128, 512))],
        out_specs=[pl.BlockSpec((tm, 512), lambda i: (i, 0))] * 3,
        out_shape=[jax.ShapeDtypeStruct((n, 512), BF16)] * 3,
        compiler_params=_cparams(("parallel",)),
        name="mla_prep",
    )(cols, cols, cols, cols, cm, sm, q_norm.reshape(1, 256), kv_norm.reshape(1, 128), wq1, wq2, wk, wv)


def _mla_flash_kernel(q_ref, k_ref, v_ref, o_ref, *, tq, tk):
    i = pl.program_id(1)
    lane = lax.broadcasted_iota(jnp.int32, (tq, 128), 1)
    qpos = i * tq + lax.broadcasted_iota(jnp.int32, (tq, tk), 0)
    kloc = lax.broadcasted_iota(jnp.int32, (tq, tk), 1)
    nfull = (i * tq) // tk
    heads = range(N_HEADS)
    qs = [q_ref[:, h * 128:(h + 1) * 128] for h in heads]

    def step(j, carry, masked):
        ks = pl.multiple_of(j * tk, tk)
        ss = [_dot_nt(qs[h], k_ref[pl.ds(ks, tk), h * 128:(h + 1) * 128]) for h in heads]
        if masked:
            mask = ks + kloc <= qpos
            ss = [jnp.where(mask, s, NEG_INF) for s in ss]
        mns = [jnp.maximum(carry[h][0], jnp.max(ss[h], axis=-1, keepdims=True)) for h in heads]
        ps = [jnp.exp((ss[h] - mns[h]).astype(BF16)) for h in heads]
        als = [jnp.exp(carry[h][0] - mns[h]) for h in heads]
        pvs = [_dot(ps[h], v_ref[pl.ds(ks, tk), h * 128:(h + 1) * 128]) for h in heads]
        return tuple((mns[h], carry[h][1] * als[h] + pvs[h]) for h in heads)

    init = (jnp.full((tq, 1), NEG_INF, F32), jnp.zeros((tq, 128), F32))
    res = lax.fori_loop(0, nfull, functools.partial(step, masked=False), (init,) * N_HEADS)
    res = step(nfull, res, True)
    for p in range(N_PAIRS):
        a0, a1 = res[2 * p][1], res[2 * p + 1][1]
        acc = jnp.where(lane < HEAD_DIM, a0, a1)
        den = jnp.where(lane < HEAD_DIM, pltpu.roll(a0, HEAD_DIM, 1), pltpu.roll(a1, HEAD_DIM, 1))
        o_ref[:, p * 128:(p + 1) * 128] = acc / den


def mla_flash(q3, k3, v3, tq=256, tk=512):
    b, t, _ = q3.shape
    assert tk % tq == 0 and t % tk == 0
    w = N_HEADS * 128
    return pl.pallas_call(
        functools.partial(_mla_flash_kernel, tq=tq, tk=tk),
        grid=(b, t // tq),
        in_specs=[pl.BlockSpec((None, tq, w), lambda b_, i: (b_, i, 0)),
                  pl.BlockSpec((None, t, w), lambda b_, i: (b_, 0, 0)),
                  pl.BlockSpec((None, t, w), lambda b_, i: (b_, 0, 0))],
        out_specs=pl.BlockSpec((None, tq, D_GROUP), lambda b_, i: (b_, i, 0)),
        out_shape=jax.ShapeDtypeStruct((b, t, D_GROUP), F32),
        compiler_params=_cparams(("parallel", "arbitrary")),
        name="mla_flash",
    )(q3, k3, v3)


DILATED_PATTERNS = ((128, 1), (512, 4), (2048, 16))
DIL_BLOCK = 128
DIL_UNROLL = 4


def _rotate_half_heads(x):
    lane = lax.broadcasted_iota(jnp.int32, x.shape, 1)
    width = x.shape[1]
    half = HEAD_DIM // 2
    return jnp.where((lane & (HEAD_DIM - 1)) < half, -pltpu.roll(x, width - half, 1), pltpu.roll(x, half, 1))


def _dil_block(qb, kb, vb, mask, out):
    nq = qb.shape[0]
    lane = lax.broadcasted_iota(jnp.int32, (nq, 128), 1)
    kbb = kb.astype(BF16)
    vbb = vb.astype(BF16)
    hms = [(lane >> 6) == h for h in range(2)]
    ss = [jnp.where(mask, _dot_nt(jnp.where(hm, qb, 0.0).astype(BF16), kbb), NEG_INF) for hm in hms]
    yield
    ms = [jnp.max(s, axis=-1, keepdims=True) for s in ss]
    ps = [jnp.exp(s - m) for s, m in zip(ss, ms)]
    yield
    ls = [jnp.sum(p, axis=-1, keepdims=True) for p in ps]
    ohs = [_dot(p.astype(BF16), vbb) for p in ps]
    yield
    out[0] = jnp.where(hms[0], ohs[0] / ls[0], ohs[1] / ls[1])
    out[1] = jnp.where(hms[0], ms[0] + jnp.log(ls[0]), ms[1] + jnp.log(ls[1]))


def _dil_kernel(qraw_ref, kraw_ref, v_ref, cd_ref, sd_ref, o_ref, q_ref, k_ref, m_ref, z_ref, acc_ref, *, t):
    blk = DIL_BLOCK
    cd = cd_ref[...]
    sd = sd_ref[...]
    qraw = qraw_ref[...]
    kraw = kraw_ref[...]
    q_ref[...] = (qraw * cd + _rotate_half_heads(qraw) * sd) * (HEAD_DIM ** -0.5)
    k_ref[...] = kraw * cd + _rotate_half_heads(kraw) * sd
    for pi, (window, dil) in enumerate(DILATED_PATTERNS):
        assert window // dil == blk
        length = t // dil
        nb = length // blk
        nk = 2 * blk if nb > 1 else blk
        uq = lax.broadcasted_iota(jnp.int32, (blk, nk), 0)
        uk = lax.broadcasted_iota(jnp.int32, (blk, nk), 1)

        def one(it, dil=dil, nb=nb, nk=nk, uq=uq, uk=uk, first=(pi == 0)):
            r = it // nb
            jb = it % nb
            kb0 = jnp.maximum(jb - 1, 0) * blk
            qs = r + dil * blk * jb
            ks = r + dil * kb0
            if dil == 1:
                qidx = pl.ds(pl.multiple_of(qs, blk), blk)
                kidx = pl.ds(pl.multiple_of(ks, blk), nk)
            else:
                qidx = pl.ds(qs, blk, stride=dil)
                kidx = pl.ds(ks, nk, stride=dil)
            dist = (jb * blk + uq) - (kb0 + uk)
            mask = (dist >= 0) & (dist <= blk)
            out = [None, None]
            yield from _dil_block(q_ref[qidx, :], k_ref[kidx, :], v_ref[kidx, :], mask, out)
            o, lse = out
            if first:
                m_ref[qidx, :] = lse
                z_ref[qidx, :] = jnp.ones_like(lse)
                acc_ref[qidx, :] = o
            else:
                m_old = m_ref[qidx, :]
                m_new = jnp.maximum(m_old, lse)
                a = jnp.exp(m_old - m_new)
                b = jnp.exp(lse - m_new)
                m_ref[qidx, :] = m_new
                z_ref[qidx, :] = z_ref[qidx, :] * a + b
                acc_ref[qidx, :] = acc_ref[qidx, :] * a + o * b

        def body(it, carry, one=one):
            _round_robin([one(it * DIL_UNROLL + u) for u in range(DIL_UNROLL)])
            return carry

        assert (dil * nb) % DIL_UNROLL == 0
        lax.fori_loop(0, dil * nb // DIL_UNROLL, body, 0)
    o_ref[...] = acc_ref[...] / z_ref[...]


def dilated_attention(cols3, cd3, sd3):
    b, t, _ = cols3.shape
    assert t % (DIL_BLOCK * DILATED_PATTERNS[-1][1]) == 0
    c0 = C_DIL // 128
    col = lambda j: pl.BlockSpec((None, t, 128), lambda i, p: (i, 0, j + p))
    tab = pl.BlockSpec((None, t, 128), lambda i, p: (i, 0, 0))
    return pl.pallas_call(
        functools.partial(_dil_kernel, t=t),
        grid=(b, 2),
        in_specs=[col(c0), col(c0 + 2), col(c0 + 4), tab, tab],
        out_specs=pl.BlockSpec((None, t, 128), lambda i, p: (i, 0, p)),
        out_shape=jax.ShapeDtypeStruct((b, t, 256), F32),
        scratch_shapes=[pltpu.VMEM((t, 128), F32)] * 5,
        compiler_params=_cparams(("parallel", "parallel")),
        name="dilated_attention",
    )(cols3, cols3, cols3, cd3, sd3)


CHUNK = 64
PAIR = 2
PAIR_LANES = PAIR * HEAD_DIM
N_PAIRS = N_HEADS // PAIR
STACK = PAIR * CHUNK


def _stack_heads(x):
    lane = lax.broadcasted_iota(jnp.int32, x.shape, 1)
    return jnp.concatenate([jnp.where((lane >> 6) == h, x, 0.0) for h in range(PAIR)], axis=0)


def _unstack_heads(y):
    out = y[0:CHUNK]
    for h in range(1, PAIR):
        out = out + y[h * CHUNK:(h + 1) * CHUNK]
    return out


def _round_robin(stages):
    live = list(stages)
    while live:
        nxt = []
        for g in live:
            try:
                next(g)
                nxt.append(g)
            except StopIteration:
                pass
        live = nxt


def _head_mean_matrix(n=PAIR_LANES):
    r = lax.broadcasted_iota(jnp.int32, (n, n), 0)
    c = lax.broadcasted_iota(jnp.int32, (n, n), 1)
    return jnp.where((r >> 6) == (c >> 6), 1.0 / HEAD_DIM, 0.0).astype(BF16)


HGRN_HALVES = (32, 16, 8, 4, 2, 1)


def _hgrn_sum_matrix():
    t = lax.broadcasted_iota(jnp.int32, (CHUNK, CHUNK), 0)
    j = lax.broadcasted_iota(jnp.int32, (CHUNK, CHUNK), 1)
    tri = jnp.where(j <= t, 1.0, 0.0)
    blocks = [tri]
    for half in HGRN_HALVES:
        ref = (t // (2 * half)) * (2 * half) + half - 1
        blocks.append(tri - jnp.where(j <= ref, 1.0, 0.0))
    return jnp.concatenate(blocks, axis=0).astype(BF16)


def _hgrn_kernel(q_ref, f_ref, i_ref, g_ref, lbl_ref, gn_ref, o_ref, st_ref, *, layer, nchunk, nb):
    @pl.when(pl.program_id(1) == 0)
    def _():
        st_ref[...] = jnp.zeros_like(st_ref)

    logits = lbl_ref[...]
    e = jnp.exp(logits - jnp.max(logits, axis=0, keepdims=True))
    sm = e / jnp.sum(e, axis=0, keepdims=True)
    lb = jnp.zeros((1, 256), F32)
    for l in range(1, layer + 1):
        lb = lb + sm[l:l + 1, :]
    gn = gn_ref[...]
    summat = _hgrn_sum_matrix()
    hmean = _head_mean_matrix()
    r = lax.broadcasted_iota(jnp.int32, (STACK, STACK), 0)
    c = lax.broadcasted_iota(jnp.int32, (STACK, STACK), 1)
    same_head = (r >> 6) == (c >> 6)
    rc = r & (CHUNK - 1)
    cc = c & (CHUNK - 1)

    def one(bi, pi, rows):
        ls = pl.ds(pi * PAIR_LANES, PAIR_LANES)
        lbp = lb[:, pi * PAIR_LANES:(pi + 1) * PAIR_LANES]
        qv = q_ref[bi, rows, ls]
        qq = qv * jax.nn.sigmoid(qv)
        forget = lbp + (1.0 - lbp) * jax.nn.sigmoid(f_ref[bi, rows, ls])
        logf = jnp.log(forget)
        kk = 1.0 - forget
        v = i_ref[bi, rows, ls]
        yield
        sums = _split_dot_lhs(summat, logf)
        b = sums[0:CHUNK]
        a = jnp.where(rc == cc, _dot_nt(_stack_heads(qq).astype(BF16), _stack_heads(kk).astype(BF16)), 0.0)
        yield
        for li, half in enumerate(HGRN_HALVES):
            d = sums[(li + 1) * CHUNK:(li + 2) * CHUNK]
            ql = qq * jnp.exp(jnp.minimum(d, 0.0))
            kl = kk * jnp.exp(jnp.minimum(-d, 0.0))
            m = ((rc // (2 * half)) == (cc // (2 * half))) & ((rc % (2 * half)) >= half) & ((cc % (2 * half)) < half)
            a = a + jnp.where(m, _dot_nt(_stack_heads(ql).astype(BF16), _stack_heads(kl).astype(BF16)), 0.0)
            yield
        o = _unstack_heads(_dot(a.astype(BF16), _stack_heads(v).astype(BF16)))
        st = st_ref[bi, pi]
        o = o + _dot_nt((qq * jnp.exp(b)).astype(BF16), st.astype(BF16))
        yield
        b_end = b[CHUNK - 1:CHUNK, :]
        kbar = kk * jnp.exp(b_end - b)
        upd = lax.dot_general(v.astype(BF16), kbar.astype(BF16), (((0,), (0,)), ((), ())), preferred_element_type=F32)
        st_ref[bi, pi] = st * jnp.exp(b_end) + jnp.where(same_head, upd, 0.0)
        ms = _split_dot(o * o, hmean)
        yield
        gv = g_ref[bi, rows, ls]
        gnp = gn[:, pi * PAIR_LANES:(pi + 1) * PAIR_LANES]
        o_ref[bi, rows, ls] = o * lax.rsqrt(ms + RMS_EPS) * gnp * (gv * jax.nn.sigmoid(gv))

    def chunk(ci, carry):
        rows = pl.ds(pl.multiple_of(ci * CHUNK, CHUNK), CHUNK)
        _round_robin([one(bi, pi, rows) for bi in range(nb) for pi in range(N_PAIRS)])
        return carry

    lax.fori_loop(0, nchunk, chunk, 0)


SEQ_PER_STEP = 4


def hgrn(cols3, lb_logits, g_norm, layer, tb=512):
    b, t, _ = cols3.shape
    nb = SEQ_PER_STEP if b % SEQ_PER_STEP == 0 else 1
    c0 = C_HGRN // 256
    spec = lambda j: pl.BlockSpec((nb, tb, 256), lambda b_, i: (b_, i, j))
    depth = lb_logits.shape[0]
    return pl.pallas_call(
        functools.partial(_hgrn_kernel, layer=layer, nchunk=tb // CHUNK, nb=nb),
        grid=(b // nb, t // tb),
        in_specs=[spec(c0), spec(c0 + 1), spec(c0 + 2), spec(c0 + 3),
                  pl.BlockSpec((depth, 256), lambda b_, i: (0, 0)),
                  pl.BlockSpec((1, 256), lambda b_, i: (0, 0))],
        out_specs=spec(0),
        out_shape=jax.ShapeDtypeStruct((b, t, 256), F32),
        scratch_shapes=[pltpu.VMEM((nb, N_PAIRS, STACK, STACK), F32)],
        compiler_params=_cparams(("parallel", "arbitrary")),
        name="hgrn",
    )(cols3, cols3, cols3, cols3, lb_logits, g_norm.reshape(1, 256))


def _head_sum_matrix():
    r = lax.broadcasted_iota(jnp.int32, (256, 256), 0)
    c = lax.broadcasted_iota(jnp.int32, (256, 256), 1)
    return jnp.where((r >> 6) == (c >> 6), 1.0, 0.0).astype(BF16)


def _rwkv_prep_tile(a, last_prev, mu, w0, w2, a0, a2, g2, k_k, k_a, r_k):
    row = lax.broadcasted_iota(jnp.int32, a.shape, 0)
    prev = jnp.where(row == 0, last_prev, pltpu.roll(a, 1, 0))
    xs = a + (prev - a) * mu
    r = xs[:, 0:256]
    k = xs[:, 256:512]
    v = xs[:, 512:768]
    xwa = xs[:, 768:896]
    xg = xs[:, 896:1024]
    z = -(w0 + _dot(jnp.tanh(xwa).astype(BF16), w2))
    softplus = jnp.maximum(z, 0.0) + jnp.log(1.0 + jnp.exp(-jnp.abs(z)))
    lw = -jnp.exp(-softplus - 0.5)
    lr = jax.nn.sigmoid(a0 + _dot(xwa.astype(BF16), a2))
    g = _dot(jax.nn.sigmoid(xg).astype(BF16), g2)
    hsum = _head_sum_matrix()
    kk = k * k_k
    kk = kk / jnp.maximum(jnp.sqrt(_split_dot(kk * kk, hsum)), 1e-12)
    k2 = k * (1.0 + (lr - 1.0) * k_a)
    bonus = _split_dot(r * k2 * r_k, hsum) * v
    return r, lw, k2, v, kk, kk * lr, bonus, g


RWKV_MERGE_HALVES = (2, 4, 8, 16, 32)


def _compact_upper(x, half):
    return jnp.concatenate([x[s + half:s + 2 * half] for s in range(0, STACK, 2 * half)], axis=0)


def _expand_upper(c, half):
    zero = jnp.zeros((half, c.shape[1]), c.dtype)
    return jnp.concatenate([p for i in range(STACK // (2 * half)) for p in (zero, c[i * half:(i + 1) * half])], axis=0)


def _rwkv_kernel(c_ref, p_ref, mu_ref, w0_ref, w2_ref, a0_ref, a2_ref, g2_ref, kk_ref, ka_ref, rk_ref,
                 lnw_ref, lnb_ref, o_ref,
                 r_ref, lw_ref, k_ref, v_ref, kap_ref, at_ref, bonus_ref, g_ref, ht_ref, *, nchunk, nb):
    first = pl.program_id(1) == 0

    @pl.when(first)
    def _():
        ht_ref[...] = jnp.zeros_like(ht_ref)

    for bi in range(nb):
        outs = _rwkv_prep_tile(c_ref[bi], jnp.where(first, 0.0, p_ref[bi, 7:8, :]), mu_ref[...], w0_ref[...],
                               w2_ref[...], a0_ref[...], a2_ref[...], g2_ref[...], kk_ref[...], ka_ref[...],
                               rk_ref[...])
        for ref, val in zip((r_ref, lw_ref, k_ref, v_ref, kap_ref, at_ref, bonus_ref, g_ref), outs):
            ref[bi] = val

    t = lax.broadcasted_iota(jnp.int32, (CHUNK, CHUNK), 0)
    j = lax.broadcasted_iota(jnp.int32, (CHUNK, CHUNK), 1)
    tri = jnp.where(j <= t, 1.0, 0.0).astype(BF16)
    hmean = _head_mean_matrix()
    rr = lax.broadcasted_iota(jnp.int32, (STACK, STACK), 0)
    cc = lax.broadcasted_iota(jnp.int32, (STACK, STACK), 1)
    same_head = (rr >> 6) == (cc >> 6)
    rc = rr & (CHUNK - 1)
    sc = cc & (CHUNK - 1)
    eye = jnp.where(rr == cc, 1.0, 0.0)
    lnw = lnw_ref[...]
    lnb = lnb_ref[...]

    def one(bi, pi, rows):
        ls = pl.ds(pi * PAIR_LANES, PAIR_LANES)
        lw = lw_ref[bi, rows, ls]
        lc = _split_dot_lhs(tri, lw)
        l_end = lc[CHUNK - 1:CHUNK, :]
        kap = kap_ref[bi, rows, ls]
        at = at_ref[bi, rows, ls]
        k2 = k_ref[bi, rows, ls]
        v = v_ref[bi, rows, ls]
        inv = jnp.exp(-lc)
        fwd = jnp.exp(l_end - lc)
        s_kap = _stack_heads(kap * jnp.exp(lc - lw)).astype(BF16)
        s_r32 = _stack_heads(r_ref[bi, rows, ls] * jnp.exp(lc))
        s_r = s_r32.astype(BF16)
        s_a = _stack_heads(at * inv).astype(BF16)
        s_k = _stack_heads(k2 * inv).astype(BF16)
        s_v = _stack_heads(v).astype(BF16)
        strict = rc > sc
        incl = rc >= sc
        yield
        a_ka = jnp.where(strict, _dot_nt(s_kap, s_a), 0.0)
        a_kk = jnp.where(strict, _dot_nt(s_kap, s_k), 0.0).astype(BF16)
        m1 = ((rc >> 1) == (sc >> 1)) & strict
        tinv = eye - jnp.where(m1, a_ka, 0.0)
        yield
        a_ra = jnp.where(incl, _dot_nt(s_r, s_a), 0.0).astype(BF16)
        a_rk = jnp.where(incl, _dot_nt(s_r, s_k), 0.0).astype(BF16)
        for half in RWKV_MERGE_HALVES:
            m = ((rc // (2 * half)) == (sc // (2 * half))) & ((rc % (2 * half)) >= half) & ((sc % (2 * half)) < half)
            off = jnp.where(m, a_ka, 0.0)
            tb = tinv.astype(BF16)
            if half % 8 == 0:
                x = _expand_upper(_dot(_compact_upper(off, half).astype(BF16), tb), half).astype(BF16)
                yield
                tinv = tinv - _expand_upper(_dot(_compact_upper(tinv, half).astype(BF16), x), half)
            else:
                x = _dot(off.astype(BF16), tb).astype(BF16)
                yield
                tinv = tinv - _dot(tb, x)
            yield
        tb = tinv.astype(BF16)
        kt_st = _dot(tb, s_kap)
        w_st = _dot(a_kk, s_v).astype(BF16)
        yield
        vt_st = _dot(tb, w_st)
        rt = _unstack_heads(s_r32 - _dot(a_ra, kt_st.astype(BF16)))
        yield
        y0 = _unstack_heads(_dot(a_rk, s_v) - _dot(a_ra, vt_st.astype(BF16)))
        kt = _unstack_heads(kt_st)
        vt = _unstack_heads(vt_st)
        ht = ht_ref[bi, pi]
        htb = ht.astype(BF16)
        yield
        y = y0 + _dot_nt(rt.astype(BF16), htb)
        u = vt + _dot_nt(kt.astype(BF16), htb)
        yield
        lhs = jnp.concatenate([v, u], axis=0).astype(BF16)
        rhs = jnp.concatenate([k2 * fwd, -(at * fwd)], axis=0).astype(BF16)
        upd = lax.dot_general(lhs, rhs, (((0,), (0,)), ((), ())), preferred_element_type=F32)
        ht_ref[bi, pi] = ht * jnp.exp(l_end) + jnp.where(same_head, upd, 0.0)
        mean = _split_dot(y, hmean)
        yield
        yc = y - mean
        var = _split_dot(yc * yc, hmean)
        yield
        lanes = slice(pi * PAIR_LANES, (pi + 1) * PAIR_LANES)
        yn = yc * lax.rsqrt(var + RWKV_LN_EPS) * lnw[:, lanes] + lnb[:, lanes]
        o_ref[bi, rows, ls] = (yn + bonus_ref[bi, rows, ls]) * g_ref[bi, rows, ls]

    def chunk(ci, carry):
        rows = pl.ds(pl.multiple_of(ci * CHUNK, CHUNK), CHUNK)
        _round_robin([one(bi, pi, rows) for bi in range(nb) for pi in range(N_PAIRS)])
        return carry

    lax.fori_loop(0, nchunk, chunk, 0)


def rwkv_mix(cols3, mu, w0, w2p, a0, a2p, g2, k_k, k_a, r_k, ln_w, ln_b, tb=512):
    b, t, _ = cols3.shape
    nb = SEQ_PER_STEP if b % SEQ_PER_STEP == 0 else 1
    spec = pl.BlockSpec((nb, tb, 256), lambda b_, i: (b_, i, 0))
    full = lambda shape: pl.BlockSpec(shape, lambda b_, i: (0, 0))
    row = lambda x: x.reshape(1, -1)
    return pl.pallas_call(
        functools.partial(_rwkv_kernel, nchunk=tb // CHUNK, nb=nb),
        grid=(b // nb, t // tb),
        in_specs=[pl.BlockSpec((nb, tb, 1024), lambda b_, i: (b_, i, C_RWKV // 1024)),
                  pl.BlockSpec((nb, 8, 1024), lambda b_, i: (b_, jnp.maximum(i * (tb // 8) - 1, 0), C_RWKV // 1024)),
                  full((1, 1024)), full((1, 256)), full((128, 256)), full((1, 256)), full((128, 256)),
                  full((128, 256)), full((1, 256)), full((1, 256)), full((1, 256)), full((1, 256)), full((1, 256))],
        out_specs=spec,
        out_shape=jax.ShapeDtypeStruct((b, t, 256), F32),
        scratch_shapes=[pltpu.VMEM((nb, tb, 256), F32)] * 8 + [pltpu.VMEM((nb, N_PAIRS, STACK, STACK), F32)],
        compiler_params=_cparams(("parallel", "arbitrary")),
        name="rwkv_mix",
    )(cols3, cols3, row(mu), row(w0), w2p, row(a0), a2p, g2, row(k_k), row(k_a), row(r_k), row(ln_w), row(ln_b))


def _rot_cols(w, half):
    g = w.reshape(w.shape[0], -1, 2, half)
    return jnp.concatenate([-g[:, :, 1:2, :], g[:, :, 0:1, :]], axis=2).reshape(w.shape)


def pack_w_in(w):
    d = w.shape[0]
    rw = w[:, 0:1024]
    dq, dk, dv = w[:, 1024:1280], w[:, 1280:1536], w[:, 1536:1792]
    cq, ckv, kr = w[:, 1792:2048], w[:, 2048:2176], w[:, 2176:2208]
    hg = w[:, 2208:3232]
    z = lambda n: jnp.zeros((d, n), w.dtype)
    kra = jnp.concatenate([z(MLA_NOPE), kr, z(128 - MLA_NOPE - MLA_ROPE)], axis=1)
    krb = jnp.concatenate([z(MLA_NOPE), _rot_cols(kr, MLA_ROPE // 2), z(128 - MLA_NOPE - MLA_ROPE)], axis=1)
    packed = jnp.concatenate([rw, dq, dk, dv, hg, cq, ckv, kra, krb], axis=1)
    return packed.astype(BF16)


def pack_mla(w_uq, w_ukv):
    rq = w_uq.shape[0]
    q = w_uq.reshape(rq, N_HEADS, MLA_NOPE + MLA_ROPE)
    nope, rope = q[..., :MLA_NOPE], q[..., MLA_NOPE:]
    pad = jnp.zeros((rq, N_HEADS, 128 - MLA_NOPE - MLA_ROPE), w_uq.dtype)
    wq1 = jnp.concatenate([nope, rope, pad], axis=-1).reshape(rq, N_HEADS * 128)
    rrot = _rot_cols(rope.reshape(rq, N_HEADS * MLA_ROPE), MLA_ROPE // 2).reshape(rq, N_HEADS, MLA_ROPE)
    wq2 = jnp.concatenate([jnp.zeros_like(nope), rrot, pad], axis=-1).reshape(rq, N_HEADS * 128)
    rk = w_ukv.shape[0]
    kv = w_ukv.reshape(rk, N_HEADS, MLA_NOPE + HEAD_DIM)
    wk = jnp.concatenate([kv[..., :MLA_NOPE], jnp.zeros((rk, N_HEADS, 128 - MLA_NOPE), w_ukv.dtype)],
                         axis=-1).reshape(rk, N_HEADS * 128)
    vz = jnp.zeros((rk, HEAD_DIM), w_ukv.dtype)
    wv = jnp.concatenate([x for h in range(N_HEADS)
                          for x in ((kv[:, h, MLA_NOPE:], vz) if h % 2 == 0 else (vz, kv[:, h, MLA_NOPE:]))], axis=1)
    return wq1.astype(BF16), wq2.astype(BF16), wk.astype(BF16), wv.astype(BF16)


def kernel(x, positions, mix_norm, w_in, w_out, rwkv_mu, rwkv_w0, rwkv_w2, rwkv_a0, rwkv_a2, rwkv_g2, rwkv_k_k,
           rwkv_k_a, rwkv_r_k, rwkv_ln_w, rwkv_ln_b, mla_q_norm, mla_w_uq, mla_kv_norm, mla_w_ukv, hgrn_lb_logits,
           hgrn_g_norm, ffn_norm, ffn_w1, ffn_w3, ffn_w2, moe_router, moe_w1, moe_w3, moe_w2, final_norm):
    b, t, d = x.shape
    n = b * t
    depth = w_in.shape[0]
    x2 = x.reshape(n, d)
    cd, sd, cm, sm = rope_tables(positions.reshape(n, 1))
    lora_pad = jnp.zeros((rwkv_w2.shape[1], D_GROUP), F32)
    fused_final = False
    for layer in range(depth):
        cols = in_proj(x2, mix_norm[layer], pack_w_in(w_in[layer]))
        cols3 = cols.reshape(b, t, N_PACK)
        y_a = rwkv_mix(cols3, rwkv_mu[layer], rwkv_w0[layer],
                       jnp.concatenate([rwkv_w2[layer], lora_pad], axis=0).astype(BF16), rwkv_a0[layer],
                       jnp.concatenate([lora_pad, rwkv_a2[layer]], axis=0).astype(BF16),
                       rwkv_g2[layer].astype(BF16), rwkv_k_k[layer], rwkv_k_a[layer], rwkv_r_k[layer],
                       rwkv_ln_w[layer], rwkv_ln_b[layer])
        y_b = dilated_attention(cols3, cd.reshape(b, t, -1), sd.reshape(b, t, -1))
        q_m, k_m, v_m = mla_prep(cols, cm, sm, mla_q_norm[layer], mla_kv_norm[layer],
                                 *pack_mla(mla_w_uq[layer], mla_w_ukv[layer]))
        y_c = mla_flash(q_m.reshape(b, t, -1), k_m.reshape(b, t, -1), v_m.reshape(b, t, -1))
        y_d = hgrn(cols3, hgrn_lb_logits, hgrn_g_norm[layer], layer)
        ys = [y.reshape(n, D_GROUP) for y in (y_a, y_b, y_c, y_d)]
        x2, h = out_proj(x2, ys, w_out[layer].astype(BF16), ffn_norm[layer])
        j = layer // 2
        if layer % 2 == 0:
            x2 = ffn_dense(h, x2, ffn_w1[j].astype(BF16), ffn_w3[j].astype(BF16), ffn_w2[j].astype(BF16))
        else:
            route, hp = moe_route(x2, ffn_norm[layer], jnp.pad(moe_router[j], ((0, 0), (0, 128 - N_EXPERTS))))
            rows_pad = TOP_K * n + N_EXPERTS * MOE_TILE
            pos, tile_expert, tile_valid, clear_tiles = moe_plan(route, rows_pad)
            xs = moe_scatter(pos, clear_tiles, hp, rows_pad)
            ys = moe_experts(tile_expert, tile_valid, xs, moe_w1[j].astype(BF16), moe_w3[j].astype(BF16),
                             moe_w2[j].astype(BF16))
            fused_final = layer == depth - 1
            x2 = moe_combine(pos, x2, route, ys, final_norm, fused_final)
    if not fused_final:
        x2 = final_rms(x2, final_norm)
    return x2.reshape(b, t, d)
```

```python
import functools

import jax
import jax.numpy as jnp
from jax import lax
from jax.experimental import pallas as pl
from jax.experimental.pallas import tpu as pltpu

F32 = jnp.float32
BF16 = jnp.bfloat16

D_GROUP = 256
HEAD_DIM = 64
N_HEADS = 4
RMS_EPS = 1e-6
RWKV_LN_EPS = 64e-5
ROPE_THETA = 10000.0
NEG_INF = -1e30
N_EXPERTS = 8
MLA_ROPE = 32
MLA_NOPE = 64

VMEM_LIMIT = 56 * 1024 * 1024

C_RWKV = 0
C_DIL = 1024
C_HGRN = 1792
C_MLA = 2816
N_PACK = 3456


def _cparams(sem, vmem=VMEM_LIMIT):
    return pltpu.CompilerParams(dimension_semantics=sem, vmem_limit_bytes=vmem)


def _rms(x, w):
    return x * lax.rsqrt(jnp.mean(x * x, axis=-1, keepdims=True) + RMS_EPS) * w


def _dot(a, b):
    return jnp.dot(a, b, preferred_element_type=F32)


def _dot_nt(a, b):
    return lax.dot_general(a, b, (((1,), (1,)), ((), ())), preferred_element_type=F32)


def _split_dot(a_f32, b_bf16):
    hi = a_f32.astype(BF16)
    lo = (a_f32 - hi.astype(F32)).astype(BF16)
    return _dot(hi, b_bf16) + _dot(lo, b_bf16)


def _split_dot_lhs(a_bf16, b_f32):
    hi = b_f32.astype(BF16)
    lo = (b_f32 - hi.astype(F32)).astype(BF16)
    return _dot(a_bf16, hi) + _dot(a_bf16, lo)


def _rope_kernel(pos_ref, cd_ref, sd_ref, cm_ref, sm_ref):
    pos = pos_ref[...].astype(F32)
    tm = pos.shape[0]
    lane = lax.broadcasted_iota(jnp.int32, (tm, 128), 1)
    nd = HEAD_DIM // 2
    nm = MLA_ROPE // 2
    is_d = lane < nd
    is_m = (lane >= nd) & (lane < nd + nm)
    expo = jnp.where(is_d, lane.astype(F32) * (-2.0 / HEAD_DIM), (lane - nd).astype(F32) * (-2.0 / MLA_ROPE))
    ang = pos * jnp.exp(expo * jnp.log(ROPE_THETA))
    cos = jnp.cos(ang)
    sin = jnp.sin(ang)
    roll = lambda x, s: pltpu.roll(x, s, 1)

    def tile_d(x):
        x0 = jnp.where(is_d, x, 0.0)
        return x0 + roll(x0, nd) + roll(x0, 2 * nd) + roll(x0, 3 * nd)

    def place_m(x):
        x0 = jnp.where(is_m, x, 0.0)
        return roll(x0, MLA_NOPE - nd) + roll(x0, MLA_NOPE - nd + nm)

    cd_ref[...] = tile_d(cos)
    sd_ref[...] = tile_d(sin)
    cm_ref[...] = place_m(cos) + jnp.where(lane < MLA_NOPE, 1.0, 0.0)
    sm_ref[...] = place_m(sin)


def rope_tables(pos_col, tm=512):
    n = pos_col.shape[0]
    return pl.pallas_call(
        _rope_kernel,
        grid=(n // tm,),
        in_specs=[pl.BlockSpec((tm, 1), lambda i: (i, 0))],
        out_specs=[pl.BlockSpec((tm, 128), lambda i: (i, 0))] * 4,
        out_shape=[jax.ShapeDtypeStruct((n, 128), F32)] * 4,
        compiler_params=_cparams(("parallel",)),
        name="rope_tables",
    )(pos_col)


def _in_proj_kernel(x_ref, nw_ref, w_ref, o_ref):
    h = _rms(x_ref[...], nw_ref[...]).astype(BF16)
    o_ref[...] = _dot(h, w_ref[...])


def in_proj(x2, norm_w, w_packed, tm=512):
    n, d = x2.shape
    nc = w_packed.shape[1]
    return pl.pallas_call(
        _in_proj_kernel,
        grid=(n // tm,),
        in_specs=[pl.BlockSpec((tm, d), lambda i: (i, 0)),
                  pl.BlockSpec((1, d), lambda i: (0, 0)),
                  pl.BlockSpec((d, nc), lambda i: (0, 0))],
        out_specs=pl.BlockSpec((tm, nc), lambda i: (i, 0)),
        out_shape=jax.ShapeDtypeStruct((n, nc), F32),
        compiler_params=_cparams(("parallel",)),
        name="in_proj",
    )(x2, norm_w.reshape(1, d), w_packed)


def _out_proj_kernel(x_ref, ya_ref, yb_ref, yc_ref, yd_ref, w_ref, nw_ref, xo_ref, *maybe_h_ref):
    acc = x_ref[...]
    for g, y_ref in enumerate((ya_ref, yb_ref, yc_ref, yd_ref)):
        acc = acc + _dot(y_ref[...].astype(BF16), w_ref[g * D_GROUP:(g + 1) * D_GROUP, :])
    xo_ref[...] = acc
    for h_ref in maybe_h_ref:
        h_ref[...] = _rms(acc, nw_ref[...]).astype(BF16)


def out_proj(x2, ys, w_out_bf16, norm_w, with_norm, tm=512):
    n, d = x2.shape
    yspec = pl.BlockSpec((tm, D_GROUP), lambda i: (i, 0))
    ospec = pl.BlockSpec((tm, d), lambda i: (i, 0))
    return pl.pallas_call(
        _out_proj_kernel,
        grid=(n // tm,),
        in_specs=[pl.BlockSpec((tm, d), lambda i: (i, 0)), yspec, yspec, yspec, yspec,
                  pl.BlockSpec((d, d), lambda i: (0, 0)),
                  pl.BlockSpec((1, d), lambda i: (0, 0))],
        out_specs=[ospec, ospec] if with_norm else [ospec],
        out_shape=[jax.ShapeDtypeStruct((n, d), F32)] + ([jax.ShapeDtypeStruct((n, d), BF16)] if with_norm else []),
        compiler_params=_cparams(("parallel",)),
        name="out_proj",
    )(x2, *ys, w_out_bf16, norm_w.reshape(1, d))


def _ffn_kernel(h_ref, x_ref, w1_ref, w3_ref, w2_ref, o_ref):
    h = h_ref[...]
    a = _dot(h, w1_ref[...])
    b = _dot(h, w3_ref[...])
    act = (a * jax.nn.sigmoid(a) * b).astype(BF16)
    o_ref[...] = x_ref[...] + _dot(act, w2_ref[...])


def ffn_dense(h_bf16, x2, w1, w3, w2, tm=512):
    n, d = x2.shape
    f = w1.shape[1]
    resident = lambda shape: pl.BlockSpec(shape, lambda i: (0, 0), pipeline_mode=pl.Buffered(1))
    return pl.pallas_call(
        _ffn_kernel,
        grid=(n // tm,),
        in_specs=[pl.BlockSpec((tm, d), lambda i: (i, 0)),
                  pl.BlockSpec((tm, d), lambda i: (i, 0)),
                  resident((d, f)), resident((d, f)), resident((f, d))],
        out_specs=pl.BlockSpec((tm, d), lambda i: (i, 0)),
        out_shape=jax.ShapeDtypeStruct((n, d), F32),
        compiler_params=_cparams(("parallel",)),
        name="ffn_dense",
    )(h_bf16, x2, w1, w3, w2)


MOE_TILE = 512
TOP_K = 2
MOE_DMA_TOKENS = 1024


def _router_kernel(x_ref, nw_ref, wr_ref, route_ref, hp_ref):
    h = _rms(x_ref[...], nw_ref[...])
    w = wr_ref[...]
    w_hi = w.astype(BF16)
    w_lo = (w - w_hi.astype(F32)).astype(BF16)
    h_hi = h.astype(BF16)
    h_lo = (h - h_hi.astype(F32)).astype(BF16)
    logits = _dot(h_hi, w_hi) + (_dot(h_hi, w_lo) + _dot(h_lo, w_hi))
    lane = lax.broadcasted_iota(jnp.int32, logits.shape, 1)
    logits = jnp.where(lane < N_EXPERTS, logits, -jnp.inf)
    m1 = jnp.max(logits, axis=-1, keepdims=True)
    i1 = jnp.min(jnp.where(logits == m1, lane, 128), axis=-1, keepdims=True)
    rest = jnp.where(lane == i1, -jnp.inf, logits)
    m2 = jnp.max(rest, axis=-1, keepdims=True)
    i2 = jnp.min(jnp.where(rest == m2, lane, 128), axis=-1, keepdims=True)
    e2 = jnp.exp(m2 - m1)
    g1 = 1.0 / (1.0 + e2)
    g2 = e2 * g1
    route_ref[...] = jnp.where(lane == 0, i1.astype(F32), jnp.where(lane == 1, i2.astype(F32),
                               jnp.where(lane == 2, g1, jnp.where(lane == 3, g2, 0.0))))
    hp_ref[...] = h


def moe_route(x2, norm_w, w_router_pad, tm=512):
    n, d = x2.shape
    return pl.pallas_call(
        _router_kernel,
        grid=(n // tm,),
        in_specs=[pl.BlockSpec((tm, d), lambda i: (i, 0)),
                  pl.BlockSpec((1, d), lambda i: (0, 0)),
                  pl.BlockSpec((d, 128), lambda i: (0, 0))],
        out_specs=[pl.BlockSpec((tm, 128), lambda i: (i, 0)), pl.BlockSpec((tm, d), lambda i: (i, 0))],
        out_shape=[jax.ShapeDtypeStruct((n, 128), F32), jax.ShapeDtypeStruct((n, d), F32)],
        compiler_params=_cparams(("parallel",)),
        name="moe_router",
    )(x2, norm_w.reshape(1, d), w_router_pad)


def moe_plan(route, n_rows_pad):
    e = route[:, :TOP_K].astype(jnp.int32)
    onehot = (e.reshape(-1, 1) == jnp.arange(N_EXPERTS, dtype=jnp.int32)[None, :]).astype(jnp.int32)
    csum = jnp.cumsum(onehot, axis=0)
    counts = csum[-1]
    rank = jnp.sum((csum - onehot) * onehot, axis=1)
    gsz = ((counts + MOE_TILE - 1) // MOE_TILE) * MOE_TILE
    gend = jnp.cumsum(gsz)
    goff = gend - gsz
    pos = (goff[e.reshape(-1)] + rank).astype(jnp.int32)
    tile_start = jnp.arange(n_rows_pad // MOE_TILE, dtype=jnp.int32) * MOE_TILE
    tile_expert = jnp.minimum(jnp.sum((tile_start[:, None] >= gend[None, :]).astype(jnp.int32), axis=1),
                              N_EXPERTS - 1).astype(jnp.int32)
    tile_valid = (tile_start < gend[-1]).astype(jnp.int32)
    last_tile = jnp.where(gsz > 0, gend - MOE_TILE, -1)
    slack = gend[-1] + jnp.arange(N_EXPERTS, dtype=jnp.int32) * MOE_TILE
    clear_tiles = jnp.concatenate([last_tile, jnp.where(slack < n_rows_pad, slack, -1)]).astype(jnp.int32)
    return pos, tile_expert, tile_valid, clear_tiles


def _moe_scatter_kernel(pos_ref, last_ref, hp_ref, xs_ref, zero_ref, sem, zsem):
    i = pl.program_id(0)
    ts = hp_ref.shape[0]

    @pl.when(i == 0)
    def _():
        zero_ref[...] = jnp.zeros_like(zero_ref)

        def clear(e):
            start = pl.multiple_of(jnp.maximum(last_ref[e], 0), MOE_TILE)
            return pltpu.make_async_copy(zero_ref, xs_ref.at[pl.ds(start, MOE_TILE), :], zsem)

        for e in range(2 * N_EXPERTS):
            @pl.when(last_ref[e] >= 0)
            def _(e=e):
                clear(e).start()
        for e in range(2 * N_EXPERTS):
            @pl.when(last_ref[e] >= 0)
            def _(e=e):
                clear(e).wait()

    def copy(r, slot):
        dst = pos_ref[(i * ts + r) * TOP_K + slot]
        return pltpu.make_async_copy(hp_ref.at[pl.ds(r, 1), :], xs_ref.at[pl.ds(dst, 1), :], sem)

    def issue(r, carry):
        for slot in range(TOP_K):
            copy(r, slot).start(priority=slot)
        return carry

    lax.fori_loop(0, ts, issue, 0, unroll=8)
    for slot in range(TOP_K):
        pltpu.make_async_copy(hp_ref, xs_ref.at[pl.ds(0, ts), :], sem).wait()


def moe_scatter(pos, clear_tiles, hp, n_rows_pad, ts=MOE_DMA_TOKENS):
    n, dh = hp.shape
    return pl.pallas_call(
        _moe_scatter_kernel,
        grid_spec=pltpu.PrefetchScalarGridSpec(
            num_scalar_prefetch=2,
            grid=(n // ts,),
            in_specs=[pl.BlockSpec((ts, dh), lambda i, pos_, last_: (i, 0))],
            out_specs=pl.BlockSpec(memory_space=pl.ANY),
            scratch_shapes=[pltpu.VMEM((MOE_TILE, dh), hp.dtype), pltpu.SemaphoreType.DMA, pltpu.SemaphoreType.DMA],
        ),
        out_shape=jax.ShapeDtypeStruct((n_rows_pad, dh), hp.dtype),
        compiler_params=_cparams(("arbitrary",)),
        name="moe_scatter",
    )(pos, clear_tiles, hp)


def _moe_expert_kernel(te_ref, tv_ref, xs_ref, w1_ref, w3_ref, w2_ref, ys_ref, h_ref, acc_ref):
    i = pl.program_id(0)
    j = pl.program_id(1)

    @pl.when(tv_ref[i] > 0)
    def _():
        @pl.when(j == 0)
        def _():
            h_ref[...] = xs_ref[...].astype(BF16)
            acc_ref[...] = jnp.zeros_like(acc_ref)

        h = h_ref[...]
        a = _dot(h, w1_ref[...])
        b = _dot(h, w3_ref[...])
        act = (a * jax.nn.sigmoid(a) * b).astype(BF16)
        acc_ref[...] += _dot(act, w2_ref[...])

        @pl.when(j == pl.num_programs(1) - 1)
        def _():
            ys_ref[...] = acc_ref[...]

    @pl.when((tv_ref[i] == 0) & (j == pl.num_programs(1) - 1))
    def _():
        ys_ref[...] = jnp.zeros_like(ys_ref)


def moe_experts(tile_expert, tile_valid, xs, w1, w3, w2, tf=1792):
    rows, d = xs.shape
    f = w1.shape[2]
    tm = MOE_TILE
    return pl.pallas_call(
        _moe_expert_kernel,
        grid_spec=pltpu.PrefetchScalarGridSpec(
            num_scalar_prefetch=2,
            grid=(rows // tm, f // tf),
            in_specs=[pl.BlockSpec((tm, d), lambda i, j, te, tv: (i, 0)),
                      pl.BlockSpec((None, d, tf), lambda i, j, te, tv: (te[i], 0, j * tv[i])),
                      pl.BlockSpec((None, d, tf), lambda i, j, te, tv: (te[i], 0, j * tv[i])),
                      pl.BlockSpec((None, tf, d), lambda i, j, te, tv: (te[i], j * tv[i], 0))],
            out_specs=pl.BlockSpec((tm, d), lambda i, j, te, tv: (i, 0)),
            scratch_shapes=[pltpu.VMEM((tm, d), BF16), pltpu.VMEM((tm, d), F32)],
        ),
        out_shape=jax.ShapeDtypeStruct((rows, d), F32),
        compiler_params=_cparams(("arbitrary", "arbitrary")),
        name="moe_experts",
    )(tile_expert, tile_valid, xs, w1, w3, w2)


def _moe_combine_kernel(pos_ref, x_ref, route_ref, nw_ref, ys_ref, o_ref, buf_ref, sem, *, final):
    i = pl.program_id(0)
    tc = x_ref.shape[0]

    def copy(r, slot):
        src = pos_ref[(i * tc + r) * TOP_K + slot]
        return pltpu.make_async_copy(ys_ref.at[pl.ds(src, 1), :], buf_ref.at[slot, pl.ds(r, 1), :], sem)

    def issue(r, carry):
        for slot in range(TOP_K):
            copy(r, slot).start(priority=slot)
        return carry

    lax.fori_loop(0, tc, issue, 0, unroll=8)
    for slot in range(TOP_K):
        pltpu.make_async_copy(ys_ref.at[pl.ds(0, tc), :], buf_ref.at[slot], sem).wait()
    route = route_ref[...]
    g1 = route[:, 2:3]
    g2 = route[:, 3:4]
    out = x_ref[...] + g1 * buf_ref[0] + g2 * buf_ref[1]
    if final:
        out = _rms(out, nw_ref[...])
    o_ref[...] = out


def moe_combine(pos, x2, route, ys, norm_w, final, tc=MOE_DMA_TOKENS):
    n, d = x2.shape
    return pl.pallas_call(
        functools.partial(_moe_combine_kernel, final=final),
        grid_spec=pltpu.PrefetchScalarGridSpec(
            num_scalar_prefetch=1,
            grid=(n // tc,),
            in_specs=[pl.BlockSpec((tc, d), lambda i, pos_: (i, 0)),
                      pl.BlockSpec((tc, 128), lambda i, pos_: (i, 0)),
                      pl.BlockSpec((1, d), lambda i, pos_: (0, 0)),
                      pl.BlockSpec(memory_space=pl.ANY)],
            out_specs=pl.BlockSpec((tc, d), lambda i, pos_: (i, 0)),
            scratch_shapes=[pltpu.VMEM((TOP_K, tc, d), F32), pltpu.SemaphoreType.DMA],
        ),
        out_shape=jax.ShapeDtypeStruct((n, d), F32),
        compiler_params=_cparams(("arbitrary",)),
        name="moe_combine",
    )(pos, x2, route, norm_w.reshape(1, d), ys)


def _final_norm_kernel(x_ref, nw_ref, o_ref):
    o_ref[...] = _rms(x_ref[...], nw_ref[...])


def final_rms(x2, norm_w, tm=1024):
    n, d = x2.shape
    return pl.pallas_call(
        _final_norm_kernel,
        grid=(n // tm,),
        in_specs=[pl.BlockSpec((tm, d), lambda i: (i, 0)), pl.BlockSpec((1, d), lambda i: (0, 0))],
        out_specs=pl.BlockSpec((tm, d), lambda i: (i, 0)),
        out_shape=jax.ShapeDtypeStruct((n, d), F32),
        compiler_params=_cparams(("parallel",)),
        name="final_norm",
    )(x2, norm_w.reshape(1, d))


def _mla_prep_kernel(cq_ref, ckv_ref, kra_ref, krb_ref, cm_ref, sm_ref, qn_ref, kvn_ref,
                     wq1_ref, wq2_ref, wk_ref, wv_ref, q_ref, k_ref, v_ref):
    qn = _rms(cq_ref[...], qn_ref[...]).astype(BF16)
    kvn = _rms(ckv_ref[...], kvn_ref[...]).astype(BF16)
    cm = cm_ref[...]
    sm = sm_ref[...]
    cm4 = jnp.concatenate([cm] * N_HEADS, axis=1)
    sm4 = jnp.concatenate([sm] * N_HEADS, axis=1)
    scale = (MLA_NOPE + MLA_ROPE) ** -0.5
    q = (_dot(qn, wq1_ref[...]) * cm4 + _dot(qn, wq2_ref[...]) * sm4) * scale
    q_ref[...] = q.astype(BF16)
    kr = kra_ref[...] * cm + krb_ref[...] * sm
    k = _dot(kvn, wk_ref[...]) + jnp.concatenate([kr] * N_HEADS, axis=1)
    k_ref[...] = k.astype(BF16)
    lane = lax.broadcasted_iota(jnp.int32, (cm.shape[0], N_HEADS * 128), 1)
    ones = jnp.where(((lane >> 6) & 1) == (((lane >> 7) & 1) ^ 1), 1.0, 0.0)
    v_ref[...] = (_dot(kvn, wv_ref[...]) + ones).astype(BF16)


def mla_prep(cols, cm, sm, q_norm, kv_norm, wq1, wq2, wk, wv, tm=512):
    n = cols.shape[0]
    b256 = C_MLA // 256
    b128 = C_MLA // 128
    full = lambda shape: pl.BlockSpec(shape, lambda i: (0, 0))
    return pl.pallas_call(
        _mla_prep_kernel,
        grid=(n // tm,),
        in_specs=[pl.BlockSpec((tm, 256), lambda i: (i, b256)),
                  pl.BlockSpec((tm, 128), lambda i: (i, b128 + 2)),
                  pl.BlockSpec((tm, 128), lambda i: (i, b128 + 3)),
                  pl.BlockSpec((tm, 128), lambda i: (i, b128 + 4)),
                  pl.BlockSpec((tm, 128), lambda i: (i, 0)),
                  pl.BlockSpec((tm, 128), lambda i: (i, 0)),
                  full((1, 256)), full((1, 128)),
                  full((256, 512)), full((256, 512)), full((128, 512)), full((128, 512))],
        out_specs=[pl.BlockSpec((tm, 512), lambda i: (i, 0))] * 3,
        out_shape=[jax.ShapeDtypeStruct((n, 512), BF16)] * 3,
        compiler_params=_cparams(("parallel",)),
        name="mla_prep",
    )(cols, cols, cols, cols, cm, sm, q_norm.reshape(1, 256), kv_norm.reshape(1, 128), wq1, wq2, wk, wv)


def _mla_flash_kernel(q_ref, k_ref, v_ref, o_ref, *, tq, tk):
    i = pl.program_id(1)
    lane = lax.broadcasted_iota(jnp.int32, (tq, 128), 1)
    qpos = i * tq + lax.broadcasted_iota(jnp.int32, (tq, tk), 0)
    kloc = lax.broadcasted_iota(jnp.int32, (tq, tk), 1)
    nfull = (i * tq) // tk
    heads = range(N_HEADS)
    qs = [q_ref[:, h * 128:(h + 1) * 128] for h in heads]

    def step(j, carry, masked):
        ks = pl.multiple_of(j * tk, tk)
        ss = [_dot_nt(qs[h], k_ref[pl.ds(ks, tk), h * 128:(h + 1) * 128]) for h in heads]
        if masked:
            mask = ks + kloc <= qpos
            ss = [jnp.where(mask, s, NEG_INF) for s in ss]
        mns = [jnp.maximum(carry[h][0], jnp.max(ss[h], axis=-1, keepdims=True)) for h in heads]
        ps = [jnp.exp((ss[h] - mns[h]).astype(BF16)) for h in heads]
        als = [jnp.exp(carry[h][0] - mns[h]) for h in heads]
        pvs = [_dot(ps[h], v_ref[pl.ds(ks, tk), h * 128:(h + 1) * 128]) for h in heads]
        return tuple((mns[h], carry[h][1] * als[h] + pvs[h]) for h in heads)

    init = (jnp.full((tq, 1), NEG_INF, F32), jnp.zeros((tq, 128), F32))
    res = lax.fori_loop(0, nfull, functools.partial(step, masked=False), (init,) * N_HEADS)
    res = step(nfull, res, True)
    for p in range(N_PAIRS):
        a0, a1 = res[2 * p][1], res[2 * p + 1][1]
        acc = jnp.where(lane < HEAD_DIM, a0, a1)
        den = jnp.where(lane < HEAD_DIM, pltpu.roll(a0, HEAD_DIM, 1), pltpu.roll(a1, HEAD_DIM, 1))
        o_ref[:, p * 128:(p + 1) * 128] = acc / den


def mla_flash(q3, k3, v3, tq=256, tk=512):
    b, t, _ = q3.shape
    assert tk % tq == 0 and t % tk == 0
    w = N_HEADS * 128
    return pl.pallas_call(
        functools.partial(_mla_flash_kernel, tq=tq, tk=tk),
        grid=(b, t // tq),
        in_specs=[pl.BlockSpec((None, tq, w), lambda b_, i: (b_, i, 0)),
                  pl.BlockSpec((None, t, w), lambda b_, i: (b_, 0, 0)),
                  pl.BlockSpec((None, t, w), lambda b_, i: (b_, 0, 0))],
        out_specs=pl.BlockSpec((None, tq, D_GROUP), lambda b_, i: (b_, i, 0)),
        out_shape=jax.ShapeDtypeStruct((b, t, D_GROUP), F32),
        compiler_params=_cparams(("parallel", "arbitrary")),
        name="mla_flash",
    )(q3, k3, v3)


DILATED_PATTERNS = ((128, 1), (512, 4), (2048, 16))
DIL_BLOCK = 128
DIL_UNROLL = 4


def _rotate_half_heads(x):
    lane = lax.broadcasted_iota(jnp.int32, x.shape, 1)
    width = x.shape[1]
    half = HEAD_DIM // 2
    return jnp.where((lane & (HEAD_DIM - 1)) < half, -pltpu.roll(x, width - half, 1), pltpu.roll(x, half, 1))


def _dil_block(qb, kb, vb, mask, out):
    nq = qb.shape[0]
    lane = lax.broadcasted_iota(jnp.int32, (nq, 128), 1)
    kbb = kb.astype(BF16)
    vbb = vb.astype(BF16)
    hms = [(lane >> 6) == h for h in range(2)]
    ss = [jnp.where(mask, _dot_nt(jnp.where(hm, qb, 0.0).astype(BF16), kbb), NEG_INF) for hm in hms]
    yield
    ms = [jnp.max(s, axis=-1, keepdims=True) for s in ss]
    ps = [jnp.exp(s - m) for s, m in zip(ss, ms)]
    yield
    ls = [jnp.sum(p, axis=-1, keepdims=True) for p in ps]
    ohs = [_dot(p.astype(BF16), vbb) for p in ps]
    yield
    out[0] = jnp.where(hms[0], ohs[0] / ls[0], ohs[1] / ls[1])
    out[1] = jnp.where(hms[0], ms[0] + jnp.log(ls[0]), ms[1] + jnp.log(ls[1]))


def _dil_kernel(qraw_ref, kraw_ref, v_ref, cd_ref, sd_ref, o_ref, q_ref, k_ref, m_ref, z_ref, acc_ref, *, t):
    blk = DIL_BLOCK
    cd = cd_ref[...]
    sd = sd_ref[...]
    qraw = qraw_ref[...]
    kraw = kraw_ref[...]
    q_ref[...] = (qraw * cd + _rotate_half_heads(qraw) * sd) * (HEAD_DIM ** -0.5)
    k_ref[...] = kraw * cd + _rotate_half_heads(kraw) * sd
    for pi, (window, dil) in enumerate(DILATED_PATTERNS):
        assert window // dil == blk
        length = t // dil
        nb = length // blk
        nk = 2 * blk if nb > 1 else blk
        uq = lax.broadcasted_iota(jnp.int32, (blk, nk), 0)
        uk = lax.broadcasted_iota(jnp.int32, (blk, nk), 1)

        def one(it, dil=dil, nb=nb, nk=nk, uq=uq, uk=uk, first=(pi == 0)):
            r = it // nb
            jb = it % nb
            kb0 = jnp.maximum(jb - 1, 0) * blk
            qs = r + dil * blk * jb
            ks = r + dil * kb0
            if dil == 1:
                qidx = pl.ds(pl.multiple_of(qs, blk), blk)
                kidx = pl.ds(pl.multiple_of(ks, blk), nk)
            else:
                qidx = pl.ds(qs, blk, stride=dil)
                kidx = pl.ds(ks, nk, stride=dil)
            dist = (jb * blk + uq) - (kb0 + uk)
            mask = (dist >= 0) & (dist <= blk)
            out = [None, None]
            yield from _dil_block(q_ref[qidx, :], k_ref[kidx, :], v_ref[kidx, :], mask, out)
            o, lse = out
            if first:
                m_ref[qidx, :] = lse
                z_ref[qidx, :] = jnp.ones_like(lse)
                acc_ref[qidx, :] = o
            else:
                m_old = m_ref[qidx, :]
                m_new = jnp.maximum(m_old, lse)
                a = jnp.exp(m_old - m_new)
                b = jnp.exp(lse - m_new)
                m_ref[qidx, :] = m_new
                z_ref[qidx, :] = z_ref[qidx, :] * a + b
                acc_ref[qidx, :] = acc_ref[qidx, :] * a + o * b

        def body(it, carry, one=one):
            _round_robin([one(it * DIL_UNROLL + u) for u in range(DIL_UNROLL)])
            return carry

        assert (dil * nb) % DIL_UNROLL == 0
        lax.fori_loop(0, dil * nb // DIL_UNROLL, body, 0)
    o_ref[...] = acc_ref[...] / z_ref[...]


def dilated_attention(cols3, cd3, sd3):
    b, t, _ = cols3.shape
    assert t % (DIL_BLOCK * DILATED_PATTERNS[-1][1]) == 0
    c0 = C_DIL // 128
    col = lambda j: pl.BlockSpec((None, t, 128), lambda i, p: (i, 0, j + p))
    tab = pl.BlockSpec((None, t, 128), lambda i, p: (i, 0, 0))
    return pl.pallas_call(
        functools.partial(_dil_kernel, t=t),
        grid=(b, 2),
        in_specs=[col(c0), col(c0 + 2), col(c0 + 4), tab, tab],
        out_specs=pl.BlockSpec((None, t, 128), lambda i, p: (i, 0, p)),
        out_shape=jax.ShapeDtypeStruct((b, t, 256), F32),
        scratch_shapes=[pltpu.VMEM((t, 128), F32)] * 5,
        compiler_params=_cparams(("parallel", "parallel")),
        name="dilated_attention",
    )(cols3, cols3, cols3, cd3, sd3)


CHUNK = 64
PAIR = 2
PAIR_LANES = PAIR * HEAD_DIM
N_PAIRS = N_HEADS // PAIR
STACK = PAIR * CHUNK


def _stack_heads(x):
    lane = lax.broadcasted_iota(jnp.int32, x.shape, 1)
    return jnp.concatenate([jnp.where((lane >> 6) == h, x, 0.0) for h in range(PAIR)], axis=0)


def _unstack_heads(y):
    out = y[0:CHUNK]
    for h in range(1, PAIR):
        out = out + y[h * CHUNK:(h + 1) * CHUNK]
    return out


def _round_robin(stages):
    live = list(stages)
    while live:
        nxt = []
        for g in live:
            try:
                next(g)
                nxt.append(g)
            except StopIteration:
                pass
        live = nxt


def _head_mean_matrix(n=PAIR_LANES):
    r = lax.broadcasted_iota(jnp.int32, (n, n), 0)
    c = lax.broadcasted_iota(jnp.int32, (n, n), 1)
    return jnp.where((r >> 6) == (c >> 6), 1.0 / HEAD_DIM, 0.0).astype(BF16)


HGRN_HALVES = (32, 16, 8, 4, 2, 1)


def _hgrn_sum_matrix():
    t = lax.broadcasted_iota(jnp.int32, (CHUNK, CHUNK), 0)
    j = lax.broadcasted_iota(jnp.int32, (CHUNK, CHUNK), 1)
    tri = jnp.where(j <= t, 1.0, 0.0)
    blocks = [tri]
    for half in HGRN_HALVES:
        ref = (t // (2 * half)) * (2 * half) + half - 1
        blocks.append(tri - jnp.where(j <= ref, 1.0, 0.0))
    return jnp.concatenate(blocks, axis=0).astype(BF16)


def _hgrn_kernel(q_ref, f_ref, i_ref, g_ref, lbl_ref, gn_ref, o_ref, st_ref, *, layer, nchunk, nb):
    @pl.when(pl.program_id(1) == 0)
    def _():
        st_ref[...] = jnp.zeros_like(st_ref)

    logits = lbl_ref[...]
    e = jnp.exp(logits - jnp.max(logits, axis=0, keepdims=True))
    sm = e / jnp.sum(e, axis=0, keepdims=True)
    lb = jnp.zeros((1, 256), F32)
    for l in range(1, layer + 1):
        lb = lb + sm[l:l + 1, :]
    gn = gn_ref[...]
    summat = _hgrn_sum_matrix()
    hmean = _head_mean_matrix()
    r = lax.broadcasted_iota(jnp.int32, (STACK, STACK), 0)
    c = lax.broadcasted_iota(jnp.int32, (STACK, STACK), 1)
    same_head = (r >> 6) == (c >> 6)
    rc = r & (CHUNK - 1)
    cc = c & (CHUNK - 1)

    def one(bi, pi, rows):
        ls = pl.ds(pi * PAIR_LANES, PAIR_LANES)
        lbp = lb[:, pi * PAIR_LANES:(pi + 1) * PAIR_LANES]
        qv = q_ref[bi, rows, ls]
        qq = qv * jax.nn.sigmoid(qv)
        forget = lbp + (1.0 - lbp) * jax.nn.sigmoid(f_ref[bi, rows, ls])
        logf = jnp.log(forget)
        kk = 1.0 - forget
        v = i_ref[bi, rows, ls]
        yield
        sums = _split_dot_lhs(summat, logf)
        b = sums[0:CHUNK]
        a = jnp.where(rc == cc, _dot_nt(_stack_heads(qq).astype(BF16), _stack_heads(kk).astype(BF16)), 0.0)
        yield
        for li, half in enumerate(HGRN_HALVES):
            d = sums[(li + 1) * CHUNK:(li + 2) * CHUNK]
            ql = qq * jnp.exp(jnp.minimum(d, 0.0))
            kl = kk * jnp.exp(jnp.minimum(-d, 0.0))
            m = ((rc // (2 * half)) == (cc // (2 * half))) & ((rc % (2 * half)) >= half) & ((cc % (2 * half)) < half)
            a = a + jnp.where(m, _dot_nt(_stack_heads(ql).astype(BF16), _stack_heads(kl).astype(BF16)), 0.0)
            yield
        o = _unstack_heads(_dot(a.astype(BF16), _stack_heads(v).astype(BF16)))
        st = st_ref[bi, pi]
        o = o + _dot_nt((qq * jnp.exp(b)).astype(BF16), st.astype(BF16))
        yield
        b_end = b[CHUNK - 1:CHUNK, :]
        kbar = kk * jnp.exp(b_end - b)
        upd = lax.dot_general(v.astype(BF16), kbar.astype(BF16), (((0,), (0,)), ((), ())), preferred_element_type=F32)
        st_ref[bi, pi] = st * jnp.exp(b_end) + jnp.where(same_head, upd, 0.0)
        ms = _split_dot(o * o, hmean)
        yield
        gv = g_ref[bi, rows, ls]
        gnp = gn[:, pi * PAIR_LANES:(pi + 1) * PAIR_LANES]
        o_ref[bi, rows, ls] = o * lax.rsqrt(ms + RMS_EPS) * gnp * (gv * jax.nn.sigmoid(gv))

    def chunk(ci, carry):
        rows = pl.ds(pl.multiple_of(ci * CHUNK, CHUNK), CHUNK)
        _round_robin([one(bi, pi, rows) for bi in range(nb) for pi in range(N_PAIRS)])
        return carry

    lax.fori_loop(0, nchunk, chunk, 0)


SEQ_PER_STEP = 4


def hgrn(cols3, lb_logits, g_norm, layer, tb=512):
    b, t, _ = cols3.shape
    nb = SEQ_PER_STEP if b % SEQ_PER_STEP == 0 else 1
    c0 = C_HGRN // 256
    spec = lambda j: pl.BlockSpec((nb, tb, 256), lambda b_, i: (b_, i, j))
    depth = lb_logits.shape[0]
    return pl.pallas_call(
        functools.partial(_hgrn_kernel, layer=layer, nchunk=tb // CHUNK, nb=nb),
        grid=(b // nb, t // tb),
        in_specs=[spec(c0), spec(c0 + 1), spec(c0 + 2), spec(c0 + 3),
                  pl.BlockSpec((depth, 256), lambda b_, i: (0, 0)),
                  pl.BlockSpec((1, 256), lambda b_, i: (0, 0))],
        out_specs=spec(0),
        out_shape=jax.ShapeDtypeStruct((b, t, 256), F32),
        scratch_shapes=[pltpu.VMEM((nb, N_PAIRS, STACK, STACK), F32)],
        compiler_params=_cparams(("parallel", "arbitrary")),
        name="hgrn",
    )(cols3, cols3, cols3, cols3, lb_logits, g_norm.reshape(1, 256))


def _head_sum_matrix():
    r = lax.broadcasted_iota(jnp.int32, (256, 256), 0)
    c = lax.broadcasted_iota(jnp.int32, (256, 256), 1)
    return jnp.where((r >> 6) == (c >> 6), 1.0, 0.0).astype(BF16)


def _rwkv_prep_tile(a, last_prev, mu, w0, w2, a0, a2, g2, k_k, k_a, r_k):
    row = lax.broadcasted_iota(jnp.int32, a.shape, 0)
    prev = jnp.where(row == 0, last_prev, pltpu.roll(a, 1, 0))
    xs = a + (prev - a) * mu
    r = xs[:, 0:256]
    k = xs[:, 256:512]
    v = xs[:, 512:768]
    xwa = xs[:, 768:896]
    xg = xs[:, 896:1024]
    z = -(w0 + _dot(jnp.tanh(xwa).astype(BF16), w2))
    softplus = jnp.maximum(z, 0.0) + jnp.log(1.0 + jnp.exp(-jnp.abs(z)))
    lw = -jnp.exp(-softplus - 0.5)
    lr = jax.nn.sigmoid(a0 + _dot(xwa.astype(BF16), a2))
    g = _dot(jax.nn.sigmoid(xg).astype(BF16), g2)
    hsum = _head_sum_matrix()
    kk = k * k_k
    kk = kk / jnp.maximum(jnp.sqrt(_split_dot(kk * kk, hsum)), 1e-12)
    k2 = k * (1.0 + (lr - 1.0) * k_a)
    bonus = _split_dot(r * k2 * r_k, hsum) * v
    return r, lw, k2, v, kk, kk * lr, bonus, g


RWKV_MERGE_HALVES = (2, 4, 8, 16, 32)


def _compact_upper(x, half):
    return jnp.concatenate([x[s + half:s + 2 * half] for s in range(0, STACK, 2 * half)], axis=0)


def _expand_upper(c, half):
    zero = jnp.zeros((half, c.shape[1]), c.dtype)
    return jnp.concatenate([p for i in range(STACK // (2 * half)) for p in (zero, c[i * half:(i + 1) * half])], axis=0)


def _rwkv_kernel(c_ref, p_ref, mu_ref, w0_ref, w2_ref, a0_ref, a2_ref, g2_ref, kk_ref, ka_ref, rk_ref,
                 lnw_ref, lnb_ref, o_ref,
                 r_ref, lw_ref, k_ref, v_ref, kap_ref, at_ref, bonus_ref, g_ref, ht_ref, *, nchunk, nb):
    first = pl.program_id(1) == 0

    @pl.when(first)
    def _():
        ht_ref[...] = jnp.zeros_like(ht_ref)

    for bi in range(nb):
        outs = _rwkv_prep_tile(c_ref[bi], jnp.where(first, 0.0, p_ref[bi, 7:8, :]), mu_ref[...], w0_ref[...],
                               w2_ref[...], a0_ref[...], a2_ref[...], g2_ref[...], kk_ref[...], ka_ref[...],
                               rk_ref[...])
        for ref, val in zip((r_ref, lw_ref, k_ref, v_ref, kap_ref, at_ref, bonus_ref, g_ref), outs):
            ref[bi] = val

    t = lax.broadcasted_iota(jnp.int32, (CHUNK, CHUNK), 0)
    j = lax.broadcasted_iota(jnp.int32, (CHUNK, CHUNK), 1)
    tri = jnp.where(j <= t, 1.0, 0.0).astype(BF16)
    hmean = _head_mean_matrix()
    rr = lax.broadcasted_iota(jnp.int32, (STACK, STACK), 0)
    cc = lax.broadcasted_iota(jnp.int32, (STACK, STACK), 1)
    same_head = (rr >> 6) == (cc >> 6)
    rc = rr & (CHUNK - 1)
    sc = cc & (CHUNK - 1)
    eye = jnp.where(rr == cc, 1.0, 0.0)
    lnw = lnw_ref[...]
    lnb = lnb_ref[...]

    def one(bi, pi, rows):
        ls = pl.ds(pi * PAIR_LANES, PAIR_LANES)
        lw = lw_ref[bi, rows, ls]
        lc = _split_dot_lhs(tri, lw)
        l_end = lc[CHUNK - 1:CHUNK, :]
        kap = kap_ref[bi, rows, ls]
        at = at_ref[bi, rows, ls]
        k2 = k_ref[bi, rows, ls]
        v = v_ref[bi, rows, ls]
        inv = jnp.exp(-lc)
        fwd = jnp.exp(l_end - lc)
        s_kap = _stack_heads(kap * jnp.exp(lc - lw)).astype(BF16)
        s_r32 = _stack_heads(r_ref[bi, rows, ls] * jnp.exp(lc))
        s_r = s_r32.astype(BF16)
        s_a = _stack_heads(at * inv).astype(BF16)
        s_k = _stack_heads(k2 * inv).astype(BF16)
        s_v = _stack_heads(v).astype(BF16)
        strict = rc > sc
        incl = rc >= sc
        yield
        a_ka = jnp.where(strict, _dot_nt(s_kap, s_a), 0.0)
        a_kk = jnp.where(strict, _dot_nt(s_kap, s_k), 0.0).astype(BF16)
        m1 = ((rc >> 1) == (sc >> 1)) & strict
        tinv = eye - jnp.where(m1, a_ka, 0.0)
        yield
        a_ra = jnp.where(incl, _dot_nt(s_r, s_a), 0.0).astype(BF16)
        a_rk = jnp.where(incl, _dot_nt(s_r, s_k), 0.0).astype(BF16)
        for half in RWKV_MERGE_HALVES:
            m = ((rc // (2 * half)) == (sc // (2 * half))) & ((rc % (2 * half)) >= half) & ((sc % (2 * half)) < half)
            off = jnp.where(m, a_ka, 0.0)
            tb = tinv.astype(BF16)
            if half % 8 == 0:
                x = _expand_upper(_dot(_compact_upper(off, half).astype(BF16), tb), half).astype(BF16)
                yield
                tinv = tinv - _expand_upper(_dot(_compact_upper(tinv, half).astype(BF16), x), half)
            else:
                x = _dot(off.astype(BF16), tb).astype(BF16)
                yield
                tinv = tinv - _dot(tb, x)
            yield
        tb = tinv.astype(BF16)
        kt_st = _dot(tb, s_kap)
        w_st = _dot(a_kk, s_v).astype(BF16)
        yield
        vt_st = _dot(tb, w_st)
        rt = _unstack_heads(s_r32 - _dot(a_ra, kt_st.astype(BF16)))
        yield
        y0 = _unstack_heads(_dot(a_rk, s_v) - _dot(a_ra, vt_st.astype(BF16)))
        kt = _unstack_heads(kt_st)
        vt = _unstack_heads(vt_st)
        ht = ht_ref[bi, pi]
        htb = ht.astype(BF16)
        yield
        y = y0 + _dot_nt(rt.astype(BF16), htb)
        u = vt + _dot_nt(kt.astype(BF16), htb)
        yield
        lhs = jnp.concatenate([v, u], axis=0).astype(BF16)
        rhs = jnp.concatenate([k2 * fwd, -(at * fwd)], axis=0).astype(BF16)
        upd = lax.dot_general(lhs, rhs, (((0,), (0,)), ((), ())), preferred_element_type=F32)
        ht_ref[bi, pi] = ht * jnp.exp(l_end) + jnp.where(same_head, upd, 0.0)
        mean = _split_dot(y, hmean)
        yield
        yc = y - mean
        var = _split_dot(yc * yc, hmean)
        yield
        lanes = slice(pi * PAIR_LANES, (pi + 1) * PAIR_LANES)
        yn = yc * lax.rsqrt(var + RWKV_LN_EPS) * lnw[:, lanes] + lnb[:, lanes]
        o_ref[bi, rows, ls] = (yn + bonus_ref[bi, rows, ls]) * g_ref[bi, rows, ls]

    def chunk(ci, carry):
        rows = pl.ds(pl.multiple_of(ci * CHUNK, CHUNK), CHUNK)
        _round_robin([one(bi, pi, rows) for bi in range(nb) for pi in range(N_PAIRS)])
        return carry

    lax.fori_loop(0, nchunk, chunk, 0)


def rwkv_mix(cols3, mu, w0, w2p, a0, a2p, g2, k_k, k_a, r_k, ln_w, ln_b, tb=512):
    b, t, _ = cols3.shape
    nb = SEQ_PER_STEP if b % SEQ_PER_STEP == 0 else 1
    spec = pl.BlockSpec((nb, tb, 256), lambda b_, i: (b_, i, 0))
    full = lambda shape: pl.BlockSpec(shape, lambda b_, i: (0, 0))
    row = lambda x: x.reshape(1, -1)
    return pl.pallas_call(
        functools.partial(_rwkv_kernel, nchunk=tb // CHUNK, nb=nb),
        grid=(b // nb, t // tb),
        in_specs=[pl.BlockSpec((nb, tb, 1024), lambda b_, i: (b_, i, C_RWKV // 1024)),
                  pl.BlockSpec((nb, 8, 1024), lambda b_, i: (b_, jnp.maximum(i * (tb // 8) - 1, 0), C_RWKV // 1024)),
                  full((1, 1024)), full((1, 256)), full((128, 256)), full((1, 256)), full((128, 256)),
                  full((128, 256)), full((1, 256)), full((1, 256)), full((1, 256)), full((1, 256)), full((1, 256))],
        out_specs=spec,
        out_shape=jax.ShapeDtypeStruct((b, t, 256), F32),
        scratch_shapes=[pltpu.VMEM((nb, tb, 256), F32)] * 8 + [pltpu.VMEM((nb, N_PAIRS, STACK, STACK), F32)],
        compiler_params=_cparams(("parallel", "arbitrary")),
        name="rwkv_mix",
    )(cols3, cols3, row(mu), row(w0), w2p, row(a0), a2p, g2, row(k_k), row(k_a), row(r_k), row(ln_w), row(ln_b))


def _rot_cols(w, half):
    g = w.reshape(w.shape[0], -1, 2, half)
    return jnp.concatenate([-g[:, :, 1:2, :], g[:, :, 0:1, :]], axis=2).reshape(w.shape)


def pack_w_in(w):
    d = w.shape[0]
    rw = w[:, 0:1024]
    dq, dk, dv = w[:, 1024:1280], w[:, 1280:1536], w[:, 1536:1792]
    cq, ckv, kr = w[:, 1792:2048], w[:, 2048:2176], w[:, 2176:2208]
    hg = w[:, 2208:3232]
    z = lambda n: jnp.zeros((d, n), w.dtype)
    kra = jnp.concatenate([z(MLA_NOPE), kr, z(128 - MLA_NOPE - MLA_ROPE)], axis=1)
    krb = jnp.concatenate([z(MLA_NOPE), _rot_cols(kr, MLA_ROPE // 2), z(128 - MLA_NOPE - MLA_ROPE)], axis=1)
    packed = jnp.concatenate([rw, dq, dk, dv, hg, cq, ckv, kra, krb], axis=1)
    return packed.astype(BF16)


def pack_mla(w_uq, w_ukv):
    rq = w_uq.shape[0]
    q = w_uq.reshape(rq, N_HEADS, MLA_NOPE + MLA_ROPE)
    nope, rope = q[..., :MLA_NOPE], q[..., MLA_NOPE:]
    pad = jnp.zeros((rq, N_HEADS, 128 - MLA_NOPE - MLA_ROPE), w_uq.dtype)
    wq1 = jnp.concatenate([nope, rope, pad], axis=-1).reshape(rq, N_HEADS * 128)
    rrot = _rot_cols(rope.reshape(rq, N_HEADS * MLA_ROPE), MLA_ROPE // 2).reshape(rq, N_HEADS, MLA_ROPE)
    wq2 = jnp.concatenate([jnp.zeros_like(nope), rrot, pad], axis=-1).reshape(rq, N_HEADS * 128)
    rk = w_ukv.shape[0]
    kv = w_ukv.reshape(rk, N_HEADS, MLA_NOPE + HEAD_DIM)
    wk = jnp.concatenate([kv[..., :MLA_NOPE], jnp.zeros((rk, N_HEADS, 128 - MLA_NOPE), w_ukv.dtype)],
                         axis=-1).reshape(rk, N_HEADS * 128)
    vz = jnp.zeros((rk, HEAD_DIM), w_ukv.dtype)
    wv = jnp.concatenate([x for h in range(N_HEADS)
                          for x in ((kv[:, h, MLA_NOPE:], vz) if h % 2 == 0 else (vz, kv[:, h, MLA_NOPE:]))], axis=1)
    return wq1.astype(BF16), wq2.astype(BF16), wk.astype(BF16), wv.astype(BF16)


def kernel(x, positions, mix_norm, w_in, w_out, rwkv_mu, rwkv_w0, rwkv_w2, rwkv_a0, rwkv_a2, rwkv_g2, rwkv_k_k,
           rwkv_k_a, rwkv_r_k, rwkv_ln_w, rwkv_ln_b, mla_q_norm, mla_w_uq, mla_kv_norm, mla_w_ukv, hgrn_lb_logits,
           hgrn_g_norm, ffn_norm, ffn_w1, ffn_w3, ffn_w2, moe_router, moe_w1, moe_w3, moe_w2, final_norm):
    b, t, d = x.shape
    n = b * t
    depth = w_in.shape[0]
    x2 = x.reshape(n, d)
    cd, sd, cm, sm = rope_tables(positions.reshape(n, 1))
    lora_pad = jnp.zeros((rwkv_w2.shape[1], D_GROUP), F32)
    fused_final = False
    for layer in range(depth):
        cols = in_proj(x2, mix_norm[layer], pack_w_in(w_in[layer]))
        cols3 = cols.reshape(b, t, N_PACK)
        y_a = rwkv_mix(cols3, rwkv_mu[layer], rwkv_w0[layer],
                       jnp.concatenate([rwkv_w2[layer], lora_pad], axis=0).astype(BF16), rwkv_a0[layer],
                       jnp.concatenate([lora_pad, rwkv_a2[layer]], axis=0).astype(BF16),
                       rwkv_g2[layer].astype(BF16), rwkv_k_k[layer], rwkv_k_a[layer], rwkv_r_k[layer],
                       rwkv_ln_w[layer], rwkv_ln_b[layer])
        y_b = dilated_attention(cols3, cd.reshape(b, t, -1), sd.reshape(b, t, -1))
        q_m, k_m, v_m = mla_prep(cols, cm, sm, mla_q_norm[layer], mla_kv_norm[layer],
                                 *pack_mla(mla_w_uq[layer], mla_w_ukv[layer]))
        y_c = mla_flash(q_m.reshape(b, t, -1), k_m.reshape(b, t, -1), v_m.reshape(b, t, -1))
        y_d = hgrn(cols3, hgrn_lb_logits, hgrn_g_norm[layer], layer)
        ys = [y.reshape(n, D_GROUP) for y in (y_a, y_b, y_c, y_d)]
        j = layer // 2
        if layer % 2 == 0:
            x2, h = out_proj(x2, ys, w_out[layer].astype(BF16), ffn_norm[layer], True)
            x2 = ffn_dense(h, x2, ffn_w1[j].astype(BF16), ffn_w3[j].astype(BF16), ffn_w2[j].astype(BF16))
        else:
            x2, = out_proj(x2, ys, w_out[layer].astype(BF16), ffn_norm[layer], False)
            route, hp = moe_route(x2, ffn_norm[layer], jnp.pad(moe_router[j], ((0, 0), (0, 128 - N_EXPERTS))))
            rows_pad = TOP_K * n + N_EXPERTS * MOE_TILE
            pos, tile_expert, tile_valid, clear_tiles = moe_plan(route, rows_pad)
            xs = moe_scatter(pos, clear_tiles, hp, rows_pad)
            ys = moe_experts(tile_expert, tile_valid, xs, moe_w1[j].astype(BF16), moe_w3[j].astype(BF16),
                             moe_w2[j].astype(BF16))
            fused_final = layer == depth - 1
            x2 = moe_combine(pos, x2, route, ys, final_norm, fused_final)
    if not fused_final:
        x2 = final_rms(x2, final_norm)
    return x2.reshape(b, t, d)
```

```python
import functools

import jax
import jax.numpy as jnp
from jax import lax
from jax.experimental import pallas as pl
from jax.experimental.pallas import tpu as pltpu

F32 = jnp.float32
BF16 = jnp.bfloat16

D_GROUP = 256
HEAD_DIM = 64
N_HEADS = 4
RMS_EPS = 1e-6
RWKV_LN_EPS = 64e-5
ROPE_THETA = 10000.0
NEG_INF = -1e30
N_EXPERTS = 8
MLA_ROPE = 32
MLA_NOPE = 64

VMEM_LIMIT = 56 * 1024 * 1024

C_RWKV = 0
C_DIL = 1024
C_HGRN = 1792
C_MLA = 2816
N_PACK = 3456


def _cparams(sem, vmem=VMEM_LIMIT):
    return pltpu.CompilerParams(dimension_semantics=sem, vmem_limit_bytes=vmem)


def _rms(x, w):
    return x * lax.rsqrt(jnp.mean(x * x, axis=-1, keepdims=True) + RMS_EPS) * w


def _dot(a, b):
    return jnp.dot(a, b, preferred_element_type=F32)


def _dot_nt(a, b):
    return lax.dot_general(a, b, (((1,), (1,)), ((), ())), preferred_element_type=F32)


def _split_dot(a_f32, b_bf16):
    hi = a_f32.astype(BF16)
    lo = (a_f32 - hi.astype(F32)).astype(BF16)
    return _dot(hi, b_bf16) + _dot(lo, b_bf16)


def _split_dot_lhs(a_bf16, b_f32):
    hi = b_f32.astype(BF16)
    lo = (b_f32 - hi.astype(F32)).astype(BF16)
    return _dot(a_bf16, hi) + _dot(a_bf16, lo)


def _rope_kernel(pos_ref, cd_ref, sd_ref, cm_ref, sm_ref):
    pos = pos_ref[...].astype(F32)
    tm = pos.shape[0]
    lane = lax.broadcasted_iota(jnp.int32, (tm, 128), 1)
    nd = HEAD_DIM // 2
    nm = MLA_ROPE // 2
    is_d = lane < nd
    is_m = (lane >= nd) & (lane < nd + nm)
    expo = jnp.where(is_d, lane.astype(F32) * (-2.0 / HEAD_DIM), (lane - nd).astype(F32) * (-2.0 / MLA_ROPE))
    ang = pos * jnp.exp(expo * jnp.log(ROPE_THETA))
    cos = jnp.cos(ang)
    sin = jnp.sin(ang)
    roll = lambda x, s: pltpu.roll(x, s, 1)

    def tile_d(x):
        x0 = jnp.where(is_d, x, 0.0)
        return x0 + roll(x0, nd) + roll(x0, 2 * nd) + roll(x0, 3 * nd)

    def place_m(x):
        x0 = jnp.where(is_m, x, 0.0)
        return roll(x0, MLA_NOPE - nd) + roll(x0, MLA_NOPE - nd + nm)

    cd_ref[...] = tile_d(cos)
    sd_ref[...] = tile_d(sin)
    cm_ref[...] = place_m(cos) + jnp.where(lane < MLA_NOPE, 1.0, 0.0)
    sm_ref[...] = place_m(sin)


def rope_tables(pos_col, tm=512):
    n = pos_col.shape[0]
    return pl.pallas_call(
        _rope_kernel,
        grid=(n // tm,),
        in_specs=[pl.BlockSpec((tm, 1), lambda i: (i, 0))],
        out_specs=[pl.BlockSpec((tm, 128), lambda i: (i, 0))] * 4,
        out_shape=[jax.ShapeDtypeStruct((n, 128), F32)] * 4,
        compiler_params=_cparams(("parallel",)),
        name="rope_tables",
    )(pos_col)


def _in_proj_kernel(x_ref, nw_ref, w_ref, o_ref):
    h = _rms(x_ref[...], nw_ref[...]).astype(BF16)
    o_ref[...] = _dot(h, w_ref[...])


def in_proj(x2, norm_w, w_packed, tm=512):
    n, d = x2.shape
    nc = w_packed.shape[1]
    return pl.pallas_call(
        _in_proj_kernel,
        grid=(n // tm,),
        in_specs=[pl.BlockSpec((tm, d), lambda i: (i, 0)),
                  pl.BlockSpec((1, d), lambda i: (0, 0)),
                  pl.BlockSpec((d, nc), lambda i: (0, 0))],
        out_specs=pl.BlockSpec((tm, nc), lambda i: (i, 0)),
        out_shape=jax.ShapeDtypeStruct((n, nc), F32),
        compiler_params=_cparams(("parallel",)),
        name="in_proj",
    )(x2, norm_w.reshape(1, d), w_packed)


def _out_proj_kernel(x_ref, ya_ref, yb_ref, yc_ref, yd_ref, w_ref, nw_ref, xo_ref, *maybe_h_ref):
    acc = x_ref[...]
    for g, y_ref in enumerate((ya_ref, yb_ref, yc_ref, yd_ref)):
        acc = acc + _dot(y_ref[...].astype(BF16), w_ref[g * D_GROUP:(g + 1) * D_GROUP, :])
    xo_ref[...] = acc
    for h_ref in maybe_h_ref:
        h_ref[...] = _rms(acc, nw_ref[...]).astype(BF16)


def out_proj(x2, ys, w_out_bf16, norm_w, with_norm, tm=512):
    n, d = x2.shape
    yspec = pl.BlockSpec((tm, D_GROUP), lambda i: (i, 0))
    ospec = pl.BlockSpec((tm, d), lambda i: (i, 0))
    return pl.pallas_call(
        _out_proj_kernel,
        grid=(n // tm,),
        in_specs=[pl.BlockSpec((tm, d), lambda i: (i, 0)), yspec, yspec, yspec, yspec,
                  pl.BlockSpec((d, d), lambda i: (0, 0)),
                  pl.BlockSpec((1, d), lambda i: (0, 0))],
        out_specs=[ospec, ospec] if with_norm else [ospec],
        out_shape=[jax.ShapeDtypeStruct((n, d), F32)] + ([jax.ShapeDtypeStruct((n, d), BF16)] if with_norm else []),
        compiler_params=_cparams(("parallel",)),
        name="out_proj",
    )(x2, *ys, w_out_bf16, norm_w.reshape(1, d))


def _ffn_kernel(h_ref, x_ref, w1_ref, w3_ref, w2_ref, o_ref):
    h = h_ref[...]
    a = _dot(h, w1_ref[...])
    b = _dot(h, w3_ref[...])
    act = (a * jax.nn.sigmoid(a) * b).astype(BF16)
    o_ref[...] = x_ref[...] + _dot(act, w2_ref[...])


def ffn_dense(h_bf16, x2, w1, w3, w2, tm=512):
    n, d = x2.shape
    f = w1.shape[1]
    resident = lambda shape: pl.BlockSpec(shape, lambda i: (0, 0), pipeline_mode=pl.Buffered(1))
    return pl.pallas_call(
        _ffn_kernel,
        grid=(n // tm,),
        in_specs=[pl.BlockSpec((tm, d), lambda i: (i, 0)),
                  pl.BlockSpec((tm, d), lambda i: (i, 0)),
                  resident((d, f)), resident((d, f)), resident((f, d))],
        out_specs=pl.BlockSpec((tm, d), lambda i: (i, 0)),
        out_shape=jax.ShapeDtypeStruct((n, d), F32),
        compiler_params=_cparams(("parallel",)),
        name="ffn_dense",
    )(h_bf16, x2, w1, w3, w2)


MOE_TILE = 512
TOP_K = 2
MOE_DMA_TOKENS = 1024


def _router_kernel(x_ref, nw_ref, wr_ref, route_ref, hp_ref):
    h = _rms(x_ref[...], nw_ref[...])
    w = wr_ref[...]
    w_hi = w.astype(BF16)
    w_lo = (w - w_hi.astype(F32)).astype(BF16)
    h_hi = h.astype(BF16)
    h_lo = (h - h_hi.astype(F32)).astype(BF16)
    logits = _dot(h_hi, w_hi) + (_dot(h_hi, w_lo) + _dot(h_lo, w_hi))
    lane = lax.broadcasted_iota(jnp.int32, logits.shape, 1)
    logits = jnp.where(lane < N_EXPERTS, logits, -jnp.inf)
    m1 = jnp.max(logits, axis=-1, keepdims=True)
    i1 = jnp.min(jnp.where(logits == m1, lane, 128), axis=-1, keepdims=True)
    rest = jnp.where(lane == i1, -jnp.inf, logits)
    m2 = jnp.max(rest, axis=-1, keepdims=True)
    i2 = jnp.min(jnp.where(rest == m2, lane, 128), axis=-1, keepdims=True)
    e2 = jnp.exp(m2 - m1)
    g1 = 1.0 / (1.0 + e2)
    g2 = e2 * g1
    route_ref[...] = jnp.where(lane == 0, i1.astype(F32), jnp.where(lane == 1, i2.astype(F32),
                               jnp.where(lane == 2, g1, jnp.where(lane == 3, g2, 0.0))))
    hp_ref[...] = h


def moe_route(x2, norm_w, w_router_pad, tm=512):
    n, d = x2.shape
    return pl.pallas_call(
        _router_kernel,
        grid=(n // tm,),
        in_specs=[pl.BlockSpec((tm, d), lambda i: (i, 0)),
                  pl.BlockSpec((1, d), lambda i: (0, 0)),
                  pl.BlockSpec((d, 128), lambda i: (0, 0))],
        out_specs=[pl.BlockSpec((tm, 128), lambda i: (i, 0)), pl.BlockSpec((tm, d), lambda i: (i, 0))],
        out_shape=[jax.ShapeDtypeStruct((n, 128), F32), jax.ShapeDtypeStruct((n, d), F32)],
        compiler_params=_cparams(("parallel",)),
        name="moe_router",
    )(x2, norm_w.reshape(1, d), w_router_pad)


def moe_plan(route, n_rows_pad):
    e = route[:, :TOP_K].astype(jnp.int32)
    onehot = (e.reshape(-1, 1) == jnp.arange(N_EXPERTS, dtype=jnp.int32)[None, :]).astype(jnp.int32)
    csum = jnp.cumsum(onehot, axis=0)
    counts = csum[-1]
    rank = jnp.sum((csum - onehot) * onehot, axis=1)
    gsz = ((counts + MOE_TILE - 1) // MOE_TILE) * MOE_TILE
    gend = jnp.cumsum(gsz)
    goff = gend - gsz
    pos = (goff[e.reshape(-1)] + rank).astype(jnp.int32)
    tile_start = jnp.arange(n_rows_pad // MOE_TILE, dtype=jnp.int32) * MOE_TILE
    tile_expert = jnp.minimum(jnp.sum((tile_start[:, None] >= gend[None, :]).astype(jnp.int32), axis=1),
                              N_EXPERTS - 1).astype(jnp.int32)
    tile_valid = (tile_start < gend[-1]).astype(jnp.int32)
    last_tile = jnp.where(gsz > 0, gend - MOE_TILE, -1)
    slack = gend[-1] + jnp.arange(N_EXPERTS, dtype=jnp.int32) * MOE_TILE
    clear_tiles = jnp.concatenate([last_tile, jnp.where(slack < n_rows_pad, slack, -1)]).astype(jnp.int32)
    return pos, tile_expert, tile_valid, clear_tiles


def _moe_scatter_kernel(pos_ref, last_ref, hp_ref, xs_ref, zero_ref, sem, zsem):
    i = pl.program_id(0)
    ts = hp_ref.shape[0]

    @pl.when(i == 0)
    def _():
        zero_ref[...] = jnp.zeros_like(zero_ref)

        def clear(e):
            start = pl.multiple_of(jnp.maximum(last_ref[e], 0), MOE_TILE)
            return pltpu.make_async_copy(zero_ref, xs_ref.at[pl.ds(start, MOE_TILE), :], zsem)

        for e in range(2 * N_EXPERTS):
            @pl.when(last_ref[e] >= 0)
            def _(e=e):
                clear(e).start()
        for e in range(2 * N_EXPERTS):
            @pl.when(last_ref[e] >= 0)
            def _(e=e):
                clear(e).wait()

    def copy(r, slot):
        dst = pos_ref[(i * ts + r) * TOP_K + slot]
        return pltpu.make_async_copy(hp_ref.at[pl.ds(r, 1), :], xs_ref.at[pl.ds(dst, 1), :], sem)

    def issue(r, carry):
        for slot in range(TOP_K):
            copy(r, slot).start(priority=slot)
        return carry

    lax.fori_loop(0, ts, issue, 0, unroll=8)
    for slot in range(TOP_K):
        pltpu.make_async_copy(hp_ref, xs_ref.at[pl.ds(0, ts), :], sem).wait()


def moe_scatter(pos, clear_tiles, hp, n_rows_pad, ts=MOE_DMA_TOKENS):
    n, dh = hp.shape
    return pl.pallas_call(
        _moe_scatter_kernel,
        grid_spec=pltpu.PrefetchScalarGridSpec(
            num_scalar_prefetch=2,
            grid=(n // ts,),
            in_specs=[pl.BlockSpec((ts, dh), lambda i, pos_, last_: (i, 0))],
            out_specs=pl.BlockSpec(memory_space=pl.ANY),
            scratch_shapes=[pltpu.VMEM((MOE_TILE, dh), hp.dtype), pltpu.SemaphoreType.DMA, pltpu.SemaphoreType.DMA],
        ),
        out_shape=jax.ShapeDtypeStruct((n_rows_pad, dh), hp.dtype),
        compiler_params=_cparams(("arbitrary",)),
        name="moe_scatter",
    )(pos, clear_tiles, hp)


def _moe_expert_kernel(te_ref, tv_ref, xs_ref, w1_ref, w3_ref, w2_ref, ys_ref, h_ref, acc_ref):
    i = pl.program_id(0)
    j = pl.program_id(1)

    @pl.when(tv_ref[i] > 0)
    def _():
        @pl.when(j == 0)
        def _():
            h_ref[...] = xs_ref[...].astype(BF16)
            acc_ref[...] = jnp.zeros_like(acc_ref)

        h = h_ref[...]
        a = _dot(h, w1_ref[...])
        b = _dot(h, w3_ref[...])
        act = (a * jax.nn.sigmoid(a) * b).astype(BF16)
        acc_ref[...] += _dot(act, w2_ref[...])

        @pl.when(j == pl.num_programs(1) - 1)
        def _():
            ys_ref[...] = acc_ref[...]

    @pl.when((tv_ref[i] == 0) & (j == pl.num_programs(1) - 1))
    def _():
        ys_ref[...] = jnp.zeros_like(ys_ref)


def moe_experts(tile_expert, tile_valid, xs, w1, w3, w2, tf=1792):
    rows, d = xs.shape
    f = w1.shape[2]
    tm = MOE_TILE
    return pl.pallas_call(
        _moe_expert_kernel,
        grid_spec=pltpu.PrefetchScalarGridSpec(
            num_scalar_prefetch=2,
            grid=(rows // tm, f // tf),
            in_specs=[pl.BlockSpec((tm, d), lambda i, j, te, tv: (i, 0)),
                      pl.BlockSpec((None, d, tf), lambda i, j, te, tv: (te[i], 0, j * tv[i])),
                      pl.BlockSpec((None, d, tf), lambda i, j, te, tv: (te[i], 0, j * tv[i])),
                      pl.BlockSpec((None, tf, d), lambda i, j, te, tv: (te[i], j * tv[i], 0))],
            out_specs=pl.BlockSpec((tm, d), lambda i, j, te, tv: (i, 0)),
            scratch_shapes=[pltpu.VMEM((tm, d), BF16), pltpu.VMEM((tm, d), F32)],
        ),
        out_shape=jax.ShapeDtypeStruct((rows, d), F32),
        compiler_params=_cparams(("arbitrary", "arbitrary")),
        name="moe_experts",
    )(tile_expert, tile_valid, xs, w1, w3, w2)


def _moe_combine_kernel(pos_ref, x_ref, route_ref, nw_ref, ys_ref, o_ref, buf_ref, sem, *, final):
    i = pl.program_id(0)
    tc = x_ref.shape[0]

    def gather(step):
        par = step % 2

        def issue(r, carry):
            for slot in range(TOP_K):
                src = pos_ref[(step * tc + r) * TOP_K + slot]
                pltpu.make_async_copy(ys_ref.at[pl.ds(src, 1), :], buf_ref.at[par, slot, pl.ds(r, 1), :],
                                      sem.at[par]).start(priority=slot)
            return carry

        lax.fori_loop(0, tc, issue, 0, unroll=8)

    @pl.when(i == 0)
    def _():
        gather(i)

    @pl.when(i + 1 < pl.num_programs(0))
    def _():
        gather(i + 1)

    par = i % 2
    for slot in range(TOP_K):
        pltpu.make_async_copy(ys_ref.at[pl.ds(0, tc), :], buf_ref.at[par, slot], sem.at[par]).wait()
    route = route_ref[...]
    g1 = route[:, 2:3]
    g2 = route[:, 3:4]
    out = x_ref[...] + g1 * buf_ref[par, 0] + g2 * buf_ref[par, 1]
    if final:
        out = _rms(out, nw_ref[...])
    o_ref[...] = out


def moe_combine(pos, x2, route, ys, norm_w, final, tc=MOE_DMA_TOKENS):
    n, d = x2.shape
    return pl.pallas_call(
        functools.partial(_moe_combine_kernel, final=final),
        grid_spec=pltpu.PrefetchScalarGridSpec(
            num_scalar_prefetch=1,
            grid=(n // tc,),
            in_specs=[pl.BlockSpec((tc, d), lambda i, pos_: (i, 0)),
                      pl.BlockSpec((tc, 128), lambda i, pos_: (i, 0)),
                      pl.BlockSpec((1, d), lambda i, pos_: (0, 0)),
                      pl.BlockSpec(memory_space=pl.ANY)],
            out_specs=pl.BlockSpec((tc, d), lambda i, pos_: (i, 0)),
            scratch_shapes=[pltpu.VMEM((2, TOP_K, tc, d), F32), pltpu.SemaphoreType.DMA((2,))],
        ),
        out_shape=jax.ShapeDtypeStruct((n, d), F32),
        compiler_params=_cparams(("arbitrary",)),
        name="moe_combine",
    )(pos, x2, route, norm_w.reshape(1, d), ys)


def _final_norm_kernel(x_ref, nw_ref, o_ref):
    o_ref[...] = _rms(x_ref[...], nw_ref[...])


def final_rms(x2, norm_w, tm=1024):
    n, d = x2.shape
    return pl.pallas_call(
        _final_norm_kernel,
        grid=(n // tm,),
        in_specs=[pl.BlockSpec((tm, d), lambda i: (i, 0)), pl.BlockSpec((1, d), lambda i: (0, 0))],
        out_specs=pl.BlockSpec((tm, d), lambda i: (i, 0)),
        out_shape=jax.ShapeDtypeStruct((n, d), F32),
        compiler_params=_cparams(("parallel",)),
        name="final_norm",
    )(x2, norm_w.reshape(1, d))


def _mla_prep_kernel(cq_ref, ckv_ref, kra_ref, krb_ref, cm_ref, sm_ref, qn_ref, kvn_ref,
                     wq1_ref, wq2_ref, wk_ref, wv_ref, q_ref, k_ref, v_ref):
    qn = _rms(cq_ref[...], qn_ref[...]).astype(BF16)
    kvn = _rms(ckv_ref[...], kvn_ref[...]).astype(BF16)
    cm = cm_ref[...]
    sm = sm_ref[...]
    cm4 = jnp.concatenate([cm] * N_HEADS, axis=1)
    sm4 = jnp.concatenate([sm] * N_HEADS, axis=1)
    scale = (MLA_NOPE + MLA_ROPE) ** -0.5
    q = (_dot(qn, wq1_ref[...]) * cm4 + _dot(qn, wq2_ref[...]) * sm4) * scale
    q_ref[...] = q.astype(BF16)
    kr = kra_ref[...] * cm + krb_ref[...] * sm
    k = _dot(kvn, wk_ref[...]) + jnp.concatenate([kr] * N_HEADS, axis=1)
    k_ref[...] = k.astype(BF16)
    lane = lax.broadcasted_iota(jnp.int32, (cm.shape[0], N_HEADS * 128), 1)
    ones = jnp.where(((lane >> 6) & 1) == (((lane >> 7) & 1) ^ 1), 1.0, 0.0)
    v_ref[...] = (_dot(kvn, wv_ref[...]) + ones).astype(BF16)


def mla_prep(cols, cm, sm, q_norm, kv_norm, wq1, wq2, wk, wv, tm=512):
    n = cols.shape[0]
    b256 = C_MLA // 256
    b128 = C_MLA // 128
    full = lambda shape: pl.BlockSpec(shape, lambda i: (0, 0))
    return pl.pallas_call(
        _mla_prep_kernel,
        grid=(n // tm,),
        in_specs=[pl.BlockSpec((tm, 256), lambda i: (i, b256)),
                  pl.BlockSpec((tm, 128), lambda i: (i, b128 + 2)),
                  pl.BlockSpec((tm, 128), lambda i: (i, b128 + 3)),
                  pl.BlockSpec((tm, 128), lambda i: (i, b128 + 4)),
                  pl.BlockSpec((tm, 128), lambda i: (i, 0)),
                  pl.BlockSpec((tm, 128), lambda i: (i, 0)),
                  full((1, 256)), full((1, 128)),
                  full((256, 512)), full((256, 512)), full((128, 512)), full((128, 512))],
        out_specs=[pl.BlockSpec((tm, 512), lambda i: (i, 0))] * 3,
        out_shape=[jax.ShapeDtypeStruct((n, 512), BF16)] * 3,
        compiler_params=_cparams(("parallel",)),
        name="mla_prep",
    )(cols, cols, cols, cols, cm, sm, q_norm.reshape(1, 256), kv_norm.reshape(1, 128), wq1, wq2, wk, wv)


def _mla_flash_kernel(q_ref, k_ref, v_ref, o_ref, *, tq, tk):
    i = pl.program_id(1)
    lane = lax.broadcasted_iota(jnp.int32, (tq, 128), 1)
    qpos = i * tq + lax.broadcasted_iota(jnp.int32, (tq, tk), 0)
    kloc = lax.broadcasted_iota(jnp.int32, (tq, tk), 1)
    nfull = (i * tq) // tk
    heads = range(N_HEADS)
    qs = [q_ref[:, h * 128:(h + 1) * 128] for h in heads]

    def step(j, carry, masked):
        ks = pl.multiple_of(j * tk, tk)
        ss = [_dot_nt(qs[h], k_ref[pl.ds(ks, tk), h * 128:(h + 1) * 128]) for h in heads]
        if masked:
            mask = ks + kloc <= qpos
            ss = [jnp.where(mask, s, NEG_INF) for s in ss]
        mns = [jnp.maximum(carry[h][0], jnp.max(ss[h], axis=-1, keepdims=True)) for h in heads]
        ps = [jnp.exp((ss[h] - mns[h]).astype(BF16)) for h in heads]
        als = [jnp.exp(carry[h][0] - mns[h]) for h in heads]
        pvs = [_dot(ps[h], v_ref[pl.ds(ks, tk), h * 128:(h + 1) * 128]) for h in heads]
        return tuple((mns[h], carry[h][1] * als[h] + pvs[h]) for h in heads)

    init = (jnp.full((tq, 1), NEG_INF, F32), jnp.zeros((tq, 128), F32))
    res = lax.fori_loop(0, nfull, functools.partial(step, masked=False), (init,) * N_HEADS)
    res = step(nfull, res, True)
    for p in range(N_PAIRS):
        a0, a1 = res[2 * p][1], res[2 * p + 1][1]
        acc = jnp.where(lane < HEAD_DIM, a0, a1)
        den = jnp.where(lane < HEAD_DIM, pltpu.roll(a0, HEAD_DIM, 1), pltpu.roll(a1, HEAD_DIM, 1))
        o_ref[:, p * 128:(p + 1) * 128] = acc / den


def mla_flash(q3, k3, v3, tq=256, tk=512):
    b, t, _ = q3.shape
    assert tk % tq == 0 and t % tk == 0
    w = N_HEADS * 128
    return pl.pallas_call(
        functools.partial(_mla_flash_kernel, tq=tq, tk=tk),
        grid=(b, t // tq),
        in_specs=[pl.BlockSpec((None, tq, w), lambda b_, i: (b_, i, 0)),
                  pl.BlockSpec((None, t, w), lambda b_, i: (b_, 0, 0)),
                  pl.BlockSpec((None, t, w), lambda b_, i: (b_, 0, 0))],
        out_specs=pl.BlockSpec((None, tq, D_GROUP), lambda b_, i: (b_, i, 0)),
        out_shape=jax.ShapeDtypeStruct((b, t, D_GROUP), F32),
        compiler_params=_cparams(("parallel", "arbitrary")),
        name="mla_flash",
    )(q3, k3, v3)


DILATED_PATTERNS = ((128, 1), (512, 4), (2048, 16))
DIL_BLOCK = 128
DIL_UNROLL = 4


def _rotate_half_heads(x):
    lane = lax.broadcasted_iota(jnp.int32, x.shape, 1)
    width = x.shape[1]
    half = HEAD_DIM // 2
    return jnp.where((lane & (HEAD_DIM - 1)) < half, -pltpu.roll(x, width - half, 1), pltpu.roll(x, half, 1))


def _dil_block(qb, kb, vb, mask, out):
    nq = qb.shape[0]
    lane = lax.broadcasted_iota(jnp.int32, (nq, 128), 1)
    kbb = kb.astype(BF16)
    vbb = vb.astype(BF16)
    hms = [(lane >> 6) == h for h in range(2)]
    ss = [jnp.where(mask, _dot_nt(jnp.where(hm, qb, 0.0).astype(BF16), kbb), NEG_INF) for hm in hms]
    yield
    ms = [jnp.max(s, axis=-1, keepdims=True) for s in ss]
    ps = [jnp.exp(s - m) for s, m in zip(ss, ms)]
    yield
    ls = [jnp.sum(p, axis=-1, keepdims=True) for p in ps]
    ohs = [_dot(p.astype(BF16), vbb) for p in ps]
    yield
    out[0] = jnp.where(hms[0], ohs[0] / ls[0], ohs[1] / ls[1])
    out[1] = jnp.where(hms[0], ms[0] + jnp.log(ls[0]), ms[1] + jnp.log(ls[1]))


def _dil_kernel(qraw_ref, kraw_ref, v_ref, cd_ref, sd_ref, o_ref, q_ref, k_ref, m_ref, z_ref, acc_ref, *, t):
    blk = DIL_BLOCK
    cd = cd_ref[...]
    sd = sd_ref[...]
    qraw = qraw_ref[...]
    kraw = kraw_ref[...]
    q_ref[...] = (qraw * cd + _rotate_half_heads(qraw) * sd) * (HEAD_DIM ** -0.5)
    k_ref[...] = kraw * cd + _rotate_half_heads(kraw) * sd
    for pi, (window, dil) in enumerate(DILATED_PATTERNS):
        assert window // dil == blk
        length = t // dil
        nb = length // blk
        nk = 2 * blk if nb > 1 else blk
        uq = lax.broadcasted_iota(jnp.int32, (blk, nk), 0)
        uk = lax.broadcasted_iota(jnp.int32, (blk, nk), 1)

        def one(it, dil=dil, nb=nb, nk=nk, uq=uq, uk=uk, first=(pi == 0)):
            r = it // nb
            jb = it % nb
            kb0 = jnp.maximum(jb - 1, 0) * blk
            qs = r + dil * blk * jb
            ks = r + dil * kb0
            if dil == 1:
                qidx = pl.ds(pl.multiple_of(qs, blk), blk)
                kidx = pl.ds(pl.multiple_of(ks, blk), nk)
            else:
                qidx = pl.ds(qs, blk, stride=dil)
                kidx = pl.ds(ks, nk, stride=dil)
            dist = (jb * blk + uq) - (kb0 + uk)
            mask = (dist >= 0) & (dist <= blk)
            out = [None, None]
            yield from _dil_block(q_ref[qidx, :], k_ref[kidx, :], v_ref[kidx, :], mask, out)
            o, lse = out
            if first:
                m_ref[qidx, :] = lse
                z_ref[qidx, :] = jnp.ones_like(lse)
                acc_ref[qidx, :] = o
            else:
                m_old = m_ref[qidx, :]
                m_new = jnp.maximum(m_old, lse)
                a = jnp.exp(m_old - m_new)
                b = jnp.exp(lse - m_new)
                m_ref[qidx, :] = m_new
                z_ref[qidx, :] = z_ref[qidx, :] * a + b
                acc_ref[qidx, :] = acc_ref[qidx, :] * a + o * b

        def body(it, carry, one=one):
            _round_robin([one(it * DIL_UNROLL + u) for u in range(DIL_UNROLL)])
            return carry

        assert (dil * nb) % DIL_UNROLL == 0
        lax.fori_loop(0, dil * nb // DIL_UNROLL, body, 0)
    o_ref[...] = acc_ref[...] / z_ref[...]


def dilated_attention(cols3, cd3, sd3):
    b, t, _ = cols3.shape
    assert t % (DIL_BLOCK * DILATED_PATTERNS[-1][1]) == 0
    c0 = C_DIL // 128
    col = lambda j: pl.BlockSpec((None, t, 128), lambda i, p: (i, 0, j + p))
    tab = pl.BlockSpec((None, t, 128), lambda i, p: (i, 0, 0))
    return pl.pallas_call(
        functools.partial(_dil_kernel, t=t),
        grid=(b, 2),
        in_specs=[col(c0), col(c0 + 2), col(c0 + 4), tab, tab],
        out_specs=pl.BlockSpec((None, t, 128), lambda i, p: (i, 0, p)),
        out_shape=jax.ShapeDtypeStruct((b, t, 256), F32),
        scratch_shapes=[pltpu.VMEM((t, 128), F32)] * 5,
        compiler_params=_cparams(("parallel", "parallel")),
        name="dilated_attention",
    )(cols3, cols3, cols3, cd3, sd3)


CHUNK = 64
PAIR = 2
PAIR_LANES = PAIR * HEAD_DIM
N_PAIRS = N_HEADS // PAIR
STACK = PAIR * CHUNK


def _stack_heads(x):
    lane = lax.broadcasted_iota(jnp.int32, x.shape, 1)
    return jnp.concatenate([jnp.where((lane >> 6) == h, x, 0.0) for h in range(PAIR)], axis=0)


def _unstack_heads(y):
    out = y[0:CHUNK]
    for h in range(1, PAIR):
        out = out + y[h * CHUNK:(h + 1) * CHUNK]
    return out


def _round_robin(stages):
    live = list(stages)
    while live:
        nxt = []
        for g in live:
            try:
                next(g)
                nxt.append(g)
            except StopIteration:
                pass
        live = nxt


def _head_mean_matrix(n=PAIR_LANES):
    r = lax.broadcasted_iota(jnp.int32, (n, n), 0)
    c = lax.broadcasted_iota(jnp.int32, (n, n), 1)
    return jnp.where((r >> 6) == (c >> 6), 1.0 / HEAD_DIM, 0.0).astype(BF16)


HGRN_HALVES = (32, 16, 8, 4, 2, 1)


def _hgrn_sum_matrix():
    t = lax.broadcasted_iota(jnp.int32, (CHUNK, CHUNK), 0)
    j = lax.broadcasted_iota(jnp.int32, (CHUNK, CHUNK), 1)
    tri = jnp.where(j <= t, 1.0, 0.0)
    blocks = [tri]
    for half in HGRN_HALVES:
        ref = (t // (2 * half)) * (2 * half) + half - 1
        blocks.append(tri - jnp.where(j <= ref, 1.0, 0.0))
    return jnp.concatenate(blocks, axis=0).astype(BF16)


def _hgrn_kernel(q_ref, f_ref, i_ref, g_ref, lbl_ref, gn_ref, o_ref, st_ref, *, layer, nchunk, nb):
    @pl.when(pl.program_id(1) == 0)
    def _():
        st_ref[...] = jnp.zeros_like(st_ref)

    logits = lbl_ref[...]
    e = jnp.exp(logits - jnp.max(logits, axis=0, keepdims=True))
    sm = e / jnp.sum(e, axis=0, keepdims=True)
    lb = jnp.zeros((1, 256), F32)
    for l in range(1, layer + 1):
        lb = lb + sm[l:l + 1, :]
    gn = gn_ref[...]
    summat = _hgrn_sum_matrix()
    hmean = _head_mean_matrix()
    r = lax.broadcasted_iota(jnp.int32, (STACK, STACK), 0)
    c = lax.broadcasted_iota(jnp.int32, (STACK, STACK), 1)
    same_head = (r >> 6) == (c >> 6)
    rc = r & (CHUNK - 1)
    cc = c & (CHUNK - 1)

    def one(bi, pi, rows):
        ls = pl.ds(pi * PAIR_LANES, PAIR_LANES)
        lbp = lb[:, pi * PAIR_LANES:(pi + 1) * PAIR_LANES]
        qv = q_ref[bi, rows, ls]
        qq = qv * jax.nn.sigmoid(qv)
        forget = lbp + (1.0 - lbp) * jax.nn.sigmoid(f_ref[bi, rows, ls])
        logf = jnp.log(forget)
        kk = 1.0 - forget
        v = i_ref[bi, rows, ls]
        yield
        sums = _split_dot_lhs(summat, logf)
        b = sums[0:CHUNK]
        a = jnp.where(rc == cc, _dot_nt(_stack_heads(qq).astype(BF16), _stack_heads(kk).astype(BF16)), 0.0)
        yield
        for li, half in enumerate(HGRN_HALVES):
            d = sums[(li + 1) * CHUNK:(li + 2) * CHUNK]
            ql = qq * jnp.exp(jnp.minimum(d, 0.0))
            kl = kk * jnp.exp(jnp.minimum(-d, 0.0))
            m = ((rc // (2 * half)) == (cc // (2 * half))) & ((rc % (2 * half)) >= half) & ((cc % (2 * half)) < half)
            a = a + jnp.where(m, _dot_nt(_stack_heads(ql).astype(BF16), _stack_heads(kl).astype(BF16)), 0.0)
            yield
        o = _unstack_heads(_dot(a.astype(BF16), _stack_heads(v).astype(BF16)))
        st = st_ref[bi, pi]
        o = o + _dot_nt((qq * jnp.exp(b)).astype(BF16), st.astype(BF16))
        yield
        b_end = b[CHUNK - 1:CHUNK, :]
        kbar = kk * jnp.exp(b_end - b)
        upd = lax.dot_general(v.astype(BF16), kbar.astype(BF16), (((0,), (0,)), ((), ())), preferred_element_type=F32)
        st_ref[bi, pi] = st * jnp.exp(b_end) + jnp.where(same_head, upd, 0.0)
        ms = _split_dot(o * o, hmean)
        yield
        gv = g_ref[bi, rows, ls]
        gnp = gn[:, pi * PAIR_LANES:(pi + 1) * PAIR_LANES]
        o_ref[bi, rows, ls] = o * lax.rsqrt(ms + RMS_EPS) * gnp * (gv * jax.nn.sigmoid(gv))

    def chunk(ci, carry):
        rows = pl.ds(pl.multiple_of(ci * CHUNK, CHUNK), CHUNK)
        _round_robin([one(bi, pi, rows) for bi in range(nb) for pi in range(N_PAIRS)])
        return carry

    lax.fori_loop(0, nchunk, chunk, 0)


SEQ_PER_STEP = 4


def hgrn(cols3, lb_logits, g_norm, layer, tb=512):
    b, t, _ = cols3.shape
    nb = SEQ_PER_STEP if b % SEQ_PER_STEP == 0 else 1
    c0 = C_HGRN // 256
    spec = lambda j: pl.BlockSpec((nb, tb, 256), lambda b_, i: (b_, i, j))
    depth = lb_logits.shape[0]
    return pl.pallas_call(
        functools.partial(_hgrn_kernel, layer=layer, nchunk=tb // CHUNK, nb=nb),
        grid=(b // nb, t // tb),
        in_specs=[spec(c0), spec(c0 + 1), spec(c0 + 2), spec(c0 + 3),
                  pl.BlockSpec((depth, 256), lambda b_, i: (0, 0)),
                  pl.BlockSpec((1, 256), lambda b_, i: (0, 0))],
        out_specs=spec(0),
        out_shape=jax.ShapeDtypeStruct((b, t, 256), F32),
        scratch_shapes=[pltpu.VMEM((nb, N_PAIRS, STACK, STACK), F32)],
        compiler_params=_cparams(("parallel", "arbitrary")),
        name="hgrn",
    )(cols3, cols3, cols3, cols3, lb_logits, g_norm.reshape(1, 256))


def _head_sum_matrix():
    r = lax.broadcasted_iota(jnp.int32, (256, 256), 0)
    c = lax.broadcasted_iota(jnp.int32, (256, 256), 1)
    return jnp.where((r >> 6) == (c >> 6), 1.0, 0.0).astype(BF16)


def _rwkv_prep_tile(a, last_prev, mu, w0, w2, a0, a2, g2, k_k, k_a, r_k):
    row = lax.broadcasted_iota(jnp.int32, a.shape, 0)
    prev = jnp.where(row == 0, last_prev, pltpu.roll(a, 1, 0))
    xs = a + (prev - a) * mu
    r = xs[:, 0:256]
    k = xs[:, 256:512]
    v = xs[:, 512:768]
    xwa = xs[:, 768:896]
    xg = xs[:, 896:1024]
    z = -(w0 + _dot(jnp.tanh(xwa).astype(BF16), w2))
    softplus = jnp.maximum(z, 0.0) + jnp.log(1.0 + jnp.exp(-jnp.abs(z)))
    lw = -jnp.exp(-softplus - 0.5)
    lr = jax.nn.sigmoid(a0 + _dot(xwa.astype(BF16), a2))
    g = _dot(jax.nn.sigmoid(xg).astype(BF16), g2)
    hsum = _head_sum_matrix()
    kk = k * k_k
    kk = kk / jnp.maximum(jnp.sqrt(_split_dot(kk * kk, hsum)), 1e-12)
    k2 = k * (1.0 + (lr - 1.0) * k_a)
    bonus = _split_dot(r * k2 * r_k, hsum) * v
    return r, lw, k2, v, kk, kk * lr, bonus, g


RWKV_MERGE_HALVES = (2, 4, 8, 16, 32)


def _compact_upper(x, half):
    return jnp.concatenate([x[s + half:s + 2 * half] for s in range(0, STACK, 2 * half)], axis=0)


def _expand_upper(c, half):
    zero = jnp.zeros((half, c.shape[1]), c.dtype)
    return jnp.concatenate([p for i in range(STACK // (2 * half)) for p in (zero, c[i * half:(i + 1) * half])], axis=0)


def _rwkv_kernel(c_ref, p_ref, mu_ref, w0_ref, w2_ref, a0_ref, a2_ref, g2_ref, kk_ref, ka_ref, rk_ref,
                 lnw_ref, lnb_ref, o_ref,
                 r_ref, lw_ref, k_ref, v_ref, kap_ref, at_ref, bonus_ref, g_ref, ht_ref, *, nchunk, nb):
    first = pl.program_id(1) == 0

    @pl.when(first)
    def _():
        ht_ref[...] = jnp.zeros_like(ht_ref)

    for bi in range(nb):
        outs = _rwkv_prep_tile(c_ref[bi], jnp.where(first, 0.0, p_ref[bi, 7:8, :]), mu_ref[...], w0_ref[...],
                               w2_ref[...], a0_ref[...], a2_ref[...], g2_ref[...], kk_ref[...], ka_ref[...],
                               rk_ref[...])
        for ref, val in zip((r_ref, lw_ref, k_ref, v_ref, kap_ref, at_ref, bonus_ref, g_ref), outs):
            ref[bi] = val

    t = lax.broadcasted_iota(jnp.int32, (CHUNK, CHUNK), 0)
    j = lax.broadcasted_iota(jnp.int32, (CHUNK, CHUNK), 1)
    tri = jnp.where(j <= t, 1.0, 0.0).astype(BF16)
    hmean = _head_mean_matrix()
    rr = lax.broadcasted_iota(jnp.int32, (STACK, STACK), 0)
    cc = lax.broadcasted_iota(jnp.int32, (STACK, STACK), 1)
    same_head = (rr >> 6) == (cc >> 6)
    rc = rr & (CHUNK - 1)
    sc = cc & (CHUNK - 1)
    eye = jnp.where(rr == cc, 1.0, 0.0)
    lnw = lnw_ref[...]
    lnb = lnb_ref[...]

    def one(bi, pi, rows):
        ls = pl.ds(pi * PAIR_LANES, PAIR_LANES)
        lw = lw_ref[bi, rows, ls]
        lc = _split_dot_lhs(tri, lw)
        l_end = lc[CHUNK - 1:CHUNK, :]
        kap = kap_ref[bi, rows, ls]
        at = at_ref[bi, rows, ls]
        k2 = k_ref[bi, rows, ls]
        v = v_ref[bi, rows, ls]
        inv = jnp.exp(-lc)
        fwd = jnp.exp(l_end - lc)
        s_kap = _stack_heads(kap * jnp.exp(lc - lw)).astype(BF16)
        s_r32 = _stack_heads(r_ref[bi, rows, ls] * jnp.exp(lc))
        s_r = s_r32.astype(BF16)
        s_a = _stack_heads(at * inv).astype(BF16)
        s_k = _stack_heads(k2 * inv).astype(BF16)
        s_v = _stack_heads(v).astype(BF16)
        strict = rc > sc
        incl = rc >= sc
        yield
        a_ka = jnp.where(strict, _dot_nt(s_kap, s_a), 0.0)
        a_kk = jnp.where(strict, _dot_nt(s_kap, s_k), 0.0).astype(BF16)
        m1 = ((rc >> 1) == (sc >> 1)) & strict
        tinv = eye - jnp.where(m1, a_ka, 0.0)
        yield
        a_ra = jnp.where(incl, _dot_nt(s_r, s_a), 0.0).astype(BF16)
        a_rk = jnp.where(incl, _dot_nt(s_r, s_k), 0.0).astype(BF16)
        for half in RWKV_MERGE_HALVES:
            m = ((rc // (2 * half)) == (sc // (2 * half))) & ((rc % (2 * half)) >= half) & ((sc % (2 * half)) < half)
            off = jnp.where(m, a_ka, 0.0)
            tb = tinv.astype(BF16)
            if half % 8 == 0:
                x = _expand_upper(_dot(_compact_upper(off, half).astype(BF16), tb), half).astype(BF16)
                yield
                tinv = tinv - _expand_upper(_dot(_compact_upper(tinv, half).astype(BF16), x), half)
            else:
                x = _dot(off.astype(BF16), tb).astype(BF16)
                yield
                tinv = tinv - _dot(tb, x)
            yield
        tb = tinv.astype(BF16)
        kt_st = _dot(tb, s_kap)
        w_st = _dot(a_kk, s_v).astype(BF16)
        yield
        vt_st = _dot(tb, w_st)
        rt = _unstack_heads(s_r32 - _dot(a_ra, kt_st.astype(BF16)))
        yield
        y0 = _unstack_heads(_dot(a_rk, s_v) - _dot(a_ra, vt_st.astype(BF16)))
        kt = _unstack_heads(kt_st)
        vt = _unstack_heads(vt_st)
        ht = ht_ref[bi, pi]
        htb = ht.astype(BF16)
        yield
        y = y0 + _dot_nt(rt.astype(BF16), htb)
        u = vt + _dot_nt(kt.astype(BF16), htb)
        yield
        lhs = jnp.concatenate([v, u], axis=0).astype(BF16)
        rhs = jnp.concatenate([k2 * fwd, -(at * fwd)], axis=0).astype(BF16)
        upd = lax.dot_general(lhs, rhs, (((0,), (0,)), ((), ())), preferred_element_type=F32)
        ht_ref[bi, pi] = ht * jnp.exp(l_end) + jnp.where(same_head, upd, 0.0)
        mean = _split_dot(y, hmean)
        yield
        yc = y - mean
        var = _split_dot(yc * yc, hmean)
        yield
        lanes = slice(pi * PAIR_LANES, (pi + 1) * PAIR_LANES)
        yn = yc * lax.rsqrt(var + RWKV_LN_EPS) * lnw[:, lanes] + lnb[:, lanes]
        o_ref[bi, rows, ls] = (yn + bonus_ref[bi, rows, ls]) * g_ref[bi, rows, ls]

    def chunk(ci, carry):
        rows = pl.ds(pl.multiple_of(ci * CHUNK, CHUNK), CHUNK)
        _round_robin([one(bi, pi, rows) for bi in range(nb) for pi in range(N_PAIRS)])
        return carry

    lax.fori_loop(0, nchunk, chunk, 0)


def rwkv_mix(cols3, mu, w0, w2p, a0, a2p, g2, k_k, k_a, r_k, ln_w, ln_b, tb=512):
    b, t, _ = cols3.shape
    nb = SEQ_PER_STEP if b % SEQ_PER_STEP == 0 else 1
    spec = pl.BlockSpec((nb, tb, 256), lambda b_, i: (b_, i, 0))
    full = lambda shape: pl.BlockSpec(shape, lambda b_, i: (0, 0))
    row = lambda x: x.reshape(1, -1)
    return pl.pallas_call(
        functools.partial(_rwkv_kernel, nchunk=tb // CHUNK, nb=nb),
        grid=(b // nb, t // tb),
        in_specs=[pl.BlockSpec((nb, tb, 1024), lambda b_, i: (b_, i, C_RWKV // 1024)),
                  pl.BlockSpec((nb, 8, 1024), lambda b_, i: (b_, jnp.maximum(i * (tb // 8) - 1, 0), C_RWKV // 1024)),
                  full((1, 1024)), full((1, 256)), full((128, 256)), full((1, 256)), full((128, 256)),
                  full((128, 256)), full((1, 256)), full((1, 256)), full((1, 256)), full((1, 256)), full((1, 256))],
        out_specs=spec,
        out_shape=jax.ShapeDtypeStruct((b, t, 256), F32),
        scratch_shapes=[pltpu.VMEM((nb, tb, 256), F32)] * 8 + [pltpu.VMEM((nb, N_PAIRS, STACK, STACK), F32)],
        compiler_params=_cparams(("parallel", "arbitrary")),
        name="rwkv_mix",
    )(cols3, cols3, row(mu), row(w0), w2p, row(a0), a2p, g2, row(k_k), row(k_a), row(r_k), row(ln_w), row(ln_b))


def _rot_cols(w, half):
    g = w.reshape(w.shape[0], -1, 2, half)
    return jnp.concatenate([-g[:, :, 1:2, :], g[:, :, 0:1, :]], axis=2).reshape(w.shape)


def pack_w_in(w):
    d = w.shape[0]
    rw = w[:, 0:1024]
    dq, dk, dv = w[:, 1024:1280], w[:, 1280:1536], w[:, 1536:1792]
    cq, ckv, kr = w[:, 1792:2048], w[:, 2048:2176], w[:, 2176:2208]
    hg = w[:, 2208:3232]
    z = lambda n: jnp.zeros((d, n), w.dtype)
    kra = jnp.concatenate([z(MLA_NOPE), kr, z(128 - MLA_NOPE - MLA_ROPE)], axis=1)
    krb = jnp.concatenate([z(MLA_NOPE), _rot_cols(kr, MLA_ROPE // 2), z(128 - MLA_NOPE - MLA_ROPE)], axis=1)
    packed = jnp.concatenate([rw, dq, dk, dv, hg, cq, ckv, kra, krb], axis=1)
    return packed.astype(BF16)


def pack_mla(w_uq, w_ukv):
    rq = w_uq.shape[0]
    q = w_uq.reshape(rq, N_HEADS, MLA_NOPE + MLA_ROPE)
    nope, rope = q[..., :MLA_NOPE], q[..., MLA_NOPE:]
    pad = jnp.zeros((rq, N_HEADS, 128 - MLA_NOPE - MLA_ROPE), w_uq.dtype)
    wq1 = jnp.concatenate([nope, rope, pad], axis=-1).reshape(rq, N_HEADS * 128)
    rrot = _rot_cols(rope.reshape(rq, N_HEADS * MLA_ROPE), MLA_ROPE // 2).reshape(rq, N_HEADS, MLA_ROPE)
    wq2 = jnp.concatenate([jnp.zeros_like(nope), rrot, pad], axis=-1).reshape(rq, N_HEADS * 128)
    rk = w_ukv.shape[0]
    kv = w_ukv.reshape(rk, N_HEADS, MLA_NOPE + HEAD_DIM)
    wk = jnp.concatenate([kv[..., :MLA_NOPE], jnp.zeros((rk, N_HEADS, 128 - MLA_NOPE), w_ukv.dtype)],
                         axis=-1).reshape(rk, N_HEADS * 128)
    vz = jnp.zeros((rk, HEAD_DIM), w_ukv.dtype)
    wv = jnp.concatenate([x for h in range(N_HEADS)
                          for x in ((kv[:, h, MLA_NOPE:], vz) if h % 2 == 0 else (vz, kv[:, h, MLA_NOPE:]))], axis=1)
    return wq1.astype(BF16), wq2.astype(BF16), wk.astype(BF16), wv.astype(BF16)


def kernel(x, positions, mix_norm, w_in, w_out, rwkv_mu, rwkv_w0, rwkv_w2, rwkv_a0, rwkv_a2, rwkv_g2, rwkv_k_k,
           rwkv_k_a, rwkv_r_k, rwkv_ln_w, rwkv_ln_b, mla_q_norm, mla_w_uq, mla_kv_norm, mla_w_ukv, hgrn_lb_logits,
           hgrn_g_norm, ffn_norm, ffn_w1, ffn_w3, ffn_w2, moe_router, moe_w1, moe_w3, moe_w2, final_norm):
    b, t, d = x.shape
    n = b * t
    depth = w_in.shape[0]
    x2 = x.reshape(n, d)
    cd, sd, cm, sm = rope_tables(positions.reshape(n, 1))
    lora_pad = jnp.zeros((rwkv_w2.shape[1], D_GROUP), F32)
    fused_final = False
    for layer in range(depth):
        cols = in_proj(x2, mix_norm[layer], pack_w_in(w_in[layer]))
        cols3 = cols.reshape(b, t, N_PACK)
        y_a = rwkv_mix(cols3, rwkv_mu[layer], rwkv_w0[layer],
                       jnp.concatenate([rwkv_w2[layer], lora_pad], axis=0).astype(BF16), rwkv_a0[layer],
                       jnp.concatenate([lora_pad, rwkv_a2[layer]], axis=0).astype(BF16),
                       rwkv_g2[layer].astype(BF16), rwkv_k_k[layer], rwkv_k_a[layer], rwkv_r_k[layer],
                       rwkv_ln_w[layer], rwkv_ln_b[layer])
        y_b = dilated_attention(cols3, cd.reshape(b, t, -1), sd.reshape(b, t, -1))
        q_m, k_m, v_m = mla_prep(cols, cm, sm, mla_q_norm[layer], mla_kv_norm[layer],
                                 *pack_mla(mla_w_uq[layer], mla_w_ukv[layer]))
        y_c = mla_flash(q_m.reshape(b, t, -1), k_m.reshape(b, t, -1), v_m.reshape(b, t, -1))
        y_d = hgrn(cols3, hgrn_lb_logits, hgrn_g_norm[layer], layer)
        ys = [y.reshape(n, D_GROUP) for y in (y_a, y_b, y_c, y_d)]
        j = layer // 2
        if layer % 2 == 0:
            x2, h = out_proj(x2, ys, w_out[layer].astype(BF16), ffn_norm[layer], True)
            x2 = ffn_dense(h, x2, ffn_w1[j].astype(BF16), ffn_w3[j].astype(BF16), ffn_w2[j].astype(BF16))
        else:
            x2, = out_proj(x2, ys, w_out[layer].astype(BF16), ffn_norm[layer], False)
            route, hp = moe_route(x2, ffn_norm[layer], jnp.pad(moe_router[j], ((0, 0), (0, 128 - N_EXPERTS))))
            rows_pad = TOP_K * n + N_EXPERTS * MOE_TILE
            pos, tile_expert, tile_valid, clear_tiles = moe_plan(route, rows_pad)
            xs = moe_scatter(pos, clear_tiles, hp, rows_pad)
            ys = moe_experts(tile_expert, tile_valid, xs, moe_w1[j].astype(BF16), moe_w3[j].astype(BF16),
                             moe_w2[j].astype(BF16))
            fused_final = layer == depth - 1
            x2 = moe_combine(pos, x2, route, ys, final_norm, fused_final)
    if not fused_final:
        x2 = final_rms(x2, final_norm)
    return x2.reshape(b, t, d)
```
